```python
import math
import jax, jax.numpy as jnp
from jax import lax
import numpy as np

D_MODEL = 1024
BATCH = 8
SEQ = 2048
DEPTH = 1

HEAD_DIM = 64
N_HEADS_MOBA = D_MODEL // HEAD_DIM // 2
N_HEADS_SWA = D_MODEL // HEAD_DIM // 2
N_KV_SWA = 2
N_HEADS = N_HEADS_MOBA + N_HEADS_SWA
D_MOBA = N_HEADS_MOBA * HEAD_DIM
D_SWA_Q = N_HEADS_SWA * HEAD_DIM
D_SWA_KV = N_KV_SWA * HEAD_DIM
D_MIX = D_MOBA + D_SWA_Q
D_IN = 3 * D_MOBA + D_SWA_Q + 2 * D_SWA_KV
MOBA_BLOCK = 256
MOBA_TOPK = 3
MOBA_QCHUNK = 32
SWA_WINDOW = 128
SWA_BLOCK = 128
REL_BUCKETS = 32
REL_MAX_DIST = 1024
N_EXPERTS = 32
TOP_K = 4
D_FF = 1024
SWIGLU_LIMIT = 7.0
SWIGLU_ALPHA = 1.702
EPS = 1e-5
NEG = -1e30

kernel_name = "hymba_moba_swa_sink_moe_adaln"


def rmsnorm(x, g):
    xf = x.astype(jnp.float32)
    y = xf * lax.rsqrt(jnp.mean(xf * xf, axis=-1, keepdims=True) + EPS)
    return (y * g.astype(jnp.float32)).astype(x.dtype)


def rel_bucket(dist):
    max_exact = REL_BUCKETS // 2
    n = jnp.maximum(dist, 0)
    nf = jnp.maximum(n, 1).astype(jnp.float32)
    large = max_exact + (jnp.log(nf / max_exact) / math.log(REL_MAX_DIST / max_exact)
                         * (REL_BUCKETS - max_exact)).astype(jnp.int32)
    large = jnp.minimum(large, REL_BUCKETS - 1)
    return jnp.where(n < max_exact, n, large)


def moba_attention(q, k, v, rel_table):
    B, S, H, Dh = q.shape
    L = MOBA_BLOCK
    Sp = -(-S // L) * L
    pad = Sp - S
    if pad:
        q, k, v = [jnp.pad(a, ((0, 0), (0, pad), (0, 0), (0, 0))) for a in (q, k, v)]
    nb = Sp // L
    kbh = k.reshape(B, nb, L, H, Dh).transpose(0, 3, 1, 2, 4)
    vbh = v.reshape(B, nb, L, H, Dh).transpose(0, 3, 1, 2, 4)
    kmean = jnp.mean(kbh.astype(jnp.float32), axis=3)
    gate = jnp.einsum('bshd,bhnd->bhsn', q.astype(jnp.float32), kmean)
    qblk = jnp.arange(Sp) // L
    past = jnp.arange(nb)[None, :] < qblk[:, None]
    gate = jnp.where(past, gate, NEG)
    kk = min(MOBA_TOPK, nb)
    _, sel = lax.top_k(gate, kk)
    QC = MOBA_QCHUNK
    nc = Sp // QC
    q_all = q.reshape(B, nc, QC, H, Dh).transpose(1, 0, 2, 3, 4)
    sel_all = sel.reshape(B, H, nc, QC, kk).transpose(2, 0, 1, 3, 4)
    table_h = rel_table.T
    b_ix = jnp.arange(B)[:, None, None, None]
    h_ix = jnp.arange(H)[None, :, None, None]
    offs = jnp.arange(L)
    scale = Dh ** -0.5

    def chunk(args):
        qc, selc, ci = args
        start = ci * QC
        qpos = start + jnp.arange(QC)
        own = start // L
        kg = kbh[b_ix, h_ix, selc]
        vg = vbh[b_ix, h_ix, selc]
        s_past = jnp.einsum('bqhd,bhqjld->bhqjl', qc, kg,
                            preferred_element_type=jnp.float32) * scale
        kpos = selc[..., None] * L + offs
        s_past = s_past + table_h[h_ix[..., None], rel_bucket(qpos[:, None, None] - kpos)].astype(jnp.float32)
        valid = jnp.arange(kk) < own
        s_past = jnp.where(valid[:, None], s_past, NEG)
        kown = lax.dynamic_index_in_dim(kbh, own, axis=2, keepdims=False)
        vown = lax.dynamic_index_in_dim(vbh, own, axis=2, keepdims=False)
        s_own = jnp.einsum('bqhd,bhld->bhql', qc, kown,
                           preferred_element_type=jnp.float32) * scale
        dist_own = qpos[:, None] - (own * L + offs)[None, :]
        s_own = jnp.where(dist_own >= 0,
                          s_own + table_h[:, rel_bucket(dist_own)][None].astype(jnp.float32), NEG)
        logits = jnp.concatenate([s_past.reshape(B, H, QC, kk * L), s_own], axis=-1)
        p = jax.nn.softmax(logits, axis=-1)
        p_past = p[..., :kk * L].reshape(B, H, QC, kk, L).astype(v.dtype)
        p_own = p[..., kk * L:].astype(v.dtype)
        return (jnp.einsum('bhqjl,bhqjld->bqhd', p_past, vg)
                + jnp.einsum('bhql,bhld->bqhd', p_own, vown))

    out = lax.map(chunk, (q_all, sel_all, jnp.arange(nc)))
    return out.transpose(1, 0, 2, 3, 4).reshape(B, Sp, H, Dh)[:, :S]


def swa_attention(q, k, v, sinks, rel_table):
    B, S, Hq, Dh = q.shape
    Hkv = k.shape[2]
    G = Hq // Hkv
    W = SWA_BLOCK
    nq = S // W
    qb = q.reshape(B, nq, W, Hkv, G, Dh)

    def band(a):
        prev = jnp.pad(a, ((0, 0), (W, 0), (0, 0), (0, 0)))[:, :S]
        return jnp.concatenate([prev.reshape(B, nq, W, Hkv, Dh),
                                a.reshape(B, nq, W, Hkv, Dh)], axis=2)

    kw, vw = band(k), band(v)
    s = jnp.einsum('bnqhgd,bnkhd->bnhgqk', qb, kw,
                   preferred_element_type=jnp.float32) * Dh ** -0.5
    qi = jnp.arange(W)
    ki = jnp.arange(2 * W)
    dist = qi[:, None] + W - ki[None, :]
    kpos = jnp.arange(nq)[:, None] * W - W + ki[None, :]
    allowed = (dist >= 0) & (dist < SWA_WINDOW) & (kpos[:, None, :] >= 0)
    bias = rel_table[rel_bucket(dist)].astype(jnp.float32).transpose(2, 0, 1).reshape(Hkv, G, W, 2 * W)
    s = jnp.where(allowed[None, :, None, None], s + bias, NEG)
    sink = jnp.broadcast_to(sinks.astype(jnp.float32).reshape(Hkv, G, 1, 1), s.shape[:-1] + (1,))
    p = jax.nn.softmax(jnp.concatenate([s, sink], axis=-1), axis=-1)[..., :-1].astype(v.dtype)
    o = jnp.einsum('bnhgqk,bnkhd->bnqhgd', p, vw)
    return o.reshape(B, S, Hq, Dh)


def moe_ffn(h, w_router, b_router, w_gate_up, b_gate_up, w_down, b_down):
    B, S, D = h.shape
    t = h.reshape(B * S, D)
    logits = (t @ w_router + b_router).astype(jnp.float32)
    top_val, top_idx = lax.top_k(logits, TOP_K)
    top_w = jax.nn.softmax(top_val, axis=-1)
    combine = jnp.sum(jax.nn.one_hot(top_idx, N_EXPERTS, dtype=jnp.float32) * top_w[..., None], axis=1)
    out = jnp.zeros((B * S, D), jnp.float32)
    for e in range(N_EXPERTS):
        gu = t @ w_gate_up[e] + b_gate_up[e]
        g = jnp.minimum(gu[:, :D_FF], SWIGLU_LIMIT)
        u = jnp.clip(gu[:, D_FF:], -SWIGLU_LIMIT, SWIGLU_LIMIT)
        y = ((u + 1) * (g * jax.nn.sigmoid(SWIGLU_ALPHA * g))) @ w_down[e] + b_down[e]
        out = out + combine[:, e:e + 1] * y
    return out.astype(h.dtype).reshape(B, S, D)


def setup_inputs(seed: int = 0) -> dict:
    key = jax.random.key(seed)
    ks = jax.random.split(key, 20)
    f32 = jnp.float32

    def nrm(k, shape, s):
        return jax.random.normal(k, shape, f32) * s

    return {
        "x": nrm(ks[0], (BATCH, SEQ, D_MODEL), 1.0),
        "c": nrm(ks[1], (BATCH, D_MODEL), 1.0),
        "g_attn": 1.0 + nrm(ks[2], (DEPTH, D_MODEL), 0.02),
        "w_ada": nrm(ks[3], (DEPTH, D_MODEL, 6 * D_MODEL), 0.5 * D_MODEL ** -0.5),
        "b_ada": nrm(ks[4], (DEPTH, 6 * D_MODEL), 0.02),
        "w_in": nrm(ks[5], (DEPTH, D_MODEL, D_IN), D_MODEL ** -0.5),
        "b_in": nrm(ks[6], (DEPTH, D_IN), 0.02),
        "w_out": nrm(ks[7], (DEPTH, D_MIX, D_MODEL), D_MIX ** -0.5),
        "b_out": nrm(ks[8], (DEPTH, D_MODEL), 0.02),
        "rel_table": nrm(ks[9], (REL_BUCKETS, N_HEADS), 0.5),
        "sinks": nrm(ks[10], (DEPTH, N_HEADS_SWA), 1.0),
        "g_ffn": 1.0 + nrm(ks[11], (DEPTH, D_MODEL), 0.02),
        "w_router": nrm(ks[12], (DEPTH, D_MODEL, N_EXPERTS), D_MODEL ** -0.5),
        "b_router": nrm(ks[13], (DEPTH, N_EXPERTS), 0.01),
        "w_gate_up": nrm(ks[14], (DEPTH, N_EXPERTS, D_MODEL, 2 * D_FF), D_MODEL ** -0.5),
        "b_gate_up": nrm(ks[15], (DEPTH, N_EXPERTS, 2 * D_FF), 0.02),
        "w_down": nrm(ks[16], (DEPTH, N_EXPERTS, D_FF, D_MODEL), D_FF ** -0.5),
        "b_down": nrm(ks[17], (DEPTH, N_EXPERTS, D_MODEL), 0.02),
        "g_final": 1.0 + nrm(ks[18], (D_MODEL,), 0.02),
    }


def reference(x, c, g_attn, w_ada, b_ada, w_in, b_in, w_out, b_out, rel_table, sinks,
              g_ffn, w_router, b_router, w_gate_up, b_gate_up, w_down, b_down, g_final):
    B, S, D = x.shape
    sc = jax.nn.silu(c)
    for l in range(DEPTH):
        mod = sc @ w_ada[l] + b_ada[l]
        shift_a, scale_a, gate_a, shift_f, scale_f, gate_f = [m[:, None, :] for m in jnp.split(mod, 6, axis=-1)]

        h = rmsnorm(x, g_attn[l]) * (1 + scale_a) + shift_a
        proj = h @ w_in[l] + b_in[l]
        i0 = D_MOBA
        i1 = 2 * D_MOBA
        i2 = 3 * D_MOBA
        i3 = i2 + D_SWA_Q
        i4 = i3 + D_SWA_KV
        q_m = proj[..., :i0].reshape(B, S, N_HEADS_MOBA, HEAD_DIM)
        k_m = proj[..., i0:i1].reshape(B, S, N_HEADS_MOBA, HEAD_DIM)
        v_m = proj[..., i1:i2].reshape(B, S, N_HEADS_MOBA, HEAD_DIM)
        q_s = proj[..., i2:i3].reshape(B, S, N_HEADS_SWA, HEAD_DIM)
        k_s = proj[..., i3:i4].reshape(B, S, N_KV_SWA, HEAD_DIM)
        v_s = proj[..., i4:].reshape(B, S, N_KV_SWA, HEAD_DIM)
        o_m = moba_attention(q_m, k_m, v_m, rel_table[:, :N_HEADS_MOBA])
        o_s = swa_attention(q_s, k_s, v_s, sinks[l], rel_table[:, N_HEADS_MOBA:])
        o = jnp.concatenate([o_m.reshape(B, S, D_MOBA), o_s.reshape(B, S, D_SWA_Q)], axis=-1)
        x = x + gate_a * (o @ w_out[l] + b_out[l])

        h = rmsnorm(x, g_ffn[l]) * (1 + scale_f) + shift_f
        x = x + gate_f * moe_ffn(h, w_router[l], b_router[l], w_gate_up[l], b_gate_up[l],
                                 w_down[l], b_down[l])
    return rmsnorm(x, g_final)
```

```python
import functools
import math

import jax
import jax.numpy as jnp
from jax import lax
from jax.experimental import pallas as pl
from jax.experimental.pallas import tpu as pltpu

F32 = jnp.float32
BF16 = jnp.bfloat16

D_MODEL = 1024
HEAD_DIM = 64
N_HEADS_MOBA = 8
N_HEADS_SWA = 8
N_KV_SWA = 2
D_MOBA = N_HEADS_MOBA * HEAD_DIM
D_SWA_Q = N_HEADS_SWA * HEAD_DIM
D_SWA_KV = N_KV_SWA * HEAD_DIM
D_IN = 3 * D_MOBA + D_SWA_Q + 2 * D_SWA_KV
MOBA_BLOCK = 256
MOBA_TOPK = 3
SWA_WINDOW = 128
SWA_BLOCK = 128
REL_BUCKETS = 32
REL_MAX_DIST = 1024
REL_MAX_EXACT = REL_BUCKETS // 2
N_EXPERTS = 32
TOP_K = 4
D_FF = 1024
SWIGLU_LIMIT = 7.0
SWIGLU_ALPHA = 1.702
EPS = 1e-5
NEG = -1e30
ATTN_SCALE = HEAD_DIM ** -0.5

LANES = 128
HEADS_PER_LANE_BLOCK = LANES // HEAD_DIM
VMEM_LIMIT_BYTES = 56 * 1024 * 1024

_REL_THRESHOLDS = tuple(
    math.ceil(REL_MAX_EXACT * (REL_MAX_DIST / REL_MAX_EXACT) ** (k / (REL_BUCKETS - REL_MAX_EXACT)) - 1e-9)
    for k in range(1, REL_BUCKETS - REL_MAX_EXACT))


def _cparams(n_axes):
    return pltpu.CompilerParams(dimension_semantics=("arbitrary",) * n_axes,
                                vmem_limit_bytes=VMEM_LIMIT_BYTES)


def _sigmoid(z):
    return 1.0 / (1.0 + jnp.exp(-z))


def _rmsnorm_rows(xf, g):
    ms = jnp.mean(xf * xf, axis=-1, keepdims=True)
    return xf * lax.rsqrt(ms + EPS) * g


def _dot_nt(a, b, **kw):
    return lax.dot_general(a, b, (((1,), (1,)), ((), ())), preferred_element_type=F32, **kw)


def _ada_kernel(c_ref, w_ref, b_ref, o_ref):
    c = c_ref[...]
    sc = c * _sigmoid(c)
    o_ref[...] = jnp.dot(sc, w_ref[...], preferred_element_type=F32,
                         precision=lax.Precision.HIGHEST) + b_ref[...]


def _ada(c, w_ada, b_ada):
    B = c.shape[0]
    n_out = w_ada.shape[1]
    bn = 1536
    return pl.pallas_call(
        _ada_kernel,
        grid=(n_out // bn,),
        in_specs=[pl.BlockSpec((B, D_MODEL), lambda n: (0, 0)),
                  pl.BlockSpec((D_MODEL, bn), lambda n: (0, n)),
                  pl.BlockSpec((1, bn), lambda n: (0, n))],
        out_specs=pl.BlockSpec((B, bn), lambda n: (0, n)),
        out_shape=jax.ShapeDtypeStruct((B, n_out), F32),
        compiler_params=_cparams(1),
        name="ada",
    )(c, w_ada, b_ada.reshape(1, n_out))


def _inproj_kernel(x_ref, mod_ref, g_ref, w_ref, b_ref,
                   qm_ref, km_ref, vmT_ref, kmean_ref, qs_ref, ks_ref, vsT_ref):
    xf = x_ref[0]
    shift = mod_ref[0, 0:1, :]
    scale = mod_ref[0, 1:2, :]
    h = _rmsnorm_rows(xf, g_ref[...]) * (1.0 + scale) + shift
    proj = jnp.dot(h.astype(BF16), w_ref[...], preferred_element_type=F32) + b_ref[...]
    i0, i1, i2 = D_MOBA, 2 * D_MOBA, 3 * D_MOBA
    i3 = i2 + D_SWA_Q
    i4 = i3 + D_SWA_KV
    qm_ref[0] = proj[:, :i0].astype(BF16)
    km = proj[:, i0:i1]
    km_ref[0] = km.astype(BF16)
    kmean_ref[0, 0] = jnp.broadcast_to(jnp.mean(km, axis=0, keepdims=True), (8, D_MOBA))
    vmT = proj[:, i1:i2].T.astype(BF16)
    vmT_ref[0, :, 0] = vmT.reshape(D_MOBA // LANES, LANES, vmT.shape[1])
    qs_ref[0] = proj[:, i2:i3].astype(BF16)
    ks_ref[0] = proj[:, i3:i4].astype(BF16)
    vsT_ref[0] = proj[:, i4:].T.astype(BF16)


def _inproj(x, mod6, g_attn, w_in_bf, b_in):
    B, S, D = x.shape
    tm = MOBA_BLOCK
    nb = S // tm
    n_pairs = D_MOBA // LANES
    out_shape = (
        jax.ShapeDtypeStruct((B, S, D_MOBA), BF16),
        jax.ShapeDtypeStruct((B, S, D_MOBA), BF16),
        jax.ShapeDtypeStruct((B, n_pairs, nb, LANES, tm), BF16),
        jax.ShapeDtypeStruct((B, nb, 8, D_MOBA), F32),
        jax.ShapeDtypeStruct((B, S, D_SWA_Q), BF16),
        jax.ShapeDtypeStruct((B, S, D_SWA_KV), BF16),
        jax.ShapeDtypeStruct((B, D_SWA_KV, S), BF16),
    )
    return pl.pallas_call(
        _inproj_kernel,
        grid=(B, nb),
        in_specs=[pl.BlockSpec((1, tm, D), lambda b, i: (b, i, 0)),
                  pl.BlockSpec((1, 6, D), lambda b, i: (b, 0, 0)),
                  pl.BlockSpec((1, D), lambda b, i: (0, 0)),
                  pl.BlockSpec((D, D_IN), lambda b, i: (0, 0)),
                  pl.BlockSpec((1, D_IN), lambda b, i: (0, 0))],
        out_specs=(pl.BlockSpec((1, tm, D_MOBA), lambda b, i: (b, i, 0)),
                   pl.BlockSpec((1, tm, D_MOBA), lambda b, i: (b, i, 0)),
                   pl.BlockSpec((1, n_pairs, 1, LANES, tm), lambda b, i: (b, 0, i, 0, 0)),
                   pl.BlockSpec((1, 1, 8, D_MOBA), lambda b, i: (b, i, 0, 0)),
                   pl.BlockSpec((1, tm, D_SWA_Q), lambda b, i: (b, i, 0)),
                   pl.BlockSpec((1, tm, D_SWA_KV), lambda b, i: (b, i, 0)),
                   pl.BlockSpec((1, D_SWA_KV, tm), lambda b, i: (b, 0, i))),
        out_shape=out_shape,
        compiler_params=_cparams(2),
        name="inproj",
    )(x, mod6, g_attn, w_in_bf, b_in)


def _rel_bias_values(dist, table_ref, h):
    n = jnp.maximum(dist, 0)
    large = jnp.full(n.shape, REL_MAX_EXACT, jnp.int32)
    for t in _REL_THRESHOLDS:
        large = large + (n >= t).astype(jnp.int32)
    bucket = jnp.where(n < REL_MAX_EXACT, n, large)
    val = jnp.full(n.shape, table_ref[0, h], F32)
    for bkt in range(1, REL_BUCKETS):
        val = jnp.where(bucket == bkt, table_ref[bkt, h], val)
    return val


def _moba_bias_kernel(table_ref, o_ref):
    h = pl.program_id(0)
    dlt = pl.program_id(1)
    L = MOBA_BLOCK
    key = lax.broadcasted_iota(jnp.int32, (L, L), 0)
    qry = lax.broadcasted_iota(jnp.int32, (L, L), 1)
    dist = dlt * L + qry - key
    val = _rel_bias_values(dist, table_ref, h)
    o_ref[0, 0] = jnp.where(dist >= 0, val, NEG)


def _moba_bias(rel_table, nb):
    L = MOBA_BLOCK
    return pl.pallas_call(
        _moba_bias_kernel,
        grid=(N_HEADS_MOBA, nb),
        in_specs=[pl.BlockSpec(memory_space=pltpu.SMEM)],
        out_specs=pl.BlockSpec((1, 1, L, L), lambda h, d: (h, d, 0, 0)),
        out_shape=jax.ShapeDtypeStruct((N_HEADS_MOBA, nb, L, L), F32),
        compiler_params=_cparams(2),
        name="moba_bias",
    )(rel_table)


def _swa_bias_kernel(table_ref, o_ref):
    slot = pl.program_id(0)
    h = N_HEADS_MOBA + (slot // 2) + (N_HEADS_SWA // N_KV_SWA) * (slot % 2)
    W = SWA_BLOCK
    key = lax.broadcasted_iota(jnp.int32, (2 * W, W), 0)
    qry = lax.broadcasted_iota(jnp.int32, (2 * W, W), 1)
    dist = qry + W - key
    val = _rel_bias_values(dist, table_ref, h)
    o_ref[0] = jnp.where((dist >= 0) & (dist < SWA_WINDOW), val, NEG)


def _swa_bias(rel_table):
    W = SWA_BLOCK
    return pl.pallas_call(
        _swa_bias_kernel,
        grid=(N_HEADS_SWA,),
        in_specs=[pl.BlockSpec(memory_space=pltpu.SMEM)],
        out_specs=pl.BlockSpec((1, 2 * W, W), lambda s: (s, 0, 0)),
        out_shape=jax.ShapeDtypeStruct((N_HEADS_SWA, 2 * W, W), F32),
        compiler_params=_cparams(1),
        name="swa_bias",
    )(rel_table)


def _gate_kernel(q_ref, kmean_ref, sel_ref):
    S = q_ref.shape[1]
    nb = kmean_ref.shape[1]
    H = N_HEADS_MOBA
    q = q_ref[0]
    km = kmean_ref[0, :, 0, :]
    km_t = jnp.concatenate([km] * H, axis=0)
    row_h = lax.broadcasted_iota(jnp.int32, (H * nb, D_MOBA), 0) // nb
    col_h = lax.broadcasted_iota(jnp.int32, (H * nb, D_MOBA), 1) // HEAD_DIM
    km_t = jnp.where(row_h == col_h, km_t, 0.0)
    km_hi = km_t.astype(BF16)
    km_lo = (km_t - km_hi.astype(F32)).astype(BF16)
    g_all = _dot_nt(km_hi, q) + _dot_nt(km_lo, q)
    qblk = lax.broadcasted_iota(jnp.int32, (nb, S), 1) // MOBA_BLOCK
    jidx = lax.broadcasted_iota(jnp.int32, (nb, S), 0)
    past = jidx < qblk
    for h in range(H):
        g = g_all[h * nb:(h + 1) * nb, :]
        cnt = jnp.zeros((nb, S), jnp.int32)
        for jp in range(nb):
            gj = g[jp:jp + 1, :]
            ahead = (gj > g) | ((gj == g) & (jp < jidx))
            cnt = cnt + ((jp < qblk) & ahead).astype(jnp.int32)
        sel = (past & (cnt < MOBA_TOPK)).astype(F32)
        for j in range(nb):
            sel_ref[0, h, j] = sel[j:j + 1, :]


def _gate(qm, kmean):
    B, S, _ = qm.shape
    nb = kmean.shape[1]
    return pl.pallas_call(
        _gate_kernel,
        grid=(B,),
        in_specs=[pl.BlockSpec((1, S, D_MOBA), lambda b: (b, 0, 0)),
                  pl.BlockSpec((1, nb, 8, D_MOBA), lambda b: (b, 0, 0, 0))],
        out_specs=pl.BlockSpec((1, N_HEADS_MOBA, nb, 1, S), lambda b: (b, 0, 0, 0, 0)),
        out_shape=jax.ShapeDtypeStruct((B, N_HEADS_MOBA, nb, 1, S), F32),
        compiler_params=_cparams(1),
        name="moba_gate",
    )(qm, kmean)


def _moba_kernel(q_ref, k_ref, vT_ref, sel_ref, bias_ref, o_ref):
    i = pl.program_id(2)
    L = MOBA_BLOCK
    q = q_ref[0]
    lane = lax.broadcasted_iota(jnp.int32, q.shape, 1)
    zero = jnp.zeros_like(q)
    q_heads = [jnp.where(lane < HEAD_DIM, q, zero), jnp.where(lane >= HEAD_DIM, q, zero)]

    kd = k_ref[0, pl.ds(pl.multiple_of(i * L, L), L), :]
    state = []
    for a in range(HEADS_PER_LANE_BLOCK):
        sT = _dot_nt(kd, q_heads[a]) + bias_ref[a, 0]
        m = jnp.max(sT, axis=0, keepdims=True)
        p = jnp.exp(sT - m)
        l = jnp.sum(p, axis=0, keepdims=True)
        vT = vT_ref[0, 0, i, a * HEAD_DIM:(a + 1) * HEAD_DIM, :]
        acc = jnp.dot(vT, p.astype(BF16), preferred_element_type=F32)
        state += [m, l, acc]

    def past_block(j, carry):
        kj = k_ref[0, pl.ds(pl.multiple_of(j * L, L), L), :]
        out = []
        for a in range(HEADS_PER_LANE_BLOCK):
            m, l, acc = carry[3 * a:3 * a + 3]
            sT = _dot_nt(kj, q_heads[a]) + bias_ref[a, i - j]
            chosen = sel_ref[0, a, j] > 0.5
            m_new = jnp.where(chosen, jnp.maximum(m, jnp.max(sT, axis=0, keepdims=True)), m)
            p = jnp.exp(sT - jnp.where(chosen, m_new, -NEG))
            alpha = jnp.exp(m - m_new)
            l = alpha * l + jnp.sum(p, axis=0, keepdims=True)
            vT = vT_ref[0, 0, j, a * HEAD_DIM:(a + 1) * HEAD_DIM, :]
            acc = alpha * acc + jnp.dot(vT, p.astype(BF16), preferred_element_type=F32)
            out += [m_new, l, acc]
        return tuple(out)

    state = lax.fori_loop(0, i, past_block, tuple(state))
    outT = jnp.concatenate([state[3 * a + 2] * (1.0 / state[3 * a + 1])
                            for a in range(HEADS_PER_LANE_BLOCK)], axis=0)
    o_ref[0] = outT.T.astype(BF16)


def _moba(qm, km, vmT, sel, bias):
    B, S, _ = qm.shape
    L = MOBA_BLOCK
    nb = S // L
    n_pairs = D_MOBA // LANES
    return pl.pallas_call(
        _moba_kernel,
        grid=(n_pairs, B, nb),
        in_specs=[pl.BlockSpec((1, L, LANES), lambda p, b, i: (b, i, p)),
                  pl.BlockSpec((1, S, LANES), lambda p, b, i: (b, 0, p)),
                  pl.BlockSpec((1, 1, nb, LANES, L), lambda p, b, i: (b, p, 0, 0, 0)),
                  pl.BlockSpec((1, HEADS_PER_LANE_BLOCK, nb, 1, L), lambda p, b, i: (b, p, 0, 0, i)),
                  pl.BlockSpec((HEADS_PER_LANE_BLOCK, nb, L, L), lambda p, b, i: (p, 0, 0, 0))],
        out_specs=pl.BlockSpec((1, L, LANES), lambda p, b, i: (b, i, p)),
        out_shape=jax.ShapeDtypeStruct((B, S, D_MOBA), BF16),
        compiler_params=_cparams(3),
        name="moba_attn",
    )(qm, km, vmT, sel, bias)


def _swa_kernel(sink_ref, q_ref, kp_ref, kc_ref, vpT_ref, vcT_ref, bias_ref, o_ref):
    n = pl.program_id(1)
    W = SWA_BLOCK
    kband = jnp.concatenate([kp_ref[0], kc_ref[0]], axis=0)
    vbandT = jnp.concatenate([vpT_ref[0], vcT_ref[0]], axis=1)
    key = lax.broadcasted_iota(jnp.int32, (2 * W, W), 0)
    key_ok = (key >= W) | (n > 0)
    outs = []
    for p in range(D_SWA_Q // LANES):
        qp = q_ref[0, :, p * LANES:(p + 1) * LANES]
        lane = lax.broadcasted_iota(jnp.int32, qp.shape, 1)
        zero = jnp.zeros_like(qp)
        for half in range(HEADS_PER_LANE_BLOCK):
            slot = HEADS_PER_LANE_BLOCK * p + half
            in_half = (lane >= HEAD_DIM) if half else (lane < HEAD_DIM)
            sT = _dot_nt(kband, jnp.where(in_half, qp, zero)) + bias_ref[slot]
            sT = jnp.where(key_ok, sT, NEG)
            sink = sink_ref[slot]
            m = jnp.maximum(jnp.max(sT, axis=0, keepdims=True), sink)
            pr = jnp.exp(sT - m)
            l = jnp.sum(pr, axis=0, keepdims=True) + jnp.exp(sink - m)
            vT = vbandT[half * HEAD_DIM:(half + 1) * HEAD_DIM, :]
            outs.append(jnp.dot(vT, pr.astype(BF16), preferred_element_type=F32) * (1.0 / l))
    o_ref[0] = jnp.concatenate(outs, axis=0).T.astype(BF16)


def _swa(sinks_perm, qs, ks, vsT, bias):
    B, S, _ = qs.shape
    W = SWA_BLOCK
    nq = S // W
    prev = lambda n: jnp.maximum(n - 1, 0)
    return pl.pallas_call(
        _swa_kernel,
        grid=(B, nq),
        in_specs=[pl.BlockSpec(memory_space=pltpu.SMEM),
                  pl.BlockSpec((1, W, D_SWA_Q), lambda b, n: (b, n, 0)),
                  pl.BlockSpec((1, W, D_SWA_KV), lambda b, n: (b, prev(n), 0)),
                  pl.BlockSpec((1, W, D_SWA_KV), lambda b, n: (b, n, 0)),
                  pl.BlockSpec((1, D_SWA_KV, W), lambda b, n: (b, 0, prev(n))),
                  pl.BlockSpec((1, D_SWA_KV, W), lambda b, n: (b, 0, n)),
                  pl.BlockSpec((N_HEADS_SWA, 2 * W, W), lambda b, n: (0, 0, 0))],
        out_specs=pl.BlockSpec((1, W, D_SWA_Q), lambda b, n: (b, n, 0)),
        out_shape=jax.ShapeDtypeStruct((B, S, D_SWA_Q), BF16),
        compiler_params=_cparams(2),
        name="swa_attn",
    )(sinks_perm, qs, ks, ks, vsT, vsT, bias)


def _outproj_kernel(x_ref, om_ref, os_ref, wm_ref, ws_ref, b_ref, mod_ref, g_ref,
                    x1_ref, h_ref, hbf_ref):
    attn = (jnp.dot(om_ref[...], wm_ref[...], preferred_element_type=F32)
            + jnp.dot(os_ref[...], ws_ref[...], preferred_element_type=F32) + b_ref[...])
    x1 = x_ref[...] + mod_ref[0, 2:3, :] * attn
    x1_ref[...] = x1
    h = _rmsnorm_rows(x1, g_ref[...]) * (1.0 + mod_ref[0, 4:5, :]) + mod_ref[0, 3:4, :]
    h_ref[...] = h
    hbf_ref[...] = h.astype(BF16)


def _outproj(x2d, om, os_, w_out_m, w_out_s, b_out, mod6, g_ffn, seq_len):
    N, D = x2d.shape
    tm = 512
    per_seq = seq_len // tm
    row = lambda t: (t, 0)
    const = lambda t: (0, 0)
    return pl.pallas_call(
        _outproj_kernel,
        grid=(N // tm,),
        in_specs=[pl.BlockSpec((tm, D), row),
                  pl.BlockSpec((tm, D_MOBA), row),
                  pl.BlockSpec((tm, D_SWA_Q), row),
                  pl.BlockSpec((D_MOBA, D), const),
                  pl.BlockSpec((D_SWA_Q, D), const),
                  pl.BlockSpec((1, D), const),
                  pl.BlockSpec((1, 6, D), lambda t: (t // per_seq, 0, 0)),
                  pl.BlockSpec((1, D), const)],
        out_specs=(pl.BlockSpec((tm, D), row), pl.BlockSpec((tm, D), row), pl.BlockSpec((tm, D), row)),
        out_shape=(jax.ShapeDtypeStruct((N, D), F32), jax.ShapeDtypeStruct((N, D), F32),
                   jax.ShapeDtypeStruct((N, D), BF16)),
        compiler_params=_cparams(1),
        name="outproj",
    )(x2d, om, os_, w_out_m, w_out_s, b_out, mod6, g_ffn)


def _router_kernel(h_ref, wT_ref, b_ref, comb_ref, idx_ref, w_ref):
    logits = _dot_nt(wT_ref[...], h_ref[...], precision=lax.Precision.HIGHEST) + b_ref[...]
    eidx = lax.broadcasted_iota(jnp.int32, logits.shape, 0)
    vals, idxs = [], []
    cur = logits
    for _ in range(TOP_K):
        m = jnp.max(cur, axis=0, keepdims=True)
        am = jnp.min(jnp.where(cur == m, eidx, N_EXPERTS), axis=0, keepdims=True)
        vals.append(m)
        idxs.append(am)
        cur = jnp.where(eidx == am, -jnp.inf, cur)
    exps = [jnp.exp(v - vals[0]) for v in vals]
    inv = 1.0 / functools.reduce(lambda a, b: a + b, exps)
    comb = jnp.zeros(logits.shape, F32)
    for am, e in zip(idxs, exps):
        comb = jnp.where(eidx == am, e * inv, comb)
    comb_ref[...] = comb
    idx_ref[...] = jnp.concatenate(idxs, axis=0)
    w_ref[...] = jnp.concatenate([e * inv for e in exps], axis=0)


def _router(h, w_routerT, b_router_col):
    N, D = h.shape
    tm = 512
    return pl.pallas_call(
        _router_kernel,
        grid=(N // tm,),
        in_specs=[pl.BlockSpec((tm, D), lambda t: (t, 0)),
                  pl.BlockSpec((N_EXPERTS, D), lambda t: (0, 0)),
                  pl.BlockSpec((N_EXPERTS, 1), lambda t: (0, 0))],
        out_specs=(pl.BlockSpec((N_EXPERTS, tm), lambda t: (0, t)),
                   pl.BlockSpec((TOP_K, tm), lambda t: (0, t)),
                   pl.BlockSpec((TOP_K, tm), lambda t: (0, t))),
        out_shape=(jax.ShapeDtypeStruct((N_EXPERTS, N), F32),
                   jax.ShapeDtypeStruct((TOP_K, N), jnp.int32),
                   jax.ShapeDtypeStruct((TOP_K, N), F32)),
        compiler_params=_cparams(1),
        name="router",
    )(h, w_routerT, b_router_col)


def _expert_mlp(h_bf, wgu, bgu, wdn, bdn):
    gu = jnp.dot(h_bf, wgu, preferred_element_type=F32) + bgu
    g = jnp.minimum(gu[:, :D_FF], SWIGLU_LIMIT)
    u = jnp.clip(gu[:, D_FF:], -SWIGLU_LIMIT, SWIGLU_LIMIT)
    act = (u + 1.0) * (g * _sigmoid(SWIGLU_ALPHA * g))
    return jnp.dot(act.astype(BF16), wdn, preferred_element_type=F32) + bdn


def _moe_dense_kernel(h_ref, comb_ref, wgu_ref, bgu_ref, wdn_ref, bdn_ref, x1_ref, mod_ref, g_ref,
                      o_ref, acc_ref):
    e = pl.program_id(1)

    @pl.when(e == 0)
    def _():
        acc_ref[...] = jnp.zeros_like(acc_ref)

    y = _expert_mlp(h_ref[...], wgu_ref[0], bgu_ref[0], wdn_ref[0], bdn_ref[0])
    comb = comb_ref[...]
    lane = lax.broadcasted_iota(jnp.int32, comb.shape, 1)
    col = jnp.sum(jnp.where(lane == e, comb, 0.0), axis=1, keepdims=True)
    acc_ref[...] += col * y

    @pl.when(e == N_EXPERTS - 1)
    def _():
        x2 = x1_ref[...] + mod_ref[0, 5:6, :] * acc_ref[...]
        o_ref[...] = _rmsnorm_rows(x2, g_ref[...])


def _moe_dense(h_bf, comb, w_gate_up, b_gate_up, w_down, b_down, x1, mod6, g_final, seq_len):
    N, D = h_bf.shape
    tm = 512
    per_seq = seq_len // tm
    return pl.pallas_call(
        _moe_dense_kernel,
        grid=(N // tm, N_EXPERTS),
        in_specs=[pl.BlockSpec((tm, D), lambda t, e: (t, 0)),
                  pl.BlockSpec((tm, N_EXPERTS), lambda t, e: (t, 0)),
                  pl.BlockSpec((1, D, 2 * D_FF), lambda t, e: (e, 0, 0)),
                  pl.BlockSpec((1, 1, 2 * D_FF), lambda t, e: (e, 0, 0)),
                  pl.BlockSpec((1, D_FF, D), lambda t, e: (e, 0, 0)),
                  pl.BlockSpec((1, 1, D), lambda t, e: (e, 0, 0)),
                  pl.BlockSpec((tm, D), lambda t, e: (t, 0)),
                  pl.BlockSpec((1, 6, D), lambda t, e: (t // per_seq, 0, 0)),
                  pl.BlockSpec((1, D), lambda t, e: (0, 0))],
        out_specs=pl.BlockSpec((tm, D), lambda t, e: (t, 0)),
        out_shape=jax.ShapeDtypeStruct((N, D), F32),
        scratch_shapes=[pltpu.VMEM((tm, D), F32)],
        compiler_params=_cparams(2),
        name="moe_dense",
    )(h_bf, comb, w_gate_up, b_gate_up.reshape(N_EXPERTS, 1, 2 * D_FF), w_down,
      b_down.reshape(N_EXPERTS, 1, D), x1, mod6, g_final)


def _swa_head_perm():
    group = N_HEADS_SWA // N_KV_SWA
    cols = []
    for p in range(group):
        for half in range(N_KV_SWA):
            hd = p + group * half
            cols.extend(range(hd * HEAD_DIM, (hd + 1) * HEAD_DIM))
    return jnp.asarray(cols, jnp.int32)


def kernel(x, c, g_attn, w_ada, b_ada, w_in, b_in, w_out, b_out, rel_table, sinks, g_ffn, w_router,
           b_router, w_gate_up, b_gate_up, w_down, b_down, g_final):
    B, S, D = x.shape
    assert w_ada.shape[0] == 1, "the final norm is fused into the last layer; one layer supported"
    l = 0
    nb = S // MOBA_BLOCK
    perm = _swa_head_perm()
    i2 = 3 * D_MOBA
    i3 = i2 + D_SWA_Q
    col_scale = jnp.concatenate([jnp.full((D_MOBA,), ATTN_SCALE, F32), jnp.ones((2 * D_MOBA,), F32),
                                 jnp.full((D_SWA_Q,), ATTN_SCALE, F32), jnp.ones((2 * D_SWA_KV,), F32)])
    col_order = jnp.concatenate([jnp.arange(i2, dtype=jnp.int32), i2 + perm,
                                 jnp.arange(i3, D_IN, dtype=jnp.int32)])
    moba_bias = _moba_bias(rel_table, nb)
    swa_bias = _swa_bias(rel_table)
    swa_slot_head = perm[::HEAD_DIM] // HEAD_DIM

    mod6 = _ada(c, w_ada[l], b_ada[l]).reshape(B, 6, D)
    w_in_l = ((w_in[l] * col_scale)[:, col_order]).astype(BF16)
    b_in_l = ((b_in[l] * col_scale)[col_order]).reshape(1, D_IN)
    qm, km, vmT, kmean, qs, ks, vsT = _inproj(x, mod6, g_attn[l].reshape(1, D), w_in_l, b_in_l)
    sel = _gate(qm, kmean)
    o_m = _moba(qm, km, vmT, sel, moba_bias)
    o_s = _swa(sinks[l][swa_slot_head], qs, ks, vsT, swa_bias)
    w_out_m = w_out[l, :D_MOBA].astype(BF16)
    w_out_s = w_out[l, D_MOBA:][perm].astype(BF16)
    x1, h, h_bf = _outproj(x.reshape(B * S, D), o_m.reshape(B * S, D_MOBA),
                           o_s.reshape(B * S, D_SWA_Q), w_out_m, w_out_s, b_out[l].reshape(1, D),
                           mod6, g_ffn[l].reshape(1, D), S)
    combT, top_idx, top_w = _router(h, w_router[l].T, b_router[l].reshape(N_EXPERTS, 1))
    y = _moe_dense(h_bf, combT.T, w_gate_up[l].astype(BF16), b_gate_up[l], w_down[l].astype(BF16),
                   b_down[l], x1, mod6, g_final.reshape(1, D), S)
    return y.reshape(B, S, D)
```

```python
import functools
import math

import jax
import jax.numpy as jnp
from jax import lax
from jax.experimental import pallas as pl
from jax.experimental.pallas import tpu as pltpu

F32 = jnp.float32
BF16 = jnp.bfloat16

D_MODEL = 1024
HEAD_DIM = 64
N_HEADS_MOBA = 8
N_HEADS_SWA = 8
N_KV_SWA = 2
D_MOBA = N_HEADS_MOBA * HEAD_DIM
D_SWA_Q = N_HEADS_SWA * HEAD_DIM
D_SWA_KV = N_KV_SWA * HEAD_DIM
D_IN = 3 * D_MOBA + D_SWA_Q + 2 * D_SWA_KV
MOBA_BLOCK = 256
MOBA_TOPK = 3
SWA_WINDOW = 128
SWA_BLOCK = 128
REL_BUCKETS = 32
REL_MAX_DIST = 1024
REL_MAX_EXACT = REL_BUCKETS // 2
N_EXPERTS = 32
TOP_K = 4
D_FF = 1024
SWIGLU_LIMIT = 7.0
SWIGLU_ALPHA = 1.702
EPS = 1e-5
NEG = -1e30
ATTN_SCALE = HEAD_DIM ** -0.5

MOE_TILE = 256
LANES = 128
HEADS_PER_LANE_BLOCK = LANES // HEAD_DIM
VMEM_LIMIT_BYTES = 56 * 1024 * 1024

_REL_THRESHOLDS = tuple(
    math.ceil(REL_MAX_EXACT * (REL_MAX_DIST / REL_MAX_EXACT) ** (k / (REL_BUCKETS - REL_MAX_EXACT)) - 1e-9)
    for k in range(1, REL_BUCKETS - REL_MAX_EXACT))


def _cparams(n_axes):
    return pltpu.CompilerParams(dimension_semantics=("arbitrary",) * n_axes,
                                vmem_limit_bytes=VMEM_LIMIT_BYTES)


def _sigmoid(z):
    return 1.0 / (1.0 + jnp.exp(-z))


def _rmsnorm_rows(xf, g):
    ms = jnp.mean(xf * xf, axis=-1, keepdims=True)
    return xf * lax.rsqrt(ms + EPS) * g


def _dot_nt(a, b, **kw):
    return lax.dot_general(a, b, (((1,), (1,)), ((), ())), preferred_element_type=F32, **kw)


def _ada_kernel(c_ref, w_ref, b_ref, o_ref):
    c = c_ref[...]
    sc = c * _sigmoid(c)
    o_ref[...] = jnp.dot(sc, w_ref[...], preferred_element_type=F32,
                         precision=lax.Precision.HIGHEST) + b_ref[...]


def _ada(c, w_ada, b_ada):
    B = c.shape[0]
    n_out = w_ada.shape[1]
    bn = 1536
    return pl.pallas_call(
        _ada_kernel,
        grid=(n_out // bn,),
        in_specs=[pl.BlockSpec((B, D_MODEL), lambda n: (0, 0)),
                  pl.BlockSpec((D_MODEL, bn), lambda n: (0, n)),
                  pl.BlockSpec((1, bn), lambda n: (0, n))],
        out_specs=pl.BlockSpec((B, bn), lambda n: (0, n)),
        out_shape=jax.ShapeDtypeStruct((B, n_out), F32),
        compiler_params=_cparams(1),
        name="ada",
    )(c, w_ada, b_ada.reshape(1, n_out))


def _inproj_kernel(x_ref, mod_ref, g_ref, w_ref, b_ref,
                   qm_ref, km_ref, vmT_ref, kmean_ref, qs_ref, ks_ref, vsT_ref):
    xf = x_ref[0]
    shift = mod_ref[0, 0:1, :]
    scale = mod_ref[0, 1:2, :]
    h = _rmsnorm_rows(xf, g_ref[...]) * (1.0 + scale) + shift
    proj = jnp.dot(h.astype(BF16), w_ref[...], preferred_element_type=F32) + b_ref[...]
    i0, i1, i2 = D_MOBA, 2 * D_MOBA, 3 * D_MOBA
    i3 = i2 + D_SWA_Q
    i4 = i3 + D_SWA_KV
    qm_ref[0] = proj[:, :i0].astype(BF16)
    km = proj[:, i0:i1]
    km_ref[0] = km.astype(BF16)
    kmean_ref[0, 0] = jnp.broadcast_to(jnp.mean(km, axis=0, keepdims=True), (8, D_MOBA))
    vmT = proj[:, i1:i2].T.astype(BF16)
    vmT_ref[0, :, 0] = vmT.reshape(D_MOBA // LANES, LANES, vmT.shape[1])
    qs_ref[0] = proj[:, i2:i3].astype(BF16)
    ks_ref[0] = proj[:, i3:i4].astype(BF16)
    vsT_ref[0] = proj[:, i4:].T.astype(BF16)


def _inproj(x, mod6, g_attn, w_in_bf, b_in):
    B, S, D = x.shape
    tm = MOBA_BLOCK
    nb = S // tm
    n_pairs = D_MOBA // LANES
    out_shape = (
        jax.ShapeDtypeStruct((B, S, D_MOBA), BF16),
        jax.ShapeDtypeStruct((B, S, D_MOBA), BF16),
        jax.ShapeDtypeStruct((B, n_pairs, nb, LANES, tm), BF16),
        jax.ShapeDtypeStruct((B, nb, 8, D_MOBA), F32),
        jax.ShapeDtypeStruct((B, S, D_SWA_Q), BF16),
        jax.ShapeDtypeStruct((B, S, D_SWA_KV), BF16),
        jax.ShapeDtypeStruct((B, D_SWA_KV, S), BF16),
    )
    return pl.pallas_call(
        _inproj_kernel,
        grid=(B, nb),
        in_specs=[pl.BlockSpec((1, tm, D), lambda b, i: (b, i, 0)),
                  pl.BlockSpec((1, 6, D), lambda b, i: (b, 0, 0)),
                  pl.BlockSpec((1, D), lambda b, i: (0, 0)),
                  pl.BlockSpec((D, D_IN), lambda b, i: (0, 0)),
                  pl.BlockSpec((1, D_IN), lambda b, i: (0, 0))],
        out_specs=(pl.BlockSpec((1, tm, D_MOBA), lambda b, i: (b, i, 0)),
                   pl.BlockSpec((1, tm, D_MOBA), lambda b, i: (b, i, 0)),
                   pl.BlockSpec((1, n_pairs, 1, LANES, tm), lambda b, i: (b, 0, i, 0, 0)),
                   pl.BlockSpec((1, 1, 8, D_MOBA), lambda b, i: (b, i, 0, 0)),
                   pl.BlockSpec((1, tm, D_SWA_Q), lambda b, i: (b, i, 0)),
                   pl.BlockSpec((1, tm, D_SWA_KV), lambda b, i: (b, i, 0)),
                   pl.BlockSpec((1, D_SWA_KV, tm), lambda b, i: (b, 0, i))),
        out_shape=out_shape,
        compiler_params=_cparams(2),
        name="inproj",
    )(x, mod6, g_attn, w_in_bf, b_in)


def _rel_bias_values(dist, table_ref, h):
    n = jnp.maximum(dist, 0)
    large = jnp.full(n.shape, REL_MAX_EXACT, jnp.int32)
    for t in _REL_THRESHOLDS:
        large = large + (n >= t).astype(jnp.int32)
    bucket = jnp.where(n < REL_MAX_EXACT, n, large)
    val = jnp.full(n.shape, table_ref[0, h], F32)
    for bkt in range(1, REL_BUCKETS):
        val = jnp.where(bucket == bkt, table_ref[bkt, h], val)
    return val


def _moba_bias_kernel(table_ref, o_ref):
    h = pl.program_id(0)
    dlt = pl.program_id(1)
    L = MOBA_BLOCK
    key = lax.broadcasted_iota(jnp.int32, (L, L), 0)
    qry = lax.broadcasted_iota(jnp.int32, (L, L), 1)
    dist = dlt * L + qry - key
    val = _rel_bias_values(dist, table_ref, h)
    o_ref[0, 0] = jnp.where(dist >= 0, val, NEG)


def _moba_bias(rel_table, nb):
    L = MOBA_BLOCK
    return pl.pallas_call(
        _moba_bias_kernel,
        grid=(N_HEADS_MOBA, nb),
        in_specs=[pl.BlockSpec(memory_space=pltpu.SMEM)],
        out_specs=pl.BlockSpec((1, 1, L, L), lambda h, d: (h, d, 0, 0)),
        out_shape=jax.ShapeDtypeStruct((N_HEADS_MOBA, nb, L, L), F32),
        compiler_params=_cparams(2),
        name="moba_bias",
    )(rel_table)


def _swa_bias_kernel(table_ref, o_ref):
    slot = pl.program_id(0)
    h = N_HEADS_MOBA + (slot // 2) + (N_HEADS_SWA // N_KV_SWA) * (slot % 2)
    W = SWA_BLOCK
    key = lax.broadcasted_iota(jnp.int32, (2 * W, W), 0)
    qry = lax.broadcasted_iota(jnp.int32, (2 * W, W), 1)
    dist = qry + W - key
    val = _rel_bias_values(dist, table_ref, h)
    o_ref[0] = jnp.where((dist >= 0) & (dist < SWA_WINDOW), val, NEG)


def _swa_bias(rel_table):
    W = SWA_BLOCK
    return pl.pallas_call(
        _swa_bias_kernel,
        grid=(N_HEADS_SWA,),
        in_specs=[pl.BlockSpec(memory_space=pltpu.SMEM)],
        out_specs=pl.BlockSpec((1, 2 * W, W), lambda s: (s, 0, 0)),
        out_shape=jax.ShapeDtypeStruct((N_HEADS_SWA, 2 * W, W), F32),
        compiler_params=_cparams(1),
        name="swa_bias",
    )(rel_table)


def _gate_kernel(q_ref, kmean_ref, sel_ref):
    S = q_ref.shape[1]
    nb = kmean_ref.shape[1]
    H = N_HEADS_MOBA
    q = q_ref[0]
    km = kmean_ref[0, :, 0, :]
    km_t = jnp.concatenate([km] * H, axis=0)
    row_h = lax.broadcasted_iota(jnp.int32, (H * nb, D_MOBA), 0) // nb
    col_h = lax.broadcasted_iota(jnp.int32, (H * nb, D_MOBA), 1) // HEAD_DIM
    km_t = jnp.where(row_h == col_h, km_t, 0.0)
    km_hi = km_t.astype(BF16)
    km_lo = (km_t - km_hi.astype(F32)).astype(BF16)
    g_all = _dot_nt(km_hi, q) + _dot_nt(km_lo, q)
    qblk = lax.broadcasted_iota(jnp.int32, (nb, S), 1) // MOBA_BLOCK
    jidx = lax.broadcasted_iota(jnp.int32, (nb, S), 0)
    past = jidx < qblk
    for h in range(H):
        g = g_all[h * nb:(h + 1) * nb, :]
        cnt = jnp.zeros((nb, S), jnp.int32)
        for jp in range(nb):
            gj = g[jp:jp + 1, :]
            ahead = (gj > g) | ((gj == g) & (jp < jidx))
            cnt = cnt + ((jp < qblk) & ahead).astype(jnp.int32)
        sel = (past & (cnt < MOBA_TOPK)).astype(F32)
        for j in range(nb):
            sel_ref[0, h, j] = sel[j:j + 1, :]


def _gate(qm, kmean):
    B, S, _ = qm.shape
    nb = kmean.shape[1]
    return pl.pallas_call(
        _gate_kernel,
        grid=(B,),
        in_specs=[pl.BlockSpec((1, S, D_MOBA), lambda b: (b, 0, 0)),
                  pl.BlockSpec((1, nb, 8, D_MOBA), lambda b: (b, 0, 0, 0))],
        out_specs=pl.BlockSpec((1, N_HEADS_MOBA, nb, 1, S), lambda b: (b, 0, 0, 0, 0)),
        out_shape=jax.ShapeDtypeStruct((B, N_HEADS_MOBA, nb, 1, S), F32),
        compiler_params=_cparams(1),
        name="moba_gate",
    )(qm, kmean)


def _moba_kernel(q_ref, k_ref, vT_ref, sel_ref, bias_ref, o_ref):
    i = pl.program_id(2)
    L = MOBA_BLOCK
    q = q_ref[0]
    lane = lax.broadcasted_iota(jnp.int32, q.shape, 1)
    zero = jnp.zeros_like(q)
    q_heads = [jnp.where(lane < HEAD_DIM, q, zero), jnp.where(lane >= HEAD_DIM, q, zero)]

    kd = k_ref[0, pl.ds(pl.multiple_of(i * L, L), L), :]
    state = []
    for a in range(HEADS_PER_LANE_BLOCK):
        sT = _dot_nt(kd, q_heads[a]) + bias_ref[a, 0]
        m = jnp.max(sT, axis=0, keepdims=True)
        p = jnp.exp(sT - m)
        l = jnp.sum(p, axis=0, keepdims=True)
        vT = vT_ref[0, 0, i, a * HEAD_DIM:(a + 1) * HEAD_DIM, :]
        acc = jnp.dot(vT, p.astype(BF16), preferred_element_type=F32)
        state += [m, l, acc]

    def past_block(j, carry):
        kj = k_ref[0, pl.ds(pl.multiple_of(j * L, L), L), :]
        out = []
        for a in range(HEADS_PER_LANE_BLOCK):
            m, l, acc = carry[3 * a:3 * a + 3]
            sT = _dot_nt(kj, q_heads[a]) + bias_ref[a, i - j]
            chosen = sel_ref[0, a, j] > 0.5
            m_new = jnp.where(chosen, jnp.maximum(m, jnp.max(sT, axis=0, keepdims=True)), m)
            p = jnp.exp(sT - jnp.where(chosen, m_new, -NEG))
            alpha = jnp.exp(m - m_new)
            l = alpha * l + jnp.sum(p, axis=0, keepdims=True)
            vT = vT_ref[0, 0, j, a * HEAD_DIM:(a + 1) * HEAD_DIM, :]
            acc = alpha * acc + jnp.dot(vT, p.astype(BF16), preferred_element_type=F32)
            out += [m_new, l, acc]
        return tuple(out)

    state = lax.fori_loop(0, i, past_block, tuple(state))
    outT = jnp.concatenate([state[3 * a + 2] * (1.0 / state[3 * a + 1])
                            for a in range(HEADS_PER_LANE_BLOCK)], axis=0)
    o_ref[0] = outT.T.astype(BF16)


def _moba(qm, km, vmT, sel, bias):
    B, S, _ = qm.shape
    L = MOBA_BLOCK
    nb = S // L
    n_pairs = D_MOBA // LANES
    return pl.pallas_call(
        _moba_kernel,
        grid=(n_pairs, B, nb),
        in_specs=[pl.BlockSpec((1, L, LANES), lambda p, b, i: (b, i, p)),
                  pl.BlockSpec((1, S, LANES), lambda p, b, i: (b, 0, p)),
                  pl.BlockSpec((1, 1, nb, LANES, L), lambda p, b, i: (b, p, 0, 0, 0)),
                  pl.BlockSpec((1, HEADS_PER_LANE_BLOCK, nb, 1, L), lambda p, b, i: (b, p, 0, 0, i)),
                  pl.BlockSpec((HEADS_PER_LANE_BLOCK, nb, L, L), lambda p, b, i: (p, 0, 0, 0))],
        out_specs=pl.BlockSpec((1, L, LANES), lambda p, b, i: (b, i, p)),
        out_shape=jax.ShapeDtypeStruct((B, S, D_MOBA), BF16),
        compiler_params=_cparams(3),
        name="moba_attn",
    )(qm, km, vmT, sel, bias)


def _swa_kernel(sink_ref, q_ref, kp_ref, kc_ref, vpT_ref, vcT_ref, bias_ref, o_ref):
    n = pl.program_id(1)
    W = SWA_BLOCK
    kband = jnp.concatenate([kp_ref[0], kc_ref[0]], axis=0)
    vbandT = jnp.concatenate([vpT_ref[0], vcT_ref[0]], axis=1)
    key = lax.broadcasted_iota(jnp.int32, (2 * W, W), 0)
    key_ok = (key >= W) | (n > 0)
    outs = []
    for p in range(D_SWA_Q // LANES):
        qp = q_ref[0, :, p * LANES:(p + 1) * LANES]
        lane = lax.broadcasted_iota(jnp.int32, qp.shape, 1)
        zero = jnp.zeros_like(qp)
        for half in range(HEADS_PER_LANE_BLOCK):
            slot = HEADS_PER_LANE_BLOCK * p + half
            in_half = (lane >= HEAD_DIM) if half else (lane < HEAD_DIM)
            sT = _dot_nt(kband, jnp.where(in_half, qp, zero)) + bias_ref[slot]
            sT = jnp.where(key_ok, sT, NEG)
            sink = sink_ref[slot]
            m = jnp.maximum(jnp.max(sT, axis=0, keepdims=True), sink)
            pr = jnp.exp(sT - m)
            l = jnp.sum(pr, axis=0, keepdims=True) + jnp.exp(sink - m)
            vT = vbandT[half * HEAD_DIM:(half + 1) * HEAD_DIM, :]
            outs.append(jnp.dot(vT, pr.astype(BF16), preferred_element_type=F32) * (1.0 / l))
    o_ref[0] = jnp.concatenate(outs, axis=0).T.astype(BF16)


def _swa(sinks_perm, qs, ks, vsT, bias):
    B, S, _ = qs.shape
    W = SWA_BLOCK
    nq = S // W
    prev = lambda n: jnp.maximum(n - 1, 0)
    return pl.pallas_call(
        _swa_kernel,
        grid=(B, nq),
        in_specs=[pl.BlockSpec(memory_space=pltpu.SMEM),
                  pl.BlockSpec((1, W, D_SWA_Q), lambda b, n: (b, n, 0)),
                  pl.BlockSpec((1, W, D_SWA_KV), lambda b, n: (b, prev(n), 0)),
                  pl.BlockSpec((1, W, D_SWA_KV), lambda b, n: (b, n, 0)),
                  pl.BlockSpec((1, D_SWA_KV, W), lambda b, n: (b, 0, prev(n))),
                  pl.BlockSpec((1, D_SWA_KV, W), lambda b, n: (b, 0, n)),
                  pl.BlockSpec((N_HEADS_SWA, 2 * W, W), lambda b, n: (0, 0, 0))],
        out_specs=pl.BlockSpec((1, W, D_SWA_Q), lambda b, n: (b, n, 0)),
        out_shape=jax.ShapeDtypeStruct((B, S, D_SWA_Q), BF16),
        compiler_params=_cparams(2),
        name="swa_attn",
    )(sinks_perm, qs, ks, ks, vsT, vsT, bias)


def _outproj_kernel(x_ref, om_ref, os_ref, wm_ref, ws_ref, b_ref, mod_ref, g_ref,
                    x1_ref, h_ref, hbf_ref):
    attn = (jnp.dot(om_ref[...], wm_ref[...], preferred_element_type=F32)
            + jnp.dot(os_ref[...], ws_ref[...], preferred_element_type=F32) + b_ref[...])
    x1 = x_ref[...] + mod_ref[0, 2:3, :] * attn
    x1_ref[...] = x1
    h = _rmsnorm_rows(x1, g_ref[...]) * (1.0 + mod_ref[0, 4:5, :]) + mod_ref[0, 3:4, :]
    h_ref[...] = h
    hbf_ref[...] = h.astype(BF16)


def _outproj(x2d, om, os_, w_out_m, w_out_s, b_out, mod6, g_ffn, seq_len):
    N, D = x2d.shape
    tm = 512
    per_seq = seq_len // tm
    row = lambda t: (t, 0)
    const = lambda t: (0, 0)
    return pl.pallas_call(
        _outproj_kernel,
        grid=(N // tm,),
        in_specs=[pl.BlockSpec((tm, D), row),
                  pl.BlockSpec((tm, D_MOBA), row),
                  pl.BlockSpec((tm, D_SWA_Q), row),
                  pl.BlockSpec((D_MOBA, D), const),
                  pl.BlockSpec((D_SWA_Q, D), const),
                  pl.BlockSpec((1, D), const),
                  pl.BlockSpec((1, 6, D), lambda t: (t // per_seq, 0, 0)),
                  pl.BlockSpec((1, D), const)],
        out_specs=(pl.BlockSpec((tm, D), row), pl.BlockSpec((tm, D), row), pl.BlockSpec((tm, D), row)),
        out_shape=(jax.ShapeDtypeStruct((N, D), F32), jax.ShapeDtypeStruct((N, D), F32),
                   jax.ShapeDtypeStruct((N, D), BF16)),
        compiler_params=_cparams(1),
        name="outproj",
    )(x2d, om, os_, w_out_m, w_out_s, b_out, mod6, g_ffn)


def _router_kernel(h_ref, wT_ref, b_ref, comb_ref, idx_ref, w_ref):
    logits = _dot_nt(wT_ref[...], h_ref[...], precision=lax.Precision.HIGHEST) + b_ref[...]
    eidx = lax.broadcasted_iota(jnp.int32, logits.shape, 0)
    vals, idxs = [], []
    cur = logits
    for _ in range(TOP_K):
        m = jnp.max(cur, axis=0, keepdims=True)
        am = jnp.min(jnp.where(cur == m, eidx, N_EXPERTS), axis=0, keepdims=True)
        vals.append(m)
        idxs.append(am)
        cur = jnp.where(eidx == am, -jnp.inf, cur)
    exps = [jnp.exp(v - vals[0]) for v in vals]
    inv = 1.0 / functools.reduce(lambda a, b: a + b, exps)
    comb = jnp.zeros(logits.shape, F32)
    for am, e in zip(idxs, exps):
        comb = jnp.where(eidx == am, e * inv, comb)
    comb_ref[...] = comb
    idx_ref[...] = jnp.concatenate(idxs, axis=0)
    w_ref[...] = jnp.concatenate([e * inv for e in exps], axis=0)


def _router(h, w_routerT, b_router_col):
    N, D = h.shape
    tm = 512
    return pl.pallas_call(
        _router_kernel,
        grid=(N // tm,),
        in_specs=[pl.BlockSpec((tm, D), lambda t: (t, 0)),
                  pl.BlockSpec((N_EXPERTS, D), lambda t: (0, 0)),
                  pl.BlockSpec((N_EXPERTS, 1), lambda t: (0, 0))],
        out_specs=(pl.BlockSpec((N_EXPERTS, tm), lambda t: (0, t)),
                   pl.BlockSpec((TOP_K, tm), lambda t: (0, t)),
                   pl.BlockSpec((TOP_K, tm), lambda t: (0, t))),
        out_shape=(jax.ShapeDtypeStruct((N_EXPERTS, N), F32),
                   jax.ShapeDtypeStruct((TOP_K, N), jnp.int32),
                   jax.ShapeDtypeStruct((TOP_K, N), F32)),
        compiler_params=_cparams(1),
        name="router",
    )(h, w_routerT, b_router_col)


def _expert_mlp(h_bf, wgu, bgu, wdn, bdn):
    gu = jnp.dot(h_bf, wgu, preferred_element_type=F32) + bgu
    g = jnp.minimum(gu[:, :D_FF], SWIGLU_LIMIT)
    u = jnp.clip(gu[:, D_FF:], -SWIGLU_LIMIT, SWIGLU_LIMIT)
    act = (u + 1.0) * (g * _sigmoid(SWIGLU_ALPHA * g))
    return jnp.dot(act.astype(BF16), wdn, preferred_element_type=F32) + bdn


def _route_metadata(top_idx):
    n_tok = top_idx.shape[1]
    n_slots = n_tok * TOP_K
    flat_e = top_idx.T.reshape(-1)
    order = jnp.argsort(flat_e, stable=True).astype(jnp.int32)
    counts = jnp.sum((flat_e[:, None] == jnp.arange(N_EXPERTS, dtype=jnp.int32)[None, :]).astype(jnp.int32),
                     axis=0)
    tiles_per = (counts + MOE_TILE - 1) // MOE_TILE
    tile_end = jnp.cumsum(tiles_per)
    tile_start = tile_end - tiles_per
    slot_start = jnp.cumsum(counts) - counts
    n_tiles_max = n_slots // MOE_TILE + N_EXPERTS
    t_ids = jnp.arange(n_tiles_max, dtype=jnp.int32)
    texp = jnp.minimum(jnp.searchsorted(tile_end, t_ids, side="right"), N_EXPERTS - 1).astype(jnp.int32)
    local = t_ids - tile_start[texp]
    nvalid = jnp.clip(counts[texp] - local * MOE_TILE, 0, MOE_TILE).astype(jnp.int32)
    rows = jnp.arange(MOE_TILE, dtype=jnp.int32)[None, :]
    src = (slot_start[texp] + local * MOE_TILE)[:, None] + rows
    slot = order[jnp.clip(src, 0, n_slots - 1)]
    valid = rows < nvalid[:, None]
    dst = jnp.where(valid, slot, 0).astype(jnp.int32)
    tok = jnp.where(valid, slot // TOP_K, 0).astype(jnp.int32)
    return texp, nvalid, tok, dst


def _moe_routed_kernel(texp_ref, nvalid_ref, tok_ref, tok_next_ref, dst_ref,
                       h_hbm, wgu_ref, bgu_ref, wdn_ref, bdn_ref, y_hbm,
                       xbuf, ybuf, wgu_bf, wdn_bf, gsem, ssem):
    t = pl.program_id(0)
    n_steps = pl.num_programs(0)
    slot = t % 2
    nv = nvalid_ref[t]

    def gather_row_copy(tok, row, s):
        return pltpu.make_async_copy(h_hbm.at[pl.ds(tok, 1)], xbuf.at[s, pl.ds(row, 1)], gsem.at[s])

    def scatter_row_copy(dst, row, s):
        return pltpu.make_async_copy(ybuf.at[s, pl.ds(row, 1)], y_hbm.at[pl.ds(dst, 1)], ssem.at[s])

    def start_gather(idx_ref, s):
        def body(r, carry):
            gather_row_copy(idx_ref[0, 0, r], r, s).start()
            return carry
        lax.fori_loop(0, MOE_TILE, body, 0, unroll=8)

    def wait_rows(copy, n):
        def body(r, carry):
            copy.wait()
            return carry
        lax.fori_loop(0, n, body, 0)

    @pl.when((t == 0) & (nv > 0))
    def _():
        start_gather(tok_ref, 0)

    nv_next = nvalid_ref[jnp.minimum(t + 1, n_steps - 1)]

    @pl.when((t + 1 < n_steps) & (nv_next > 0))
    def _():
        start_gather(tok_next_ref, 1 - slot)

    prev_e = texp_ref[jnp.maximum(t - 1, 0)]

    @pl.when((nv > 0) & ((t == 0) | (texp_ref[t] != prev_e)))
    def _():
        rows = 256
        def cast_gu(i, carry):
            r0 = pl.multiple_of(i * rows, rows)
            wgu_bf[pl.ds(r0, rows), :] = wgu_ref[0, pl.ds(r0, rows), :].astype(BF16)
            return carry
        lax.fori_loop(0, D_MODEL // rows, cast_gu, 0)
        def cast_dn(i, carry):
            r0 = pl.multiple_of(i * rows, rows)
            wdn_bf[pl.ds(r0, rows), :] = wdn_ref[0, pl.ds(r0, rows), :].astype(BF16)
            return carry
        lax.fori_loop(0, D_FF // rows, cast_dn, 0)

    @pl.when(nv > 0)
    def _():
        wait_rows(gather_row_copy(0, 0, slot), MOE_TILE)
        x = xbuf[slot].astype(BF16)
        ybuf[slot] = _expert_mlp(x, wgu_bf[...], bgu_ref[0], wdn_bf[...], bdn_ref[0])

    nv_prev = nvalid_ref[jnp.maximum(t - 1, 0)]

    @pl.when(t > 0)
    def _():
        wait_rows(scatter_row_copy(0, 0, 1 - slot), nv_prev)

    def start_scatter(r, carry):
        scatter_row_copy(dst_ref[0, 0, r], r, slot).start()
        return carry
    lax.fori_loop(0, nv, start_scatter, 0)

    @pl.when(t == n_steps - 1)
    def _():
        wait_rows(scatter_row_copy(0, 0, slot), nv)


def _moe_routed(texp, nvalid, tok, dst, h, w_gate_up, b_gate_up, w_down, b_down):
    N, D = h.shape
    n_tiles = tok.shape[0]
    grid_spec = pltpu.PrefetchScalarGridSpec(
        num_scalar_prefetch=2,
        grid=(n_tiles,),
        in_specs=[pl.BlockSpec((1, 1, MOE_TILE), lambda t, te, nv: (t, 0, 0), memory_space=pltpu.SMEM),
                  pl.BlockSpec((1, 1, MOE_TILE), lambda t, te, nv: (jnp.minimum(t + 1, n_tiles - 1), 0, 0),
                               memory_space=pltpu.SMEM),
                  pl.BlockSpec((1, 1, MOE_TILE), lambda t, te, nv: (t, 0, 0), memory_space=pltpu.SMEM),
                  pl.BlockSpec(memory_space=pl.ANY),
                  pl.BlockSpec((1, D, 2 * D_FF), lambda t, te, nv: (te[t], 0, 0)),
                  pl.BlockSpec((1, 1, 2 * D_FF), lambda t, te, nv: (te[t], 0, 0)),
                  pl.BlockSpec((1, D_FF, D), lambda t, te, nv: (te[t], 0, 0)),
                  pl.BlockSpec((1, 1, D), lambda t, te, nv: (te[t], 0, 0))],
        out_specs=pl.BlockSpec(memory_space=pl.ANY),
        scratch_shapes=[pltpu.VMEM((2, MOE_TILE, D), F32),
                        pltpu.VMEM((2, MOE_TILE, D), F32),
                        pltpu.VMEM((D, 2 * D_FF), BF16),
                        pltpu.VMEM((D_FF, D), BF16),
                        pltpu.SemaphoreType.DMA((2,)),
                        pltpu.SemaphoreType.DMA((2,))],
    )
    return pl.pallas_call(
        _moe_routed_kernel,
        grid_spec=grid_spec,
        out_shape=jax.ShapeDtypeStruct((N * TOP_K, D), F32),
        compiler_params=_cparams(1),
        name="moe_routed",
    )(texp, nvalid, tok[:, None, :], tok[:, None, :], dst[:, None, :], h, w_gate_up, b_gate_up.reshape(N_EXPERTS, 1, 2 * D_FF), w_down,
      b_down.reshape(N_EXPERTS, 1, D))


def _combine_kernel(y_ref, w_ref, x1_ref, mod_ref, g_ref, o_ref):
    w = w_ref[...]
    acc = w[:, 0:1] * y_ref[:, 0:D_MODEL]
    for k in range(1, TOP_K):
        acc = acc + w[:, k:k + 1] * y_ref[:, k * D_MODEL:(k + 1) * D_MODEL]
    x2 = x1_ref[...] + mod_ref[0, 5:6, :] * acc
    o_ref[...] = _rmsnorm_rows(x2, g_ref[...])


def _combine(y_slots, top_w, x1, mod6, g_final, seq_len):
    N, D = x1.shape
    tm = 256
    per_seq = seq_len // tm
    return pl.pallas_call(
        _combine_kernel,
        grid=(N // tm,),
        in_specs=[pl.BlockSpec((tm, TOP_K * D), lambda t: (t, 0)),
                  pl.BlockSpec((tm, TOP_K), lambda t: (t, 0)),
                  pl.BlockSpec((tm, D), lambda t: (t, 0)),
                  pl.BlockSpec((1, 6, D), lambda t: (t // per_seq, 0, 0)),
                  pl.BlockSpec((1, D), lambda t: (0, 0))],
        out_specs=pl.BlockSpec((tm, D), lambda t: (t, 0)),
        out_shape=jax.ShapeDtypeStruct((N, D), F32),
        compiler_params=_cparams(1),
        name="moe_combine",
    )(y_slots.reshape(N, TOP_K * D), top_w, x1, mod6, g_final)


def _swa_head_perm():
    group = N_HEADS_SWA // N_KV_SWA
    cols = []
    for p in range(group):
        for half in range(N_KV_SWA):
            hd = p + group * half
            cols.extend(range(hd * HEAD_DIM, (hd + 1) * HEAD_DIM))
    return jnp.asarray(cols, jnp.int32)


def kernel(x, c, g_attn, w_ada, b_ada, w_in, b_in, w_out, b_out, rel_table, sinks, g_ffn, w_router,
           b_router, w_gate_up, b_gate_up, w_down, b_down, g_final):
    B, S, D = x.shape
    assert w_ada.shape[0] == 1, "the final norm is fused into the last layer; one layer supported"
    l = 0
    nb = S // MOBA_BLOCK
    perm = _swa_head_perm()
    i2 = 3 * D_MOBA
    i3 = i2 + D_SWA_Q
    col_scale = jnp.concatenate([jnp.full((D_MOBA,), ATTN_SCALE, F32), jnp.ones((2 * D_MOBA,), F32),
                                 jnp.full((D_SWA_Q,), ATTN_SCALE, F32), jnp.ones((2 * D_SWA_KV,), F32)])
    col_order = jnp.concatenate([jnp.arange(i2, dtype=jnp.int32), i2 + perm,
                                 jnp.arange(i3, D_IN, dtype=jnp.int32)])
    moba_bias = _moba_bias(rel_table, nb)
    swa_bias = _swa_bias(rel_table)
    swa_slot_head = perm[::HEAD_DIM] // HEAD_DIM

    mod6 = _ada(c, w_ada[l], b_ada[l]).reshape(B, 6, D)
    w_in_l = ((w_in[l] * col_scale)[:, col_order]).astype(BF16)
    b_in_l = ((b_in[l] * col_scale)[col_order]).reshape(1, D_IN)
    qm, km, vmT, kmean, qs, ks, vsT = _inproj(x, mod6, g_attn[l].reshape(1, D), w_in_l, b_in_l)
    sel = _gate(qm, kmean)
    o_m = _moba(qm, km, vmT, sel, moba_bias)
    o_s = _swa(sinks[l][swa_slot_head], qs, ks, vsT, swa_bias)
    w_out_m = w_out[l, :D_MOBA].astype(BF16)
    w_out_s = w_out[l, D_MOBA:][perm].astype(BF16)
    x1, h, h_bf = _outproj(x.reshape(B * S, D), o_m.reshape(B * S, D_MOBA),
                           o_s.reshape(B * S, D_SWA_Q), w_out_m, w_out_s, b_out[l].reshape(1, D),
                           mod6, g_ffn[l].reshape(1, D), S)
    combT, top_idx, top_w = _router(h, w_router[l].T, b_router[l].reshape(N_EXPERTS, 1))
    texp, nvalid, tok, dst = _route_metadata(top_idx)
    y_slots = _moe_routed(texp, nvalid, tok, dst, h, w_gate_up[l], b_gate_up[l], w_down[l], b_down[l])
    y = _combine(y_slots, top_w.T, x1, mod6, g_final.reshape(1, D), S)
    return y.reshape(B, S, D)
```

```python
import functools
import math

import jax
import jax.numpy as jnp
from jax import lax
from jax.experimental import pallas as pl
from jax.experimental.pallas import tpu as pltpu

F32 = jnp.float32
BF16 = jnp.bfloat16

D_MODEL = 1024
HEAD_DIM = 64
N_HEADS_MOBA = 8
N_HEADS_SWA = 8
N_KV_SWA = 2
D_MOBA = N_HEADS_MOBA * HEAD_DIM
D_SWA_Q = N_HEADS_SWA * HEAD_DIM
D_SWA_KV = N_KV_SWA * HEAD_DIM
D_IN = 3 * D_MOBA + D_SWA_Q + 2 * D_SWA_KV
MOBA_BLOCK = 256
MOBA_TOPK = 3
SWA_WINDOW = 128
SWA_BLOCK = 128
REL_BUCKETS = 32
REL_MAX_DIST = 1024
REL_MAX_EXACT = REL_BUCKETS // 2
N_EXPERTS = 32
TOP_K = 4
D_FF = 1024
SWIGLU_LIMIT = 7.0
SWIGLU_ALPHA = 1.702
EPS = 1e-5
NEG = -1e30
ATTN_SCALE = HEAD_DIM ** -0.5

MOE_TILE = 256
ROW_CHUNKS = D_MODEL // 128
DISPATCH_TOKENS = 256
COMBINE_TOKENS = 256
LANES = 128
HEADS_PER_LANE_BLOCK = LANES // HEAD_DIM
VMEM_LIMIT_BYTES = 56 * 1024 * 1024

_REL_THRESHOLDS = tuple(
    math.ceil(REL_MAX_EXACT * (REL_MAX_DIST / REL_MAX_EXACT) ** (k / (REL_BUCKETS - REL_MAX_EXACT)) - 1e-9)
    for k in range(1, REL_BUCKETS - REL_MAX_EXACT))


def _cparams(n_axes):
    return pltpu.CompilerParams(dimension_semantics=("arbitrary",) * n_axes,
                                vmem_limit_bytes=VMEM_LIMIT_BYTES)


def _sigmoid(z):
    return 1.0 / (1.0 + jnp.exp(-z))


def _rmsnorm_rows(xf, g):
    ms = jnp.mean(xf * xf, axis=-1, keepdims=True)
    return xf * lax.rsqrt(ms + EPS) * g


def _dot_nt(a, b, **kw):
    return lax.dot_general(a, b, (((1,), (1,)), ((), ())), preferred_element_type=F32, **kw)


def _ada_kernel(c_ref, w_ref, b_ref, o_ref):
    c = c_ref[...]
    sc = c * _sigmoid(c)
    o_ref[...] = jnp.dot(sc, w_ref[...], preferred_element_type=F32,
                         precision=lax.Precision.HIGHEST) + b_ref[...]


def _ada(c, w_ada, b_ada):
    B = c.shape[0]
    n_out = w_ada.shape[1]
    bn = 1536
    return pl.pallas_call(
        _ada_kernel,
        grid=(n_out // bn,),
        in_specs=[pl.BlockSpec((B, D_MODEL), lambda n: (0, 0)),
                  pl.BlockSpec((D_MODEL, bn), lambda n: (0, n)),
                  pl.BlockSpec((1, bn), lambda n: (0, n))],
        out_specs=pl.BlockSpec((B, bn), lambda n: (0, n)),
        out_shape=jax.ShapeDtypeStruct((B, n_out), F32),
        compiler_params=_cparams(1),
        name="ada",
    )(c, w_ada, b_ada.reshape(1, n_out))


def _inproj_kernel(x_ref, mod_ref, g_ref, w_ref, b_ref,
                   qm_ref, km_ref, vmT_ref, kmean_ref, qs_ref, ks_ref, vsT_ref):
    xf = x_ref[0]
    shift = mod_ref[0, 0:1, :]
    scale = mod_ref[0, 1:2, :]
    h = _rmsnorm_rows(xf, g_ref[...]) * (1.0 + scale) + shift
    proj = jnp.dot(h.astype(BF16), w_ref[...], preferred_element_type=F32) + b_ref[...]
    i0, i1, i2 = D_MOBA, 2 * D_MOBA, 3 * D_MOBA
    i3 = i2 + D_SWA_Q
    i4 = i3 + D_SWA_KV
    qm_ref[0] = proj[:, :i0].astype(BF16)
    km = proj[:, i0:i1]
    km_ref[0] = km.astype(BF16)
    kmean_ref[0, 0] = jnp.broadcast_to(jnp.mean(km, axis=0, keepdims=True), (8, D_MOBA))
    vmT = proj[:, i1:i2].T.astype(BF16)
    vmT_ref[0, :, 0] = vmT.reshape(D_MOBA // LANES, LANES, vmT.shape[1])
    qs_ref[0] = proj[:, i2:i3].astype(BF16)
    ks_ref[0] = proj[:, i3:i4].astype(BF16)
    vsT_ref[0] = proj[:, i4:].T.astype(BF16)


def _inproj(x, mod6, g_attn, w_in_bf, b_in):
    B, S, D = x.shape
    tm = MOBA_BLOCK
    nb = S // tm
    n_pairs = D_MOBA // LANES
    out_shape = (
        jax.ShapeDtypeStruct((B, S, D_MOBA), BF16),
        jax.ShapeDtypeStruct((B, S, D_MOBA), BF16),
        jax.ShapeDtypeStruct((B, n_pairs, nb, LANES, tm), BF16),
        jax.ShapeDtypeStruct((B, nb, 8, D_MOBA), F32),
        jax.ShapeDtypeStruct((B, S, D_SWA_Q), BF16),
        jax.ShapeDtypeStruct((B, S, D_SWA_KV), BF16),
        jax.ShapeDtypeStruct((B, D_SWA_KV, S), BF16),
    )
    return pl.pallas_call(
        _inproj_kernel,
        grid=(B, nb),
        in_specs=[pl.BlockSpec((1, tm, D), lambda b, i: (b, i, 0)),
                  pl.BlockSpec((1, 6, D), lambda b, i: (b, 0, 0)),
                  pl.BlockSpec((1, D), lambda b, i: (0, 0)),
                  pl.BlockSpec((D, D_IN), lambda b, i: (0, 0)),
                  pl.BlockSpec((1, D_IN), lambda b, i: (0, 0))],
        out_specs=(pl.BlockSpec((1, tm, D_MOBA), lambda b, i: (b, i, 0)),
                   pl.BlockSpec((1, tm, D_MOBA), lambda b, i: (b, i, 0)),
                   pl.BlockSpec((1, n_pairs, 1, LANES, tm), lambda b, i: (b, 0, i, 0, 0)),
                   pl.BlockSpec((1, 1, 8, D_MOBA), lambda b, i: (b, i, 0, 0)),
                   pl.BlockSpec((1, tm, D_SWA_Q), lambda b, i: (b, i, 0)),
                   pl.BlockSpec((1, tm, D_SWA_KV), lambda b, i: (b, i, 0)),
                   pl.BlockSpec((1, D_SWA_KV, tm), lambda b, i: (b, 0, i))),
        out_shape=out_shape,
        compiler_params=_cparams(2),
        name="inproj",
    )(x, mod6, g_attn, w_in_bf, b_in)


def _rel_bias_values(dist, table_ref, h):
    n = jnp.maximum(dist, 0)
    large = jnp.full(n.shape, REL_MAX_EXACT, jnp.int32)
    for t in _REL_THRESHOLDS:
        large = large + (n >= t).astype(jnp.int32)
    bucket = jnp.where(n < REL_MAX_EXACT, n, large)
    val = jnp.full(n.shape, table_ref[0, h], F32)
    for bkt in range(1, REL_BUCKETS):
        val = jnp.where(bucket == bkt, table_ref[bkt, h], val)
    return val


def _moba_bias_kernel(table_ref, o_ref):
    h = pl.program_id(0)
    dlt = pl.program_id(1)
    L = MOBA_BLOCK
    key = lax.broadcasted_iota(jnp.int32, (L, L), 0)
    qry = lax.broadcasted_iota(jnp.int32, (L, L), 1)
    dist = dlt * L + qry - key
    val = _rel_bias_values(dist, table_ref, h)
    o_ref[0, 0] = jnp.where(dist >= 0, val, NEG)


def _moba_bias(rel_table, nb):
    L = MOBA_BLOCK
    return pl.pallas_call(
        _moba_bias_kernel,
        grid=(N_HEADS_MOBA, nb),
        in_specs=[pl.BlockSpec(memory_space=pltpu.SMEM)],
        out_specs=pl.BlockSpec((1, 1, L, L), lambda h, d: (h, d, 0, 0)),
        out_shape=jax.ShapeDtypeStruct((N_HEADS_MOBA, nb, L, L), F32),
        compiler_params=_cparams(2),
        name="moba_bias",
    )(rel_table)


def _swa_bias_kernel(table_ref, o_ref):
    slot = pl.program_id(0)
    h = N_HEADS_MOBA + (slot // 2) + (N_HEADS_SWA // N_KV_SWA) * (slot % 2)
    W = SWA_BLOCK
    key = lax.broadcasted_iota(jnp.int32, (2 * W, W), 0)
    qry = lax.broadcasted_iota(jnp.int32, (2 * W, W), 1)
    dist = qry + W - key
    val = _rel_bias_values(dist, table_ref, h)
    o_ref[0] = jnp.where((dist >= 0) & (dist < SWA_WINDOW), val, NEG)


def _swa_bias(rel_table):
    W = SWA_BLOCK
    return pl.pallas_call(
        _swa_bias_kernel,
        grid=(N_HEADS_SWA,),
        in_specs=[pl.BlockSpec(memory_space=pltpu.SMEM)],
        out_specs=pl.BlockSpec((1, 2 * W, W), lambda s: (s, 0, 0)),
        out_shape=jax.ShapeDtypeStruct((N_HEADS_SWA, 2 * W, W), F32),
        compiler_params=_cparams(1),
        name="swa_bias",
    )(rel_table)


def _gate_kernel(q_ref, kmean_ref, sel_ref):
    S = q_ref.shape[1]
    nb = kmean_ref.shape[1]
    H = N_HEADS_MOBA
    q = q_ref[0]
    km = kmean_ref[0, :, 0, :]
    km_t = jnp.concatenate([km] * H, axis=0)
    row_h = lax.broadcasted_iota(jnp.int32, (H * nb, D_MOBA), 0) // nb
    col_h = lax.broadcasted_iota(jnp.int32, (H * nb, D_MOBA), 1) // HEAD_DIM
    km_t = jnp.where(row_h == col_h, km_t, 0.0)
    km_hi = km_t.astype(BF16)
    km_lo = (km_t - km_hi.astype(F32)).astype(BF16)
    g_all = _dot_nt(km_hi, q) + _dot_nt(km_lo, q)
    qblk = lax.broadcasted_iota(jnp.int32, (nb, S), 1) // MOBA_BLOCK
    jidx = lax.broadcasted_iota(jnp.int32, (nb, S), 0)
    past = jidx < qblk
    for h in range(H):
        g = g_all[h * nb:(h + 1) * nb, :]
        cnt = jnp.zeros((nb, S), jnp.int32)
        for jp in range(nb):
            gj = g[jp:jp + 1, :]
            ahead = (gj > g) | ((gj == g) & (jp < jidx))
            cnt = cnt + ((jp < qblk) & ahead).astype(jnp.int32)
        sel = (past & (cnt < MOBA_TOPK)).astype(F32)
        for j in range(nb):
            sel_ref[0, h, j] = sel[j:j + 1, :]


def _gate(qm, kmean):
    B, S, _ = qm.shape
    nb = kmean.shape[1]
    return pl.pallas_call(
        _gate_kernel,
        grid=(B,),
        in_specs=[pl.BlockSpec((1, S, D_MOBA), lambda b: (b, 0, 0)),
                  pl.BlockSpec((1, nb, 8, D_MOBA), lambda b: (b, 0, 0, 0))],
        out_specs=pl.BlockSpec((1, N_HEADS_MOBA, nb, 1, S), lambda b: (b, 0, 0, 0, 0)),
        out_shape=jax.ShapeDtypeStruct((B, N_HEADS_MOBA, nb, 1, S), F32),
        compiler_params=_cparams(1),
        name="moba_gate",
    )(qm, kmean)


def _moba_kernel(q_ref, k_ref, vT_ref, sel_ref, bias_ref, o_ref):
    i = pl.program_id(2)
    L = MOBA_BLOCK
    q = q_ref[0]
    lane = lax.broadcasted_iota(jnp.int32, q.shape, 1)
    zero = jnp.zeros_like(q)
    q_heads = [jnp.where(lane < HEAD_DIM, q, zero), jnp.where(lane >= HEAD_DIM, q, zero)]

    kd = k_ref[0, pl.ds(pl.multiple_of(i * L, L), L), :]
    state = []
    for a in range(HEADS_PER_LANE_BLOCK):
        sT = _dot_nt(kd, q_heads[a]) + bias_ref[a, 0]
        m = jnp.max(sT, axis=0, keepdims=True)
        p = jnp.exp(sT - m)
        l = jnp.sum(p, axis=0, keepdims=True)
        vT = vT_ref[0, 0, i, a * HEAD_DIM:(a + 1) * HEAD_DIM, :]
        acc = jnp.dot(vT, p.astype(BF16), preferred_element_type=F32)
        state += [m, l, acc]

    def past_block(j, carry):
        kj = k_ref[0, pl.ds(pl.multiple_of(j * L, L), L), :]
        out = []
        for a in range(HEADS_PER_LANE_BLOCK):
            m, l, acc = carry[3 * a:3 * a + 3]
            sT = _dot_nt(kj, q_heads[a]) + bias_ref[a, i - j]
            chosen = sel_ref[0, a, j] > 0.5
            m_new = jnp.where(chosen, jnp.maximum(m, jnp.max(sT, axis=0, keepdims=True)), m)
            p = jnp.exp(sT - jnp.where(chosen, m_new, -NEG))
            alpha = jnp.exp(m - m_new)
            l = alpha * l + jnp.sum(p, axis=0, keepdims=True)
            vT = vT_ref[0, 0, j, a * HEAD_DIM:(a + 1) * HEAD_DIM, :]
            acc = alpha * acc + jnp.dot(vT, p.astype(BF16), preferred_element_type=F32)
            out += [m_new, l, acc]
        return tuple(out)

    state = lax.fori_loop(0, i, past_block, tuple(state))
    outT = jnp.concatenate([state[3 * a + 2] * (1.0 / state[3 * a + 1])
                            for a in range(HEADS_PER_LANE_BLOCK)], axis=0)
    o_ref[0] = outT.T.astype(BF16)


def _moba(qm, km, vmT, sel, bias):
    B, S, _ = qm.shape
    L = MOBA_BLOCK
    nb = S // L
    n_pairs = D_MOBA // LANES
    return pl.pallas_call(
        _moba_kernel,
        grid=(n_pairs, B, nb),
        in_specs=[pl.BlockSpec((1, L, LANES), lambda p, b, i: (b, i, p)),
                  pl.BlockSpec((1, S, LANES), lambda p, b, i: (b, 0, p)),
                  pl.BlockSpec((1, 1, nb, LANES, L), lambda p, b, i: (b, p, 0, 0, 0)),
                  pl.BlockSpec((1, HEADS_PER_LANE_BLOCK, nb, 1, L), lambda p, b, i: (b, p, 0, 0, i)),
                  pl.BlockSpec((HEADS_PER_LANE_BLOCK, nb, L, L), lambda p, b, i: (p, 0, 0, 0))],
        out_specs=pl.BlockSpec((1, L, LANES), lambda p, b, i: (b, i, p)),
        out_shape=jax.ShapeDtypeStruct((B, S, D_MOBA), BF16),
        compiler_params=_cparams(3),
        name="moba_attn",
    )(qm, km, vmT, sel, bias)


def _swa_kernel(sink_ref, q_ref, kp_ref, kc_ref, vpT_ref, vcT_ref, bias_ref, o_ref):
    n = pl.program_id(1)
    W = SWA_BLOCK
    kband = jnp.concatenate([kp_ref[0], kc_ref[0]], axis=0)
    vbandT = jnp.concatenate([vpT_ref[0], vcT_ref[0]], axis=1)
    key = lax.broadcasted_iota(jnp.int32, (2 * W, W), 0)
    key_ok = (key >= W) | (n > 0)
    outs = []
    for p in range(D_SWA_Q // LANES):
        qp = q_ref[0, :, p * LANES:(p + 1) * LANES]
        lane = lax.broadcasted_iota(jnp.int32, qp.shape, 1)
        zero = jnp.zeros_like(qp)
        for half in range(HEADS_PER_LANE_BLOCK):
            slot = HEADS_PER_LANE_BLOCK * p + half
            in_half = (lane >= HEAD_DIM) if half else (lane < HEAD_DIM)
            sT = _dot_nt(kband, jnp.where(in_half, qp, zero)) + bias_ref[slot]
            sT = jnp.where(key_ok, sT, NEG)
            sink = sink_ref[slot]
            m = jnp.maximum(jnp.max(sT, axis=0, keepdims=True), sink)
            pr = jnp.exp(sT - m)
            l = jnp.sum(pr, axis=0, keepdims=True) + jnp.exp(sink - m)
            vT = vbandT[half * HEAD_DIM:(half + 1) * HEAD_DIM, :]
            outs.append(jnp.dot(vT, pr.astype(BF16), preferred_element_type=F32) * (1.0 / l))
    o_ref[0] = jnp.concatenate(outs, axis=0).T.astype(BF16)


def _swa(sinks_perm, qs, ks, vsT, bias):
    B, S, _ = qs.shape
    W = SWA_BLOCK
    nq = S // W
    prev = lambda n: jnp.maximum(n - 1, 0)
    return pl.pallas_call(
        _swa_kernel,
        grid=(B, nq),
        in_specs=[pl.BlockSpec(memory_space=pltpu.SMEM),
                  pl.BlockSpec((1, W, D_SWA_Q), lambda b, n: (b, n, 0)),
                  pl.BlockSpec((1, W, D_SWA_KV), lambda b, n: (b, prev(n), 0)),
                  pl.BlockSpec((1, W, D_SWA_KV), lambda b, n: (b, n, 0)),
                  pl.BlockSpec((1, D_SWA_KV, W), lambda b, n: (b, 0, prev(n))),
                  pl.BlockSpec((1, D_SWA_KV, W), lambda b, n: (b, 0, n)),
                  pl.BlockSpec((N_HEADS_SWA, 2 * W, W), lambda b, n: (0, 0, 0))],
        out_specs=pl.BlockSpec((1, W, D_SWA_Q), lambda b, n: (b, n, 0)),
        out_shape=jax.ShapeDtypeStruct((B, S, D_SWA_Q), BF16),
        compiler_params=_cparams(2),
        name="swa_attn",
    )(sinks_perm, qs, ks, ks, vsT, vsT, bias)


def _outproj_kernel(x_ref, om_ref, os_ref, wm_ref, ws_ref, b_ref, mod_ref, g_ref,
                    x1_ref, h_ref, hrt_ref):
    attn = (jnp.dot(om_ref[...], wm_ref[...], preferred_element_type=F32)
            + jnp.dot(os_ref[...], ws_ref[...], preferred_element_type=F32) + b_ref[...])
    x1 = x_ref[...] + mod_ref[0, 2:3, :] * attn
    x1_ref[...] = x1
    h = _rmsnorm_rows(x1, g_ref[...]) * (1.0 + mod_ref[0, 4:5, :]) + mod_ref[0, 3:4, :]
    h_ref[...] = h
    _store_row_tiles(hrt_ref, h)


def _store_row_tiles(ref, val):
    rows = val.shape[0]
    for c in range(ROW_CHUNKS):
        ref[pl.ds(c, rows, stride=ROW_CHUNKS), :] = val[:, c * LANES:(c + 1) * LANES]


def _load_row_tiles(ref, rows, lead=(), first_row=0):
    idx = tuple(lead)
    base = first_row * ROW_CHUNKS
    return jnp.concatenate([ref[idx + (pl.ds(base + c, rows, stride=ROW_CHUNKS), slice(None))]
                            for c in range(ROW_CHUNKS)], axis=1)


def _outproj(x2d, om, os_, w_out_m, w_out_s, b_out, mod6, g_ffn, seq_len):
    N, D = x2d.shape
    tm = 512
    per_seq = seq_len // tm
    row = lambda t: (t, 0)
    const = lambda t: (0, 0)
    return pl.pallas_call(
        _outproj_kernel,
        grid=(N // tm,),
        in_specs=[pl.BlockSpec((tm, D), row),
                  pl.BlockSpec((tm, D_MOBA), row),
                  pl.BlockSpec((tm, D_SWA_Q), row),
                  pl.BlockSpec((D_MOBA, D), const),
                  pl.BlockSpec((D_SWA_Q, D), const),
                  pl.BlockSpec((1, D), const),
                  pl.BlockSpec((1, 6, D), lambda t: (t // per_seq, 0, 0)),
                  pl.BlockSpec((1, D), const)],
        out_specs=(pl.BlockSpec((tm, D), row), pl.BlockSpec((tm, D), row),
                   pl.BlockSpec((tm * ROW_CHUNKS, LANES), row)),
        out_shape=(jax.ShapeDtypeStruct((N, D), F32), jax.ShapeDtypeStruct((N, D), F32),
                   jax.ShapeDtypeStruct((N * ROW_CHUNKS, LANES), F32)),
        compiler_params=_cparams(1),
        name="outproj",
    )(x2d, om, os_, w_out_m, w_out_s, b_out, mod6, g_ffn)


def _router_kernel(h_ref, wT_ref, b_ref, idx_ref, w_ref, rank_ref, counts_ref, carry_ref):
    @pl.when(pl.program_id(0) == 0)
    def _():
        carry_ref[...] = jnp.zeros_like(carry_ref)

    logits = _dot_nt(wT_ref[...], h_ref[...], precision=lax.Precision.HIGHEST) + b_ref[...]
    tm = logits.shape[1]
    eidx = lax.broadcasted_iota(jnp.int32, logits.shape, 0)
    vals, idxs = [], []
    cur = logits
    for _ in range(TOP_K):
        m = jnp.max(cur, axis=0, keepdims=True)
        am = jnp.min(jnp.where(cur == m, eidx, N_EXPERTS), axis=0, keepdims=True)
        vals.append(m)
        idxs.append(am)
        cur = jnp.where(eidx == am, -jnp.inf, cur)
    exps = [jnp.exp(v - vals[0]) for v in vals]
    inv = 1.0 / functools.reduce(lambda a, b: a + b, exps)
    idx_ref[...] = jnp.concatenate(idxs, axis=0)
    w_ref[...] = jnp.concatenate([e * inv for e in exps], axis=0)

    member = functools.reduce(lambda a, b: a | b, [eidx == am for am in idxs])
    earlier = (lax.broadcasted_iota(jnp.int32, (tm, tm), 0)
               < lax.broadcasted_iota(jnp.int32, (tm, tm), 1)).astype(BF16)
    before = jnp.dot(member.astype(BF16), earlier, preferred_element_type=F32) + carry_ref[:, 0:1]
    rank_ref[...] = jnp.concatenate(
        [jnp.sum(jnp.where(eidx == am, before, 0.0), axis=0, keepdims=True) for am in idxs],
        axis=0).astype(jnp.int32)
    total = carry_ref[...] + jnp.sum(member.astype(F32), axis=1, keepdims=True)
    carry_ref[...] = total
    counts_ref[...] = total.astype(jnp.int32)


def _router(h, w_routerT, b_router_col):
    N, D = h.shape
    tm = 512
    return pl.pallas_call(
        _router_kernel,
        grid=(N // tm,),
        in_specs=[pl.BlockSpec((tm, D), lambda t: (t, 0)),
                  pl.BlockSpec((N_EXPERTS, D), lambda t: (0, 0)),
                  pl.BlockSpec((N_EXPERTS, 1), lambda t: (0, 0))],
        out_specs=(pl.BlockSpec((TOP_K, tm), lambda t: (0, t)),
                   pl.BlockSpec((TOP_K, tm), lambda t: (0, t)),
                   pl.BlockSpec((TOP_K, tm), lambda t: (0, t)),
                   pl.BlockSpec((N_EXPERTS, LANES), lambda t: (0, 0))),
        out_shape=(jax.ShapeDtypeStruct((TOP_K, N), jnp.int32),
                   jax.ShapeDtypeStruct((TOP_K, N), F32),
                   jax.ShapeDtypeStruct((TOP_K, N), jnp.int32),
                   jax.ShapeDtypeStruct((N_EXPERTS, LANES), jnp.int32)),
        scratch_shapes=[pltpu.VMEM((N_EXPERTS, LANES), F32)],
        compiler_params=_cparams(1),
        name="router",
    )(h, w_routerT, b_router_col)


def _expert_mlp(h_bf, wgu, bgu, wdn, bdn):
    gu = jnp.dot(h_bf, wgu, preferred_element_type=F32) + bgu
    g = jnp.minimum(gu[:, :D_FF], SWIGLU_LIMIT)
    u = jnp.clip(gu[:, D_FF:], -SWIGLU_LIMIT, SWIGLU_LIMIT)
    act = (u + 1.0) * (g * _sigmoid(SWIGLU_ALPHA * g))
    return jnp.dot(act.astype(BF16), wdn, preferred_element_type=F32) + bdn


def _tile_plan(counts, n_tiles_max):
    tiles_per = (counts + MOE_TILE - 1) // MOE_TILE
    tile_end = jnp.cumsum(tiles_per)
    tile_start = tile_end - tiles_per
    t_ids = jnp.arange(n_tiles_max, dtype=jnp.int32)
    texp = jnp.minimum(jnp.sum((t_ids[:, None] >= tile_end[None, :]).astype(jnp.int32), axis=1),
                       N_EXPERTS - 1)
    onehot = (texp[:, None] == jnp.arange(N_EXPERTS, dtype=jnp.int32)[None, :]).astype(jnp.int32)
    local = t_ids - onehot @ tile_start
    nvalid = jnp.clip(onehot @ counts - local * MOE_TILE, 0, MOE_TILE)
    n_active = tile_end[-1]
    src_tile = jnp.minimum(t_ids, n_active - 1)
    last_tile = jnp.where(counts > 0, tile_end - 1, -1)
    tail = n_active + jnp.arange(N_EXPERTS, dtype=jnp.int32)
    zero_tiles = jnp.concatenate([last_tile, jnp.where(tail < n_tiles_max, tail, -1)])
    return (texp.astype(jnp.int32), nvalid.astype(jnp.int32), src_tile.astype(jnp.int32),
            (tile_start * MOE_TILE).astype(jnp.int32), zero_tiles.astype(jnp.int32))


def _row_copy(src, src_row, dst, dst_row, sem):
    return pltpu.make_async_copy(src.at[pl.ds(pl.multiple_of(src_row * ROW_CHUNKS, ROW_CHUNKS), ROW_CHUNKS)],
                                 dst.at[pl.ds(pl.multiple_of(dst_row * ROW_CHUNKS, ROW_CHUNKS), ROW_CHUNKS)],
                                 sem)


def _dispatch_kernel(zero_tiles_ref, pos_ref, h_hbm, x_hbm, zbuf, zsem, sem):
    t = pl.program_id(0)
    tile_rows = MOE_TILE * ROW_CHUNKS

    def zero_copy(i):
        start = pl.multiple_of(zero_tiles_ref[i] * tile_rows, tile_rows)
        return pltpu.make_async_copy(zbuf, x_hbm.at[pl.ds(start, tile_rows)], zsem)

    @pl.when(t == 0)
    def _():
        zbuf[...] = jnp.zeros_like(zbuf)
        for i in range(2 * N_EXPERTS):
            @pl.when(zero_tiles_ref[i] >= 0)
            def _():
                zero_copy(i).start()
        for i in range(2 * N_EXPERTS):
            @pl.when(zero_tiles_ref[i] >= 0)
            def _():
                zero_copy(i).wait()

    def start_rows(r, carry):
        tok = t * DISPATCH_TOKENS + r
        for k in range(TOP_K):
            _row_copy(h_hbm, tok, x_hbm, pos_ref[0, 0, k * DISPATCH_TOKENS + r], sem).start()
        return carry
    lax.fori_loop(0, DISPATCH_TOKENS, start_rows, 0, unroll=8)

    step_rows = TOP_K * DISPATCH_TOKENS * ROW_CHUNKS
    pltpu.make_async_copy(h_hbm.at[pl.ds(0, step_rows)], x_hbm.at[pl.ds(0, step_rows)], sem).wait()


def _dispatch(zero_tiles, pos_tiles, h_rt, n_rows):
    n_steps = pos_tiles.shape[0]
    grid_spec = pltpu.PrefetchScalarGridSpec(
        num_scalar_prefetch=1,
        grid=(n_steps,),
        in_specs=[pl.BlockSpec((1, 1, TOP_K * DISPATCH_TOKENS), lambda t, zt: (t, 0, 0),
                               memory_space=pltpu.SMEM),
                  pl.BlockSpec(memory_space=pl.ANY)],
        out_specs=pl.BlockSpec(memory_space=pl.ANY),
        scratch_shapes=[pltpu.VMEM((MOE_TILE * ROW_CHUNKS, LANES), F32),
                        pltpu.SemaphoreType.DMA(()),
                        pltpu.SemaphoreType.DMA(())],
    )
    return pl.pallas_call(
        _dispatch_kernel,
        grid_spec=grid_spec,
        out_shape=jax.ShapeDtypeStruct((n_rows * ROW_CHUNKS, LANES), F32),
        compiler_params=_cparams(1),
        name="moe_dispatch",
    )(zero_tiles, pos_tiles, h_rt)


def _experts_kernel(texp_ref, nvalid_ref, src_ref, x_ref, wgu_ref, bgu_ref, wdn_ref, bdn_ref, y_ref,
                    wgu_bf, wdn_bf):
    t = pl.program_id(0)
    nv = nvalid_ref[t]
    prev_e = texp_ref[jnp.maximum(t - 1, 0)]

    @pl.when((nv > 0) & ((t == 0) | (texp_ref[t] != prev_e)))
    def _():
        rows = 256
        def cast_gu(i, carry):
            r0 = pl.multiple_of(i * rows, rows)
            wgu_bf[pl.ds(r0, rows), :] = wgu_ref[0, pl.ds(r0, rows), :].astype(BF16)
            return carry
        lax.fori_loop(0, D_MODEL // rows, cast_gu, 0)
        def cast_dn(i, carry):
            r0 = pl.multiple_of(i * rows, rows)
            wdn_bf[pl.ds(r0, rows), :] = wdn_ref[0, pl.ds(r0, rows), :].astype(BF16)
            return carry
        lax.fori_loop(0, D_FF // rows, cast_dn, 0)

    @pl.when(nv > 0)
    def _():
        x = _load_row_tiles(x_ref, MOE_TILE).astype(BF16)
        _store_row_tiles(y_ref, _expert_mlp(x, wgu_bf[...], bgu_ref[0], wdn_bf[...], bdn_ref[0]))

    @pl.when(nv == 0)
    def _():
        y_ref[...] = jnp.zeros_like(y_ref)


def _experts(texp, nvalid, src_tile, x_sorted, w_gate_up, b_gate_up, w_down, b_down):
    n_tiles = texp.shape[0]
    D = D_MODEL
    tile_rows = MOE_TILE * ROW_CHUNKS
    grid_spec = pltpu.PrefetchScalarGridSpec(
        num_scalar_prefetch=3,
        grid=(n_tiles,),
        in_specs=[pl.BlockSpec((tile_rows, LANES), lambda t, te, nv, st: (st[t], 0)),
                  pl.BlockSpec((1, D, 2 * D_FF), lambda t, te, nv, st: (te[t], 0, 0)),
                  pl.BlockSpec((1, 1, 2 * D_FF), lambda t, te, nv, st: (te[t], 0, 0)),
                  pl.BlockSpec((1, D_FF, D), lambda t, te, nv, st: (te[t], 0, 0)),
                  pl.BlockSpec((1, 1, D), lambda t, te, nv, st: (te[t], 0, 0))],
        out_specs=pl.BlockSpec((tile_rows, LANES), lambda t, te, nv, st: (t, 0)),
        scratch_shapes=[pltpu.VMEM((D, 2 * D_FF), BF16),
                        pltpu.VMEM((D_FF, D), BF16)],
    )
    return pl.pallas_call(
        _experts_kernel,
        grid_spec=grid_spec,
        out_shape=jax.ShapeDtypeStruct(x_sorted.shape, F32),
        compiler_params=_cparams(1),
        name="moe_experts",
    )(texp, nvalid, src_tile, x_sorted, w_gate_up, b_gate_up.reshape(N_EXPERTS, 1, 2 * D_FF), w_down,
      b_down.reshape(N_EXPERTS, 1, D))


def _combine_kernel(pos_ref, pos_next_ref, y_hbm, w_ref, x1_ref, mod_ref, g_ref, o_ref, ybuf, sem):
    t = pl.program_id(0)
    n_steps = pl.num_programs(0)
    slot = t % 2

    def start_gather(idx_ref, s):
        def body(i, carry):
            _row_copy(y_hbm, idx_ref[0, 0, i], ybuf.at[s], i, sem.at[s]).start()
            return carry
        lax.fori_loop(0, TOP_K * COMBINE_TOKENS, body, 0, unroll=8)

    @pl.when(t == 0)
    def _():
        start_gather(pos_ref, 0)

    @pl.when(t + 1 < n_steps)
    def _():
        start_gather(pos_next_ref, 1 - slot)

    slot_rows = TOP_K * COMBINE_TOKENS * ROW_CHUNKS
    pltpu.make_async_copy(y_hbm.at[pl.ds(0, slot_rows)], ybuf.at[slot], sem.at[slot]).wait()

    w = w_ref[...]
    acc = w[:, 0:1] * _load_row_tiles(ybuf, COMBINE_TOKENS, (slot,))
    for k in range(1, TOP_K):
        acc = acc + w[:, k:k + 1] * _load_row_tiles(ybuf, COMBINE_TOKENS, (slot,), k * COMBINE_TOKENS)
    x2 = x1_ref[...] + mod_ref[0, 5:6, :] * acc
    o_ref[...] = _rmsnorm_rows(x2, g_ref[...])


def _combine(pos_tiles, y_sorted, top_w, x1, mod6, g_final, seq_len):
    N, D = x1.shape
    tm = COMBINE_TOKENS
    n_steps = N // tm
    per_seq = seq_len // tm
    return pl.pallas_call(
        _combine_kernel,
        grid=(n_steps,),
        in_specs=[pl.BlockSpec((1, 1, TOP_K * tm), lambda t: (t, 0, 0), memory_space=pltpu.SMEM),
                  pl.BlockSpec((1, 1, TOP_K * tm), lambda t: (jnp.minimum(t + 1, n_steps - 1), 0, 0),
                               memory_space=pltpu.SMEM),
                  pl.BlockSpec(memory_space=pl.ANY),
                  pl.BlockSpec((tm, TOP_K), lambda t: (t, 0)),
                  pl.BlockSpec((tm, D), lambda t: (t, 0)),
                  pl.BlockSpec((1, 6, D), lambda t: (t // per_seq, 0, 0)),
                  pl.BlockSpec((1, D), lambda t: (0, 0))],
        out_specs=pl.BlockSpec((tm, D), lambda t: (t, 0)),
        out_shape=jax.ShapeDtypeStruct((N, D), F32),
        scratch_shapes=[pltpu.VMEM((2, TOP_K * tm * ROW_CHUNKS, LANES), F32),
                        pltpu.SemaphoreType.DMA((2,))],
        compiler_params=_cparams(1),
        name="moe_combine",
    )(pos_tiles, pos_tiles, y_sorted, top_w, x1, mod6, g_final)


def _swa_head_perm():
    group = N_HEADS_SWA // N_KV_SWA
    cols = []
    for p in range(group):
        for half in range(N_KV_SWA):
            hd = p + group * half
            cols.extend(range(hd * HEAD_DIM, (hd + 1) * HEAD_DIM))
    return jnp.asarray(cols, jnp.int32)


def kernel(x, c, g_attn, w_ada, b_ada, w_in, b_in, w_out, b_out, rel_table, sinks, g_ffn, w_router,
           b_router, w_gate_up, b_gate_up, w_down, b_down, g_final):
    B, S, D = x.shape
    assert w_ada.shape[0] == 1, "the final norm is fused into the last layer; one layer supported"
    l = 0
    nb = S // MOBA_BLOCK
    perm = _swa_head_perm()
    i2 = 3 * D_MOBA
    i3 = i2 + D_SWA_Q
    col_scale = jnp.concatenate([jnp.full((D_MOBA,), ATTN_SCALE, F32), jnp.ones((2 * D_MOBA,), F32),
                                 jnp.full((D_SWA_Q,), ATTN_SCALE, F32), jnp.ones((2 * D_SWA_KV,), F32)])
    col_order = jnp.concatenate([jnp.arange(i2, dtype=jnp.int32), i2 + perm,
                                 jnp.arange(i3, D_IN, dtype=jnp.int32)])
    moba_bias = _moba_bias(rel_table, nb)
    swa_bias = _swa_bias(rel_table)
    swa_slot_head = perm[::HEAD_DIM] // HEAD_DIM

    mod6 = _ada(c, w_ada[l], b_ada[l]).reshape(B, 6, D)
    w_in_l = ((w_in[l] * col_scale)[:, col_order]).astype(BF16)
    b_in_l = ((b_in[l] * col_scale)[col_order]).reshape(1, D_IN)
    qm, km, vmT, kmean, qs, ks, vsT = _inproj(x, mod6, g_attn[l].reshape(1, D), w_in_l, b_in_l)
    sel = _gate(qm, kmean)
    o_m = _moba(qm, km, vmT, sel, moba_bias)
    o_s = _swa(sinks[l][swa_slot_head], qs, ks, vsT, swa_bias)
    w_out_m = w_out[l, :D_MOBA].astype(BF16)
    w_out_s = w_out[l, D_MOBA:][perm].astype(BF16)
    N = B * S
    x1, h, h_rt = _outproj(x.reshape(N, D), o_m.reshape(N, D_MOBA), o_s.reshape(N, D_SWA_Q), w_out_m,
                           w_out_s, b_out[l].reshape(1, D), mod6, g_ffn[l].reshape(1, D), S)
    top_idx, top_w, rank, counts = _router(h, w_router[l].T, b_router[l].reshape(N_EXPERTS, 1))
    counts = counts[:, 0]
    n_tiles_max = N * TOP_K // MOE_TILE + N_EXPERTS
    texp, nvalid, src_tile, group_start, zero_tiles = _tile_plan(counts, n_tiles_max)
    pos = group_start[top_idx] + rank
    assert DISPATCH_TOKENS == COMBINE_TOKENS
    pos_tiles = pos.reshape(TOP_K, N // DISPATCH_TOKENS, DISPATCH_TOKENS).transpose(1, 0, 2).reshape(
        N // DISPATCH_TOKENS, 1, TOP_K * DISPATCH_TOKENS)
    x_sorted = _dispatch(zero_tiles, pos_tiles, h_rt, n_tiles_max * MOE_TILE)
    y_sorted = _experts(texp, nvalid, src_tile, x_sorted, w_gate_up[l], b_gate_up[l], w_down[l], b_down[l])
    y = _combine(pos_tiles, y_sorted, top_w.T, x1, mod6, g_final.reshape(1, D), S)
    return y.reshape(B, S, D)
```

```python
import functools
import math

import jax
import jax.numpy as jnp
from jax import lax
from jax.experimental import pallas as pl
from jax.experimental.pallas import tpu as pltpu

F32 = jnp.float32
BF16 = jnp.bfloat16

D_MODEL = 1024
HEAD_DIM = 64
N_HEADS_MOBA = 8
N_HEADS_SWA = 8
N_KV_SWA = 2
D_MOBA = N_HEADS_MOBA * HEAD_DIM
D_SWA_Q = N_HEADS_SWA * HEAD_DIM
D_SWA_KV = N_KV_SWA * HEAD_DIM
D_IN = 3 * D_MOBA + D_SWA_Q + 2 * D_SWA_KV
MOBA_BLOCK = 256
MOBA_TOPK = 3
SWA_WINDOW = 128
SWA_BLOCK = 128
REL_BUCKETS = 32
REL_MAX_DIST = 1024
REL_MAX_EXACT = REL_BUCKETS // 2
N_EXPERTS = 32
TOP_K = 4
D_FF = 1024
SWIGLU_LIMIT = 7.0
SWIGLU_ALPHA = 1.702
EPS = 1e-5
NEG = -1e30
ATTN_SCALE = HEAD_DIM ** -0.5

MOE_TILE = 256
ROW_CHUNKS = D_MODEL // 128
DISPATCH_TOKENS = 256
COMBINE_TOKENS = 256
LANES = 128
HEADS_PER_LANE_BLOCK = LANES // HEAD_DIM
VMEM_LIMIT_BYTES = 56 * 1024 * 1024

_REL_THRESHOLDS = tuple(
    math.ceil(REL_MAX_EXACT * (REL_MAX_DIST / REL_MAX_EXACT) ** (k / (REL_BUCKETS - REL_MAX_EXACT)) - 1e-9)
    for k in range(1, REL_BUCKETS - REL_MAX_EXACT))


def _cparams(n_axes):
    return pltpu.CompilerParams(dimension_semantics=("arbitrary",) * n_axes,
                                vmem_limit_bytes=VMEM_LIMIT_BYTES)


def _sigmoid(z):
    return 1.0 / (1.0 + jnp.exp(-z))


def _rmsnorm_rows(xf, g):
    ms = jnp.mean(xf * xf, axis=-1, keepdims=True)
    return xf * lax.rsqrt(ms + EPS) * g


def _dot_nt(a, b, **kw):
    return lax.dot_general(a, b, (((1,), (1,)), ((), ())), preferred_element_type=F32, **kw)


def _ada_kernel(c_ref, w_ref, b_ref, o_ref):
    c = c_ref[...]
    sc = c * _sigmoid(c)
    o_ref[...] = jnp.dot(sc, w_ref[...], preferred_element_type=F32,
                         precision=lax.Precision.HIGHEST) + b_ref[...]


def _ada(c, w_ada, b_ada):
    B = c.shape[0]
    n_out = w_ada.shape[1]
    bn = 1536
    return pl.pallas_call(
        _ada_kernel,
        grid=(n_out // bn,),
        in_specs=[pl.BlockSpec((B, D_MODEL), lambda n: (0, 0)),
                  pl.BlockSpec((D_MODEL, bn), lambda n: (0, n)),
                  pl.BlockSpec((1, bn), lambda n: (0, n))],
        out_specs=pl.BlockSpec((B, bn), lambda n: (0, n)),
        out_shape=jax.ShapeDtypeStruct((B, n_out), F32),
        compiler_params=_cparams(1),
        name="ada",
    )(c, w_ada, b_ada.reshape(1, n_out))


def _inproj_kernel(x_ref, mod_ref, g_ref, w_ref, b_ref,
                   qm_ref, km_ref, vmT_ref, kmean_ref, qs_ref, ks_ref, vsT_ref):
    xf = x_ref[0]
    shift = mod_ref[0, 0:1, :]
    scale = mod_ref[0, 1:2, :]
    h = _rmsnorm_rows(xf, g_ref[...]) * (1.0 + scale) + shift
    proj = jnp.dot(h.astype(BF16), w_ref[...], preferred_element_type=F32) + b_ref[...]
    i0, i1, i2 = D_MOBA, 2 * D_MOBA, 3 * D_MOBA
    i3 = i2 + D_SWA_Q
    i4 = i3 + D_SWA_KV
    qm_ref[0] = proj[:, :i0].astype(BF16)
    km = proj[:, i0:i1]
    km_ref[0] = km.astype(BF16)
    kmean_ref[0, 0] = jnp.broadcast_to(jnp.mean(km, axis=0, keepdims=True), (8, D_MOBA))
    vmT = proj[:, i1:i2].T.astype(BF16)
    vmT_ref[0, :, 0] = vmT.reshape(D_MOBA // LANES, LANES, vmT.shape[1])
    qs_ref[0] = proj[:, i2:i3].astype(BF16)
    ks_ref[0] = proj[:, i3:i4].astype(BF16)
    vsT_ref[0] = proj[:, i4:].T.astype(BF16)


def _inproj(x, mod6, g_attn, w_in_bf, b_in):
    B, S, D = x.shape
    tm = MOBA_BLOCK
    nb = S // tm
    n_pairs = D_MOBA // LANES
    out_shape = (
        jax.ShapeDtypeStruct((B, S, D_MOBA), BF16),
        jax.ShapeDtypeStruct((B, S, D_MOBA), BF16),
        jax.ShapeDtypeStruct((B, n_pairs, nb, LANES, tm), BF16),
        jax.ShapeDtypeStruct((B, nb, 8, D_MOBA), F32),
        jax.ShapeDtypeStruct((B, S, D_SWA_Q), BF16),
        jax.ShapeDtypeStruct((B, S, D_SWA_KV), BF16),
        jax.ShapeDtypeStruct((B, D_SWA_KV, S), BF16),
    )
    return pl.pallas_call(
        _inproj_kernel,
        grid=(B, nb),
        in_specs=[pl.BlockSpec((1, tm, D), lambda b, i: (b, i, 0)),
                  pl.BlockSpec((1, 6, D), lambda b, i: (b, 0, 0)),
                  pl.BlockSpec((1, D), lambda b, i: (0, 0)),
                  pl.BlockSpec((D, D_IN), lambda b, i: (0, 0)),
                  pl.BlockSpec((1, D_IN), lambda b, i: (0, 0))],
        out_specs=(pl.BlockSpec((1, tm, D_MOBA), lambda b, i: (b, i, 0)),
                   pl.BlockSpec((1, tm, D_MOBA), lambda b, i: (b, i, 0)),
                   pl.BlockSpec((1, n_pairs, 1, LANES, tm), lambda b, i: (b, 0, i, 0, 0)),
                   pl.BlockSpec((1, 1, 8, D_MOBA), lambda b, i: (b, i, 0, 0)),
                   pl.BlockSpec((1, tm, D_SWA_Q), lambda b, i: (b, i, 0)),
                   pl.BlockSpec((1, tm, D_SWA_KV), lambda b, i: (b, i, 0)),
                   pl.BlockSpec((1, D_SWA_KV, tm), lambda b, i: (b, 0, i))),
        out_shape=out_shape,
        compiler_params=_cparams(2),
        name="inproj",
    )(x, mod6, g_attn, w_in_bf, b_in)


def _rel_bias_values(dist, table_ref, h):
    n = jnp.maximum(dist, 0)
    large = jnp.full(n.shape, REL_MAX_EXACT, jnp.int32)
    for t in _REL_THRESHOLDS:
        large = large + (n >= t).astype(jnp.int32)
    bucket = jnp.where(n < REL_MAX_EXACT, n, large)
    val = jnp.full(n.shape, table_ref[0, h], F32)
    for bkt in range(1, REL_BUCKETS):
        val = jnp.where(bucket == bkt, table_ref[bkt, h], val)
    return val


def _moba_bias_kernel(table_ref, o_ref):
    h = pl.program_id(0)
    dlt = pl.program_id(1)
    L = MOBA_BLOCK
    key = lax.broadcasted_iota(jnp.int32, (L, L), 0)
    qry = lax.broadcasted_iota(jnp.int32, (L, L), 1)
    dist = dlt * L + qry - key
    val = _rel_bias_values(dist, table_ref, h)
    o_ref[0, 0] = jnp.where(dist >= 0, val, NEG)


def _moba_bias(rel_table, nb):
    L = MOBA_BLOCK
    return pl.pallas_call(
        _moba_bias_kernel,
        grid=(N_HEADS_MOBA, nb),
        in_specs=[pl.BlockSpec(memory_space=pltpu.SMEM)],
        out_specs=pl.BlockSpec((1, 1, L, L), lambda h, d: (h, d, 0, 0)),
        out_shape=jax.ShapeDtypeStruct((N_HEADS_MOBA, nb, L, L), F32),
        compiler_params=_cparams(2),
        name="moba_bias",
    )(rel_table)


def _swa_bias_kernel(table_ref, o_ref):
    h = N_HEADS_MOBA + pl.program_id(0)
    W = SWA_BLOCK
    key = lax.broadcasted_iota(jnp.int32, (2 * W, W), 0)
    qry = lax.broadcasted_iota(jnp.int32, (2 * W, W), 1)
    dist = qry + W - key
    val = _rel_bias_values(dist, table_ref, h)
    o_ref[...] = jnp.where((dist >= 0) & (dist < SWA_WINDOW), val, NEG)


def _swa_bias(rel_table):
    W = SWA_BLOCK
    return pl.pallas_call(
        _swa_bias_kernel,
        grid=(N_HEADS_SWA,),
        in_specs=[pl.BlockSpec(memory_space=pltpu.SMEM)],
        out_specs=pl.BlockSpec((2 * W, W), lambda h: (0, h)),
        out_shape=jax.ShapeDtypeStruct((2 * W, N_HEADS_SWA * W), F32),
        compiler_params=_cparams(1),
        name="swa_bias",
    )(rel_table)


def _gate_kernel(q_ref, kmean_ref, sel_ref):
    S = q_ref.shape[1]
    nb = kmean_ref.shape[1]
    H = N_HEADS_MOBA
    q = q_ref[0]
    km = kmean_ref[0, :, 0, :]
    km_t = jnp.concatenate([km] * H, axis=0)
    row_h = lax.broadcasted_iota(jnp.int32, (H * nb, D_MOBA), 0) // nb
    col_h = lax.broadcasted_iota(jnp.int32, (H * nb, D_MOBA), 1) // HEAD_DIM
    km_t = jnp.where(row_h == col_h, km_t, 0.0)
    km_hi = km_t.astype(BF16)
    km_lo = (km_t - km_hi.astype(F32)).astype(BF16)
    g_all = _dot_nt(km_hi, q) + _dot_nt(km_lo, q)
    qblk = lax.broadcasted_iota(jnp.int32, (nb, S), 1) // MOBA_BLOCK
    jidx = lax.broadcasted_iota(jnp.int32, (nb, S), 0)
    past = jidx < qblk
    for h in range(H):
        g = g_all[h * nb:(h + 1) * nb, :]
        cnt = jnp.zeros((nb, S), jnp.int32)
        for jp in range(nb):
            gj = g[jp:jp + 1, :]
            ahead = (gj > g) | ((gj == g) & (jp < jidx))
            cnt = cnt + ((jp < qblk) & ahead).astype(jnp.int32)
        sel = (past & (cnt < MOBA_TOPK)).astype(F32)
        for j in range(nb):
            sel_ref[0, h, j] = sel[j:j + 1, :]


def _gate(qm, kmean):
    B, S, _ = qm.shape
    nb = kmean.shape[1]
    return pl.pallas_call(
        _gate_kernel,
        grid=(B,),
        in_specs=[pl.BlockSpec((1, S, D_MOBA), lambda b: (b, 0, 0)),
                  pl.BlockSpec((1, nb, 8, D_MOBA), lambda b: (b, 0, 0, 0))],
        out_specs=pl.BlockSpec((1, N_HEADS_MOBA, nb, 1, S), lambda b: (b, 0, 0, 0, 0)),
        out_shape=jax.ShapeDtypeStruct((B, N_HEADS_MOBA, nb, 1, S), F32),
        compiler_params=_cparams(1),
        name="moba_gate",
    )(qm, kmean)


def _moba_kernel(q_ref, k_ref, vT_ref, sel_ref, bias_ref, o_ref):
    i = pl.program_id(2)
    L = MOBA_BLOCK
    nb = k_ref.shape[1] // L
    q = q_ref[0]
    lane = lax.broadcasted_iota(jnp.int32, q.shape, 1)
    zero = jnp.zeros_like(q)
    q_both = jnp.concatenate([jnp.where(lane < HEAD_DIM, q, zero), jnp.where(lane >= HEAD_DIM, q, zero)],
                             axis=0)

    def attend(n_past):
        keys = k_ref[0, 0:(n_past + 1) * L, :]
        sT_all = _dot_nt(keys, q_both)
        outs = []
        for a in range(HEADS_PER_LANE_BLOCK):
            scores, shifts_on = [], []
            m = None
            for j in range(n_past + 1):
                s = sT_all[j * L:(j + 1) * L, a * L:(a + 1) * L] + bias_ref[a, n_past - j]
                mj = jnp.max(s, axis=0, keepdims=True)
                chosen = None if j == n_past else sel_ref[0, a, j] > 0.5
                if chosen is not None:
                    mj = jnp.where(chosen, mj, NEG)
                m = mj if m is None else jnp.maximum(m, mj)
                scores.append(s)
                shifts_on.append(chosen)
            l = None
            acc = None
            for j in range(n_past + 1):
                shift = m if shifts_on[j] is None else jnp.where(shifts_on[j], m, -NEG)
                p = jnp.exp(scores[j] - shift)
                lj = jnp.sum(p, axis=0, keepdims=True)
                vT = vT_ref[0, 0, j, a * HEAD_DIM:(a + 1) * HEAD_DIM, :]
                oj = jnp.dot(vT, p.astype(BF16), preferred_element_type=F32)
                l = lj if l is None else l + lj
                acc = oj if acc is None else acc + oj
            outs.append(acc * (1.0 / l))
        o_ref[0] = jnp.concatenate(outs, axis=0).T.astype(BF16)

    for n_past in range(nb):
        pl.when(i == n_past)(functools.partial(attend, n_past))


def _moba(qm, km, vmT, sel, bias):
    B, S, _ = qm.shape
    L = MOBA_BLOCK
    nb = S // L
    n_pairs = D_MOBA // LANES
    return pl.pallas_call(
        _moba_kernel,
        grid=(n_pairs, B, nb),
        in_specs=[pl.BlockSpec((1, L, LANES), lambda p, b, i: (b, i, p)),
                  pl.BlockSpec((1, S, LANES), lambda p, b, i: (b, 0, p)),
                  pl.BlockSpec((1, 1, nb, LANES, L), lambda p, b, i: (b, p, 0, 0, 0)),
                  pl.BlockSpec((1, HEADS_PER_LANE_BLOCK, nb, 1, L), lambda p, b, i: (b, p, 0, 0, i)),
                  pl.BlockSpec((HEADS_PER_LANE_BLOCK, nb, L, L), lambda p, b, i: (p, 0, 0, 0))],
        out_specs=pl.BlockSpec((1, L, LANES), lambda p, b, i: (b, i, p)),
        out_shape=jax.ShapeDtypeStruct((B, S, D_MOBA), BF16),
        compiler_params=_cparams(3),
        name="moba_attn",
    )(qm, km, vmT, sel, bias)


def _swa_kernel(sink_ref, q_ref, kp_ref, kc_ref, vpT_ref, vcT_ref, bias_ref, o_ref):
    n = pl.program_id(1)
    W = SWA_BLOCK
    group = N_HEADS_SWA // N_KV_SWA
    kband = jnp.concatenate([kp_ref[0], kc_ref[0]], axis=0)
    vbandT = jnp.concatenate([vpT_ref[0], vcT_ref[0]], axis=1)
    q_rows = []
    for h in range(N_HEADS_SWA):
        qp = q_ref[0, :, (h % group) * LANES:(h % group + 1) * LANES]
        lane = lax.broadcasted_iota(jnp.int32, qp.shape, 1)
        in_half = (lane >= HEAD_DIM) if h // group else (lane < HEAD_DIM)
        q_rows.append(jnp.where(in_half, qp, jnp.zeros_like(qp)))
    q_all = jnp.concatenate(q_rows, axis=0)
    sT = _dot_nt(kband, q_all) + bias_ref[...]
    key = lax.broadcasted_iota(jnp.int32, sT.shape, 0)
    sT = jnp.where((key >= W) | (n > 0), sT, NEG)
    sink = sink_ref[...]
    m = jnp.maximum(jnp.max(sT, axis=0, keepdims=True), sink)
    pr = jnp.exp(sT - m)
    inv_l = 1.0 / (jnp.sum(pr, axis=0, keepdims=True) + jnp.exp(sink - m))
    pr = pr.astype(BF16)
    cols = group * W
    outs = [jnp.dot(vbandT[g * HEAD_DIM:(g + 1) * HEAD_DIM, :], pr[:, g * cols:(g + 1) * cols],
                    preferred_element_type=F32) * inv_l[:, g * cols:(g + 1) * cols]
            for g in range(N_KV_SWA)]
    heads = [outs[h // group][:, (h % group) * W:(h % group + 1) * W] for h in range(N_HEADS_SWA)]
    o_ref[0] = jnp.concatenate(heads, axis=0).T.astype(BF16)


def _swa(sink_row, qs, ks, vsT, bias):
    B, S, _ = qs.shape
    W = SWA_BLOCK
    nq = S // W
    prev = lambda n: jnp.maximum(n - 1, 0)
    return pl.pallas_call(
        _swa_kernel,
        grid=(B, nq),
        in_specs=[pl.BlockSpec((1, N_HEADS_SWA * W), lambda b, n: (0, 0)),
                  pl.BlockSpec((1, W, D_SWA_Q), lambda b, n: (b, n, 0)),
                  pl.BlockSpec((1, W, D_SWA_KV), lambda b, n: (b, prev(n), 0)),
                  pl.BlockSpec((1, W, D_SWA_KV), lambda b, n: (b, n, 0)),
                  pl.BlockSpec((1, D_SWA_KV, W), lambda b, n: (b, 0, prev(n))),
                  pl.BlockSpec((1, D_SWA_KV, W), lambda b, n: (b, 0, n)),
                  pl.BlockSpec((2 * W, N_HEADS_SWA * W), lambda b, n: (0, 0))],
        out_specs=pl.BlockSpec((1, W, D_SWA_Q), lambda b, n: (b, n, 0)),
        out_shape=jax.ShapeDtypeStruct((B, S, D_SWA_Q), BF16),
        compiler_params=_cparams(2),
        name="swa_attn",
    )(sink_row, qs, ks, ks, vsT, vsT, bias)


def _outproj_kernel(x_ref, om_ref, os_ref, wm_ref, ws_ref, b_ref, mod_ref, g_ref,
                    x1_ref, h_ref, hrt_ref):
    attn = (jnp.dot(om_ref[...], wm_ref[...], preferred_element_type=F32)
            + jnp.dot(os_ref[...], ws_ref[...], preferred_element_type=F32) + b_ref[...])
    x1 = x_ref[...] + mod_ref[0, 2:3, :] * attn
    x1_ref[...] = x1
    h = _rmsnorm_rows(x1, g_ref[...]) * (1.0 + mod_ref[0, 4:5, :]) + mod_ref[0, 3:4, :]
    h_ref[...] = h
    _store_row_tiles(hrt_ref, h)


def _store_row_tiles(ref, val):
    rows = val.shape[0]
    for c in range(ROW_CHUNKS):
        ref[pl.ds(c, rows, stride=ROW_CHUNKS), :] = val[:, c * LANES:(c + 1) * LANES]


def _load_row_tiles(ref, rows, lead=(), first_row=0):
    idx = tuple(lead)
    base = first_row * ROW_CHUNKS
    return jnp.concatenate([ref[idx + (pl.ds(base + c, rows, stride=ROW_CHUNKS), slice(None))]
                            for c in range(ROW_CHUNKS)], axis=1)


def _outproj(x2d, om, os_, w_out_m, w_out_s, b_out, mod6, g_ffn, seq_len):
    N, D = x2d.shape
    tm = 512
    per_seq = seq_len // tm
    row = lambda t: (t, 0)
    const = lambda t: (0, 0)
    return pl.pallas_call(
        _outproj_kernel,
        grid=(N // tm,),
        in_specs=[pl.BlockSpec((tm, D), row),
                  pl.BlockSpec((tm, D_MOBA), row),
                  pl.BlockSpec((tm, D_SWA_Q), row),
                  pl.BlockSpec((D_MOBA, D), const),
                  pl.BlockSpec((D_SWA_Q, D), const),
                  pl.BlockSpec((1, D), const),
                  pl.BlockSpec((1, 6, D), lambda t: (t // per_seq, 0, 0)),
                  pl.BlockSpec((1, D), const)],
        out_specs=(pl.BlockSpec((tm, D), row), pl.BlockSpec((tm, D), row),
                   pl.BlockSpec((tm * ROW_CHUNKS, LANES), row)),
        out_shape=(jax.ShapeDtypeStruct((N, D), F32), jax.ShapeDtypeStruct((N, D), F32),
                   jax.ShapeDtypeStruct((N * ROW_CHUNKS, LANES), F32)),
        compiler_params=_cparams(1),
        name="outproj",
    )(x2d, om, os_, w_out_m, w_out_s, b_out, mod6, g_ffn)


def _router_kernel(h_ref, wT_ref, b_ref, idx_ref, w_ref, rank_ref, counts_ref, carry_ref):
    @pl.when(pl.program_id(0) == 0)
    def _():
        carry_ref[...] = jnp.zeros_like(carry_ref)

    logits = _dot_nt(wT_ref[...], h_ref[...], precision=lax.Precision.HIGHEST) + b_ref[...]
    tm = logits.shape[1]
    eidx = lax.broadcasted_iota(jnp.int32, logits.shape, 0)
    vals, idxs = [], []
    cur = logits
    for _ in range(TOP_K):
        m = jnp.max(cur, axis=0, keepdims=True)
        am = jnp.min(jnp.where(cur == m, eidx, N_EXPERTS), axis=0, keepdims=True)
        vals.append(m)
        idxs.append(am)
        cur = jnp.where(eidx == am, -jnp.inf, cur)
    exps = [jnp.exp(v - vals[0]) for v in vals]
    inv = 1.0 / functools.reduce(lambda a, b: a + b, exps)
    idx_ref[...] = jnp.concatenate(idxs, axis=0)
    w_ref[...] = jnp.concatenate([e * inv for e in exps], axis=0)

    member = functools.reduce(lambda a, b: a | b, [eidx == am for am in idxs])
    earlier = (lax.broadcasted_iota(jnp.int32, (tm, tm), 0)
               < lax.broadcasted_iota(jnp.int32, (tm, tm), 1)).astype(BF16)
    before = jnp.dot(member.astype(BF16), earlier, preferred_element_type=F32) + carry_ref[:, 0:1]
    rank_ref[...] = jnp.concatenate(
        [jnp.sum(jnp.where(eidx == am, before, 0.0), axis=0, keepdims=True) for am in idxs],
        axis=0).astype(jnp.int32)
    total = carry_ref[...] + jnp.sum(member.astype(F32), axis=1, keepdims=True)
    carry_ref[...] = total
    counts_ref[...] = total.astype(jnp.int32)


def _router(h, w_routerT, b_router_col):
    N, D = h.shape
    tm = 512
    return pl.pallas_call(
        _router_kernel,
        grid=(N // tm,),
        in_specs=[pl.BlockSpec((tm, D), lambda t: (t, 0)),
                  pl.BlockSpec((N_EXPERTS, D), lambda t: (0, 0)),
                  pl.BlockSpec((N_EXPERTS, 1), lambda t: (0, 0))],
        out_specs=(pl.BlockSpec((TOP_K, tm), lambda t: (0, t)),
                   pl.BlockSpec((TOP_K, tm), lambda t: (0, t)),
                   pl.BlockSpec((TOP_K, tm), lambda t: (0, t)),
                   pl.BlockSpec((N_EXPERTS, LANES), lambda t: (0, 0))),
        out_shape=(jax.ShapeDtypeStruct((TOP_K, N), jnp.int32),
                   jax.ShapeDtypeStruct((TOP_K, N), F32),
                   jax.ShapeDtypeStruct((TOP_K, N), jnp.int32),
                   jax.ShapeDtypeStruct((N_EXPERTS, LANES), jnp.int32)),
        scratch_shapes=[pltpu.VMEM((N_EXPERTS, LANES), F32)],
        compiler_params=_cparams(1),
        name="router",
    )(h, w_routerT, b_router_col)


def _expert_mlp(h_bf, wgu, bgu, wdn, bdn):
    gu = jnp.dot(h_bf, wgu, preferred_element_type=F32) + bgu
    g = jnp.minimum(gu[:, :D_FF], SWIGLU_LIMIT)
    u = jnp.clip(gu[:, D_FF:], -SWIGLU_LIMIT, SWIGLU_LIMIT)
    act = (u + 1.0) * (g * _sigmoid(SWIGLU_ALPHA * g))
    return jnp.dot(act.astype(BF16), wdn, preferred_element_type=F32) + bdn


def _tile_plan(counts, n_tiles_max):
    tiles_per = (counts + MOE_TILE - 1) // MOE_TILE
    tile_end = jnp.cumsum(tiles_per)
    tile_start = tile_end - tiles_per
    t_ids = jnp.arange(n_tiles_max, dtype=jnp.int32)
    texp = jnp.minimum(jnp.sum((t_ids[:, None] >= tile_end[None, :]).astype(jnp.int32), axis=1),
                       N_EXPERTS - 1)
    onehot = (texp[:, None] == jnp.arange(N_EXPERTS, dtype=jnp.int32)[None, :]).astype(jnp.int32)
    local = t_ids - onehot @ tile_start
    nvalid = jnp.clip(onehot @ counts - local * MOE_TILE, 0, MOE_TILE)
    n_active = tile_end[-1]
    src_tile = jnp.minimum(t_ids, n_active - 1)
    last_tile = jnp.where(counts > 0, tile_end - 1, -1)
    tail = n_active + jnp.arange(N_EXPERTS, dtype=jnp.int32)
    zero_tiles = jnp.concatenate([last_tile, jnp.where(tail < n_tiles_max, tail, -1)])
    return (texp.astype(jnp.int32), nvalid.astype(jnp.int32), src_tile.astype(jnp.int32),
            (tile_start * MOE_TILE).astype(jnp.int32), zero_tiles.astype(jnp.int32))


def _row_copy(src, src_row, dst, dst_row, sem):
    return pltpu.make_async_copy(src.at[pl.ds(pl.multiple_of(src_row * ROW_CHUNKS, ROW_CHUNKS), ROW_CHUNKS)],
                                 dst.at[pl.ds(pl.multiple_of(dst_row * ROW_CHUNKS, ROW_CHUNKS), ROW_CHUNKS)],
                                 sem)


def _dispatch_kernel(zero_tiles_ref, pos_ref, h_ref, x_hbm, zbuf, zsem, sem):
    t = pl.program_id(0)
    tile_rows = MOE_TILE * ROW_CHUNKS

    def zero_copy(i):
        start = pl.multiple_of(zero_tiles_ref[i] * tile_rows, tile_rows)
        return pltpu.make_async_copy(zbuf, x_hbm.at[pl.ds(start, tile_rows)], zsem)

    @pl.when(t == 0)
    def _():
        zbuf[...] = jnp.zeros_like(zbuf)
        for i in range(2 * N_EXPERTS):
            @pl.when(zero_tiles_ref[i] >= 0)
            def _():
                zero_copy(i).start()
        for i in range(2 * N_EXPERTS):
            @pl.when(zero_tiles_ref[i] >= 0)
            def _():
                zero_copy(i).wait()

    def start_rows(r, carry):
        for k in range(TOP_K):
            _row_copy(h_ref, r, x_hbm, pos_ref[0, 0, k * DISPATCH_TOKENS + r], sem).start(priority=k % 2)
        return carry
    lax.fori_loop(0, DISPATCH_TOKENS, start_rows, 0, unroll=8)

    for k in range(TOP_K):
        pltpu.make_async_copy(h_ref, x_hbm.at[pl.ds(0, DISPATCH_TOKENS * ROW_CHUNKS)], sem).wait()


def _dispatch(zero_tiles, pos_tiles, h_rt, n_rows):
    n_steps = pos_tiles.shape[0]
    grid_spec = pltpu.PrefetchScalarGridSpec(
        num_scalar_prefetch=1,
        grid=(n_steps,),
        in_specs=[pl.BlockSpec((1, 1, TOP_K * DISPATCH_TOKENS), lambda t, zt: (t, 0, 0),
                               memory_space=pltpu.SMEM),
                  pl.BlockSpec((DISPATCH_TOKENS * ROW_CHUNKS, LANES), lambda t, zt: (t, 0))],
        out_specs=pl.BlockSpec(memory_space=pl.ANY),
        scratch_shapes=[pltpu.VMEM((MOE_TILE * ROW_CHUNKS, LANES), F32),
                        pltpu.SemaphoreType.DMA(()),
                        pltpu.SemaphoreType.DMA(())],
    )
    return pl.pallas_call(
        _dispatch_kernel,
        grid_spec=grid_spec,
        out_shape=jax.ShapeDtypeStruct((n_rows * ROW_CHUNKS, LANES), F32),
        compiler_params=_cparams(1),
        name="moe_dispatch",
    )(zero_tiles, pos_tiles, h_rt)


def _experts_kernel(texp_ref, nvalid_ref, src_ref, x_ref, wgu_ref, bgu_ref, wdn_ref, bdn_ref, y_ref,
                    wgu_bf, wdn_bf):
    t = pl.program_id(0)
    nv = nvalid_ref[t]
    prev_e = texp_ref[jnp.maximum(t - 1, 0)]

    @pl.when((nv > 0) & ((t == 0) | (texp_ref[t] != prev_e)))
    def _():
        rows = 256
        def cast_gu(i, carry):
            r0 = pl.multiple_of(i * rows, rows)
            wgu_bf[pl.ds(r0, rows), :] = wgu_ref[0, pl.ds(r0, rows), :].astype(BF16)
            return carry
        lax.fori_loop(0, D_MODEL // rows, cast_gu, 0)
        def cast_dn(i, carry):
            r0 = pl.multiple_of(i * rows, rows)
            wdn_bf[pl.ds(r0, rows), :] = wdn_ref[0, pl.ds(r0, rows), :].astype(BF16)
            return carry
        lax.fori_loop(0, D_FF // rows, cast_dn, 0)

    @pl.when(nv > 0)
    def _():
        x = _load_row_tiles(x_ref, MOE_TILE).astype(BF16)
        _store_row_tiles(y_ref, _expert_mlp(x, wgu_bf[...], bgu_ref[0], wdn_bf[...], bdn_ref[0]))

    @pl.when(nv == 0)
    def _():
        y_ref[...] = jnp.zeros_like(y_ref)


def _experts(texp, nvalid, src_tile, x_sorted, w_gate_up, b_gate_up, w_down, b_down):
    n_tiles = texp.shape[0]
    D = D_MODEL
    tile_rows = MOE_TILE * ROW_CHUNKS
    grid_spec = pltpu.PrefetchScalarGridSpec(
        num_scalar_prefetch=3,
        grid=(n_tiles,),
        in_specs=[pl.BlockSpec((tile_rows, LANES), lambda t, te, nv, st: (st[t], 0)),
                  pl.BlockSpec((1, D, 2 * D_FF), lambda t, te, nv, st: (te[t], 0, 0)),
                  pl.BlockSpec((1, 1, 2 * D_FF), lambda t, te, nv, st: (te[t], 0, 0)),
                  pl.BlockSpec((1, D_FF, D), lambda t, te, nv, st: (te[t], 0, 0)),
                  pl.BlockSpec((1, 1, D), lambda t, te, nv, st: (te[t], 0, 0))],
        out_specs=pl.BlockSpec((tile_rows, LANES), lambda t, te, nv, st: (t, 0)),
        scratch_shapes=[pltpu.VMEM((D, 2 * D_FF), BF16),
                        pltpu.VMEM((D_FF, D), BF16)],
    )
    return pl.pallas_call(
        _experts_kernel,
        grid_spec=grid_spec,
        out_shape=jax.ShapeDtypeStruct(x_sorted.shape, F32),
        compiler_params=_cparams(1),
        name="moe_experts",
    )(texp, nvalid, src_tile, x_sorted, w_gate_up, b_gate_up.reshape(N_EXPERTS, 1, 2 * D_FF), w_down,
      b_down.reshape(N_EXPERTS, 1, D))


def _combine_kernel(pos_ref, pos_next_ref, y_hbm, w_ref, x1_ref, mod_ref, g_ref, o_ref, ybuf, sem):
    t = pl.program_id(0)
    n_steps = pl.num_programs(0)
    slot = t % 2

    def start_gather(idx_ref, s):
        def body(pair, carry):
            for queue in range(2):
                i = 2 * pair + queue
                _row_copy(y_hbm, idx_ref[0, 0, i], ybuf.at[s], i, sem.at[s]).start(priority=queue)
            return carry
        lax.fori_loop(0, TOP_K * COMBINE_TOKENS // 2, body, 0, unroll=4)

    @pl.when(t == 0)
    def _():
        start_gather(pos_ref, 0)

    @pl.when(t + 1 < n_steps)
    def _():
        start_gather(pos_next_ref, 1 - slot)

    slot_rows = TOP_K * COMBINE_TOKENS * ROW_CHUNKS
    pltpu.make_async_copy(y_hbm.at[pl.ds(0, slot_rows)], ybuf.at[slot], sem.at[slot]).wait()

    w = w_ref[...]
    acc = w[:, 0:1] * _load_row_tiles(ybuf, COMBINE_TOKENS, (slot,))
    for k in range(1, TOP_K):
        acc = acc + w[:, k:k + 1] * _load_row_tiles(ybuf, COMBINE_TOKENS, (slot,), k * COMBINE_TOKENS)
    x2 = x1_ref[...] + mod_ref[0, 5:6, :] * acc
    o_ref[...] = _rmsnorm_rows(x2, g_ref[...])


def _combine(pos_tiles, y_sorted, top_w, x1, mod6, g_final, seq_len):
    N, D = x1.shape
    tm = COMBINE_TOKENS
    n_steps = N // tm
    per_seq = seq_len // tm
    return pl.pallas_call(
        _combine_kernel,
        grid=(n_steps,),
        in_specs=[pl.BlockSpec((1, 1, TOP_K * tm), lambda t: (t, 0, 0), memory_space=pltpu.SMEM),
                  pl.BlockSpec((1, 1, TOP_K * tm), lambda t: (jnp.minimum(t + 1, n_steps - 1), 0, 0),
                               memory_space=pltpu.SMEM),
                  pl.BlockSpec(memory_space=pl.ANY),
                  pl.BlockSpec((tm, TOP_K), lambda t: (t, 0)),
                  pl.BlockSpec((tm, D), lambda t: (t, 0)),
                  pl.BlockSpec((1, 6, D), lambda t: (t // per_seq, 0, 0)),
                  pl.BlockSpec((1, D), lambda t: (0, 0))],
        out_specs=pl.BlockSpec((tm, D), lambda t: (t, 0)),
        out_shape=jax.ShapeDtypeStruct((N, D), F32),
        scratch_shapes=[pltpu.VMEM((2, TOP_K * tm * ROW_CHUNKS, LANES), F32),
                        pltpu.SemaphoreType.DMA((2,))],
        compiler_params=_cparams(1),
        name="moe_combine",
    )(pos_tiles, pos_tiles, y_sorted, top_w, x1, mod6, g_final)


def _swa_head_perm():
    group = N_HEADS_SWA // N_KV_SWA
    cols = []
    for p in range(group):
        for half in range(N_KV_SWA):
            hd = p + group * half
            cols.extend(range(hd * HEAD_DIM, (hd + 1) * HEAD_DIM))
    return jnp.asarray(cols, jnp.int32)


def kernel(x, c, g_attn, w_ada, b_ada, w_in, b_in, w_out, b_out, rel_table, sinks, g_ffn, w_router,
           b_router, w_gate_up, b_gate_up, w_down, b_down, g_final):
    B, S, D = x.shape
    assert w_ada.shape[0] == 1, "the final norm is fused into the last layer; one layer supported"
    l = 0
    nb = S // MOBA_BLOCK
    perm = _swa_head_perm()
    i2 = 3 * D_MOBA
    i3 = i2 + D_SWA_Q
    col_scale = jnp.concatenate([jnp.full((D_MOBA,), ATTN_SCALE, F32), jnp.ones((2 * D_MOBA,), F32),
                                 jnp.full((D_SWA_Q,), ATTN_SCALE, F32), jnp.ones((2 * D_SWA_KV,), F32)])
    col_order = jnp.concatenate([jnp.arange(i2, dtype=jnp.int32), i2 + perm,
                                 jnp.arange(i3, D_IN, dtype=jnp.int32)])
    moba_bias = _moba_bias(rel_table, nb)
    swa_bias = _swa_bias(rel_table)

    mod6 = _ada(c, w_ada[l], b_ada[l]).reshape(B, 6, D)
    w_in_l = ((w_in[l] * col_scale)[:, col_order]).astype(BF16)
    b_in_l = ((b_in[l] * col_scale)[col_order]).reshape(1, D_IN)
    qm, km, vmT, kmean, qs, ks, vsT = _inproj(x, mod6, g_attn[l].reshape(1, D), w_in_l, b_in_l)
    sel = _gate(qm, kmean)
    o_m = _moba(qm, km, vmT, sel, moba_bias)
    sink_row = jnp.repeat(sinks[l], SWA_BLOCK).reshape(1, N_HEADS_SWA * SWA_BLOCK)
    o_s = _swa(sink_row, qs, ks, vsT, swa_bias)
    w_out_m = w_out[l, :D_MOBA].astype(BF16)
    w_out_s = w_out[l, D_MOBA:].astype(BF16)
    N = B * S
    x1, h, h_rt = _outproj(x.reshape(N, D), o_m.reshape(N, D_MOBA), o_s.reshape(N, D_SWA_Q), w_out_m,
                           w_out_s, b_out[l].reshape(1, D), mod6, g_ffn[l].reshape(1, D), S)
    top_idx, top_w, rank, counts = _router(h, w_router[l].T, b_router[l].reshape(N_EXPERTS, 1))
    counts = counts[:, 0]
    n_tiles_max = N * TOP_K // MOE_TILE + N_EXPERTS
    texp, nvalid, src_tile, group_start, zero_tiles = _tile_plan(counts, n_tiles_max)
    experts = jnp.arange(N_EXPERTS, dtype=jnp.int32)
    pos = rank + jnp.sum(jnp.where(top_idx[..., None] == experts, group_start, 0), axis=-1)
    assert DISPATCH_TOKENS == COMBINE_TOKENS
    pos_tiles = pos.reshape(TOP_K, N // DISPATCH_TOKENS, DISPATCH_TOKENS).transpose(1, 0, 2).reshape(
        N // DISPATCH_TOKENS, 1, TOP_K * DISPATCH_TOKENS)
    x_sorted = _dispatch(zero_tiles, pos_tiles, h_rt, n_tiles_max * MOE_TILE)
    y_sorted = _experts(texp, nvalid, src_tile, x_sorted, w_gate_up[l], b_gate_up[l], w_down[l], b_down[l])
    y = _combine(pos_tiles, y_sorted, top_w.T, x1, mod6, g_final.reshape(1, D), S)
    return y.reshape(B, S, D)
```

```python
import functools
import math

import jax
import jax.numpy as jnp
from jax import lax
from jax.experimental import pallas as pl
from jax.experimental.pallas import tpu as pltpu

F32 = jnp.float32
BF16 = jnp.bfloat16

D_MODEL = 1024
HEAD_DIM = 64
N_HEADS_MOBA = 8
N_HEADS_SWA = 8
N_KV_SWA = 2
D_MOBA = N_HEADS_MOBA * HEAD_DIM
D_SWA_Q = N_HEADS_SWA * HEAD_DIM
D_SWA_KV = N_KV_SWA * HEAD_DIM
D_IN = 3 * D_MOBA + D_SWA_Q + 2 * D_SWA_KV
MOBA_BLOCK = 256
MOBA_TOPK = 3
SWA_WINDOW = 128
SWA_BLOCK = 128
REL_BUCKETS = 32
REL_MAX_DIST = 1024
REL_MAX_EXACT = REL_BUCKETS // 2
N_EXPERTS = 32
TOP_K = 4
D_FF = 1024
SWIGLU_LIMIT = 7.0
SWIGLU_ALPHA = 1.702
EPS = 1e-5
NEG = -1e30
ATTN_SCALE = HEAD_DIM ** -0.5

MOE_TILE = 256
ROW_CHUNKS = D_MODEL // 128
DISPATCH_TOKENS = 256
COMBINE_TOKENS = 256
LANES = 128
HEADS_PER_LANE_BLOCK = LANES // HEAD_DIM
VMEM_LIMIT_BYTES = 56 * 1024 * 1024

_REL_THRESHOLDS = tuple(
    math.ceil(REL_MAX_EXACT * (REL_MAX_DIST / REL_MAX_EXACT) ** (k / (REL_BUCKETS - REL_MAX_EXACT)) - 1e-9)
    for k in range(1, REL_BUCKETS - REL_MAX_EXACT))


def _cparams(n_axes):
    return pltpu.CompilerParams(dimension_semantics=("arbitrary",) * n_axes,
                                vmem_limit_bytes=VMEM_LIMIT_BYTES)


def _sigmoid(z):
    return 1.0 / (1.0 + jnp.exp(-z))


def _rmsnorm_rows(xf, g):
    ms = jnp.mean(xf * xf, axis=-1, keepdims=True)
    return xf * lax.rsqrt(ms + EPS) * g


def _dot_nt(a, b, **kw):
    return lax.dot_general(a, b, (((1,), (1,)), ((), ())), preferred_element_type=F32, **kw)


def _ada_kernel(c_ref, w_ref, b_ref, o_ref):
    c = c_ref[...]
    sc = c * _sigmoid(c)
    o_ref[...] = jnp.dot(sc, w_ref[...], preferred_element_type=F32,
                         precision=lax.Precision.HIGHEST) + b_ref[...]


def _ada(c, w_ada, b_ada):
    B = c.shape[0]
    n_out = w_ada.shape[1]
    bn = 1536
    return pl.pallas_call(
        _ada_kernel,
        grid=(n_out // bn,),
        in_specs=[pl.BlockSpec((B, D_MODEL), lambda n: (0, 0)),
                  pl.BlockSpec((D_MODEL, bn), lambda n: (0, n)),
                  pl.BlockSpec((1, bn), lambda n: (0, n))],
        out_specs=pl.BlockSpec((B, bn), lambda n: (0, n)),
        out_shape=jax.ShapeDtypeStruct((B, n_out), F32),
        compiler_params=_cparams(1),
        name="ada",
    )(c, w_ada, b_ada.reshape(1, n_out))


def _inproj_kernel(x_ref, mod_ref, g_ref, w_ref, b_ref,
                   qm_ref, km_ref, vmT_ref, kmean_ref, qs_ref, ks_ref, vsT_ref):
    xf = x_ref[0]
    shift = mod_ref[0, 0:1, :]
    scale = mod_ref[0, 1:2, :]
    h = _rmsnorm_rows(xf, g_ref[...]) * (1.0 + scale) + shift
    proj = jnp.dot(h.astype(BF16), w_ref[...], preferred_element_type=F32) + b_ref[...]
    i0, i1, i2 = D_MOBA, 2 * D_MOBA, 3 * D_MOBA
    i3 = i2 + D_SWA_Q
    i4 = i3 + D_SWA_KV
    qm_ref[0] = proj[:, :i0].astype(BF16)
    km = proj[:, i0:i1]
    km_ref[0] = km.astype(BF16)
    kmean_ref[0, 0] = jnp.broadcast_to(jnp.mean(km, axis=0, keepdims=True), (8, D_MOBA))
    vmT = proj[:, i1:i2].T.astype(BF16)
    vmT_ref[0, :, 0] = vmT.reshape(D_MOBA // LANES, LANES, vmT.shape[1])
    qs_ref[0] = proj[:, i2:i3].astype(BF16)
    ks_ref[0] = proj[:, i3:i4].astype(BF16)
    vsT_ref[0] = proj[:, i4:].T.astype(BF16)


def _inproj(x, mod6, g_attn, w_in_bf, b_in):
    B, S, D = x.shape
    tm = MOBA_BLOCK
    nb = S // tm
    n_pairs = D_MOBA // LANES
    out_shape = (
        jax.ShapeDtypeStruct((B, S, D_MOBA), BF16),
        jax.ShapeDtypeStruct((B, S, D_MOBA), BF16),
        jax.ShapeDtypeStruct((B, n_pairs, nb, LANES, tm), BF16),
        jax.ShapeDtypeStruct((B, nb, 8, D_MOBA), F32),
        jax.ShapeDtypeStruct((B, S, D_SWA_Q), BF16),
        jax.ShapeDtypeStruct((B, S, D_SWA_KV), BF16),
        jax.ShapeDtypeStruct((B, D_SWA_KV, S), BF16),
    )
    return pl.pallas_call(
        _inproj_kernel,
        grid=(B, nb),
        in_specs=[pl.BlockSpec((1, tm, D), lambda b, i: (b, i, 0)),
                  pl.BlockSpec((1, 6, D), lambda b, i: (b, 0, 0)),
                  pl.BlockSpec((1, D), lambda b, i: (0, 0)),
                  pl.BlockSpec((D, D_IN), lambda b, i: (0, 0)),
                  pl.BlockSpec((1, D_IN), lambda b, i: (0, 0))],
        out_specs=(pl.BlockSpec((1, tm, D_MOBA), lambda b, i: (b, i, 0)),
                   pl.BlockSpec((1, tm, D_MOBA), lambda b, i: (b, i, 0)),
                   pl.BlockSpec((1, n_pairs, 1, LANES, tm), lambda b, i: (b, 0, i, 0, 0)),
                   pl.BlockSpec((1, 1, 8, D_MOBA), lambda b, i: (b, i, 0, 0)),
                   pl.BlockSpec((1, tm, D_SWA_Q), lambda b, i: (b, i, 0)),
                   pl.BlockSpec((1, tm, D_SWA_KV), lambda b, i: (b, i, 0)),
                   pl.BlockSpec((1, D_SWA_KV, tm), lambda b, i: (b, 0, i))),
        out_shape=out_shape,
        compiler_params=_cparams(2),
        name="inproj",
    )(x, mod6, g_attn, w_in_bf, b_in)


def _rel_bucket_static(n):
    n = max(n, 0)
    return n if n < REL_MAX_EXACT else REL_MAX_EXACT + sum(n >= t for t in _REL_THRESHOLDS)


def _rel_bias_values(dist, table_ref, h, d_min, d_max):
    lo, hi = _rel_bucket_static(d_min), _rel_bucket_static(d_max)
    n = jnp.maximum(dist, 0)
    val = jnp.full(n.shape, table_ref[lo, h], F32)
    for bkt in range(lo + 1, hi + 1):
        start = bkt if bkt <= REL_MAX_EXACT else _REL_THRESHOLDS[bkt - REL_MAX_EXACT - 1]
        val = jnp.where(n >= start, table_ref[bkt, h], val)
    return val


def _moba_bias_kernel(table_ref, o_ref):
    h = pl.program_id(0)
    L = MOBA_BLOCK
    key = lax.broadcasted_iota(jnp.int32, (L, L), 0)
    qry = lax.broadcasted_iota(jnp.int32, (L, L), 1)
    for dlt in range(o_ref.shape[1]):
        dist = dlt * L + qry - key
        val = _rel_bias_values(dist, table_ref, h, dlt * L - (L - 1), dlt * L + (L - 1))
        o_ref[0, dlt] = jnp.where(dist >= 0, val, NEG) if dlt == 0 else val


def _moba_bias(rel_table, nb):
    L = MOBA_BLOCK
    return pl.pallas_call(
        _moba_bias_kernel,
        grid=(N_HEADS_MOBA,),
        in_specs=[pl.BlockSpec(memory_space=pltpu.SMEM)],
        out_specs=pl.BlockSpec((1, nb, L, L), lambda h: (h, 0, 0, 0)),
        out_shape=jax.ShapeDtypeStruct((N_HEADS_MOBA, nb, L, L), F32),
        compiler_params=_cparams(1),
        name="moba_bias",
    )(rel_table)


def _swa_bias_kernel(table_ref, o_ref):
    h = N_HEADS_MOBA + pl.program_id(0)
    W = SWA_BLOCK
    key = lax.broadcasted_iota(jnp.int32, (2 * W, W), 0)
    qry = lax.broadcasted_iota(jnp.int32, (2 * W, W), 1)
    dist = qry + W - key
    val = _rel_bias_values(dist, table_ref, h, 0, SWA_WINDOW - 1)
    o_ref[...] = jnp.where((dist >= 0) & (dist < SWA_WINDOW), val, NEG)


def _swa_bias(rel_table):
    W = SWA_BLOCK
    return pl.pallas_call(
        _swa_bias_kernel,
        grid=(N_HEADS_SWA,),
        in_specs=[pl.BlockSpec(memory_space=pltpu.SMEM)],
        out_specs=pl.BlockSpec((2 * W, W), lambda h: (0, h)),
        out_shape=jax.ShapeDtypeStruct((2 * W, N_HEADS_SWA * W), F32),
        compiler_params=_cparams(1),
        name="swa_bias",
    )(rel_table)


def _gate_kernel(q_ref, kmean_ref, sel_ref):
    S = q_ref.shape[1]
    nb = kmean_ref.shape[1]
    H = N_HEADS_MOBA
    q = q_ref[0]
    km = kmean_ref[0, :, 0, :]
    km_t = jnp.concatenate([km] * H, axis=0)
    row_h = lax.broadcasted_iota(jnp.int32, (H * nb, D_MOBA), 0) // nb
    col_h = lax.broadcasted_iota(jnp.int32, (H * nb, D_MOBA), 1) // HEAD_DIM
    km_t = jnp.where(row_h == col_h, km_t, 0.0)
    km_hi = km_t.astype(BF16)
    km_lo = (km_t - km_hi.astype(F32)).astype(BF16)
    g_all = _dot_nt(km_hi, q) + _dot_nt(km_lo, q)
    qblk = lax.broadcasted_iota(jnp.int32, (nb, S), 1) // MOBA_BLOCK
    jidx = lax.broadcasted_iota(jnp.int32, (nb, S), 0)
    past = jidx < qblk
    for h in range(H):
        g = g_all[h * nb:(h + 1) * nb, :]
        cnt = jnp.zeros((nb, S), jnp.int32)
        for jp in range(nb):
            gj = g[jp:jp + 1, :]
            ahead = (gj > g) | ((gj == g) & (jp < jidx))
            cnt = cnt + ((jp < qblk) & ahead).astype(jnp.int32)
        sel = (past & (cnt < MOBA_TOPK)).astype(F32)
        for j in range(nb):
            sel_ref[0, h, j] = sel[j:j + 1, :]


def _gate(qm, kmean):
    B, S, _ = qm.shape
    nb = kmean.shape[1]
    return pl.pallas_call(
        _gate_kernel,
        grid=(B,),
        in_specs=[pl.BlockSpec((1, S, D_MOBA), lambda b: (b, 0, 0)),
                  pl.BlockSpec((1, nb, 8, D_MOBA), lambda b: (b, 0, 0, 0))],
        out_specs=pl.BlockSpec((1, N_HEADS_MOBA, nb, 1, S), lambda b: (b, 0, 0, 0, 0)),
        out_shape=jax.ShapeDtypeStruct((B, N_HEADS_MOBA, nb, 1, S), F32),
        compiler_params=_cparams(1),
        name="moba_gate",
    )(qm, kmean)


def _moba_kernel(q_ref, k_ref, vT_ref, sel_ref, bias_ref, o_ref):
    i = pl.program_id(2)
    L = MOBA_BLOCK
    nb = k_ref.shape[1] // L
    q = q_ref[0]
    lane = lax.broadcasted_iota(jnp.int32, q.shape, 1)
    zero = jnp.zeros_like(q)
    q_both = jnp.concatenate([jnp.where(lane < HEAD_DIM, q, zero), jnp.where(lane >= HEAD_DIM, q, zero)],
                             axis=0)

    def attend(n_past):
        keys = k_ref[0, 0:(n_past + 1) * L, :]
        sT_all = _dot_nt(keys, q_both)
        outs = []
        for a in range(HEADS_PER_LANE_BLOCK):
            scores, shifts_on = [], []
            m = None
            for j in range(n_past + 1):
                s = sT_all[j * L:(j + 1) * L, a * L:(a + 1) * L] + bias_ref[a, n_past - j]
                mj = jnp.max(s, axis=0, keepdims=True)
                chosen = None if j == n_past else sel_ref[0, a, j] > 0.5
                if chosen is not None:
                    mj = jnp.where(chosen, mj, NEG)
                m = mj if m is None else jnp.maximum(m, mj)
                scores.append(s)
                shifts_on.append(chosen)
            l = None
            acc = None
            for j in range(n_past + 1):
                shift = m if shifts_on[j] is None else jnp.where(shifts_on[j], m, -NEG)
                p = jnp.exp(scores[j] - shift)
                lj = jnp.sum(p, axis=0, keepdims=True)
                vT = vT_ref[0, 0, j, a * HEAD_DIM:(a + 1) * HEAD_DIM, :]
                oj = jnp.dot(vT, p.astype(BF16), preferred_element_type=F32)
                l = lj if l is None else l + lj
                acc = oj if acc is None else acc + oj
            outs.append(acc * (1.0 / l))
        o_ref[0] = jnp.concatenate(outs, axis=0).T.astype(BF16)

    for n_past in range(nb):
        pl.when(i == n_past)(functools.partial(attend, n_past))


def _moba(qm, km, vmT, sel, bias):
    B, S, _ = qm.shape
    L = MOBA_BLOCK
    nb = S // L
    n_pairs = D_MOBA // LANES
    return pl.pallas_call(
        _moba_kernel,
        grid=(n_pairs, B, nb),
        in_specs=[pl.BlockSpec((1, L, LANES), lambda p, b, i: (b, i, p)),
                  pl.BlockSpec((1, S, LANES), lambda p, b, i: (b, 0, p)),
                  pl.BlockSpec((1, 1, nb, LANES, L), lambda p, b, i: (b, p, 0, 0, 0)),
                  pl.BlockSpec((1, HEADS_PER_LANE_BLOCK, nb, 1, L), lambda p, b, i: (b, p, 0, 0, i)),
                  pl.BlockSpec((HEADS_PER_LANE_BLOCK, nb, L, L), lambda p, b, i: (p, 0, 0, 0))],
        out_specs=pl.BlockSpec((1, L, LANES), lambda p, b, i: (b, i, p)),
        out_shape=jax.ShapeDtypeStruct((B, S, D_MOBA), BF16),
        compiler_params=_cparams(3),
        name="moba_attn",
    )(qm, km, vmT, sel, bias)


def _swa_kernel(sink_ref, q_ref, kp_ref, kc_ref, vpT_ref, vcT_ref, bias_ref, o_ref):
    n = pl.program_id(1)
    W = SWA_BLOCK
    group = N_HEADS_SWA // N_KV_SWA
    kband = jnp.concatenate([kp_ref[0], kc_ref[0]], axis=0)
    vbandT = jnp.concatenate([vpT_ref[0], vcT_ref[0]], axis=1)
    q_rows = []
    for h in range(N_HEADS_SWA):
        qp = q_ref[0, :, (h % group) * LANES:(h % group + 1) * LANES]
        lane = lax.broadcasted_iota(jnp.int32, qp.shape, 1)
        in_half = (lane >= HEAD_DIM) if h // group else (lane < HEAD_DIM)
        q_rows.append(jnp.where(in_half, qp, jnp.zeros_like(qp)))
    q_all = jnp.concatenate(q_rows, axis=0)
    sT = _dot_nt(kband, q_all) + bias_ref[...]
    key = lax.broadcasted_iota(jnp.int32, sT.shape, 0)
    sT = jnp.where((key >= W) | (n > 0), sT, NEG)
    sink = sink_ref[...]
    m = jnp.maximum(jnp.max(sT, axis=0, keepdims=True), sink)
    pr = jnp.exp(sT - m)
    inv_l = 1.0 / (jnp.sum(pr, axis=0, keepdims=True) + jnp.exp(sink - m))
    pr = pr.astype(BF16)
    cols = group * W
    outs = [jnp.dot(vbandT[g * HEAD_DIM:(g + 1) * HEAD_DIM, :], pr[:, g * cols:(g + 1) * cols],
                    preferred_element_type=F32) * inv_l[:, g * cols:(g + 1) * cols]
            for g in range(N_KV_SWA)]
    heads = [outs[h // group][:, (h % group) * W:(h % group + 1) * W] for h in range(N_HEADS_SWA)]
    o_ref[0] = jnp.concatenate(heads, axis=0).T.astype(BF16)


def _swa(sink_row, qs, ks, vsT, bias):
    B, S, _ = qs.shape
    W = SWA_BLOCK
    nq = S // W
    prev = lambda n: jnp.maximum(n - 1, 0)
    return pl.pallas_call(
        _swa_kernel,
        grid=(B, nq),
        in_specs=[pl.BlockSpec((1, N_HEADS_SWA * W), lambda b, n: (0, 0)),
                  pl.BlockSpec((1, W, D_SWA_Q), lambda b, n: (b, n, 0)),
                  pl.BlockSpec((1, W, D_SWA_KV), lambda b, n: (b, prev(n), 0)),
                  pl.BlockSpec((1, W, D_SWA_KV), lambda b, n: (b, n, 0)),
                  pl.BlockSpec((1, D_SWA_KV, W), lambda b, n: (b, 0, prev(n))),
                  pl.BlockSpec((1, D_SWA_KV, W), lambda b, n: (b, 0, n)),
                  pl.BlockSpec((2 * W, N_HEADS_SWA * W), lambda b, n: (0, 0))],
        out_specs=pl.BlockSpec((1, W, D_SWA_Q), lambda b, n: (b, n, 0)),
        out_shape=jax.ShapeDtypeStruct((B, S, D_SWA_Q), BF16),
        compiler_params=_cparams(2),
        name="swa_attn",
    )(sink_row, qs, ks, ks, vsT, vsT, bias)


def _outproj_kernel(x_ref, om_ref, os_ref, wm_ref, ws_ref, b_ref, mod_ref, g_ref, wrT_ref, br_ref,
                    x1_ref, hrt_ref, idx_ref, w_ref, rank_ref, counts_ref, carry_ref):
    attn = (jnp.dot(om_ref[...], wm_ref[...], preferred_element_type=F32)
            + jnp.dot(os_ref[...], ws_ref[...], preferred_element_type=F32) + b_ref[...])
    x1 = x_ref[...] + mod_ref[0, 2:3, :] * attn
    x1_ref[...] = x1
    h = _rmsnorm_rows(x1, g_ref[...]) * (1.0 + mod_ref[0, 4:5, :]) + mod_ref[0, 3:4, :]
    _store_row_tiles(hrt_ref, h)
    _route(h, wrT_ref, br_ref, idx_ref, w_ref, rank_ref, counts_ref, carry_ref)


def _store_row_tiles(ref, val):
    rows = val.shape[0]
    for c in range(ROW_CHUNKS):
        ref[pl.ds(c, rows, stride=ROW_CHUNKS), :] = val[:, c * LANES:(c + 1) * LANES]


def _load_row_tiles(ref, rows, lead=(), first_row=0):
    idx = tuple(lead)
    base = first_row * ROW_CHUNKS
    return jnp.concatenate([ref[idx + (pl.ds(base + c, rows, stride=ROW_CHUNKS), slice(None))]
                            for c in range(ROW_CHUNKS)], axis=1)


def _outproj(x2d, om, os_, w_out_m, w_out_s, b_out, mod6, g_ffn, w_routerT, b_router_col, seq_len):
    N, D = x2d.shape
    tm = 512
    per_seq = seq_len // tm
    row = lambda t: (t, 0)
    col = lambda t: (0, t)
    const = lambda t: (0, 0)
    return pl.pallas_call(
        _outproj_kernel,
        grid=(N // tm,),
        in_specs=[pl.BlockSpec((tm, D), row),
                  pl.BlockSpec((tm, D_MOBA), row),
                  pl.BlockSpec((tm, D_SWA_Q), row),
                  pl.BlockSpec((D_MOBA, D), const),
                  pl.BlockSpec((D_SWA_Q, D), const),
                  pl.BlockSpec((1, D), const),
                  pl.BlockSpec((1, 6, D), lambda t: (t // per_seq, 0, 0)),
                  pl.BlockSpec((1, D), const),
                  pl.BlockSpec((N_EXPERTS, D), const),
                  pl.BlockSpec((N_EXPERTS, 1), const)],
        out_specs=(pl.BlockSpec((tm, D), row),
                   pl.BlockSpec((tm * ROW_CHUNKS, LANES), row),
                   pl.BlockSpec((TOP_K, tm), col),
                   pl.BlockSpec((TOP_K, tm), col),
                   pl.BlockSpec((TOP_K, tm), col),
                   pl.BlockSpec((N_EXPERTS, LANES), const)),
        out_shape=(jax.ShapeDtypeStruct((N, D), F32),
                   jax.ShapeDtypeStruct((N * ROW_CHUNKS, LANES), F32),
                   jax.ShapeDtypeStruct((TOP_K, N), jnp.int32),
                   jax.ShapeDtypeStruct((TOP_K, N), F32),
                   jax.ShapeDtypeStruct((TOP_K, N), jnp.int32),
                   jax.ShapeDtypeStruct((N_EXPERTS, LANES), jnp.int32)),
        scratch_shapes=[pltpu.VMEM((N_EXPERTS, LANES), F32)],
        compiler_params=_cparams(1),
        name="outproj_route",
    )(x2d, om, os_, w_out_m, w_out_s, b_out, mod6, g_ffn, w_routerT, b_router_col)


def _route(h, wT_ref, b_ref, idx_ref, w_ref, rank_ref, counts_ref, carry_ref):
    @pl.when(pl.program_id(0) == 0)
    def _():
        carry_ref[...] = jnp.zeros_like(carry_ref)

    logits = _dot_nt(wT_ref[...], h, precision=lax.Precision.HIGHEST) + b_ref[...]
    tm = logits.shape[1]
    eidx = lax.broadcasted_iota(jnp.int32, logits.shape, 0)
    vals, idxs = [], []
    cur = logits
    for _ in range(TOP_K):
        m = jnp.max(cur, axis=0, keepdims=True)
        am = jnp.min(jnp.where(cur == m, eidx, N_EXPERTS), axis=0, keepdims=True)
        vals.append(m)
        idxs.append(am)
        cur = jnp.where(eidx == am, -jnp.inf, cur)
    exps = [jnp.exp(v - vals[0]) for v in vals]
    inv = 1.0 / functools.reduce(lambda a, b: a + b, exps)
    idx_ref[...] = jnp.concatenate(idxs, axis=0)
    w_ref[...] = jnp.concatenate([e * inv for e in exps], axis=0)

    member = functools.reduce(lambda a, b: a | b, [eidx == am for am in idxs])
    earlier = (lax.broadcasted_iota(jnp.int32, (tm, tm), 0)
               < lax.broadcasted_iota(jnp.int32, (tm, tm), 1)).astype(BF16)
    before = jnp.dot(member.astype(BF16), earlier, preferred_element_type=F32) + carry_ref[:, 0:1]
    rank_ref[...] = jnp.concatenate(
        [jnp.sum(jnp.where(eidx == am, before, 0.0), axis=0, keepdims=True) for am in idxs],
        axis=0).astype(jnp.int32)
    total = carry_ref[...] + jnp.sum(member.astype(F32), axis=1, keepdims=True)
    carry_ref[...] = total
    counts_ref[...] = total.astype(jnp.int32)


def _expert_mlp(h_bf, wgu, bgu, wdn, bdn):
    gu = jnp.dot(h_bf, wgu, preferred_element_type=F32) + bgu
    g = jnp.minimum(gu[:, :D_FF], SWIGLU_LIMIT)
    u = jnp.clip(gu[:, D_FF:], -SWIGLU_LIMIT, SWIGLU_LIMIT)
    act = (u + 1.0) * (g * _sigmoid(SWIGLU_ALPHA * g))
    return jnp.dot(act.astype(BF16), wdn, preferred_element_type=F32) + bdn


def _tile_plan(counts, n_tiles_max):
    tiles_per = (counts + MOE_TILE - 1) // MOE_TILE
    tile_end = jnp.cumsum(tiles_per)
    tile_start = tile_end - tiles_per
    t_ids = jnp.arange(n_tiles_max, dtype=jnp.int32)
    texp = jnp.minimum(jnp.sum((t_ids[:, None] >= tile_end[None, :]).astype(jnp.int32), axis=1),
                       N_EXPERTS - 1)
    onehot = (texp[:, None] == jnp.arange(N_EXPERTS, dtype=jnp.int32)[None, :]).astype(jnp.int32)
    local = t_ids - onehot @ tile_start
    nvalid = jnp.clip(onehot @ counts - local * MOE_TILE, 0, MOE_TILE)
    n_active = tile_end[-1]
    src_tile = jnp.minimum(t_ids, n_active - 1)
    last_tile = jnp.where(counts > 0, tile_end - 1, -1)
    tail = n_active + jnp.arange(N_EXPERTS, dtype=jnp.int32)
    zero_tiles = jnp.concatenate([last_tile, jnp.where(tail < n_tiles_max, tail, -1)])
    ids = jnp.arange(N_EXPERTS, dtype=jnp.int32)
    later = jnp.where((ids[None, :] > ids[:, None]) & (counts[None, :] > 0), ids[None, :], N_EXPERTS)
    next_expert = jnp.min(later, axis=1)
    next_expert = jnp.where(next_expert < N_EXPERTS, next_expert, -1)
    return (texp.astype(jnp.int32), nvalid.astype(jnp.int32), src_tile.astype(jnp.int32),
            (tile_start * MOE_TILE).astype(jnp.int32), zero_tiles.astype(jnp.int32),
            next_expert.astype(jnp.int32))


def _row_copy(src, src_row, dst, dst_row, sem):
    return pltpu.make_async_copy(src.at[pl.ds(pl.multiple_of(src_row * ROW_CHUNKS, ROW_CHUNKS), ROW_CHUNKS)],
                                 dst.at[pl.ds(pl.multiple_of(dst_row * ROW_CHUNKS, ROW_CHUNKS), ROW_CHUNKS)],
                                 sem)


DISPATCH_SLOTS = 3


def _dispatch_kernel(zero_tiles_ref, pos_ref, h_hbm, x_hbm, zbuf, hbuf, zsem, load_sem, row_sem):
    t = pl.program_id(0)
    n_steps = pl.num_programs(0)
    tile_rows = MOE_TILE * ROW_CHUNKS
    block_rows = DISPATCH_TOKENS * ROW_CHUNKS

    def block_load(step, slot):
        start = pl.multiple_of(step * block_rows, block_rows)
        return pltpu.make_async_copy(h_hbm.at[pl.ds(start, block_rows)], hbuf.at[slot], load_sem.at[slot])

    def rows_done(slot):
        for k in range(TOP_K):
            pltpu.make_async_copy(hbuf.at[slot], x_hbm.at[pl.ds(0, block_rows)], row_sem.at[slot]).wait()

    def zero_copy(i):
        start = pl.multiple_of(zero_tiles_ref[i] * tile_rows, tile_rows)
        return pltpu.make_async_copy(zbuf, x_hbm.at[pl.ds(start, tile_rows)], zsem)

    @pl.when(t == 0)
    def _():
        zbuf[...] = jnp.zeros_like(zbuf)
        for i in range(2 * N_EXPERTS):
            @pl.when(zero_tiles_ref[i] >= 0)
            def _():
                zero_copy(i).start()
        block_load(0, 0).start()
        if DISPATCH_SLOTS > 2:
            @pl.when(n_steps > 1)
            def _():
                block_load(1, 1).start()
        for i in range(2 * N_EXPERTS):
            @pl.when(zero_tiles_ref[i] >= 0)
            def _():
                zero_copy(i).wait()

    slot = t % DISPATCH_SLOTS
    block_load(t, slot).wait()

    def start_rows(r, carry):
        for k in range(TOP_K):
            _row_copy(hbuf.at[slot], r, x_hbm, pos_ref[0, 0, k * DISPATCH_TOKENS + r],
                      row_sem.at[slot]).start(priority=k % 2)
        return carry
    lax.fori_loop(0, DISPATCH_TOKENS, start_rows, 0, unroll=8)

    refill = (t + 2) % DISPATCH_SLOTS

    @pl.when(t > 0)
    def _():
        rows_done(refill)

    @pl.when(t + 2 < n_steps)
    def _():
        block_load(t + 2, refill).start()

    @pl.when(t == n_steps - 1)
    def _():
        rows_done(slot)


def _dispatch(zero_tiles, pos_tiles, h_rt, n_rows):
    n_steps = pos_tiles.shape[0]
    grid_spec = pltpu.PrefetchScalarGridSpec(
        num_scalar_prefetch=1,
        grid=(n_steps,),
        in_specs=[pl.BlockSpec((1, 1, TOP_K * DISPATCH_TOKENS), lambda t, zt: (t, 0, 0),
                               memory_space=pltpu.SMEM),
                  pl.BlockSpec(memory_space=pl.ANY)],
        out_specs=pl.BlockSpec(memory_space=pl.ANY),
        scratch_shapes=[pltpu.VMEM((MOE_TILE * ROW_CHUNKS, LANES), F32),
                        pltpu.VMEM((DISPATCH_SLOTS, DISPATCH_TOKENS * ROW_CHUNKS, LANES), F32),
                        pltpu.SemaphoreType.DMA(()),
                        pltpu.SemaphoreType.DMA((DISPATCH_SLOTS,)),
                        pltpu.SemaphoreType.DMA((DISPATCH_SLOTS,))],
    )
    return pl.pallas_call(
        _dispatch_kernel,
        grid_spec=grid_spec,
        out_shape=jax.ShapeDtypeStruct((n_rows * ROW_CHUNKS, LANES), F32),
        compiler_params=_cparams(1),
        name="moe_dispatch",
    )(zero_tiles, pos_tiles, h_rt)


def _experts_kernel(texp_ref, nvalid_ref, src_ref, next_ref, x_ref, wgu_hbm, bgu_ref, wdn_hbm, bdn_ref,
                    y_ref, wgu_f32, wdn_f32, wgu_bf, wdn_bf, wsem):
    t = pl.program_id(0)
    nv = nvalid_ref[t]
    e = texp_ref[t]
    prev_e = texp_ref[jnp.maximum(t - 1, 0)]

    def weight_loads(expert):
        return (pltpu.make_async_copy(wgu_hbm.at[expert], wgu_f32, wsem.at[0]),
                pltpu.make_async_copy(wdn_hbm.at[expert], wdn_f32, wsem.at[1]))

    @pl.when((nv > 0) & ((t == 0) | (e != prev_e)))
    def _():
        @pl.when(t == 0)
        def _():
            for cp in weight_loads(e):
                cp.start()
        for cp in weight_loads(e):
            cp.wait()
        rows = 256
        def cast_gu(i, carry):
            r0 = pl.multiple_of(i * rows, rows)
            wgu_bf[pl.ds(r0, rows), :] = wgu_f32[pl.ds(r0, rows), :].astype(BF16)
            return carry
        lax.fori_loop(0, D_MODEL // rows, cast_gu, 0)
        def cast_dn(i, carry):
            r0 = pl.multiple_of(i * rows, rows)
            wdn_bf[pl.ds(r0, rows), :] = wdn_f32[pl.ds(r0, rows), :].astype(BF16)
            return carry
        lax.fori_loop(0, D_FF // rows, cast_dn, 0)
        nxt = next_ref[e]

        @pl.when(nxt >= 0)
        def _():
            for cp in weight_loads(nxt):
                cp.start()

    @pl.when(nv > 0)
    def _():
        x = _load_row_tiles(x_ref, MOE_TILE).astype(BF16)
        _store_row_tiles(y_ref, _expert_mlp(x, wgu_bf[...], bgu_ref[0], wdn_bf[...], bdn_ref[0]))

    @pl.when(nv == 0)
    def _():
        y_ref[...] = jnp.zeros_like(y_ref)


def _experts(texp, nvalid, src_tile, next_expert, x_sorted, w_gate_up, b_gate_up, w_down, b_down):
    n_tiles = texp.shape[0]
    D = D_MODEL
    tile_rows = MOE_TILE * ROW_CHUNKS
    grid_spec = pltpu.PrefetchScalarGridSpec(
        num_scalar_prefetch=4,
        grid=(n_tiles,),
        in_specs=[pl.BlockSpec((tile_rows, LANES), lambda t, te, nv, st, nx: (st[t], 0)),
                  pl.BlockSpec(memory_space=pl.ANY),
                  pl.BlockSpec((1, 1, 2 * D_FF), lambda t, te, nv, st, nx: (te[t], 0, 0)),
                  pl.BlockSpec(memory_space=pl.ANY),
                  pl.BlockSpec((1, 1, D), lambda t, te, nv, st, nx: (te[t], 0, 0))],
        out_specs=pl.BlockSpec((tile_rows, LANES), lambda t, te, nv, st, nx: (t, 0)),
        scratch_shapes=[pltpu.VMEM((D, 2 * D_FF), F32),
                        pltpu.VMEM((D_FF, D), F32),
                        pltpu.VMEM((D, 2 * D_FF), BF16),
                        pltpu.VMEM((D_FF, D), BF16),
                        pltpu.SemaphoreType.DMA((2,))],
    )
    return pl.pallas_call(
        _experts_kernel,
        grid_spec=grid_spec,
        out_shape=jax.ShapeDtypeStruct(x_sorted.shape, F32),
        compiler_params=_cparams(1),
        name="moe_experts",
    )(texp, nvalid, src_tile, next_expert, x_sorted, w_gate_up, b_gate_up.reshape(N_EXPERTS, 1, 2 * D_FF),
      w_down, b_down.reshape(N_EXPERTS, 1, D))


def _combine_kernel(pos_ref, pos_next_ref, y_hbm, w_ref, x1_ref, mod_ref, g_ref, o_ref, ybuf, sem):
    t = pl.program_id(0)
    n_steps = pl.num_programs(0)
    slot = t % 2

    def start_gather(idx_ref, s):
        def body(pair, carry):
            for queue in range(2):
                i = 2 * pair + queue
                _row_copy(y_hbm, idx_ref[0, 0, i], ybuf.at[s], i, sem.at[s]).start(priority=queue)
            return carry
        lax.fori_loop(0, TOP_K * COMBINE_TOKENS // 2, body, 0, unroll=4)

    @pl.when(t == 0)
    def _():
        start_gather(pos_ref, 0)

    @pl.when(t + 1 < n_steps)
    def _():
        start_gather(pos_next_ref, 1 - slot)

    slot_rows = TOP_K * COMBINE_TOKENS * ROW_CHUNKS
    pltpu.make_async_copy(y_hbm.at[pl.ds(0, slot_rows)], ybuf.at[slot], sem.at[slot]).wait()

    w = w_ref[...]
    acc = w[:, 0:1] * _load_row_tiles(ybuf, COMBINE_TOKENS, (slot,))
    for k in range(1, TOP_K):
        acc = acc + w[:, k:k + 1] * _load_row_tiles(ybuf, COMBINE_TOKENS, (slot,), k * COMBINE_TOKENS)
    x2 = x1_ref[...] + mod_ref[0, 5:6, :] * acc
    o_ref[...] = _rmsnorm_rows(x2, g_ref[...])


def _combine(pos_tiles, y_sorted, top_w, x1, mod6, g_final, seq_len):
    N, D = x1.shape
    tm = COMBINE_TOKENS
    n_steps = N // tm
    per_seq = seq_len // tm
    return pl.pallas_call(
        _combine_kernel,
        grid=(n_steps,),
        in_specs=[pl.BlockSpec((1, 1, TOP_K * tm), lambda t: (t, 0, 0), memory_space=pltpu.SMEM),
                  pl.BlockSpec((1, 1, TOP_K * tm), lambda t: (jnp.minimum(t + 1, n_steps - 1), 0, 0),
                               memory_space=pltpu.SMEM),
                  pl.BlockSpec(memory_space=pl.ANY),
                  pl.BlockSpec((tm, TOP_K), lambda t: (t, 0)),
                  pl.BlockSpec((tm, D), lambda t: (t, 0)),
                  pl.BlockSpec((1, 6, D), lambda t: (t // per_seq, 0, 0)),
                  pl.BlockSpec((1, D), lambda t: (0, 0))],
        out_specs=pl.BlockSpec((tm, D), lambda t: (t, 0)),
        out_shape=jax.ShapeDtypeStruct((N, D), F32),
        scratch_shapes=[pltpu.VMEM((2, TOP_K * tm * ROW_CHUNKS, LANES), F32),
                        pltpu.SemaphoreType.DMA((2,))],
        compiler_params=_cparams(1),
        name="moe_combine",
    )(pos_tiles, pos_tiles, y_sorted, top_w, x1, mod6, g_final)


def _swa_head_perm():
    group = N_HEADS_SWA // N_KV_SWA
    cols = []
    for p in range(group):
        for half in range(N_KV_SWA):
            hd = p + group * half
            cols.extend(range(hd * HEAD_DIM, (hd + 1) * HEAD_DIM))
    return jnp.asarray(cols, jnp.int32)


def kernel(x, c, g_attn, w_ada, b_ada, w_in, b_in, w_out, b_out, rel_table, sinks, g_ffn, w_router,
           b_router, w_gate_up, b_gate_up, w_down, b_down, g_final):
    B, S, D = x.shape
    assert w_ada.shape[0] == 1, "the final norm is fused into the last layer; one layer supported"
    l = 0
    nb = S // MOBA_BLOCK
    perm = _swa_head_perm()
    i2 = 3 * D_MOBA
    i3 = i2 + D_SWA_Q
    col_scale = jnp.concatenate([jnp.full((D_MOBA,), ATTN_SCALE, F32), jnp.ones((2 * D_MOBA,), F32),
                                 jnp.full((D_SWA_Q,), ATTN_SCALE, F32), jnp.ones((2 * D_SWA_KV,), F32)])
    col_order = jnp.concatenate([jnp.arange(i2, dtype=jnp.int32), i2 + perm,
                                 jnp.arange(i3, D_IN, dtype=jnp.int32)])
    moba_bias = _moba_bias(rel_table, nb)
    swa_bias = _swa_bias(rel_table)

    mod6 = _ada(c, w_ada[l], b_ada[l]).reshape(B, 6, D)
    w_in_l = ((w_in[l] * col_scale)[:, col_order]).astype(BF16)
    b_in_l = ((b_in[l] * col_scale)[col_order]).reshape(1, D_IN)
    qm, km, vmT, kmean, qs, ks, vsT = _inproj(x, mod6, g_attn[l].reshape(1, D), w_in_l, b_in_l)
    sel = _gate(qm, kmean)
    o_m = _moba(qm, km, vmT, sel, moba_bias)
    sink_row = jnp.repeat(sinks[l], SWA_BLOCK).reshape(1, N_HEADS_SWA * SWA_BLOCK)
    o_s = _swa(sink_row, qs, ks, vsT, swa_bias)
    w_out_m = w_out[l, :D_MOBA].astype(BF16)
    w_out_s = w_out[l, D_MOBA:].astype(BF16)
    N = B * S
    x1, h_rt, top_idx, top_w, rank, counts = _outproj(
        x.reshape(N, D), o_m.reshape(N, D_MOBA), o_s.reshape(N, D_SWA_Q), w_out_m, w_out_s,
        b_out[l].reshape(1, D), mod6, g_ffn[l].reshape(1, D), w_router[l].T,
        b_router[l].reshape(N_EXPERTS, 1), S)
    counts = counts[:, 0]
    n_tiles_max = N * TOP_K // MOE_TILE + N_EXPERTS
    texp, nvalid, src_tile, group_start, zero_tiles, next_expert = _tile_plan(counts, n_tiles_max)
    experts = jnp.arange(N_EXPERTS, dtype=jnp.int32)
    pos = rank + jnp.sum(jnp.where(top_idx[..., None] == experts, group_start, 0), axis=-1)
    assert DISPATCH_TOKENS == COMBINE_TOKENS
    pos_tiles = pos.reshape(TOP_K, N // DISPATCH_TOKENS, DISPATCH_TOKENS).transpose(1, 0, 2).reshape(
        N // DISPATCH_TOKENS, 1, TOP_K * DISPATCH_TOKENS)
    x_sorted = _dispatch(zero_tiles, pos_tiles, h_rt, n_tiles_max * MOE_TILE)
    y_sorted = _experts(texp, nvalid, src_tile, next_expert, x_sorted, w_gate_up[l], b_gate_up[l], w_down[l],
                        b_down[l])
    y = _combine(pos_tiles, y_sorted, top_w.T, x1, mod6, g_final.reshape(1, D), S)
    return y.reshape(B, S, D)
```

```python
import functools
import math

import jax
import jax.numpy as jnp
from jax import lax
from jax.experimental import pallas as pl
from jax.experimental.pallas import tpu as pltpu

F32 = jnp.float32
BF16 = jnp.bfloat16

D_MODEL = 1024
HEAD_DIM = 64
N_HEADS_MOBA = 8
N_HEADS_SWA = 8
N_KV_SWA = 2
D_MOBA = N_HEADS_MOBA * HEAD_DIM
D_SWA_Q = N_HEADS_SWA * HEAD_DIM
D_SWA_KV = N_KV_SWA * HEAD_DIM
D_IN = 3 * D_MOBA + D_SWA_Q + 2 * D_SWA_KV
MOBA_BLOCK = 256
MOBA_TOPK = 3
SWA_WINDOW = 128
SWA_BLOCK = 128
REL_BUCKETS = 32
REL_MAX_DIST = 1024
REL_MAX_EXACT = REL_BUCKETS // 2
N_EXPERTS = 32
TOP_K = 4
D_FF = 1024
SWIGLU_LIMIT = 7.0
SWIGLU_ALPHA = 1.702
EPS = 1e-5
NEG = -1e30
ATTN_SCALE = HEAD_DIM ** -0.5

MOE_TILE = 256
ROW_CHUNKS = D_MODEL // 128
DISPATCH_TOKENS = 256
COMBINE_TOKENS = 256
LANES = 128
HEADS_PER_LANE_BLOCK = LANES // HEAD_DIM
VMEM_LIMIT_BYTES = 56 * 1024 * 1024

_REL_THRESHOLDS = tuple(
    math.ceil(REL_MAX_EXACT * (REL_MAX_DIST / REL_MAX_EXACT) ** (k / (REL_BUCKETS - REL_MAX_EXACT)) - 1e-9)
    for k in range(1, REL_BUCKETS - REL_MAX_EXACT))


def _cparams(n_axes):
    return pltpu.CompilerParams(dimension_semantics=("arbitrary",) * n_axes,
                                vmem_limit_bytes=VMEM_LIMIT_BYTES)


def _sigmoid(z):
    return 1.0 / (1.0 + jnp.exp(-z))


def _rmsnorm_rows(xf, g):
    ms = jnp.mean(xf * xf, axis=-1, keepdims=True)
    return xf * lax.rsqrt(ms + EPS) * g


def _dot_nt(a, b, **kw):
    return lax.dot_general(a, b, (((1,), (1,)), ((), ())), preferred_element_type=F32, **kw)


def _ada_kernel(c_ref, w_ref, b_ref, o_ref):
    c = c_ref[...]
    sc = c * _sigmoid(c)
    o_ref[...] = jnp.dot(sc, w_ref[...], preferred_element_type=F32,
                         precision=lax.Precision.HIGHEST) + b_ref[...]


def _ada(c, w_ada, b_ada):
    B = c.shape[0]
    n_out = w_ada.shape[1]
    bn = 1536
    return pl.pallas_call(
        _ada_kernel,
        grid=(n_out // bn,),
        in_specs=[pl.BlockSpec((B, D_MODEL), lambda n: (0, 0)),
                  pl.BlockSpec((D_MODEL, bn), lambda n: (0, n)),
                  pl.BlockSpec((1, bn), lambda n: (0, n))],
        out_specs=pl.BlockSpec((B, bn), lambda n: (0, n)),
        out_shape=jax.ShapeDtypeStruct((B, n_out), F32),
        compiler_params=_cparams(1),
        name="ada",
    )(c, w_ada, b_ada.reshape(1, n_out))


def _inproj_kernel(x_ref, mod_ref, g_ref, w_ref, b_ref,
                   qm_ref, km_ref, vmT_ref, kmean_ref, qs_ref, ks_ref, vsT_ref):
    xf = x_ref[0]
    shift = mod_ref[0, 0:1, :]
    scale = mod_ref[0, 1:2, :]
    h = _rmsnorm_rows(xf, g_ref[...]) * (1.0 + scale) + shift
    proj = jnp.dot(h.astype(BF16), w_ref[...], preferred_element_type=F32) + b_ref[...]
    i0, i1, i2 = D_MOBA, 2 * D_MOBA, 3 * D_MOBA
    i3 = i2 + D_SWA_Q
    i4 = i3 + D_SWA_KV
    qm_ref[0] = proj[:, :i0].astype(BF16)
    km = proj[:, i0:i1]
    km_ref[0] = km.astype(BF16)
    kmean_ref[0, 0] = jnp.broadcast_to(jnp.mean(km, axis=0, keepdims=True), (8, D_MOBA))
    vmT = proj[:, i1:i2].T.astype(BF16)
    vmT_ref[0, :, 0] = vmT.reshape(D_MOBA // LANES, LANES, vmT.shape[1])
    qs_ref[0] = proj[:, i2:i3].astype(BF16)
    ks_ref[0] = proj[:, i3:i4].astype(BF16)
    vsT_ref[0] = proj[:, i4:].T.astype(BF16)


def _inproj(x, mod6, g_attn, w_in_bf, b_in):
    B, S, D = x.shape
    tm = MOBA_BLOCK
    nb = S // tm
    n_pairs = D_MOBA // LANES
    out_shape = (
        jax.ShapeDtypeStruct((B, S, D_MOBA), BF16),
        jax.ShapeDtypeStruct((B, S, D_MOBA), BF16),
        jax.ShapeDtypeStruct((B, n_pairs, nb, LANES, tm), BF16),
        jax.ShapeDtypeStruct((B, nb, 8, D_MOBA), F32),
        jax.ShapeDtypeStruct((B, S, D_SWA_Q), BF16),
        jax.ShapeDtypeStruct((B, S, D_SWA_KV), BF16),
        jax.ShapeDtypeStruct((B, D_SWA_KV, S), BF16),
    )
    return pl.pallas_call(
        _inproj_kernel,
        grid=(B, nb),
        in_specs=[pl.BlockSpec((1, tm, D), lambda b, i: (b, i, 0)),
                  pl.BlockSpec((1, 6, D), lambda b, i: (b, 0, 0)),
                  pl.BlockSpec((1, D), lambda b, i: (0, 0)),
                  pl.BlockSpec((D, D_IN), lambda b, i: (0, 0)),
                  pl.BlockSpec((1, D_IN), lambda b, i: (0, 0))],
        out_specs=(pl.BlockSpec((1, tm, D_MOBA), lambda b, i: (b, i, 0)),
                   pl.BlockSpec((1, tm, D_MOBA), lambda b, i: (b, i, 0)),
                   pl.BlockSpec((1, n_pairs, 1, LANES, tm), lambda b, i: (b, 0, i, 0, 0)),
                   pl.BlockSpec((1, 1, 8, D_MOBA), lambda b, i: (b, i, 0, 0)),
                   pl.BlockSpec((1, tm, D_SWA_Q), lambda b, i: (b, i, 0)),
                   pl.BlockSpec((1, tm, D_SWA_KV), lambda b, i: (b, i, 0)),
                   pl.BlockSpec((1, D_SWA_KV, tm), lambda b, i: (b, 0, i))),
        out_shape=out_shape,
        compiler_params=_cparams(2),
        name="inproj",
    )(x, mod6, g_attn, w_in_bf, b_in)


def _rel_bucket_static(n):
    n = max(n, 0)
    return n if n < REL_MAX_EXACT else REL_MAX_EXACT + sum(n >= t for t in _REL_THRESHOLDS)


def _rel_bias_values(dist, table_ref, h, d_min, d_max):
    lo, hi = _rel_bucket_static(d_min), _rel_bucket_static(d_max)
    n = jnp.maximum(dist, 0)
    val = jnp.full(n.shape, table_ref[lo, h], F32)
    for bkt in range(lo + 1, hi + 1):
        start = bkt if bkt <= REL_MAX_EXACT else _REL_THRESHOLDS[bkt - REL_MAX_EXACT - 1]
        val = jnp.where(n >= start, table_ref[bkt, h], val)
    return val


def _moba_bias_kernel(table_ref, o_ref):
    h = pl.program_id(0)
    L = MOBA_BLOCK
    key = lax.broadcasted_iota(jnp.int32, (L, L), 0)
    qry = lax.broadcasted_iota(jnp.int32, (L, L), 1)
    for dlt in range(o_ref.shape[1]):
        dist = dlt * L + qry - key
        val = _rel_bias_values(dist, table_ref, h, dlt * L - (L - 1), dlt * L + (L - 1))
        o_ref[0, dlt] = jnp.where(dist >= 0, val, NEG) if dlt == 0 else val


def _moba_bias(rel_table, nb):
    L = MOBA_BLOCK
    return pl.pallas_call(
        _moba_bias_kernel,
        grid=(N_HEADS_MOBA,),
        in_specs=[pl.BlockSpec(memory_space=pltpu.SMEM)],
        out_specs=pl.BlockSpec((1, nb, L, L), lambda h: (h, 0, 0, 0)),
        out_shape=jax.ShapeDtypeStruct((N_HEADS_MOBA, nb, L, L), F32),
        compiler_params=_cparams(1),
        name="moba_bias",
    )(rel_table)


def _swa_bias_kernel(table_ref, o_ref):
    h = N_HEADS_MOBA + pl.program_id(0)
    W = SWA_BLOCK
    key = lax.broadcasted_iota(jnp.int32, (2 * W, W), 0)
    qry = lax.broadcasted_iota(jnp.int32, (2 * W, W), 1)
    dist = qry + W - key
    val = _rel_bias_values(dist, table_ref, h, 0, SWA_WINDOW - 1)
    o_ref[...] = jnp.where((dist >= 0) & (dist < SWA_WINDOW), val, NEG)


def _swa_bias(rel_table):
    W = SWA_BLOCK
    return pl.pallas_call(
        _swa_bias_kernel,
        grid=(N_HEADS_SWA,),
        in_specs=[pl.BlockSpec(memory_space=pltpu.SMEM)],
        out_specs=pl.BlockSpec((2 * W, W), lambda h: (0, h)),
        out_shape=jax.ShapeDtypeStruct((2 * W, N_HEADS_SWA * W), F32),
        compiler_params=_cparams(1),
        name="swa_bias",
    )(rel_table)


def _gate_kernel(q_ref, kmean_ref, sel_ref):
    S = q_ref.shape[1]
    nb = kmean_ref.shape[1]
    H = N_HEADS_MOBA
    q = q_ref[0]
    km = kmean_ref[0, :, 0, :]
    km_t = jnp.concatenate([km] * H, axis=0)
    row_h = lax.broadcasted_iota(jnp.int32, (H * nb, D_MOBA), 0) // nb
    col_h = lax.broadcasted_iota(jnp.int32, (H * nb, D_MOBA), 1) // HEAD_DIM
    km_t = jnp.where(row_h == col_h, km_t, 0.0)
    km_hi = km_t.astype(BF16)
    km_lo = (km_t - km_hi.astype(F32)).astype(BF16)
    g_all = _dot_nt(km_hi, q) + _dot_nt(km_lo, q)
    qblk = lax.broadcasted_iota(jnp.int32, (nb, S), 1) // MOBA_BLOCK
    jidx = lax.broadcasted_iota(jnp.int32, (nb, S), 0)
    past = jidx < qblk
    for h in range(H):
        g = g_all[h * nb:(h + 1) * nb, :]
        cnt = jnp.zeros((nb, S), jnp.int32)
        for jp in range(nb):
            gj = g[jp:jp + 1, :]
            ahead = (gj > g) | ((gj == g) & (jp < jidx))
            cnt = cnt + ((jp < qblk) & ahead).astype(jnp.int32)
        sel = (past & (cnt < MOBA_TOPK)).astype(F32)
        for j in range(nb):
            sel_ref[0, h, j] = sel[j:j + 1, :]


def _gate(qm, kmean):
    B, S, _ = qm.shape
    nb = kmean.shape[1]
    return pl.pallas_call(
        _gate_kernel,
        grid=(B,),
        in_specs=[pl.BlockSpec((1, S, D_MOBA), lambda b: (b, 0, 0)),
                  pl.BlockSpec((1, nb, 8, D_MOBA), lambda b: (b, 0, 0, 0))],
        out_specs=pl.BlockSpec((1, N_HEADS_MOBA, nb, 1, S), lambda b: (b, 0, 0, 0, 0)),
        out_shape=jax.ShapeDtypeStruct((B, N_HEADS_MOBA, nb, 1, S), F32),
        compiler_params=_cparams(1),
        name="moba_gate",
    )(qm, kmean)


def _moba_kernel(q_ref, k_ref, vT_ref, sel_ref, bias_ref, o_ref, s_buf):
    c = pl.program_id(2)
    L = MOBA_BLOCK
    nb = k_ref.shape[1] // L

    def attend(half, n_past):
        q = q_ref[0, half]
        lane = lax.broadcasted_iota(jnp.int32, q.shape, 1)
        zero = jnp.zeros_like(q)
        q_heads = [jnp.where(lane < HEAD_DIM, q, zero), jnp.where(lane >= HEAD_DIM, q, zero)]
        outs = []
        for a in range(HEADS_PER_LANE_BLOCK):
            chosen = [None if j == n_past else sel_ref[0, a, j, half:half + 1, :] > 0.5
                      for j in range(n_past + 1)]
            m = None
            for j in range(n_past + 1):
                s = _dot_nt(k_ref[0, j * L:(j + 1) * L, :], q_heads[a]) + bias_ref[a, n_past - j]
                if n_past > 0:
                    s_buf[half, a, j * L:(j + 1) * L, :] = s
                mj = jnp.max(s, axis=0, keepdims=True)
                if chosen[j] is not None:
                    mj = jnp.where(chosen[j], mj, NEG)
                m = mj if m is None else jnp.maximum(m, mj)
            l = None
            acc = None
            for j in range(n_past + 1):
                shift = m if chosen[j] is None else jnp.where(chosen[j], m, -NEG)
                p = jnp.exp((s_buf[half, a, j * L:(j + 1) * L, :] if n_past > 0 else s) - shift)
                lj = jnp.sum(p, axis=0, keepdims=True)
                vT = vT_ref[0, 0, j, a * HEAD_DIM:(a + 1) * HEAD_DIM, :]
                oj = jnp.dot(vT, p.astype(BF16), preferred_element_type=F32)
                l = lj if l is None else l + lj
                acc = oj if acc is None else acc + oj
            outs.append(acc * (1.0 / l))
        o_ref[0, half] = jnp.concatenate(outs, axis=0).T.astype(BF16)

    def both(c_static):
        for half in range(2):
            attend(half, c_static + half * (nb // 2))

    for c_static in range(nb // 2):
        pl.when(c == c_static)(functools.partial(both, c_static))


def _moba(qm, km, vmT, sel, bias):
    B, S, _ = qm.shape
    L = MOBA_BLOCK
    nb = S // L
    n_pairs = D_MOBA // LANES
    half_s = S // 2
    out = pl.pallas_call(
        _moba_kernel,
        grid=(n_pairs, B, nb // 2),
        in_specs=[pl.BlockSpec((1, 2, L, LANES), lambda p, b, c: (b, 0, c, p)),
                  pl.BlockSpec((1, S, LANES), lambda p, b, c: (b, 0, p)),
                  pl.BlockSpec((1, 1, nb, LANES, L), lambda p, b, c: (b, p, 0, 0, 0)),
                  pl.BlockSpec((1, HEADS_PER_LANE_BLOCK, nb, 2, L), lambda p, b, c: (b, p, 0, 0, c)),
                  pl.BlockSpec((HEADS_PER_LANE_BLOCK, nb, L, L), lambda p, b, c: (p, 0, 0, 0))],
        out_specs=pl.BlockSpec((1, 2, L, LANES), lambda p, b, c: (b, 0, c, p)),
        out_shape=jax.ShapeDtypeStruct((B, 2, half_s, D_MOBA), BF16),
        scratch_shapes=[pltpu.VMEM((2, HEADS_PER_LANE_BLOCK, S, L), F32)],
        compiler_params=_cparams(3),
        name="moba_attn",
    )(qm.reshape(B, 2, half_s, D_MOBA), km, vmT, sel.reshape(B, N_HEADS_MOBA, nb, 2, half_s), bias)
    return out.reshape(B, S, D_MOBA)


def _swa_kernel(sink_ref, q_ref, kp0_ref, kc0_ref, kp1_ref, kc1_ref, vp0_ref, vc0_ref, vp1_ref, vc1_ref,
                bias_ref, o_ref):
    c = pl.program_id(1)
    _swa_block(sink_ref, q_ref[0, 0], kp0_ref[0], kc0_ref[0], vp0_ref[0], vc0_ref[0], bias_ref,
               o_ref.at[0, 0], c > 0)
    _swa_block(sink_ref, q_ref[0, 1], kp1_ref[0], kc1_ref[0], vp1_ref[0], vc1_ref[0], bias_ref,
               o_ref.at[0, 1], None)


def _swa_block(sink_ref, q, k_prev, k_cur, vT_prev, vT_cur, bias_ref, o_ref, has_prev):
    W = SWA_BLOCK
    group = N_HEADS_SWA // N_KV_SWA
    kband = jnp.concatenate([k_prev, k_cur], axis=0)
    vbandT = jnp.concatenate([vT_prev, vT_cur], axis=1)
    q_rows = []
    for h in range(N_HEADS_SWA):
        qp = q[:, (h % group) * LANES:(h % group + 1) * LANES]
        lane = lax.broadcasted_iota(jnp.int32, qp.shape, 1)
        in_half = (lane >= HEAD_DIM) if h // group else (lane < HEAD_DIM)
        q_rows.append(jnp.where(in_half, qp, jnp.zeros_like(qp)))
    q_all = jnp.concatenate(q_rows, axis=0)
    sT = _dot_nt(kband, q_all) + bias_ref[...]
    if has_prev is not None:
        key = lax.broadcasted_iota(jnp.int32, sT.shape, 0)
        sT = jnp.where((key >= W) | has_prev, sT, NEG)
    sink = sink_ref[...]
    m = jnp.maximum(jnp.max(sT, axis=0, keepdims=True), sink)
    pr = jnp.exp(sT - m)
    inv_l = 1.0 / (jnp.sum(pr, axis=0, keepdims=True) + jnp.exp(sink - m))
    pr = pr.astype(BF16)
    cols = group * W
    outs = [jnp.dot(vbandT[g * HEAD_DIM:(g + 1) * HEAD_DIM, :], pr[:, g * cols:(g + 1) * cols],
                    preferred_element_type=F32) * inv_l[:, g * cols:(g + 1) * cols]
            for g in range(N_KV_SWA)]
    heads = [outs[h // group][:, (h % group) * W:(h % group + 1) * W] for h in range(N_HEADS_SWA)]
    o_ref[...] = jnp.concatenate(heads, axis=0).T.astype(BF16)


def _swa(sink_row, qs, ks, vsT, bias):
    B, S, _ = qs.shape
    W = SWA_BLOCK
    nq = S // W
    far = nq // 2
    prev = lambda c: jnp.maximum(c - 1, 0)
    k_spec = lambda f: pl.BlockSpec((1, W, D_SWA_KV), lambda b, c: (b, f(c), 0))
    v_spec = lambda f: pl.BlockSpec((1, D_SWA_KV, W), lambda b, c: (b, 0, f(c)))
    blocks = [prev, lambda c: c, lambda c: c + far - 1, lambda c: c + far]
    out = pl.pallas_call(
        _swa_kernel,
        grid=(B, far),
        in_specs=[pl.BlockSpec((1, N_HEADS_SWA * W), lambda b, c: (0, 0)),
                  pl.BlockSpec((1, 2, W, D_SWA_Q), lambda b, c: (b, 0, c, 0))]
                 + [k_spec(f) for f in blocks] + [v_spec(f) for f in blocks]
                 + [pl.BlockSpec((2 * W, N_HEADS_SWA * W), lambda b, c: (0, 0))],
        out_specs=pl.BlockSpec((1, 2, W, D_SWA_Q), lambda b, c: (b, 0, c, 0)),
        out_shape=jax.ShapeDtypeStruct((B, 2, S // 2, D_SWA_Q), BF16),
        compiler_params=_cparams(2),
        name="swa_attn",
    )(sink_row, qs.reshape(B, 2, S // 2, D_SWA_Q), ks, ks, ks, ks, vsT, vsT, vsT, vsT, bias)
    return out.reshape(B, S, D_SWA_Q)


def _outproj_kernel(x_ref, om_ref, os_ref, wm_ref, ws_ref, b_ref, mod_ref, g_ref, wrT_ref, br_ref,
                    x1_ref, hrt_ref, idx_ref, w_ref, rank_ref, counts_ref, carry_ref):
    attn = (jnp.dot(om_ref[...], wm_ref[...], preferred_element_type=F32)
            + jnp.dot(os_ref[...], ws_ref[...], preferred_element_type=F32) + b_ref[...])
    x1 = x_ref[...] + mod_ref[0, 2:3, :] * attn
    x1_ref[...] = x1
    h = _rmsnorm_rows(x1, g_ref[...]) * (1.0 + mod_ref[0, 4:5, :]) + mod_ref[0, 3:4, :]
    _store_row_tiles(hrt_ref, h)
    _route(h, wrT_ref, br_ref, idx_ref, w_ref, rank_ref, counts_ref, carry_ref)


def _store_row_tiles(ref, val):
    rows = val.shape[0]
    for c in range(ROW_CHUNKS):
        ref[pl.ds(c, rows, stride=ROW_CHUNKS), :] = val[:, c * LANES:(c + 1) * LANES]


def _load_row_tiles(ref, rows, lead=(), first_row=0):
    idx = tuple(lead)
    base = first_row * ROW_CHUNKS
    return jnp.concatenate([ref[idx + (pl.ds(base + c, rows, stride=ROW_CHUNKS), slice(None))]
                            for c in range(ROW_CHUNKS)], axis=1)


def _outproj(x2d, om, os_, w_out_m, w_out_s, b_out, mod6, g_ffn, w_routerT, b_router_col, seq_len):
    N, D = x2d.shape
    tm = 512
    per_seq = seq_len // tm
    row = lambda t: (t, 0)
    col = lambda t: (0, t)
    const = lambda t: (0, 0)
    return pl.pallas_call(
        _outproj_kernel,
        grid=(N // tm,),
        in_specs=[pl.BlockSpec((tm, D), row),
                  pl.BlockSpec((tm, D_MOBA), row),
                  pl.BlockSpec((tm, D_SWA_Q), row),
                  pl.BlockSpec((D_MOBA, D), const),
                  pl.BlockSpec((D_SWA_Q, D), const),
                  pl.BlockSpec((1, D), const),
                  pl.BlockSpec((1, 6, D), lambda t: (t // per_seq, 0, 0)),
                  pl.BlockSpec((1, D), const),
                  pl.BlockSpec((N_EXPERTS, D), const),
                  pl.BlockSpec((N_EXPERTS, 1), const)],
        out_specs=(pl.BlockSpec((tm, D), row),
                   pl.BlockSpec((tm * ROW_CHUNKS, LANES), row),
                   pl.BlockSpec((TOP_K, tm), col),
                   pl.BlockSpec((TOP_K, tm), col),
                   pl.BlockSpec((TOP_K, tm), col),
                   pl.BlockSpec((N_EXPERTS, LANES), const)),
        out_shape=(jax.ShapeDtypeStruct((N, D), F32),
                   jax.ShapeDtypeStruct((N * ROW_CHUNKS, LANES), F32),
                   jax.ShapeDtypeStruct((TOP_K, N), jnp.int32),
                   jax.ShapeDtypeStruct((TOP_K, N), F32),
                   jax.ShapeDtypeStruct((TOP_K, N), jnp.int32),
                   jax.ShapeDtypeStruct((N_EXPERTS, LANES), jnp.int32)),
        scratch_shapes=[pltpu.VMEM((N_EXPERTS, LANES), F32)],
        compiler_params=_cparams(1),
        name="outproj_route",
    )(x2d, om, os_, w_out_m, w_out_s, b_out, mod6, g_ffn, w_routerT, b_router_col)


def _route(h, wT_ref, b_ref, idx_ref, w_ref, rank_ref, counts_ref, carry_ref):
    @pl.when(pl.program_id(0) == 0)
    def _():
        carry_ref[...] = jnp.zeros_like(carry_ref)

    w = wT_ref[...]
    w_hi = w.astype(BF16)
    w_lo = (w - w_hi.astype(F32)).astype(BF16)
    h_hi = h.astype(BF16)
    h_lo = (h - h_hi.astype(F32)).astype(BF16)
    by_h_hi = _dot_nt(jnp.concatenate([w_hi, w_lo], axis=0), h_hi)
    logits = (by_h_hi[:N_EXPERTS] + by_h_hi[N_EXPERTS:] + _dot_nt(w_hi, h_lo)) + b_ref[...]
    tm = logits.shape[1]
    eidx = lax.broadcasted_iota(jnp.int32, logits.shape, 0)
    vals, idxs = [], []
    cur = logits
    for _ in range(TOP_K):
        m = jnp.max(cur, axis=0, keepdims=True)
        am = jnp.min(jnp.where(cur == m, eidx, N_EXPERTS), axis=0, keepdims=True)
        vals.append(m)
        idxs.append(am)
        cur = jnp.where(eidx == am, -jnp.inf, cur)
    exps = [jnp.exp(v - vals[0]) for v in vals]
    inv = 1.0 / functools.reduce(lambda a, b: a + b, exps)
    idx_ref[...] = jnp.concatenate(idxs, axis=0)
    w_ref[...] = jnp.concatenate([e * inv for e in exps], axis=0)

    member = functools.reduce(lambda a, b: a | b, [eidx == am for am in idxs])
    earlier = (lax.broadcasted_iota(jnp.int32, (tm, tm), 0)
               < lax.broadcasted_iota(jnp.int32, (tm, tm), 1)).astype(BF16)
    before = jnp.dot(member.astype(BF16), earlier, preferred_element_type=F32) + carry_ref[:, 0:1]
    rank_ref[...] = jnp.concatenate(
        [jnp.sum(jnp.where(eidx == am, before, 0.0), axis=0, keepdims=True) for am in idxs],
        axis=0).astype(jnp.int32)
    total = carry_ref[...] + jnp.sum(member.astype(F32), axis=1, keepdims=True)
    carry_ref[...] = total
    counts_ref[...] = total.astype(jnp.int32)


def _expert_mlp(h_bf, wgu, bgu, wdn, bdn):
    gu = jnp.dot(h_bf, wgu, preferred_element_type=F32) + bgu
    g = jnp.minimum(gu[:, :D_FF], SWIGLU_LIMIT)
    u = jnp.clip(gu[:, D_FF:], -SWIGLU_LIMIT, SWIGLU_LIMIT)
    act = (u + 1.0) * (g * _sigmoid(SWIGLU_ALPHA * g))
    return jnp.dot(act.astype(BF16), wdn, preferred_element_type=F32) + bdn


def _tile_plan(counts, n_tiles_max):
    tiles_per = (counts + MOE_TILE - 1) // MOE_TILE
    tile_end = jnp.cumsum(tiles_per)
    tile_start = tile_end - tiles_per
    t_ids = jnp.arange(n_tiles_max, dtype=jnp.int32)
    texp = jnp.minimum(jnp.sum((t_ids[:, None] >= tile_end[None, :]).astype(jnp.int32), axis=1),
                       N_EXPERTS - 1)
    onehot = (texp[:, None] == jnp.arange(N_EXPERTS, dtype=jnp.int32)[None, :]).astype(jnp.int32)
    local = t_ids - onehot @ tile_start
    nvalid = jnp.clip(onehot @ counts - local * MOE_TILE, 0, MOE_TILE)
    n_active = tile_end[-1]
    src_tile = jnp.minimum(t_ids, n_active - 1)
    last_tile = jnp.where(counts > 0, tile_end - 1, -1)
    tail = n_active + jnp.arange(N_EXPERTS, dtype=jnp.int32)
    zero_tiles = jnp.concatenate([last_tile, jnp.where(tail < n_tiles_max, tail, -1)])
    ids = jnp.arange(N_EXPERTS, dtype=jnp.int32)
    later = jnp.where((ids[None, :] > ids[:, None]) & (counts[None, :] > 0), ids[None, :], N_EXPERTS)
    next_expert = jnp.min(later, axis=1)
    next_expert = jnp.where(next_expert < N_EXPERTS, next_expert, -1)
    return (texp.astype(jnp.int32), nvalid.astype(jnp.int32), src_tile.astype(jnp.int32),
            (tile_start * MOE_TILE).astype(jnp.int32), zero_tiles.astype(jnp.int32),
            next_expert.astype(jnp.int32))


def _row_copy(src, src_row, dst, dst_row, sem):
    return pltpu.make_async_copy(src.at[pl.ds(pl.multiple_of(src_row * ROW_CHUNKS, ROW_CHUNKS), ROW_CHUNKS)],
                                 dst.at[pl.ds(pl.multiple_of(dst_row * ROW_CHUNKS, ROW_CHUNKS), ROW_CHUNKS)],
                                 sem)


DISPATCH_SLOTS = 3


def _dispatch_kernel(zero_tiles_ref, pos_ref, h_hbm, x_hbm, zbuf, hbuf, zsem, load_sem, row_sem):
    t = pl.program_id(0)
    n_steps = pl.num_programs(0)
    tile_rows = MOE_TILE * ROW_CHUNKS
    block_rows = DISPATCH_TOKENS * ROW_CHUNKS

    def block_load(step, slot):
        start = pl.multiple_of(step * block_rows, block_rows)
        return pltpu.make_async_copy(h_hbm.at[pl.ds(start, block_rows)], hbuf.at[slot], load_sem.at[slot])

    def rows_done(slot):
        for k in range(TOP_K):
            pltpu.make_async_copy(hbuf.at[slot], x_hbm.at[pl.ds(0, block_rows)], row_sem.at[slot]).wait()

    def zero_copy(i):
        start = pl.multiple_of(zero_tiles_ref[i] * tile_rows, tile_rows)
        return pltpu.make_async_copy(zbuf, x_hbm.at[pl.ds(start, tile_rows)], zsem)

    @pl.when(t == 0)
    def _():
        zbuf[...] = jnp.zeros_like(zbuf)
        for i in range(2 * N_EXPERTS):
            @pl.when(zero_tiles_ref[i] >= 0)
            def _():
                zero_copy(i).start()
        block_load(0, 0).start()
        if DISPATCH_SLOTS > 2:
            @pl.when(n_steps > 1)
            def _():
                block_load(1, 1).start()
        for i in range(2 * N_EXPERTS):
            @pl.when(zero_tiles_ref[i] >= 0)
            def _():
                zero_copy(i).wait()

    slot = t % DISPATCH_SLOTS
    block_load(t, slot).wait()

    def start_rows(r, carry):
        for k in range(TOP_K):
            _row_copy(hbuf.at[slot], r, x_hbm, pos_ref[0, 0, k * DISPATCH_TOKENS + r],
                      row_sem.at[slot]).start(priority=k % 2)
        return carry
    lax.fori_loop(0, DISPATCH_TOKENS, start_rows, 0, unroll=8)

    refill = (t + 2) % DISPATCH_SLOTS

    @pl.when(t > 0)
    def _():
        rows_done(refill)

    @pl.when(t + 2 < n_steps)
    def _():
        block_load(t + 2, refill).start()

    @pl.when(t == n_steps - 1)
    def _():
        rows_done(slot)


def _dispatch(zero_tiles, pos_tiles, h_rt, n_rows):
    n_steps = pos_tiles.shape[0]
    grid_spec = pltpu.PrefetchScalarGridSpec(
        num_scalar_prefetch=1,
        grid=(n_steps,),
        in_specs=[pl.BlockSpec((1, 1, TOP_K * DISPATCH_TOKENS), lambda t, zt: (t, 0, 0),
                               memory_space=pltpu.SMEM),
                  pl.BlockSpec(memory_space=pl.ANY)],
        out_specs=pl.BlockSpec(memory_space=pl.ANY),
        scratch_shapes=[pltpu.VMEM((MOE_TILE * ROW_CHUNKS, LANES), F32),
                        pltpu.VMEM((DISPATCH_SLOTS, DISPATCH_TOKENS * ROW_CHUNKS, LANES), F32),
                        pltpu.SemaphoreType.DMA(()),
                        pltpu.SemaphoreType.DMA((DISPATCH_SLOTS,)),
                        pltpu.SemaphoreType.DMA((DISPATCH_SLOTS,))],
    )
    return pl.pallas_call(
        _dispatch_kernel,
        grid_spec=grid_spec,
        out_shape=jax.ShapeDtypeStruct((n_rows * ROW_CHUNKS, LANES), F32),
        compiler_params=_cparams(1),
        name="moe_dispatch",
    )(zero_tiles, pos_tiles, h_rt)


def _experts_kernel(texp_ref, nvalid_ref, src_ref, next_ref, x_ref, wgu_hbm, bgu_ref, wdn_hbm, bdn_ref,
                    y_ref, wgu_f32, wdn_f32, wgu_bf, wdn_bf, wsem):
    t = pl.program_id(0)
    nv = nvalid_ref[t]
    e = texp_ref[t]
    prev_e = texp_ref[jnp.maximum(t - 1, 0)]

    def weight_loads(expert):
        return (pltpu.make_async_copy(wgu_hbm.at[expert], wgu_f32, wsem.at[0]),
                pltpu.make_async_copy(wdn_hbm.at[expert], wdn_f32, wsem.at[1]))

    @pl.when((nv > 0) & ((t == 0) | (e != prev_e)))
    def _():
        @pl.when(t == 0)
        def _():
            for cp in weight_loads(e):
                cp.start()
        for cp in weight_loads(e):
            cp.wait()
        rows = 256
        def cast_gu(i, carry):
            r0 = pl.multiple_of(i * rows, rows)
            wgu_bf[pl.ds(r0, rows), :] = wgu_f32[pl.ds(r0, rows), :].astype(BF16)
            return carry
        lax.fori_loop(0, D_MODEL // rows, cast_gu, 0)
        def cast_dn(i, carry):
            r0 = pl.multiple_of(i * rows, rows)
            wdn_bf[pl.ds(r0, rows), :] = wdn_f32[pl.ds(r0, rows), :].astype(BF16)
            return carry
        lax.fori_loop(0, D_FF // rows, cast_dn, 0)
        nxt = next_ref[e]

        @pl.when(nxt >= 0)
        def _():
            for cp in weight_loads(nxt):
                cp.start()

    @pl.when(nv > 0)
    def _():
        x = _load_row_tiles(x_ref, MOE_TILE).astype(BF16)
        _store_row_tiles(y_ref, _expert_mlp(x, wgu_bf[...], bgu_ref[0], wdn_bf[...], bdn_ref[0]))

    @pl.when(nv == 0)
    def _():
        y_ref[...] = jnp.zeros_like(y_ref)


def _experts(texp, nvalid, src_tile, next_expert, x_sorted, w_gate_up, b_gate_up, w_down, b_down):
    n_tiles = texp.shape[0]
    D = D_MODEL
    tile_rows = MOE_TILE * ROW_CHUNKS
    grid_spec = pltpu.PrefetchScalarGridSpec(
        num_scalar_prefetch=4,
        grid=(n_tiles,),
        in_specs=[pl.BlockSpec((tile_rows, LANES), lambda t, te, nv, st, nx: (st[t], 0)),
                  pl.BlockSpec(memory_space=pl.ANY),
                  pl.BlockSpec((1, 1, 2 * D_FF), lambda t, te, nv, st, nx: (te[t], 0, 0)),
                  pl.BlockSpec(memory_space=pl.ANY),
                  pl.BlockSpec((1, 1, D), lambda t, te, nv, st, nx: (te[t], 0, 0))],
        out_specs=pl.BlockSpec((tile_rows, LANES), lambda t, te, nv, st, nx: (t, 0)),
        scratch_shapes=[pltpu.VMEM((D, 2 * D_FF), F32),
                        pltpu.VMEM((D_FF, D), F32),
                        pltpu.VMEM((D, 2 * D_FF), BF16),
                        pltpu.VMEM((D_FF, D), BF16),
                        pltpu.SemaphoreType.DMA((2,))],
    )
    return pl.pallas_call(
        _experts_kernel,
        grid_spec=grid_spec,
        out_shape=jax.ShapeDtypeStruct(x_sorted.shape, F32),
        compiler_params=_cparams(1),
        name="moe_experts",
    )(texp, nvalid, src_tile, next_expert, x_sorted, w_gate_up, b_gate_up.reshape(N_EXPERTS, 1, 2 * D_FF),
      w_down, b_down.reshape(N_EXPERTS, 1, D))


def _combine_kernel(pos_ref, pos_next_ref, y_hbm, w_ref, x1_ref, mod_ref, g_ref, o_ref, ybuf, sem):
    t = pl.program_id(0)
    n_steps = pl.num_programs(0)
    slot = t % 2

    def start_gather(idx_ref, s):
        def body(pair, carry):
            for queue in range(2):
                i = 2 * pair + queue
                _row_copy(y_hbm, idx_ref[0, 0, i], ybuf.at[s], i, sem.at[s]).start(priority=queue)
            return carry
        lax.fori_loop(0, TOP_K * COMBINE_TOKENS // 2, body, 0, unroll=4)

    @pl.when(t == 0)
    def _():
        start_gather(pos_ref, 0)

    @pl.when(t + 1 < n_steps)
    def _():
        start_gather(pos_next_ref, 1 - slot)

    slot_rows = TOP_K * COMBINE_TOKENS * ROW_CHUNKS
    pltpu.make_async_copy(y_hbm.at[pl.ds(0, slot_rows)], ybuf.at[slot], sem.at[slot]).wait()

    w = w_ref[...]
    acc = w[:, 0:1] * _load_row_tiles(ybuf, COMBINE_TOKENS, (slot,))
    for k in range(1, TOP_K):
        acc = acc + w[:, k:k + 1] * _load_row_tiles(ybuf, COMBINE_TOKENS, (slot,), k * COMBINE_TOKENS)
    x2 = x1_ref[...] + mod_ref[0, 5:6, :] * acc
    o_ref[...] = _rmsnorm_rows(x2, g_ref[...])


def _combine(pos_tiles, y_sorted, top_w, x1, mod6, g_final, seq_len):
    N, D = x1.shape
    tm = COMBINE_TOKENS
    n_steps = N // tm
    per_seq = seq_len // tm
    return pl.pallas_call(
        _combine_kernel,
        grid=(n_steps,),
        in_specs=[pl.BlockSpec((1, 1, TOP_K * tm), lambda t: (t, 0, 0), memory_space=pltpu.SMEM),
                  pl.BlockSpec((1, 1, TOP_K * tm), lambda t: (jnp.minimum(t + 1, n_steps - 1), 0, 0),
                               memory_space=pltpu.SMEM),
                  pl.BlockSpec(memory_space=pl.ANY),
                  pl.BlockSpec((tm, TOP_K), lambda t: (t, 0)),
                  pl.BlockSpec((tm, D), lambda t: (t, 0)),
                  pl.BlockSpec((1, 6, D), lambda t: (t // per_seq, 0, 0)),
                  pl.BlockSpec((1, D), lambda t: (0, 0))],
        out_specs=pl.BlockSpec((tm, D), lambda t: (t, 0)),
        out_shape=jax.ShapeDtypeStruct((N, D), F32),
        scratch_shapes=[pltpu.VMEM((2, TOP_K * tm * ROW_CHUNKS, LANES), F32),
                        pltpu.SemaphoreType.DMA((2,))],
        compiler_params=_cparams(1),
        name="moe_combine",
    )(pos_tiles, pos_tiles, y_sorted, top_w, x1, mod6, g_final)


def _swa_head_perm():
    group = N_HEADS_SWA // N_KV_SWA
    cols = []
    for p in range(group):
        for half in range(N_KV_SWA):
            hd = p + group * half
            cols.extend(range(hd * HEAD_DIM, (hd + 1) * HEAD_DIM))
    return jnp.asarray(cols, jnp.int32)


def kernel(x, c, g_attn, w_ada, b_ada, w_in, b_in, w_out, b_out, rel_table, sinks, g_ffn, w_router,
           b_router, w_gate_up, b_gate_up, w_down, b_down, g_final):
    B, S, D = x.shape
    assert w_ada.shape[0] == 1, "the final norm is fused into the last layer; one layer supported"
    l = 0
    nb = S // MOBA_BLOCK
    perm = _swa_head_perm()
    i2 = 3 * D_MOBA
    i3 = i2 + D_SWA_Q
    col_scale = jnp.concatenate([jnp.full((D_MOBA,), ATTN_SCALE, F32), jnp.ones((2 * D_MOBA,), F32),
                                 jnp.full((D_SWA_Q,), ATTN_SCALE, F32), jnp.ones((2 * D_SWA_KV,), F32)])
    col_order = jnp.concatenate([jnp.arange(i2, dtype=jnp.int32), i2 + perm,
                                 jnp.arange(i3, D_IN, dtype=jnp.int32)])
    moba_bias = _moba_bias(rel_table, nb)
    swa_bias = _swa_bias(rel_table)

    mod6 = _ada(c, w_ada[l], b_ada[l]).reshape(B, 6, D)
    w_in_l = ((w_in[l] * col_scale)[:, col_order]).astype(BF16)
    b_in_l = ((b_in[l] * col_scale)[col_order]).reshape(1, D_IN)
    qm, km, vmT, kmean, qs, ks, vsT = _inproj(x, mod6, g_attn[l].reshape(1, D), w_in_l, b_in_l)
    sel = _gate(qm, kmean)
    o_m = _moba(qm, km, vmT, sel, moba_bias)
    sink_row = jnp.repeat(sinks[l], SWA_BLOCK).reshape(1, N_HEADS_SWA * SWA_BLOCK)
    o_s = _swa(sink_row, qs, ks, vsT, swa_bias)
    w_out_m = w_out[l, :D_MOBA].astype(BF16)
    w_out_s = w_out[l, D_MOBA:].astype(BF16)
    N = B * S
    x1, h_rt, top_idx, top_w, rank, counts = _outproj(
        x.reshape(N, D), o_m.reshape(N, D_MOBA), o_s.reshape(N, D_SWA_Q), w_out_m, w_out_s,
        b_out[l].reshape(1, D), mod6, g_ffn[l].reshape(1, D), w_router[l].T,
        b_router[l].reshape(N_EXPERTS, 1), S)
    counts = counts[:, 0]
    n_tiles_max = N * TOP_K // MOE_TILE + N_EXPERTS
    texp, nvalid, src_tile, group_start, zero_tiles, next_expert = _tile_plan(counts, n_tiles_max)
    experts = jnp.arange(N_EXPERTS, dtype=jnp.int32)
    pos = rank + jnp.sum(jnp.where(top_idx[..., None] == experts, group_start, 0), axis=-1)
    assert DISPATCH_TOKENS == COMBINE_TOKENS
    pos_tiles = pos.reshape(TOP_K, N // DISPATCH_TOKENS, DISPATCH_TOKENS).transpose(1, 0, 2).reshape(
        N // DISPATCH_TOKENS, 1, TOP_K * DISPATCH_TOKENS)
    x_sorted = _dispatch(zero_tiles, pos_tiles, h_rt, n_tiles_max * MOE_TILE)
    y_sorted = _experts(texp, nvalid, src_tile, next_expert, x_sorted, w_gate_up[l], b_gate_up[l], w_down[l],
                        b_down[l])
    y = _combine(pos_tiles, y_sorted, top_w.T, x1, mod6, g_final.reshape(1, D), S)
    return y.reshape(B, S, D)
```

```python
import functools
import math

import jax
import jax.numpy as jnp
from jax import lax
from jax.experimental import pallas as pl
from jax.experimental.pallas import tpu as pltpu

F32 = jnp.float32
BF16 = jnp.bfloat16

D_MODEL = 1024
HEAD_DIM = 64
N_HEADS_MOBA = 8
N_HEADS_SWA = 8
N_KV_SWA = 2
D_MOBA = N_HEADS_MOBA * HEAD_DIM
D_SWA_Q = N_HEADS_SWA * HEAD_DIM
D_SWA_KV = N_KV_SWA * HEAD_DIM
D_IN = 3 * D_MOBA + D_SWA_Q + 2 * D_SWA_KV
MOBA_BLOCK = 256
MOBA_TOPK = 3
SWA_WINDOW = 128
SWA_BLOCK = 128
REL_BUCKETS = 32
REL_MAX_DIST = 1024
REL_MAX_EXACT = REL_BUCKETS // 2
N_EXPERTS = 32
TOP_K = 4
D_FF = 1024
SWIGLU_LIMIT = 7.0
SWIGLU_ALPHA = 1.702
EPS = 1e-5
NEG = -1e30
ATTN_SCALE = HEAD_DIM ** -0.5
LOG2E = math.log2(math.e)

MOE_TILE = 256
ROW_CHUNKS = D_MODEL // 128
DISPATCH_TOKENS = 256
COMBINE_TOKENS = 256
LANES = 128
HEADS_PER_LANE_BLOCK = LANES // HEAD_DIM
VMEM_LIMIT_BYTES = 56 * 1024 * 1024

_REL_THRESHOLDS = tuple(
    math.ceil(REL_MAX_EXACT * (REL_MAX_DIST / REL_MAX_EXACT) ** (k / (REL_BUCKETS - REL_MAX_EXACT)) - 1e-9)
    for k in range(1, REL_BUCKETS - REL_MAX_EXACT))


def _cparams(n_axes):
    return pltpu.CompilerParams(dimension_semantics=("arbitrary",) * n_axes,
                                vmem_limit_bytes=VMEM_LIMIT_BYTES)


def _sigmoid(z):
    return 1.0 / (1.0 + jnp.exp(-z))


def _rmsnorm_rows(xf, g):
    ms = jnp.mean(xf * xf, axis=-1, keepdims=True)
    return xf * lax.rsqrt(ms + EPS) * g


def _dot_nt(a, b, **kw):
    return lax.dot_general(a, b, (((1,), (1,)), ((), ())), preferred_element_type=F32, **kw)


def _ada_kernel(c_ref, w_ref, b_ref, o_ref):
    c = c_ref[...]
    sc = c * _sigmoid(c)
    o_ref[...] = jnp.dot(sc, w_ref[...], preferred_element_type=F32,
                         precision=lax.Precision.HIGHEST) + b_ref[...]


def _ada(c, w_ada, b_ada):
    B = c.shape[0]
    n_out = w_ada.shape[1]
    bn = 1536
    return pl.pallas_call(
        _ada_kernel,
        grid=(n_out // bn,),
        in_specs=[pl.BlockSpec((B, D_MODEL), lambda n: (0, 0)),
                  pl.BlockSpec((D_MODEL, bn), lambda n: (0, n)),
                  pl.BlockSpec((1, bn), lambda n: (0, n))],
        out_specs=pl.BlockSpec((B, bn), lambda n: (0, n)),
        out_shape=jax.ShapeDtypeStruct((B, n_out), F32),
        compiler_params=_cparams(1),
        name="ada",
    )(c, w_ada, b_ada.reshape(1, n_out))


def _inproj_kernel(x_ref, mod_ref, g_ref, w_ref, b_ref,
                   qm_ref, km_ref, vmT_ref, kmean_ref, qs_ref, ks_ref, vsT_ref):
    xf = x_ref[0]
    shift = mod_ref[0, 0:1, :]
    scale = mod_ref[0, 1:2, :]
    h = _rmsnorm_rows(xf, g_ref[...]) * (1.0 + scale) + shift
    proj = jnp.dot(h.astype(BF16), w_ref[...], preferred_element_type=F32) + b_ref[...]
    i0, i1, i2 = D_MOBA, 2 * D_MOBA, 3 * D_MOBA
    i3 = i2 + D_SWA_Q
    i4 = i3 + D_SWA_KV
    qm_ref[0] = proj[:, :i0].astype(BF16)
    km = proj[:, i0:i1]
    km_ref[0] = km.astype(BF16)
    kmean_ref[0, 0] = jnp.broadcast_to(jnp.mean(km, axis=0, keepdims=True), (8, D_MOBA))
    vmT = proj[:, i1:i2].T.astype(BF16)
    vmT_ref[0, :, 0] = vmT.reshape(D_MOBA // LANES, LANES, vmT.shape[1])
    qs_ref[0] = proj[:, i2:i3].astype(BF16)
    ks_ref[0] = proj[:, i3:i4].astype(BF16)
    vsT_ref[0] = proj[:, i4:].T.astype(BF16)


def _inproj(x, mod6, g_attn, w_in_bf, b_in):
    B, S, D = x.shape
    tm = MOBA_BLOCK
    nb = S // tm
    n_pairs = D_MOBA // LANES
    out_shape = (
        jax.ShapeDtypeStruct((B, S, D_MOBA), BF16),
        jax.ShapeDtypeStruct((B, S, D_MOBA), BF16),
        jax.ShapeDtypeStruct((B, n_pairs, nb, LANES, tm), BF16),
        jax.ShapeDtypeStruct((B, nb, 8, D_MOBA), F32),
        jax.ShapeDtypeStruct((B, S, D_SWA_Q), BF16),
        jax.ShapeDtypeStruct((B, S, D_SWA_KV), BF16),
        jax.ShapeDtypeStruct((B, D_SWA_KV, S), BF16),
    )
    return pl.pallas_call(
        _inproj_kernel,
        grid=(B, nb),
        in_specs=[pl.BlockSpec((1, tm, D), lambda b, i: (b, i, 0)),
                  pl.BlockSpec((1, 6, D), lambda b, i: (b, 0, 0)),
                  pl.BlockSpec((1, D), lambda b, i: (0, 0)),
                  pl.BlockSpec((D, D_IN), lambda b, i: (0, 0)),
                  pl.BlockSpec((1, D_IN), lambda b, i: (0, 0))],
        out_specs=(pl.BlockSpec((1, tm, D_MOBA), lambda b, i: (b, i, 0)),
                   pl.BlockSpec((1, tm, D_MOBA), lambda b, i: (b, i, 0)),
                   pl.BlockSpec((1, n_pairs, 1, LANES, tm), lambda b, i: (b, 0, i, 0, 0)),
                   pl.BlockSpec((1, 1, 8, D_MOBA), lambda b, i: (b, i, 0, 0)),
                   pl.BlockSpec((1, tm, D_SWA_Q), lambda b, i: (b, i, 0)),
                   pl.BlockSpec((1, tm, D_SWA_KV), lambda b, i: (b, i, 0)),
                   pl.BlockSpec((1, D_SWA_KV, tm), lambda b, i: (b, 0, i))),
        out_shape=out_shape,
        compiler_params=_cparams(2),
        name="inproj",
    )(x, mod6, g_attn, w_in_bf, b_in)


def _rel_bucket_static(n):
    n = max(n, 0)
    return n if n < REL_MAX_EXACT else REL_MAX_EXACT + sum(n >= t for t in _REL_THRESHOLDS)


def _rel_bias_values(dist, table_ref, h, d_min, d_max):
    lo, hi = _rel_bucket_static(d_min), _rel_bucket_static(d_max)
    n = jnp.maximum(dist, 0)
    val = jnp.full(n.shape, table_ref[lo, h], F32)
    for bkt in range(lo + 1, hi + 1):
        start = bkt if bkt <= REL_MAX_EXACT else _REL_THRESHOLDS[bkt - REL_MAX_EXACT - 1]
        val = jnp.where(n >= start, table_ref[bkt, h], val)
    return val


def _moba_bias_kernel(table_ref, o_ref):
    h = pl.program_id(0)
    L = MOBA_BLOCK
    key = lax.broadcasted_iota(jnp.int32, (L, L), 0)
    qry = lax.broadcasted_iota(jnp.int32, (L, L), 1)
    for dlt in range(o_ref.shape[1]):
        dist = dlt * L + qry - key
        val = _rel_bias_values(dist, table_ref, h, dlt * L - (L - 1), dlt * L + (L - 1))
        val = val * LOG2E
        o_ref[0, dlt] = jnp.where(dist >= 0, val, NEG) if dlt == 0 else val


def _moba_bias(rel_table, nb):
    L = MOBA_BLOCK
    return pl.pallas_call(
        _moba_bias_kernel,
        grid=(N_HEADS_MOBA,),
        in_specs=[pl.BlockSpec(memory_space=pltpu.SMEM)],
        out_specs=pl.BlockSpec((1, nb, L, L), lambda h: (h, 0, 0, 0)),
        out_shape=jax.ShapeDtypeStruct((N_HEADS_MOBA, nb, L, L), F32),
        compiler_params=_cparams(1),
        name="moba_bias",
    )(rel_table)


def _swa_bias_kernel(table_ref, o_ref):
    h = N_HEADS_MOBA + pl.program_id(0)
    W = SWA_BLOCK
    key = lax.broadcasted_iota(jnp.int32, (2 * W, W), 0)
    qry = lax.broadcasted_iota(jnp.int32, (2 * W, W), 1)
    dist = qry + W - key
    val = _rel_bias_values(dist, table_ref, h, 0, SWA_WINDOW - 1)
    o_ref[...] = jnp.where((dist >= 0) & (dist < SWA_WINDOW), val * LOG2E, NEG)


def _swa_bias(rel_table):
    W = SWA_BLOCK
    return pl.pallas_call(
        _swa_bias_kernel,
        grid=(N_HEADS_SWA,),
        in_specs=[pl.BlockSpec(memory_space=pltpu.SMEM)],
        out_specs=pl.BlockSpec((2 * W, W), lambda h: (0, h)),
        out_shape=jax.ShapeDtypeStruct((2 * W, N_HEADS_SWA * W), F32),
        compiler_params=_cparams(1),
        name="swa_bias",
    )(rel_table)


def _gate_kernel(q_ref, kmean_ref, sel_ref):
    S = q_ref.shape[1]
    nb = kmean_ref.shape[1]
    H = N_HEADS_MOBA
    q = q_ref[0]
    km = kmean_ref[0, :, 0, :]
    km_t = jnp.concatenate([km] * H, axis=0)
    row_h = lax.broadcasted_iota(jnp.int32, (H * nb, D_MOBA), 0) // nb
    col_h = lax.broadcasted_iota(jnp.int32, (H * nb, D_MOBA), 1) // HEAD_DIM
    km_t = jnp.where(row_h == col_h, km_t, 0.0)
    km_hi = km_t.astype(BF16)
    km_lo = (km_t - km_hi.astype(F32)).astype(BF16)
    g_all = _dot_nt(km_hi, q) + _dot_nt(km_lo, q)
    qblk = lax.broadcasted_iota(jnp.int32, (nb, S), 1) // MOBA_BLOCK
    jidx = lax.broadcasted_iota(jnp.int32, (nb, S), 0)
    past = jidx < qblk
    for h in range(H):
        g = g_all[h * nb:(h + 1) * nb, :]
        cnt = jnp.zeros((nb, S), jnp.int32)
        for jp in range(nb):
            gj = g[jp:jp + 1, :]
            ahead = (gj > g) | ((gj == g) & (jp < jidx))
            cnt = cnt + ((jp < qblk) & ahead).astype(jnp.int32)
        sel = (past & (cnt < MOBA_TOPK)).astype(F32)
        for j in range(nb):
            for half in range(2):
                sel_ref[0, h, j, half:half + 1, :] = sel[j:j + 1, half * (S // 2):(half + 1) * (S // 2)]


def _gate(qm, kmean):
    B, S, _ = qm.shape
    nb = kmean.shape[1]
    return pl.pallas_call(
        _gate_kernel,
        grid=(B,),
        in_specs=[pl.BlockSpec((1, S, D_MOBA), lambda b: (b, 0, 0)),
                  pl.BlockSpec((1, nb, 8, D_MOBA), lambda b: (b, 0, 0, 0))],
        out_specs=pl.BlockSpec((1, N_HEADS_MOBA, nb, 2, S // 2), lambda b: (b, 0, 0, 0, 0)),
        out_shape=jax.ShapeDtypeStruct((B, N_HEADS_MOBA, nb, 2, S // 2), F32),
        compiler_params=_cparams(1),
        name="moba_gate",
    )(qm, kmean)


def _moba_kernel(q_ref, k_ref, vT_ref, sel_ref, bias_ref, o_ref, s_buf):
    c = pl.program_id(2)
    L = MOBA_BLOCK
    nb = k_ref.shape[1] // L

    def attend(half, n_past):
        q = q_ref[0, half]
        lane = lax.broadcasted_iota(jnp.int32, q.shape, 1)
        zero = jnp.zeros_like(q)
        q_heads = [jnp.where(lane < HEAD_DIM, q, zero), jnp.where(lane >= HEAD_DIM, q, zero)]
        outs = []
        for a in range(HEADS_PER_LANE_BLOCK):
            chosen = [None if j == n_past else sel_ref[0, a, j, half:half + 1, :] > 0.5
                      for j in range(n_past + 1)]
            m = None
            for j in range(n_past + 1):
                s = _dot_nt(k_ref[0, j * L:(j + 1) * L, :], q_heads[a]) + bias_ref[a, n_past - j]
                if n_past > 0:
                    s_buf[half, a, j * L:(j + 1) * L, :] = s
                mj = jnp.max(s, axis=0, keepdims=True)
                if chosen[j] is not None:
                    mj = jnp.where(chosen[j], mj, NEG)
                m = mj if m is None else jnp.maximum(m, mj)
            l = None
            acc = None
            for j in range(n_past + 1):
                shift = m if chosen[j] is None else jnp.where(chosen[j], m, -NEG)
                p = jnp.exp2((s_buf[half, a, j * L:(j + 1) * L, :] if n_past > 0 else s) - shift)
                lj = jnp.sum(p, axis=0, keepdims=True)
                vT = vT_ref[0, 0, j, a * HEAD_DIM:(a + 1) * HEAD_DIM, :]
                oj = jnp.dot(vT, p.astype(BF16), preferred_element_type=F32)
                l = lj if l is None else l + lj
                acc = oj if acc is None else acc + oj
            outs.append(acc * (1.0 / l))
        o_ref[0, half] = jnp.concatenate(outs, axis=0).T.astype(BF16)

    def both(c_static):
        for half in range(2):
            attend(half, c_static + half * (nb // 2))

    for c_static in range(nb // 2):
        pl.when(c == c_static)(functools.partial(both, c_static))


def _moba(qm, km, vmT, sel, bias):
    B, S, _ = qm.shape
    L = MOBA_BLOCK
    nb = S // L
    n_pairs = D_MOBA // LANES
    half_s = S // 2
    out = pl.pallas_call(
        _moba_kernel,
        grid=(n_pairs, B, nb // 2),
        in_specs=[pl.BlockSpec((1, 2, L, LANES), lambda p, b, c: (b, 0, c, p)),
                  pl.BlockSpec((1, S, LANES), lambda p, b, c: (b, 0, p)),
                  pl.BlockSpec((1, 1, nb, LANES, L), lambda p, b, c: (b, p, 0, 0, 0)),
                  pl.BlockSpec((1, HEADS_PER_LANE_BLOCK, nb, 2, L), lambda p, b, c: (b, p, 0, 0, c)),
                  pl.BlockSpec((HEADS_PER_LANE_BLOCK, nb, L, L), lambda p, b, c: (p, 0, 0, 0))],
        out_specs=pl.BlockSpec((1, 2, L, LANES), lambda p, b, c: (b, 0, c, p)),
        out_shape=jax.ShapeDtypeStruct((B, 2, half_s, D_MOBA), BF16),
        scratch_shapes=[pltpu.VMEM((2, HEADS_PER_LANE_BLOCK, S, L), F32)],
        compiler_params=_cparams(3),
        name="moba_attn",
    )(qm.reshape(B, 2, half_s, D_MOBA), km, vmT, sel, bias)
    return out.reshape(B, S, D_MOBA)


def _swa_kernel(sink_ref, q_ref, kp0_ref, kc0_ref, kp1_ref, kc1_ref, vp0_ref, vc0_ref, vp1_ref, vc1_ref,
                bias_ref, o_ref):
    c = pl.program_id(1)
    _swa_block(sink_ref, q_ref[0, 0], kp0_ref[0], kc0_ref[0], vp0_ref[0], vc0_ref[0], bias_ref,
               o_ref.at[0, 0], c > 0)
    _swa_block(sink_ref, q_ref[0, 1], kp1_ref[0], kc1_ref[0], vp1_ref[0], vc1_ref[0], bias_ref,
               o_ref.at[0, 1], None)


def _swa_block(sink_ref, q, k_prev, k_cur, vT_prev, vT_cur, bias_ref, o_ref, has_prev):
    W = SWA_BLOCK
    group = N_HEADS_SWA // N_KV_SWA
    kband = jnp.concatenate([k_prev, k_cur], axis=0)
    vbandT = jnp.concatenate([vT_prev, vT_cur], axis=1)
    q_rows = []
    for h in range(N_HEADS_SWA):
        qp = q[:, (h % group) * LANES:(h % group + 1) * LANES]
        lane = lax.broadcasted_iota(jnp.int32, qp.shape, 1)
        in_half = (lane >= HEAD_DIM) if h // group else (lane < HEAD_DIM)
        q_rows.append(jnp.where(in_half, qp, jnp.zeros_like(qp)))
    q_all = jnp.concatenate(q_rows, axis=0)
    sT = _dot_nt(kband, q_all) + bias_ref[...]
    if has_prev is not None:
        key = lax.broadcasted_iota(jnp.int32, sT.shape, 0)
        sT = jnp.where((key >= W) | has_prev, sT, NEG)
    sink = sink_ref[...]
    m = jnp.maximum(jnp.max(sT, axis=0, keepdims=True), sink)
    pr = jnp.exp2(sT - m)
    inv_l = 1.0 / (jnp.sum(pr, axis=0, keepdims=True) + jnp.exp2(sink - m))
    pr = pr.astype(BF16)
    cols = group * W
    outs = [jnp.dot(vbandT[g * HEAD_DIM:(g + 1) * HEAD_DIM, :], pr[:, g * cols:(g + 1) * cols],
                    preferred_element_type=F32) * inv_l[:, g * cols:(g + 1) * cols]
            for g in range(N_KV_SWA)]
    heads = [outs[h // group][:, (h % group) * W:(h % group + 1) * W] for h in range(N_HEADS_SWA)]
    o_ref[...] = jnp.concatenate(heads, axis=0).T.astype(BF16)


def _swa(sink_row, qs, ks, vsT, bias):
    B, S, _ = qs.shape
    W = SWA_BLOCK
    nq = S // W
    far = nq // 2
    prev = lambda c: jnp.maximum(c - 1, 0)
    k_spec = lambda f: pl.BlockSpec((1, W, D_SWA_KV), lambda b, c: (b, f(c), 0))
    v_spec = lambda f: pl.BlockSpec((1, D_SWA_KV, W), lambda b, c: (b, 0, f(c)))
    blocks = [prev, lambda c: c, lambda c: c + far - 1, lambda c: c + far]
    out = pl.pallas_call(
        _swa_kernel,
        grid=(B, far),
        in_specs=[pl.BlockSpec((1, N_HEADS_SWA * W), lambda b, c: (0, 0)),
                  pl.BlockSpec((1, 2, W, D_SWA_Q), lambda b, c: (b, 0, c, 0))]
                 + [k_spec(f) for f in blocks] + [v_spec(f) for f in blocks]
                 + [pl.BlockSpec((2 * W, N_HEADS_SWA * W), lambda b, c: (0, 0))],
        out_specs=pl.BlockSpec((1, 2, W, D_SWA_Q), lambda b, c: (b, 0, c, 0)),
        out_shape=jax.ShapeDtypeStruct((B, 2, S // 2, D_SWA_Q), BF16),
        compiler_params=_cparams(2),
        name="swa_attn",
    )(sink_row, qs.reshape(B, 2, S // 2, D_SWA_Q), ks, ks, ks, ks, vsT, vsT, vsT, vsT, bias)
    return out.reshape(B, S, D_SWA_Q)


def _outproj_kernel(x_ref, om_ref, os_ref, wm_ref, ws_ref, b_ref, mod_ref, g_ref, wrT_ref, br_ref,
                    x1_ref, hrt_ref, idx_ref, w_ref, rank_ref, counts_ref, carry_ref):
    attn = (jnp.dot(om_ref[...], wm_ref[...], preferred_element_type=F32)
            + jnp.dot(os_ref[...], ws_ref[...], preferred_element_type=F32) + b_ref[...])
    x1 = x_ref[...] + mod_ref[0, 2:3, :] * attn
    x1_ref[...] = x1
    h = _rmsnorm_rows(x1, g_ref[...]) * (1.0 + mod_ref[0, 4:5, :]) + mod_ref[0, 3:4, :]
    _store_row_tiles(hrt_ref, h)
    _route(h, wrT_ref, br_ref, idx_ref, w_ref, rank_ref, counts_ref, carry_ref)


def _store_row_tiles(ref, val):
    rows = val.shape[0]
    for c in range(ROW_CHUNKS):
        ref[pl.ds(c, rows, stride=ROW_CHUNKS), :] = val[:, c * LANES:(c + 1) * LANES]


def _load_row_tiles(ref, rows, lead=(), first_row=0):
    idx = tuple(lead)
    base = first_row * ROW_CHUNKS
    return jnp.concatenate([ref[idx + (pl.ds(base + c, rows, stride=ROW_CHUNKS), slice(None))]
                            for c in range(ROW_CHUNKS)], axis=1)


def _outproj(x2d, om, os_, w_out_m, w_out_s, b_out, mod6, g_ffn, w_routerT, b_router_col, seq_len):
    N, D = x2d.shape
    tm = 512
    per_seq = seq_len // tm
    row = lambda t: (t, 0)
    col = lambda t: (0, t)
    const = lambda t: (0, 0)
    return pl.pallas_call(
        _outproj_kernel,
        grid=(N // tm,),
        in_specs=[pl.BlockSpec((tm, D), row),
                  pl.BlockSpec((tm, D_MOBA), row),
                  pl.BlockSpec((tm, D_SWA_Q), row),
                  pl.BlockSpec((D_MOBA, D), const),
                  pl.BlockSpec((D_SWA_Q, D), const),
                  pl.BlockSpec((1, D), const),
                  pl.BlockSpec((1, 6, D), lambda t: (t // per_seq, 0, 0)),
                  pl.BlockSpec((1, D), const),
                  pl.BlockSpec((N_EXPERTS, D), const),
                  pl.BlockSpec((N_EXPERTS, 1), const)],
        out_specs=(pl.BlockSpec((tm, D), row),
                   pl.BlockSpec((tm * ROW_CHUNKS, LANES), row),
                   pl.BlockSpec((TOP_K, tm), col),
                   pl.BlockSpec((TOP_K, tm), col),
                   pl.BlockSpec((TOP_K, tm), col),
                   pl.BlockSpec((N_EXPERTS, LANES), const)),
        out_shape=(jax.ShapeDtypeStruct((N, D), F32),
                   jax.ShapeDtypeStruct((N * ROW_CHUNKS, LANES), F32),
                   jax.ShapeDtypeStruct((TOP_K, N), jnp.int32),
                   jax.ShapeDtypeStruct((TOP_K, N), F32),
                   jax.ShapeDtypeStruct((TOP_K, N), jnp.int32),
                   jax.ShapeDtypeStruct((N_EXPERTS, LANES), jnp.int32)),
        scratch_shapes=[pltpu.VMEM((N_EXPERTS, LANES), F32)],
        compiler_params=_cparams(1),
        name="outproj_route",
    )(x2d, om, os_, w_out_m, w_out_s, b_out, mod6, g_ffn, w_routerT, b_router_col)


def _route(h, wT_ref, b_ref, idx_ref, w_ref, rank_ref, counts_ref, carry_ref):
    @pl.when(pl.program_id(0) == 0)
    def _():
        carry_ref[...] = jnp.zeros_like(carry_ref)

    w = wT_ref[...]
    w_hi = w.astype(BF16)
    w_lo = (w - w_hi.astype(F32)).astype(BF16)
    h_hi = h.astype(BF16)
    h_lo = (h - h_hi.astype(F32)).astype(BF16)
    by_h_hi = _dot_nt(jnp.concatenate([w_hi, w_lo], axis=0), h_hi)
    logits = (by_h_hi[:N_EXPERTS] + by_h_hi[N_EXPERTS:] + _dot_nt(w_hi, h_lo)) + b_ref[...]
    tm = logits.shape[1]
    eidx = lax.broadcasted_iota(jnp.int32, logits.shape, 0)
    vals, idxs = [], []
    cur = logits
    for _ in range(TOP_K):
        m = jnp.max(cur, axis=0, keepdims=True)
        am = jnp.min(jnp.where(cur == m, eidx, N_EXPERTS), axis=0, keepdims=True)
        vals.append(m)
        idxs.append(am)
        cur = jnp.where(eidx == am, -jnp.inf, cur)
    exps = [jnp.exp(v - vals[0]) for v in vals]
    inv = 1.0 / functools.reduce(lambda a, b: a + b, exps)
    idx_ref[...] = jnp.concatenate(idxs, axis=0)
    w_ref[...] = jnp.concatenate([e * inv for e in exps], axis=0)

    member = functools.reduce(lambda a, b: a | b, [eidx == am for am in idxs])
    earlier = (lax.broadcasted_iota(jnp.int32, (tm, tm), 0)
               < lax.broadcasted_iota(jnp.int32, (tm, tm), 1)).astype(BF16)
    before = jnp.dot(member.astype(BF16), earlier, preferred_element_type=F32) + carry_ref[:, 0:1]
    rank_ref[...] = jnp.concatenate(
        [jnp.sum(jnp.where(eidx == am, before, 0.0), axis=0, keepdims=True) for am in idxs],
        axis=0).astype(jnp.int32)
    total = carry_ref[...] + jnp.sum(member.astype(F32), axis=1, keepdims=True)
    carry_ref[...] = total
    counts_ref[...] = total.astype(jnp.int32)


def _expert_mlp(h_bf, wgu, bgu, wdn, bdn):
    gu = jnp.dot(h_bf, wgu, preferred_element_type=F32) + bgu
    g = jnp.minimum(gu[:, :D_FF], SWIGLU_LIMIT)
    u = jnp.clip(gu[:, D_FF:], -SWIGLU_LIMIT, SWIGLU_LIMIT)
    act = (u + 1.0) * (g * _sigmoid(SWIGLU_ALPHA * g))
    return jnp.dot(act.astype(BF16), wdn, preferred_element_type=F32) + bdn


def _tile_plan(counts, n_tiles_max):
    tiles_per = (counts + MOE_TILE - 1) // MOE_TILE
    tile_end = jnp.cumsum(tiles_per)
    tile_start = tile_end - tiles_per
    t_ids = jnp.arange(n_tiles_max, dtype=jnp.int32)
    texp = jnp.minimum(jnp.sum((t_ids[:, None] >= tile_end[None, :]).astype(jnp.int32), axis=1),
                       N_EXPERTS - 1)
    onehot = (texp[:, None] == jnp.arange(N_EXPERTS, dtype=jnp.int32)[None, :]).astype(jnp.int32)
    local = t_ids - onehot @ tile_start
    nvalid = jnp.clip(onehot @ counts - local * MOE_TILE, 0, MOE_TILE)
    n_active = tile_end[-1]
    src_tile = jnp.minimum(t_ids, n_active - 1)
    last_tile = jnp.where(counts > 0, tile_end - 1, -1)
    tail = n_active + jnp.arange(N_EXPERTS, dtype=jnp.int32)
    zero_tiles = jnp.concatenate([last_tile, jnp.where(tail < n_tiles_max, tail, -1)])
    ids = jnp.arange(N_EXPERTS, dtype=jnp.int32)
    later = jnp.where((ids[None, :] > ids[:, None]) & (counts[None, :] > 0), ids[None, :], N_EXPERTS)
    next_expert = jnp.min(later, axis=1)
    next_expert = jnp.where(next_expert < N_EXPERTS, next_expert, -1)
    return (texp.astype(jnp.int32), nvalid.astype(jnp.int32), src_tile.astype(jnp.int32),
            (tile_start * MOE_TILE).astype(jnp.int32), zero_tiles.astype(jnp.int32),
            next_expert.astype(jnp.int32))


def _row_copy(src, src_row, dst, dst_row, sem):
    return pltpu.make_async_copy(src.at[pl.ds(pl.multiple_of(src_row * ROW_CHUNKS, ROW_CHUNKS), ROW_CHUNKS)],
                                 dst.at[pl.ds(pl.multiple_of(dst_row * ROW_CHUNKS, ROW_CHUNKS), ROW_CHUNKS)],
                                 sem)


DISPATCH_SLOTS = 3


def _dispatch_kernel(zero_tiles_ref, pos_ref, h_hbm, x_hbm, zbuf, hbuf, zsem, load_sem, row_sem):
    t = pl.program_id(0)
    n_steps = pl.num_programs(0)
    tile_rows = MOE_TILE * ROW_CHUNKS
    block_rows = DISPATCH_TOKENS * ROW_CHUNKS

    def block_load(step, slot):
        start = pl.multiple_of(step * block_rows, block_rows)
        return pltpu.make_async_copy(h_hbm.at[pl.ds(start, block_rows)], hbuf.at[slot], load_sem.at[slot])

    def rows_done(slot):
        for k in range(TOP_K):
            pltpu.make_async_copy(hbuf.at[slot], x_hbm.at[pl.ds(0, block_rows)], row_sem.at[slot]).wait()

    def zero_copy(i):
        start = pl.multiple_of(zero_tiles_ref[i] * tile_rows, tile_rows)
        return pltpu.make_async_copy(zbuf, x_hbm.at[pl.ds(start, tile_rows)], zsem)

    @pl.when(t == 0)
    def _():
        zbuf[...] = jnp.zeros_like(zbuf)
        for i in range(2 * N_EXPERTS):
            @pl.when(zero_tiles_ref[i] >= 0)
            def _():
                zero_copy(i).start()
        block_load(0, 0).start()
        if DISPATCH_SLOTS > 2:
            @pl.when(n_steps > 1)
            def _():
                block_load(1, 1).start()
        for i in range(2 * N_EXPERTS):
            @pl.when(zero_tiles_ref[i] >= 0)
            def _():
                zero_copy(i).wait()

    slot = t % DISPATCH_SLOTS
    block_load(t, slot).wait()

    def start_rows(r, carry):
        for k in range(TOP_K):
            _row_copy(hbuf.at[slot], r, x_hbm, pos_ref[0, 0, k * DISPATCH_TOKENS + r],
                      row_sem.at[slot]).start(priority=k % 2)
        return carry
    lax.fori_loop(0, DISPATCH_TOKENS, start_rows, 0, unroll=8)

    refill = (t + 2) % DISPATCH_SLOTS

    @pl.when(t > 0)
    def _():
        rows_done(refill)

    @pl.when(t + 2 < n_steps)
    def _():
        block_load(t + 2, refill).start()

    @pl.when(t == n_steps - 1)
    def _():
        rows_done(slot)


def _dispatch(zero_tiles, pos_tiles, h_rt, n_rows):
    n_steps = pos_tiles.shape[0]
    grid_spec = pltpu.PrefetchScalarGridSpec(
        num_scalar_prefetch=1,
        grid=(n_steps,),
        in_specs=[pl.BlockSpec((1, 1, TOP_K * DISPATCH_TOKENS), lambda t, zt: (t, 0, 0),
                               memory_space=pltpu.SMEM),
                  pl.BlockSpec(memory_space=pl.ANY)],
        out_specs=pl.BlockSpec(memory_space=pl.ANY),
        scratch_shapes=[pltpu.VMEM((MOE_TILE * ROW_CHUNKS, LANES), F32),
                        pltpu.VMEM((DISPATCH_SLOTS, DISPATCH_TOKENS * ROW_CHUNKS, LANES), F32),
                        pltpu.SemaphoreType.DMA(()),
                        pltpu.SemaphoreType.DMA((DISPATCH_SLOTS,)),
                        pltpu.SemaphoreType.DMA((DISPATCH_SLOTS,))],
    )
    return pl.pallas_call(
        _dispatch_kernel,
        grid_spec=grid_spec,
        out_shape=jax.ShapeDtypeStruct((n_rows * ROW_CHUNKS, LANES), F32),
        compiler_params=_cparams(1),
        name="moe_dispatch",
    )(zero_tiles, pos_tiles, h_rt)


def _experts_kernel(texp_ref, nvalid_ref, src_ref, next_ref, x_ref, wgu_hbm, bgu_ref, wdn_hbm, bdn_ref,
                    y_ref, wgu_f32, wdn_f32, wgu_bf, wdn_bf, wsem):
    t = pl.program_id(0)
    nv = nvalid_ref[t]
    e = texp_ref[t]
    prev_e = texp_ref[jnp.maximum(t - 1, 0)]

    def weight_loads(expert):
        return (pltpu.make_async_copy(wgu_hbm.at[expert], wgu_f32, wsem.at[0]),
                pltpu.make_async_copy(wdn_hbm.at[expert], wdn_f32, wsem.at[1]))

    @pl.when((nv > 0) & ((t == 0) | (e != prev_e)))
    def _():
        @pl.when(t == 0)
        def _():
            for cp in weight_loads(e):
                cp.start()
        for cp in weight_loads(e):
            cp.wait()
        rows = 256
        def cast_gu(i, carry):
            r0 = pl.multiple_of(i * rows, rows)
            wgu_bf[pl.ds(r0, rows), :] = wgu_f32[pl.ds(r0, rows), :].astype(BF16)
            return carry
        lax.fori_loop(0, D_MODEL // rows, cast_gu, 0)
        def cast_dn(i, carry):
            r0 = pl.multiple_of(i * rows, rows)
            wdn_bf[pl.ds(r0, rows), :] = wdn_f32[pl.ds(r0, rows), :].astype(BF16)
            return carry
        lax.fori_loop(0, D_FF // rows, cast_dn, 0)
        nxt = next_ref[e]

        @pl.when(nxt >= 0)
        def _():
            for cp in weight_loads(nxt):
                cp.start()

    @pl.when(nv > 0)
    def _():
        x = _load_row_tiles(x_ref, MOE_TILE).astype(BF16)
        _store_row_tiles(y_ref, _expert_mlp(x, wgu_bf[...], bgu_ref[0], wdn_bf[...], bdn_ref[0]))

    @pl.when(nv == 0)
    def _():
        y_ref[...] = jnp.zeros_like(y_ref)


def _experts(texp, nvalid, src_tile, next_expert, x_sorted, w_gate_up, b_gate_up, w_down, b_down):
    n_tiles = texp.shape[0]
    D = D_MODEL
    tile_rows = MOE_TILE * ROW_CHUNKS
    grid_spec = pltpu.PrefetchScalarGridSpec(
        num_scalar_prefetch=4,
        grid=(n_tiles,),
        in_specs=[pl.BlockSpec((tile_rows, LANES), lambda t, te, nv, st, nx: (st[t], 0)),
                  pl.BlockSpec(memory_space=pl.ANY),
                  pl.BlockSpec((1, 1, 2 * D_FF), lambda t, te, nv, st, nx: (te[t], 0, 0)),
                  pl.BlockSpec(memory_space=pl.ANY),
                  pl.BlockSpec((1, 1, D), lambda t, te, nv, st, nx: (te[t], 0, 0))],
        out_specs=pl.BlockSpec((tile_rows, LANES), lambda t, te, nv, st, nx: (t, 0)),
        scratch_shapes=[pltpu.VMEM((D, 2 * D_FF), F32),
                        pltpu.VMEM((D_FF, D), F32),
                        pltpu.VMEM((D, 2 * D_FF), BF16),
                        pltpu.VMEM((D_FF, D), BF16),
                        pltpu.SemaphoreType.DMA((2,))],
    )
    return pl.pallas_call(
        _experts_kernel,
        grid_spec=grid_spec,
        out_shape=jax.ShapeDtypeStruct(x_sorted.shape, F32),
        compiler_params=_cparams(1),
        name="moe_experts",
    )(texp, nvalid, src_tile, next_expert, x_sorted, w_gate_up, b_gate_up.reshape(N_EXPERTS, 1, 2 * D_FF),
      w_down, b_down.reshape(N_EXPERTS, 1, D))


def _combine_kernel(pos_ref, pos_next_ref, y_hbm, w_ref, x1_ref, mod_ref, g_ref, o_ref, ybuf, sem):
    t = pl.program_id(0)
    n_steps = pl.num_programs(0)
    slot = t % 2

    def start_gather(idx_ref, s):
        def body(pair, carry):
            for queue in range(2):
                i = 2 * pair + queue
                _row_copy(y_hbm, idx_ref[0, 0, i], ybuf.at[s], i, sem.at[s]).start(priority=queue)
            return carry
        lax.fori_loop(0, TOP_K * COMBINE_TOKENS // 2, body, 0, unroll=4)

    @pl.when(t == 0)
    def _():
        start_gather(pos_ref, 0)

    @pl.when(t + 1 < n_steps)
    def _():
        start_gather(pos_next_ref, 1 - slot)

    slot_rows = TOP_K * COMBINE_TOKENS * ROW_CHUNKS
    pltpu.make_async_copy(y_hbm.at[pl.ds(0, slot_rows)], ybuf.at[slot], sem.at[slot]).wait()

    w = w_ref[...]
    acc = w[:, 0:1] * _load_row_tiles(ybuf, COMBINE_TOKENS, (slot,))
    for k in range(1, TOP_K):
        acc = acc + w[:, k:k + 1] * _load_row_tiles(ybuf, COMBINE_TOKENS, (slot,), k * COMBINE_TOKENS)
    x2 = x1_ref[...] + mod_ref[0, 5:6, :] * acc
    o_ref[...] = _rmsnorm_rows(x2, g_ref[...])


def _combine(pos_tiles, y_sorted, top_w, x1, mod6, g_final, seq_len):
    N, D = x1.shape
    tm = COMBINE_TOKENS
    n_steps = N // tm
    per_seq = seq_len // tm
    return pl.pallas_call(
        _combine_kernel,
        grid=(n_steps,),
        in_specs=[pl.BlockSpec((1, 1, TOP_K * tm), lambda t: (t, 0, 0), memory_space=pltpu.SMEM),
                  pl.BlockSpec((1, 1, TOP_K * tm), lambda t: (jnp.minimum(t + 1, n_steps - 1), 0, 0),
                               memory_space=pltpu.SMEM),
                  pl.BlockSpec(memory_space=pl.ANY),
                  pl.BlockSpec((tm, TOP_K), lambda t: (t, 0)),
                  pl.BlockSpec((tm, D), lambda t: (t, 0)),
                  pl.BlockSpec((1, 6, D), lambda t: (t // per_seq, 0, 0)),
                  pl.BlockSpec((1, D), lambda t: (0, 0))],
        out_specs=pl.BlockSpec((tm, D), lambda t: (t, 0)),
        out_shape=jax.ShapeDtypeStruct((N, D), F32),
        scratch_shapes=[pltpu.VMEM((2, TOP_K * tm * ROW_CHUNKS, LANES), F32),
                        pltpu.SemaphoreType.DMA((2,))],
        compiler_params=_cparams(1),
        name="moe_combine",
    )(pos_tiles, pos_tiles, y_sorted, top_w, x1, mod6, g_final)


def kernel(x, c, g_attn, w_ada, b_ada, w_in, b_in, w_out, b_out, rel_table, sinks, g_ffn, w_router,
           b_router, w_gate_up, b_gate_up, w_down, b_down, g_final):
    B, S, D = x.shape
    assert w_ada.shape[0] == 1, "the final norm is fused into the last layer; one layer supported"
    l = 0
    nb = S // MOBA_BLOCK
    i2 = 3 * D_MOBA
    i3 = i2 + D_SWA_Q
    group = N_HEADS_SWA // N_KV_SWA
    q_scale = ATTN_SCALE * LOG2E

    def prep_in(w):
        lead = w.shape[:-1]
        q_swa = (w[..., i2:i3] * q_scale).reshape(lead + (N_KV_SWA, group, HEAD_DIM))
        q_swa = jnp.swapaxes(q_swa, -3, -2).reshape(lead + (D_SWA_Q,))
        return jnp.concatenate([w[..., :D_MOBA] * q_scale, w[..., D_MOBA:i2], q_swa, w[..., i3:]], axis=-1)

    moba_bias = _moba_bias(rel_table, nb)
    swa_bias = _swa_bias(rel_table)

    mod6 = _ada(c, w_ada[l], b_ada[l]).reshape(B, 6, D)
    w_in_l = prep_in(w_in[l]).astype(BF16)
    b_in_l = prep_in(b_in[l]).reshape(1, D_IN)
    qm, km, vmT, kmean, qs, ks, vsT = _inproj(x, mod6, g_attn[l].reshape(1, D), w_in_l, b_in_l)
    sel = _gate(qm, kmean)
    o_m = _moba(qm, km, vmT, sel, moba_bias)
    sink_row = jnp.repeat(sinks[l] * LOG2E, SWA_BLOCK).reshape(1, N_HEADS_SWA * SWA_BLOCK)
    o_s = _swa(sink_row, qs, ks, vsT, swa_bias)
    w_out_m = w_out[l, :D_MOBA].astype(BF16)
    w_out_s = w_out[l, D_MOBA:].astype(BF16)
    N = B * S
    x1, h_rt, top_idx, top_w, rank, counts = _outproj(
        x.reshape(N, D), o_m.reshape(N, D_MOBA), o_s.reshape(N, D_SWA_Q), w_out_m, w_out_s,
        b_out[l].reshape(1, D), mod6, g_ffn[l].reshape(1, D), w_router[l].T,
        b_router[l].reshape(N_EXPERTS, 1), S)
    counts = counts[:, 0]
    n_tiles_max = N * TOP_K // MOE_TILE + N_EXPERTS
    texp, nvalid, src_tile, group_start, zero_tiles, next_expert = _tile_plan(counts, n_tiles_max)
    experts = jnp.arange(N_EXPERTS, dtype=jnp.int32)
    pos = rank + jnp.sum(jnp.where(top_idx[..., None] == experts, group_start, 0), axis=-1)
    assert DISPATCH_TOKENS == COMBINE_TOKENS
    pos_tiles = pos.reshape(TOP_K, N // DISPATCH_TOKENS, DISPATCH_TOKENS).transpose(1, 0, 2).reshape(
        N // DISPATCH_TOKENS, 1, TOP_K * DISPATCH_TOKENS)
    x_sorted = _dispatch(zero_tiles, pos_tiles, h_rt, n_tiles_max * MOE_TILE)
    y_sorted = _experts(texp, nvalid, src_tile, next_expert, x_sorted, w_gate_up[l], b_gate_up[l], w_down[l],
                        b_down[l])
    y = _combine(pos_tiles, y_sorted, top_w.T, x1, mod6, g_final.reshape(1, D), S)
    return y.reshape(B, S, D)
```

```python
import functools
import math

import jax
import jax.numpy as jnp
from jax import lax
from jax.experimental import pallas as pl
from jax.experimental.pallas import tpu as pltpu

F32 = jnp.float32
BF16 = jnp.bfloat16

D_MODEL = 1024
HEAD_DIM = 64
N_HEADS_MOBA = 8
N_HEADS_SWA = 8
N_KV_SWA = 2
D_MOBA = N_HEADS_MOBA * HEAD_DIM
D_SWA_Q = N_HEADS_SWA * HEAD_DIM
D_SWA_KV = N_KV_SWA * HEAD_DIM
D_IN = 3 * D_MOBA + D_SWA_Q + 2 * D_SWA_KV
MOBA_BLOCK = 256
MOBA_TOPK = 3
SWA_WINDOW = 128
SWA_BLOCK = 128
REL_BUCKETS = 32
REL_MAX_DIST = 1024
REL_MAX_EXACT = REL_BUCKETS // 2
N_EXPERTS = 32
TOP_K = 4
D_FF = 1024
SWIGLU_LIMIT = 7.0
SWIGLU_ALPHA = 1.702
EPS = 1e-5
NEG = -1e30
ATTN_SCALE = HEAD_DIM ** -0.5
LOG2E = math.log2(math.e)

MOE_TILE = 512
MOE_MM_ROWS = 256
ROW_CHUNKS = D_MODEL // 128
DISPATCH_TOKENS = 256
COMBINE_TOKENS = 256
LANES = 128
HEADS_PER_LANE_BLOCK = LANES // HEAD_DIM
VMEM_LIMIT_BYTES = 56 * 1024 * 1024

_REL_THRESHOLDS = tuple(
    math.ceil(REL_MAX_EXACT * (REL_MAX_DIST / REL_MAX_EXACT) ** (k / (REL_BUCKETS - REL_MAX_EXACT)) - 1e-9)
    for k in range(1, REL_BUCKETS - REL_MAX_EXACT))


def _cparams(n_axes):
    return pltpu.CompilerParams(dimension_semantics=("arbitrary",) * n_axes,
                                vmem_limit_bytes=VMEM_LIMIT_BYTES)


def _sigmoid(z):
    return 1.0 / (1.0 + jnp.exp(-z))


def _rmsnorm_rows(xf, g):
    ms = jnp.mean(xf * xf, axis=-1, keepdims=True)
    return xf * lax.rsqrt(ms + EPS) * g


def _dot_nt(a, b, **kw):
    return lax.dot_general(a, b, (((1,), (1,)), ((), ())), preferred_element_type=F32, **kw)


def _ada_kernel(c_ref, w_ref, b_ref, o_ref):
    c = c_ref[...]
    sc = c * _sigmoid(c)
    o_ref[...] = jnp.dot(sc, w_ref[...], preferred_element_type=F32,
                         precision=lax.Precision.HIGHEST) + b_ref[...]


def _ada(c, w_ada, b_ada):
    B = c.shape[0]
    n_out = w_ada.shape[1]
    bn = 1536
    return pl.pallas_call(
        _ada_kernel,
        grid=(n_out // bn,),
        in_specs=[pl.BlockSpec((B, D_MODEL), lambda n: (0, 0)),
                  pl.BlockSpec((D_MODEL, bn), lambda n: (0, n)),
                  pl.BlockSpec((1, bn), lambda n: (0, n))],
        out_specs=pl.BlockSpec((B, bn), lambda n: (0, n)),
        out_shape=jax.ShapeDtypeStruct((B, n_out), F32),
        compiler_params=_cparams(1),
        name="ada",
    )(c, w_ada, b_ada.reshape(1, n_out))


def _inproj_kernel(x_ref, mod_ref, g_ref, w_ref, b_ref,
                   qm_ref, km_ref, vmT_ref, kmean_ref, qs_ref, ks_ref, vsT_ref):
    xf = x_ref[0]
    shift = mod_ref[0, 0:1, :]
    scale = mod_ref[0, 1:2, :]
    h = _rmsnorm_rows(xf, g_ref[...]) * (1.0 + scale) + shift
    proj = jnp.dot(h.astype(BF16), w_ref[...], preferred_element_type=F32) + b_ref[...]
    i0, i1, i2 = D_MOBA, 2 * D_MOBA, 3 * D_MOBA
    i3 = i2 + D_SWA_Q
    i4 = i3 + D_SWA_KV
    qm_ref[0] = proj[:, :i0].astype(BF16)
    km = proj[:, i0:i1]
    km_ref[0] = km.astype(BF16)
    kmean_ref[0, 0] = jnp.broadcast_to(jnp.mean(km, axis=0, keepdims=True), (8, D_MOBA))
    vmT = proj[:, i1:i2].T.astype(BF16)
    vmT_ref[0, :, 0] = vmT.reshape(D_MOBA // LANES, LANES, vmT.shape[1])
    qs_ref[0] = proj[:, i2:i3].astype(BF16)
    ks_ref[0] = proj[:, i3:i4].astype(BF16)
    vsT_ref[0] = proj[:, i4:].T.astype(BF16)


def _inproj(x, mod6, g_attn, w_in_bf, b_in):
    B, S, D = x.shape
    tm = MOBA_BLOCK
    nb = S // tm
    n_pairs = D_MOBA // LANES
    out_shape = (
        jax.ShapeDtypeStruct((B, S, D_MOBA), BF16),
        jax.ShapeDtypeStruct((B, S, D_MOBA), BF16),
        jax.ShapeDtypeStruct((B, n_pairs, nb, LANES, tm), BF16),
        jax.ShapeDtypeStruct((B, nb, 8, D_MOBA), F32),
        jax.ShapeDtypeStruct((B, S, D_SWA_Q), BF16),
        jax.ShapeDtypeStruct((B, S, D_SWA_KV), BF16),
        jax.ShapeDtypeStruct((B, D_SWA_KV, S), BF16),
    )
    return pl.pallas_call(
        _inproj_kernel,
        grid=(B, nb),
        in_specs=[pl.BlockSpec((1, tm, D), lambda b, i: (b, i, 0)),
                  pl.BlockSpec((1, 6, D), lambda b, i: (b, 0, 0)),
                  pl.BlockSpec((1, D), lambda b, i: (0, 0)),
                  pl.BlockSpec((D, D_IN), lambda b, i: (0, 0)),
                  pl.BlockSpec((1, D_IN), lambda b, i: (0, 0))],
        out_specs=(pl.BlockSpec((1, tm, D_MOBA), lambda b, i: (b, i, 0)),
                   pl.BlockSpec((1, tm, D_MOBA), lambda b, i: (b, i, 0)),
                   pl.BlockSpec((1, n_pairs, 1, LANES, tm), lambda b, i: (b, 0, i, 0, 0)),
                   pl.BlockSpec((1, 1, 8, D_MOBA), lambda b, i: (b, i, 0, 0)),
                   pl.BlockSpec((1, tm, D_SWA_Q), lambda b, i: (b, i, 0)),
                   pl.BlockSpec((1, tm, D_SWA_KV), lambda b, i: (b, i, 0)),
                   pl.BlockSpec((1, D_SWA_KV, tm), lambda b, i: (b, 0, i))),
        out_shape=out_shape,
        compiler_params=_cparams(2),
        name="inproj",
    )(x, mod6, g_attn, w_in_bf, b_in)


def _rel_bucket_static(n):
    n = max(n, 0)
    return n if n < REL_MAX_EXACT else REL_MAX_EXACT + sum(n >= t for t in _REL_THRESHOLDS)


def _rel_bias_values(dist, table_ref, h, d_min, d_max):
    lo, hi = _rel_bucket_static(d_min), _rel_bucket_static(d_max)
    n = jnp.maximum(dist, 0)
    val = jnp.full(n.shape, table_ref[lo, h], F32)
    for bkt in range(lo + 1, hi + 1):
        start = bkt if bkt <= REL_MAX_EXACT else _REL_THRESHOLDS[bkt - REL_MAX_EXACT - 1]
        val = jnp.where(n >= start, table_ref[bkt, h], val)
    return val


def _moba_bias_kernel(table_ref, o_ref):
    h = pl.program_id(0)
    L = MOBA_BLOCK
    key = lax.broadcasted_iota(jnp.int32, (L, L), 0)
    qry = lax.broadcasted_iota(jnp.int32, (L, L), 1)
    for dlt in range(o_ref.shape[1]):
        dist = dlt * L + qry - key
        val = _rel_bias_values(dist, table_ref, h, dlt * L - (L - 1), dlt * L + (L - 1))
        val = val * LOG2E
        o_ref[0, dlt] = jnp.where(dist >= 0, val, NEG) if dlt == 0 else val


def _moba_bias(rel_table, nb):
    L = MOBA_BLOCK
    return pl.pallas_call(
        _moba_bias_kernel,
        grid=(N_HEADS_MOBA,),
        in_specs=[pl.BlockSpec(memory_space=pltpu.SMEM)],
        out_specs=pl.BlockSpec((1, nb, L, L), lambda h: (h, 0, 0, 0)),
        out_shape=jax.ShapeDtypeStruct((N_HEADS_MOBA, nb, L, L), F32),
        compiler_params=_cparams(1),
        name="moba_bias",
    )(rel_table)


def _swa_bias_kernel(table_ref, o_ref):
    h = N_HEADS_MOBA + pl.program_id(0)
    W = SWA_BLOCK
    key = lax.broadcasted_iota(jnp.int32, (2 * W, W), 0)
    qry = lax.broadcasted_iota(jnp.int32, (2 * W, W), 1)
    dist = qry + W - key
    val = _rel_bias_values(dist, table_ref, h, 0, SWA_WINDOW - 1)
    o_ref[...] = jnp.where((dist >= 0) & (dist < SWA_WINDOW), val * LOG2E, NEG)


def _swa_bias(rel_table):
    W = SWA_BLOCK
    return pl.pallas_call(
        _swa_bias_kernel,
        grid=(N_HEADS_SWA,),
        in_specs=[pl.BlockSpec(memory_space=pltpu.SMEM)],
        out_specs=pl.BlockSpec((2 * W, W), lambda h: (0, h)),
        out_shape=jax.ShapeDtypeStruct((2 * W, N_HEADS_SWA * W), F32),
        compiler_params=_cparams(1),
        name="swa_bias",
    )(rel_table)


def _gate_kernel(q_ref, kmean_ref, sel_ref):
    S = q_ref.shape[1]
    nb = kmean_ref.shape[1]
    H = N_HEADS_MOBA
    q = q_ref[0]
    km = kmean_ref[0, :, 0, :]
    km_t = jnp.concatenate([km] * H, axis=0)
    row_h = lax.broadcasted_iota(jnp.int32, (H * nb, D_MOBA), 0) // nb
    col_h = lax.broadcasted_iota(jnp.int32, (H * nb, D_MOBA), 1) // HEAD_DIM
    km_t = jnp.where(row_h == col_h, km_t, 0.0)
    km_hi = km_t.astype(BF16)
    km_lo = (km_t - km_hi.astype(F32)).astype(BF16)
    g_all = _dot_nt(km_hi, q) + _dot_nt(km_lo, q)
    qblk = lax.broadcasted_iota(jnp.int32, (nb, S), 1) // MOBA_BLOCK
    jidx = lax.broadcasted_iota(jnp.int32, (nb, S), 0)
    past = jidx < qblk
    for h in range(H):
        g = g_all[h * nb:(h + 1) * nb, :]
        cnt = jnp.zeros((nb, S), jnp.int32)
        for jp in range(nb):
            gj = g[jp:jp + 1, :]
            ahead = (gj > g) | ((gj == g) & (jp < jidx))
            cnt = cnt + ((jp < qblk) & ahead).astype(jnp.int32)
        sel = (past & (cnt < MOBA_TOPK)).astype(F32)
        for j in range(nb):
            for half in range(2):
                sel_ref[0, h, j, half:half + 1, :] = sel[j:j + 1, half * (S // 2):(half + 1) * (S // 2)]


def _gate(qm, kmean):
    B, S, _ = qm.shape
    nb = kmean.shape[1]
    return pl.pallas_call(
        _gate_kernel,
        grid=(B,),
        in_specs=[pl.BlockSpec((1, S, D_MOBA), lambda b: (b, 0, 0)),
                  pl.BlockSpec((1, nb, 8, D_MOBA), lambda b: (b, 0, 0, 0))],
        out_specs=pl.BlockSpec((1, N_HEADS_MOBA, nb, 2, S // 2), lambda b: (b, 0, 0, 0, 0)),
        out_shape=jax.ShapeDtypeStruct((B, N_HEADS_MOBA, nb, 2, S // 2), F32),
        compiler_params=_cparams(1),
        name="moba_gate",
    )(qm, kmean)


def _moba_kernel(q_ref, k_ref, vT_ref, sel_ref, bias_ref, o_ref, s_buf):
    c = pl.program_id(2)
    L = MOBA_BLOCK
    nb = k_ref.shape[1] // L

    def attend(half, n_past):
        q = q_ref[0, half]
        lane = lax.broadcasted_iota(jnp.int32, q.shape, 1)
        zero = jnp.zeros_like(q)
        q_heads = [jnp.where(lane < HEAD_DIM, q, zero), jnp.where(lane >= HEAD_DIM, q, zero)]
        outs = []
        for a in range(HEADS_PER_LANE_BLOCK):
            chosen = [None if j == n_past else sel_ref[0, a, j, half:half + 1, :] > 0.5
                      for j in range(n_past + 1)]
            m = None
            for j in range(n_past + 1):
                s = _dot_nt(k_ref[0, j * L:(j + 1) * L, :], q_heads[a]) + bias_ref[a, n_past - j]
                if n_past > 0:
                    s_buf[half, a, j * L:(j + 1) * L, :] = s
                mj = jnp.max(s, axis=0, keepdims=True)
                if chosen[j] is not None:
                    mj = jnp.where(chosen[j], mj, NEG)
                m = mj if m is None else jnp.maximum(m, mj)
            l = None
            acc = None
            for j in range(n_past + 1):
                shift = m if chosen[j] is None else jnp.where(chosen[j], m, -NEG)
                p = jnp.exp2((s_buf[half, a, j * L:(j + 1) * L, :] if n_past > 0 else s) - shift)
                lj = jnp.sum(p, axis=0, keepdims=True)
                vT = vT_ref[0, 0, j, a * HEAD_DIM:(a + 1) * HEAD_DIM, :]
                oj = jnp.dot(vT, p.astype(BF16), preferred_element_type=F32)
                l = lj if l is None else l + lj
                acc = oj if acc is None else acc + oj
            outs.append(acc * (1.0 / l))
        o_ref[0, half] = jnp.concatenate(outs, axis=0).T.astype(BF16)

    def both(c_static):
        for half in range(2):
            attend(half, c_static + half * (nb // 2))

    for c_static in range(nb // 2):
        pl.when(c == c_static)(functools.partial(both, c_static))


def _moba(qm, km, vmT, sel, bias):
    B, S, _ = qm.shape
    L = MOBA_BLOCK
    nb = S // L
    n_pairs = D_MOBA // LANES
    half_s = S // 2
    out = pl.pallas_call(
        _moba_kernel,
        grid=(n_pairs, B, nb // 2),
        in_specs=[pl.BlockSpec((1, 2, L, LANES), lambda p, b, c: (b, 0, c, p)),
                  pl.BlockSpec((1, S, LANES), lambda p, b, c: (b, 0, p)),
                  pl.BlockSpec((1, 1, nb, LANES, L), lambda p, b, c: (b, p, 0, 0, 0)),
                  pl.BlockSpec((1, HEADS_PER_LANE_BLOCK, nb, 2, L), lambda p, b, c: (b, p, 0, 0, c)),
                  pl.BlockSpec((HEADS_PER_LANE_BLOCK, nb, L, L), lambda p, b, c: (p, 0, 0, 0))],
        out_specs=pl.BlockSpec((1, 2, L, LANES), lambda p, b, c: (b, 0, c, p)),
        out_shape=jax.ShapeDtypeStruct((B, 2, half_s, D_MOBA), BF16),
        scratch_shapes=[pltpu.VMEM((2, HEADS_PER_LANE_BLOCK, S, L), F32)],
        compiler_params=_cparams(3),
        name="moba_attn",
    )(qm.reshape(B, 2, half_s, D_MOBA), km, vmT, sel, bias)
    return out.reshape(B, S, D_MOBA)


def _swa_kernel(sink_ref, q_ref, kp0_ref, kc0_ref, kp1_ref, kc1_ref, vp0_ref, vc0_ref, vp1_ref, vc1_ref,
                bias_ref, o_ref):
    c = pl.program_id(1)
    _swa_block(sink_ref, q_ref[0, 0], kp0_ref[0], kc0_ref[0], vp0_ref[0], vc0_ref[0], bias_ref,
               o_ref.at[0, 0], c > 0)
    _swa_block(sink_ref, q_ref[0, 1], kp1_ref[0], kc1_ref[0], vp1_ref[0], vc1_ref[0], bias_ref,
               o_ref.at[0, 1], None)


def _swa_block(sink_ref, q, k_prev, k_cur, vT_prev, vT_cur, bias_ref, o_ref, has_prev):
    W = SWA_BLOCK
    group = N_HEADS_SWA // N_KV_SWA
    kband = jnp.concatenate([k_prev, k_cur], axis=0)
    vbandT = jnp.concatenate([vT_prev, vT_cur], axis=1)
    q_rows = []
    for h in range(N_HEADS_SWA):
        qp = q[:, (h % group) * LANES:(h % group + 1) * LANES]
        lane = lax.broadcasted_iota(jnp.int32, qp.shape, 1)
        in_half = (lane >= HEAD_DIM) if h // group else (lane < HEAD_DIM)
        q_rows.append(jnp.where(in_half, qp, jnp.zeros_like(qp)))
    q_all = jnp.concatenate(q_rows, axis=0)
    sT = _dot_nt(kband, q_all) + bias_ref[...]
    if has_prev is not None:
        key = lax.broadcasted_iota(jnp.int32, sT.shape, 0)
        sT = jnp.where((key >= W) | has_prev, sT, NEG)
    sink = sink_ref[...]
    m = jnp.maximum(jnp.max(sT, axis=0, keepdims=True), sink)
    pr = jnp.exp2(sT - m)
    inv_l = 1.0 / (jnp.sum(pr, axis=0, keepdims=True) + jnp.exp2(sink - m))
    pr = pr.astype(BF16)
    cols = group * W
    outs = [jnp.dot(vbandT[g * HEAD_DIM:(g + 1) * HEAD_DIM, :], pr[:, g * cols:(g + 1) * cols],
                    preferred_element_type=F32) * inv_l[:, g * cols:(g + 1) * cols]
            for g in range(N_KV_SWA)]
    heads = [outs[h // group][:, (h % group) * W:(h % group + 1) * W] for h in range(N_HEADS_SWA)]
    o_ref[...] = jnp.concatenate(heads, axis=0).T.astype(BF16)


def _swa(sink_row, qs, ks, vsT, bias):
    B, S, _ = qs.shape
    W = SWA_BLOCK
    nq = S // W
    far = nq // 2
    prev = lambda c: jnp.maximum(c - 1, 0)
    k_spec = lambda f: pl.BlockSpec((1, W, D_SWA_KV), lambda b, c: (b, f(c), 0))
    v_spec = lambda f: pl.BlockSpec((1, D_SWA_KV, W), lambda b, c: (b, 0, f(c)))
    blocks = [prev, lambda c: c, lambda c: c + far - 1, lambda c: c + far]
    out = pl.pallas_call(
        _swa_kernel,
        grid=(B, far),
        in_specs=[pl.BlockSpec((1, N_HEADS_SWA * W), lambda b, c: (0, 0)),
                  pl.BlockSpec((1, 2, W, D_SWA_Q), lambda b, c: (b, 0, c, 0))]
                 + [k_spec(f) for f in blocks] + [v_spec(f) for f in blocks]
                 + [pl.BlockSpec((2 * W, N_HEADS_SWA * W), lambda b, c: (0, 0))],
        out_specs=pl.BlockSpec((1, 2, W, D_SWA_Q), lambda b, c: (b, 0, c, 0)),
        out_shape=jax.ShapeDtypeStruct((B, 2, S // 2, D_SWA_Q), BF16),
        compiler_params=_cparams(2),
        name="swa_attn",
    )(sink_row, qs.reshape(B, 2, S // 2, D_SWA_Q), ks, ks, ks, ks, vsT, vsT, vsT, vsT, bias)
    return out.reshape(B, S, D_SWA_Q)


def _outproj_kernel(x_ref, om_ref, os_ref, wm_ref, ws_ref, b_ref, mod_ref, g_ref, wrT_ref, br_ref,
                    x1_ref, hrt_ref, idx_ref, w_ref, rank_ref, counts_ref, carry_ref):
    attn = (jnp.dot(om_ref[...], wm_ref[...], preferred_element_type=F32)
            + jnp.dot(os_ref[...], ws_ref[...], preferred_element_type=F32) + b_ref[...])
    x1 = x_ref[...] + mod_ref[0, 2:3, :] * attn
    x1_ref[...] = x1
    h = _rmsnorm_rows(x1, g_ref[...]) * (1.0 + mod_ref[0, 4:5, :]) + mod_ref[0, 3:4, :]
    _store_row_tiles(hrt_ref, h)
    _route(h, wrT_ref, br_ref, idx_ref, w_ref, rank_ref, counts_ref, carry_ref)


def _store_row_tiles(ref, val, first_row=0):
    rows = val.shape[0]
    base = first_row * ROW_CHUNKS
    for c in range(ROW_CHUNKS):
        ref[pl.ds(base + c, rows, stride=ROW_CHUNKS), :] = val[:, c * LANES:(c + 1) * LANES]


def _load_row_tiles(ref, rows, lead=(), first_row=0):
    idx = tuple(lead)
    base = first_row * ROW_CHUNKS
    return jnp.concatenate([ref[idx + (pl.ds(base + c, rows, stride=ROW_CHUNKS), slice(None))]
                            for c in range(ROW_CHUNKS)], axis=1)


def _outproj(x2d, om, os_, w_out_m, w_out_s, b_out, mod6, g_ffn, w_routerT, b_router_col, seq_len):
    N, D = x2d.shape
    tm = 512
    per_seq = seq_len // tm
    row = lambda t: (t, 0)
    col = lambda t: (0, t)
    const = lambda t: (0, 0)
    return pl.pallas_call(
        _outproj_kernel,
        grid=(N // tm,),
        in_specs=[pl.BlockSpec((tm, D), row),
                  pl.BlockSpec((tm, D_MOBA), row),
                  pl.BlockSpec((tm, D_SWA_Q), row),
                  pl.BlockSpec((D_MOBA, D), const),
                  pl.BlockSpec((D_SWA_Q, D), const),
                  pl.BlockSpec((1, D), const),
                  pl.BlockSpec((1, 6, D), lambda t: (t // per_seq, 0, 0)),
                  pl.BlockSpec((1, D), const),
                  pl.BlockSpec((N_EXPERTS, D), const),
                  pl.BlockSpec((N_EXPERTS, 1), const)],
        out_specs=(pl.BlockSpec((tm, D), row),
                   pl.BlockSpec((tm * ROW_CHUNKS, LANES), row),
                   pl.BlockSpec((TOP_K, tm), col),
                   pl.BlockSpec((TOP_K, tm), col),
                   pl.BlockSpec((TOP_K, tm), col),
                   pl.BlockSpec((N_EXPERTS, LANES), const)),
        out_shape=(jax.ShapeDtypeStruct((N, D), F32),
                   jax.ShapeDtypeStruct((N * ROW_CHUNKS, LANES), F32),
                   jax.ShapeDtypeStruct((TOP_K, N), jnp.int32),
                   jax.ShapeDtypeStruct((TOP_K, N), F32),
                   jax.ShapeDtypeStruct((TOP_K, N), jnp.int32),
                   jax.ShapeDtypeStruct((N_EXPERTS, LANES), jnp.int32)),
        scratch_shapes=[pltpu.VMEM((N_EXPERTS, LANES), F32)],
        compiler_params=_cparams(1),
        name="outproj_route",
    )(x2d, om, os_, w_out_m, w_out_s, b_out, mod6, g_ffn, w_routerT, b_router_col)


def _route(h, wT_ref, b_ref, idx_ref, w_ref, rank_ref, counts_ref, carry_ref):
    @pl.when(pl.program_id(0) == 0)
    def _():
        carry_ref[...] = jnp.zeros_like(carry_ref)

    w = wT_ref[...]
    w_hi = w.astype(BF16)
    w_lo = (w - w_hi.astype(F32)).astype(BF16)
    h_hi = h.astype(BF16)
    h_lo = (h - h_hi.astype(F32)).astype(BF16)
    by_h_hi = _dot_nt(jnp.concatenate([w_hi, w_lo], axis=0), h_hi)
    logits = (by_h_hi[:N_EXPERTS] + by_h_hi[N_EXPERTS:] + _dot_nt(w_hi, h_lo)) + b_ref[...]
    tm = logits.shape[1]
    eidx = lax.broadcasted_iota(jnp.int32, logits.shape, 0)
    vals, idxs = [], []
    cur = logits
    for _ in range(TOP_K):
        m = jnp.max(cur, axis=0, keepdims=True)
        am = jnp.min(jnp.where(cur == m, eidx, N_EXPERTS), axis=0, keepdims=True)
        vals.append(m)
        idxs.append(am)
        cur = jnp.where(eidx == am, -jnp.inf, cur)
    exps = [jnp.exp(v - vals[0]) for v in vals]
    inv = 1.0 / functools.reduce(lambda a, b: a + b, exps)
    idx_ref[...] = jnp.concatenate(idxs, axis=0)
    w_ref[...] = jnp.concatenate([e * inv for e in exps], axis=0)

    member = functools.reduce(lambda a, b: a | b, [eidx == am for am in idxs])
    earlier = (lax.broadcasted_iota(jnp.int32, (tm, tm), 0)
               < lax.broadcasted_iota(jnp.int32, (tm, tm), 1)).astype(BF16)
    before = jnp.dot(member.astype(BF16), earlier, preferred_element_type=F32) + carry_ref[:, 0:1]
    rank_ref[...] = jnp.concatenate(
        [jnp.sum(jnp.where(eidx == am, before, 0.0), axis=0, keepdims=True) for am in idxs],
        axis=0).astype(jnp.int32)
    total = carry_ref[...] + jnp.sum(member.astype(F32), axis=1, keepdims=True)
    carry_ref[...] = total
    counts_ref[...] = total.astype(jnp.int32)


def _expert_mlp(h_bf, wgu, bgu, wdn, bdn):
    gu = jnp.dot(h_bf, wgu, preferred_element_type=F32) + bgu
    g = jnp.minimum(gu[:, :D_FF], SWIGLU_LIMIT)
    u = jnp.clip(gu[:, D_FF:], -SWIGLU_LIMIT, SWIGLU_LIMIT)
    act = (u + 1.0) * (g * _sigmoid(SWIGLU_ALPHA * g))
    return jnp.dot(act.astype(BF16), wdn, preferred_element_type=F32) + bdn


def _tile_plan(counts, n_tiles_max):
    tiles_per = (counts + MOE_TILE - 1) // MOE_TILE
    tile_end = jnp.cumsum(tiles_per)
    tile_start = tile_end - tiles_per
    t_ids = jnp.arange(n_tiles_max, dtype=jnp.int32)
    texp = jnp.minimum(jnp.sum((t_ids[:, None] >= tile_end[None, :]).astype(jnp.int32), axis=1),
                       N_EXPERTS - 1)
    onehot = (texp[:, None] == jnp.arange(N_EXPERTS, dtype=jnp.int32)[None, :]).astype(jnp.int32)
    local = t_ids - onehot @ tile_start
    nvalid = jnp.clip(onehot @ counts - local * MOE_TILE, 0, MOE_TILE)
    n_active = tile_end[-1]
    src_tile = jnp.minimum(t_ids, n_active - 1)
    last_tile = jnp.where(counts > 0, tile_end - 1, -1)
    tail = n_active + jnp.arange(N_EXPERTS, dtype=jnp.int32)
    zero_tiles = jnp.concatenate([last_tile, jnp.where(tail < n_tiles_max, tail, -1)])
    ids = jnp.arange(N_EXPERTS, dtype=jnp.int32)
    later = jnp.where((ids[None, :] > ids[:, None]) & (counts[None, :] > 0), ids[None, :], N_EXPERTS)
    next_expert = jnp.min(later, axis=1)
    next_expert = jnp.where(next_expert < N_EXPERTS, next_expert, -1)
    return (texp.astype(jnp.int32), nvalid.astype(jnp.int32), src_tile.astype(jnp.int32),
            (tile_start * MOE_TILE).astype(jnp.int32), zero_tiles.astype(jnp.int32),
            next_expert.astype(jnp.int32))


def _row_copy(src, src_row, dst, dst_row, sem):
    return pltpu.make_async_copy(src.at[pl.ds(pl.multiple_of(src_row * ROW_CHUNKS, ROW_CHUNKS), ROW_CHUNKS)],
                                 dst.at[pl.ds(pl.multiple_of(dst_row * ROW_CHUNKS, ROW_CHUNKS), ROW_CHUNKS)],
                                 sem)


DISPATCH_SLOTS = 3


def _dispatch_kernel(zero_tiles_ref, pos_ref, h_hbm, x_hbm, zbuf, hbuf, zsem, load_sem, row_sem):
    t = pl.program_id(0)
    n_steps = pl.num_programs(0)
    tile_rows = MOE_TILE * ROW_CHUNKS
    block_rows = DISPATCH_TOKENS * ROW_CHUNKS

    def block_load(step, slot):
        start = pl.multiple_of(step * block_rows, block_rows)
        return pltpu.make_async_copy(h_hbm.at[pl.ds(start, block_rows)], hbuf.at[slot], load_sem.at[slot])

    def rows_done(slot):
        for k in range(TOP_K):
            pltpu.make_async_copy(hbuf.at[slot], x_hbm.at[pl.ds(0, block_rows)], row_sem.at[slot]).wait()

    def zero_copy(i):
        start = pl.multiple_of(zero_tiles_ref[i] * tile_rows, tile_rows)
        return pltpu.make_async_copy(zbuf, x_hbm.at[pl.ds(start, tile_rows)], zsem)

    @pl.when(t == 0)
    def _():
        zbuf[...] = jnp.zeros_like(zbuf)
        for i in range(2 * N_EXPERTS):
            @pl.when(zero_tiles_ref[i] >= 0)
            def _():
                zero_copy(i).start()
        block_load(0, 0).start()
        if DISPATCH_SLOTS > 2:
            @pl.when(n_steps > 1)
            def _():
                block_load(1, 1).start()
        for i in range(2 * N_EXPERTS):
            @pl.when(zero_tiles_ref[i] >= 0)
            def _():
                zero_copy(i).wait()

    slot = t % DISPATCH_SLOTS
    block_load(t, slot).wait()

    def start_rows(r, carry):
        for k in range(TOP_K):
            _row_copy(hbuf.at[slot], r, x_hbm, pos_ref[0, 0, k * DISPATCH_TOKENS + r],
                      row_sem.at[slot]).start(priority=k % 2)
        return carry
    lax.fori_loop(0, DISPATCH_TOKENS, start_rows, 0, unroll=8)

    refill = (t + 2) % DISPATCH_SLOTS

    @pl.when(t > 0)
    def _():
        rows_done(refill)

    @pl.when(t + 2 < n_steps)
    def _():
        block_load(t + 2, refill).start()

    @pl.when(t == n_steps - 1)
    def _():
        rows_done(slot)


def _dispatch(zero_tiles, pos_tiles, h_rt, n_rows):
    n_steps = pos_tiles.shape[0]
    grid_spec = pltpu.PrefetchScalarGridSpec(
        num_scalar_prefetch=1,
        grid=(n_steps,),
        in_specs=[pl.BlockSpec((1, 1, TOP_K * DISPATCH_TOKENS), lambda t, zt: (t, 0, 0),
                               memory_space=pltpu.SMEM),
                  pl.BlockSpec(memory_space=pl.ANY)],
        out_specs=pl.BlockSpec(memory_space=pl.ANY),
        scratch_shapes=[pltpu.VMEM((MOE_TILE * ROW_CHUNKS, LANES), F32),
                        pltpu.VMEM((DISPATCH_SLOTS, DISPATCH_TOKENS * ROW_CHUNKS, LANES), F32),
                        pltpu.SemaphoreType.DMA(()),
                        pltpu.SemaphoreType.DMA((DISPATCH_SLOTS,)),
                        pltpu.SemaphoreType.DMA((DISPATCH_SLOTS,))],
    )
    return pl.pallas_call(
        _dispatch_kernel,
        grid_spec=grid_spec,
        out_shape=jax.ShapeDtypeStruct((n_rows * ROW_CHUNKS, LANES), F32),
        compiler_params=_cparams(1),
        name="moe_dispatch",
    )(zero_tiles, pos_tiles, h_rt)


def _experts_kernel(texp_ref, nvalid_ref, src_ref, next_ref, x_ref, wgu_hbm, bgu_ref, wdn_hbm, bdn_ref,
                    y_ref, wgu_f32, wdn_f32, wgu_bf, wdn_bf, wsem):
    t = pl.program_id(0)
    nv = nvalid_ref[t]
    e = texp_ref[t]
    prev_e = texp_ref[jnp.maximum(t - 1, 0)]

    def weight_loads(expert):
        return (pltpu.make_async_copy(wgu_hbm.at[expert], wgu_f32, wsem.at[0]),
                pltpu.make_async_copy(wdn_hbm.at[expert], wdn_f32, wsem.at[1]))

    @pl.when((nv > 0) & ((t == 0) | (e != prev_e)))
    def _():
        @pl.when(t == 0)
        def _():
            for cp in weight_loads(e):
                cp.start()
        for cp in weight_loads(e):
            cp.wait()
        rows = 256
        def cast_gu(i, carry):
            r0 = pl.multiple_of(i * rows, rows)
            wgu_bf[pl.ds(r0, rows), :] = wgu_f32[pl.ds(r0, rows), :].astype(BF16)
            return carry
        lax.fori_loop(0, D_MODEL // rows, cast_gu, 0)
        def cast_dn(i, carry):
            r0 = pl.multiple_of(i * rows, rows)
            wdn_bf[pl.ds(r0, rows), :] = wdn_f32[pl.ds(r0, rows), :].astype(BF16)
            return carry
        lax.fori_loop(0, D_FF // rows, cast_dn, 0)
        nxt = next_ref[e]

        @pl.when(nxt >= 0)
        def _():
            for cp in weight_loads(nxt):
                cp.start()

    def mlp_rows(first_row):
        x = _load_row_tiles(x_ref, MOE_MM_ROWS, first_row=first_row).astype(BF16)
        _store_row_tiles(y_ref, _expert_mlp(x, wgu_bf[...], bgu_ref[0], wdn_bf[...], bdn_ref[0]), first_row)

    n_groups = MOE_TILE // MOE_MM_ROWS
    for live in range(n_groups + 1):
        lo, hi = (live - 1) * MOE_MM_ROWS, live * MOE_MM_ROWS

        @pl.when((nv > lo) & (nv <= hi) if live else (nv == 0))
        def _():
            for g in range(live):
                mlp_rows(g * MOE_MM_ROWS)
            if live < n_groups:
                y_ref[pl.ds(live * MOE_MM_ROWS * ROW_CHUNKS, (n_groups - live) * MOE_MM_ROWS * ROW_CHUNKS), :] = (
                    jnp.zeros(((n_groups - live) * MOE_MM_ROWS * ROW_CHUNKS, LANES), F32))


def _experts(texp, nvalid, src_tile, next_expert, x_sorted, w_gate_up, b_gate_up, w_down, b_down):
    n_tiles = texp.shape[0]
    D = D_MODEL
    tile_rows = MOE_TILE * ROW_CHUNKS
    grid_spec = pltpu.PrefetchScalarGridSpec(
        num_scalar_prefetch=4,
        grid=(n_tiles,),
        in_specs=[pl.BlockSpec((tile_rows, LANES), lambda t, te, nv, st, nx: (st[t], 0)),
                  pl.BlockSpec(memory_space=pl.ANY),
                  pl.BlockSpec((1, 1, 2 * D_FF), lambda t, te, nv, st, nx: (te[t], 0, 0)),
                  pl.BlockSpec(memory_space=pl.ANY),
                  pl.BlockSpec((1, 1, D), lambda t, te, nv, st, nx: (te[t], 0, 0))],
        out_specs=pl.BlockSpec((tile_rows, LANES), lambda t, te, nv, st, nx: (t, 0)),
        scratch_shapes=[pltpu.VMEM((D, 2 * D_FF), F32),
                        pltpu.VMEM((D_FF, D), F32),
                        pltpu.VMEM((D, 2 * D_FF), BF16),
                        pltpu.VMEM((D_FF, D), BF16),
                        pltpu.SemaphoreType.DMA((2,))],
    )
    return pl.pallas_call(
        _experts_kernel,
        grid_spec=grid_spec,
        out_shape=jax.ShapeDtypeStruct(x_sorted.shape, F32),
        compiler_params=_cparams(1),
        name="moe_experts",
    )(texp, nvalid, src_tile, next_expert, x_sorted, w_gate_up, b_gate_up.reshape(N_EXPERTS, 1, 2 * D_FF),
      w_down, b_down.reshape(N_EXPERTS, 1, D))


def _combine_kernel(pos_ref, pos_next_ref, y_hbm, w_ref, x1_ref, mod_ref, g_ref, o_ref, ybuf, sem):
    t = pl.program_id(0)
    n_steps = pl.num_programs(0)
    slot = t % 2

    def start_gather(idx_ref, s):
        def body(pair, carry):
            for queue in range(2):
                i = 2 * pair + queue
                _row_copy(y_hbm, idx_ref[0, 0, i], ybuf.at[s], i, sem.at[s]).start(priority=queue)
            return carry
        lax.fori_loop(0, TOP_K * COMBINE_TOKENS // 2, body, 0, unroll=4)

    @pl.when(t == 0)
    def _():
        start_gather(pos_ref, 0)

    @pl.when(t + 1 < n_steps)
    def _():
        start_gather(pos_next_ref, 1 - slot)

    slot_rows = TOP_K * COMBINE_TOKENS * ROW_CHUNKS
    pltpu.make_async_copy(y_hbm.at[pl.ds(0, slot_rows)], ybuf.at[slot], sem.at[slot]).wait()

    w = w_ref[...]
    acc = w[:, 0:1] * _load_row_tiles(ybuf, COMBINE_TOKENS, (slot,))
    for k in range(1, TOP_K):
        acc = acc + w[:, k:k + 1] * _load_row_tiles(ybuf, COMBINE_TOKENS, (slot,), k * COMBINE_TOKENS)
    x2 = x1_ref[...] + mod_ref[0, 5:6, :] * acc
    o_ref[...] = _rmsnorm_rows(x2, g_ref[...])


def _combine(pos_tiles, y_sorted, top_w, x1, mod6, g_final, seq_len):
    N, D = x1.shape
    tm = COMBINE_TOKENS
    n_steps = N // tm
    per_seq = seq_len // tm
    return pl.pallas_call(
        _combine_kernel,
        grid=(n_steps,),
        in_specs=[pl.BlockSpec((1, 1, TOP_K * tm), lambda t: (t, 0, 0), memory_space=pltpu.SMEM),
                  pl.BlockSpec((1, 1, TOP_K * tm), lambda t: (jnp.minimum(t + 1, n_steps - 1), 0, 0),
                               memory_space=pltpu.SMEM),
                  pl.BlockSpec(memory_space=pl.ANY),
                  pl.BlockSpec((tm, TOP_K), lambda t: (t, 0)),
                  pl.BlockSpec((tm, D), lambda t: (t, 0)),
                  pl.BlockSpec((1, 6, D), lambda t: (t // per_seq, 0, 0)),
                  pl.BlockSpec((1, D), lambda t: (0, 0))],
        out_specs=pl.BlockSpec((tm, D), lambda t: (t, 0)),
        out_shape=jax.ShapeDtypeStruct((N, D), F32),
        scratch_shapes=[pltpu.VMEM((2, TOP_K * tm * ROW_CHUNKS, LANES), F32),
                        pltpu.SemaphoreType.DMA((2,))],
        compiler_params=_cparams(1),
        name="moe_combine",
    )(pos_tiles, pos_tiles, y_sorted, top_w, x1, mod6, g_final)


def kernel(x, c, g_attn, w_ada, b_ada, w_in, b_in, w_out, b_out, rel_table, sinks, g_ffn, w_router,
           b_router, w_gate_up, b_gate_up, w_down, b_down, g_final):
    B, S, D = x.shape
    assert w_ada.shape[0] == 1, "the final norm is fused into the last layer; one layer supported"
    l = 0
    nb = S // MOBA_BLOCK
    i2 = 3 * D_MOBA
    i3 = i2 + D_SWA_Q
    group = N_HEADS_SWA // N_KV_SWA
    q_scale = ATTN_SCALE * LOG2E

    def prep_in(w):
        lead = w.shape[:-1]
        q_swa = (w[..., i2:i3] * q_scale).reshape(lead + (N_KV_SWA, group, HEAD_DIM))
        q_swa = jnp.swapaxes(q_swa, -3, -2).reshape(lead + (D_SWA_Q,))
        return jnp.concatenate([w[..., :D_MOBA] * q_scale, w[..., D_MOBA:i2], q_swa, w[..., i3:]], axis=-1)

    moba_bias = _moba_bias(rel_table, nb)
    swa_bias = _swa_bias(rel_table)

    mod6 = _ada(c, w_ada[l], b_ada[l]).reshape(B, 6, D)
    w_in_l = prep_in(w_in[l]).astype(BF16)
    b_in_l = prep_in(b_in[l]).reshape(1, D_IN)
    qm, km, vmT, kmean, qs, ks, vsT = _inproj(x, mod6, g_attn[l].reshape(1, D), w_in_l, b_in_l)
    sel = _gate(qm, kmean)
    o_m = _moba(qm, km, vmT, sel, moba_bias)
    sink_row = jnp.repeat(sinks[l] * LOG2E, SWA_BLOCK).reshape(1, N_HEADS_SWA * SWA_BLOCK)
    o_s = _swa(sink_row, qs, ks, vsT, swa_bias)
    w_out_m = w_out[l, :D_MOBA].astype(BF16)
    w_out_s = w_out[l, D_MOBA:].astype(BF16)
    N = B * S
    x1, h_rt, top_idx, top_w, rank, counts = _outproj(
        x.reshape(N, D), o_m.reshape(N, D_MOBA), o_s.reshape(N, D_SWA_Q), w_out_m, w_out_s,
        b_out[l].reshape(1, D), mod6, g_ffn[l].reshape(1, D), w_router[l].T,
        b_router[l].reshape(N_EXPERTS, 1), S)
    counts = counts[:, 0]
    n_tiles_max = N * TOP_K // MOE_TILE + N_EXPERTS
    texp, nvalid, src_tile, group_start, zero_tiles, next_expert = _tile_plan(counts, n_tiles_max)
    experts = jnp.arange(N_EXPERTS, dtype=jnp.int32)
    pos = rank + jnp.sum(jnp.where(top_idx[..., None] == experts, group_start, 0), axis=-1)
    assert DISPATCH_TOKENS == COMBINE_TOKENS
    pos_tiles = pos.reshape(TOP_K, N // DISPATCH_TOKENS, DISPATCH_TOKENS).transpose(1, 0, 2).reshape(
        N // DISPATCH_TOKENS, 1, TOP_K * DISPATCH_TOKENS)
    x_sorted = _dispatch(zero_tiles, pos_tiles, h_rt, n_tiles_max * MOE_TILE)
    y_sorted = _experts(texp, nvalid, src_tile, next_expert, x_sorted, w_gate_up[l], b_gate_up[l], w_down[l],
                        b_down[l])
    y = _combine(pos_tiles, y_sorted, top_w.T, x1, mod6, g_final.reshape(1, D), S)
    return y.reshape(B, S, D)
```

```python
import functools
import math

import jax
import jax.numpy as jnp
from jax import lax
from jax.experimental import pallas as pl
from jax.experimental.pallas import tpu as pltpu

F32 = jnp.float32
BF16 = jnp.bfloat16

D_MODEL = 1024
HEAD_DIM = 64
N_HEADS_MOBA = 8
N_HEADS_SWA = 8
N_KV_SWA = 2
D_MOBA = N_HEADS_MOBA * HEAD_DIM
D_SWA_Q = N_HEADS_SWA * HEAD_DIM
D_SWA_KV = N_KV_SWA * HEAD_DIM
D_IN = 3 * D_MOBA + D_SWA_Q + 2 * D_SWA_KV
MOBA_BLOCK = 256
MOBA_TOPK = 3
SWA_WINDOW = 128
SWA_BLOCK = 128
REL_BUCKETS = 32
REL_MAX_DIST = 1024
REL_MAX_EXACT = REL_BUCKETS // 2
N_EXPERTS = 32
TOP_K = 4
D_FF = 1024
SWIGLU_LIMIT = 7.0
SWIGLU_ALPHA = 1.702
EPS = 1e-5
NEG = -1e30
ATTN_SCALE = HEAD_DIM ** -0.5
LOG2E = math.log2(math.e)

MOE_TILE = 512
MOE_MM_ROWS = 256
ROW_CHUNKS = D_MODEL // 128
DISPATCH_TOKENS = 512
COMBINE_TOKENS = 512
MOBA_PAIRS_PER_STEP = 2
LANES = 128
HEADS_PER_LANE_BLOCK = LANES // HEAD_DIM
VMEM_LIMIT_BYTES = 56 * 1024 * 1024

_REL_THRESHOLDS = tuple(
    math.ceil(REL_MAX_EXACT * (REL_MAX_DIST / REL_MAX_EXACT) ** (k / (REL_BUCKETS - REL_MAX_EXACT)) - 1e-9)
    for k in range(1, REL_BUCKETS - REL_MAX_EXACT))


def _cparams(n_axes):
    return pltpu.CompilerParams(dimension_semantics=("arbitrary",) * n_axes,
                                vmem_limit_bytes=VMEM_LIMIT_BYTES)


def _sigmoid(z):
    return 1.0 / (1.0 + jnp.exp(-z))


def _rmsnorm_rows(xf, g):
    ms = jnp.mean(xf * xf, axis=-1, keepdims=True)
    return xf * lax.rsqrt(ms + EPS) * g


def _dot_nt(a, b, **kw):
    return lax.dot_general(a, b, (((1,), (1,)), ((), ())), preferred_element_type=F32, **kw)


def _ada_kernel(c_ref, w_ref, b_ref, o_ref):
    c = c_ref[...]
    sc = c * _sigmoid(c)
    o_ref[...] = jnp.dot(sc, w_ref[...], preferred_element_type=F32,
                         precision=lax.Precision.HIGHEST) + b_ref[...]


def _ada(c, w_ada, b_ada):
    B = c.shape[0]
    n_out = w_ada.shape[1]
    bn = 1536
    return pl.pallas_call(
        _ada_kernel,
        grid=(n_out // bn,),
        in_specs=[pl.BlockSpec((B, D_MODEL), lambda n: (0, 0)),
                  pl.BlockSpec((D_MODEL, bn), lambda n: (0, n)),
                  pl.BlockSpec((1, bn), lambda n: (0, n))],
        out_specs=pl.BlockSpec((B, bn), lambda n: (0, n)),
        out_shape=jax.ShapeDtypeStruct((B, n_out), F32),
        compiler_params=_cparams(1),
        name="ada",
    )(c, w_ada, b_ada.reshape(1, n_out))


def _inproj_kernel(x_ref, mod_ref, g_ref, w_ref, b_ref,
                   qm_ref, km_ref, vmT_ref, kmean_ref, qs_ref, ks_ref, vsT_ref):
    xf = x_ref[0]
    shift = mod_ref[0, 0:1, :]
    scale = mod_ref[0, 1:2, :]
    h = _rmsnorm_rows(xf, g_ref[...]) * (1.0 + scale) + shift
    proj = jnp.dot(h.astype(BF16), w_ref[...], preferred_element_type=F32) + b_ref[...]
    i0, i1, i2 = D_MOBA, 2 * D_MOBA, 3 * D_MOBA
    i3 = i2 + D_SWA_Q
    i4 = i3 + D_SWA_KV
    qm_ref[0] = proj[:, :i0].astype(BF16)
    km = proj[:, i0:i1]
    km_ref[0] = km.astype(BF16)
    L = MOBA_BLOCK
    for blk in range(proj.shape[0] // L):
        rows = slice(blk * L, (blk + 1) * L)
        kmean_ref[0, blk] = jnp.broadcast_to(jnp.mean(km[rows], axis=0, keepdims=True), (8, D_MOBA))
        vmT = proj[rows, i1:i2].T.astype(BF16)
        vmT_ref[0, :, blk] = vmT.reshape(D_MOBA // LANES, LANES, L)
    qs_ref[0] = proj[:, i2:i3].astype(BF16)
    ks_ref[0] = proj[:, i3:i4].astype(BF16)
    vsT_ref[0] = proj[:, i4:].T.astype(BF16)


def _inproj(x, mod6, g_attn, w_in_bf, b_in):
    B, S, D = x.shape
    L = MOBA_BLOCK
    per_step = 2
    tm = per_step * L
    nb = S // L
    n_pairs = D_MOBA // LANES
    out_shape = (
        jax.ShapeDtypeStruct((B, S, D_MOBA), BF16),
        jax.ShapeDtypeStruct((B, S, D_MOBA), BF16),
        jax.ShapeDtypeStruct((B, n_pairs, nb, LANES, L), BF16),
        jax.ShapeDtypeStruct((B, nb, 8, D_MOBA), F32),
        jax.ShapeDtypeStruct((B, S, D_SWA_Q), BF16),
        jax.ShapeDtypeStruct((B, S, D_SWA_KV), BF16),
        jax.ShapeDtypeStruct((B, D_SWA_KV, S), BF16),
    )
    return pl.pallas_call(
        _inproj_kernel,
        grid=(B, S // tm),
        in_specs=[pl.BlockSpec((1, tm, D), lambda b, i: (b, i, 0)),
                  pl.BlockSpec((1, 6, D), lambda b, i: (b, 0, 0)),
                  pl.BlockSpec((1, D), lambda b, i: (0, 0)),
                  pl.BlockSpec((D, D_IN), lambda b, i: (0, 0)),
                  pl.BlockSpec((1, D_IN), lambda b, i: (0, 0))],
        out_specs=(pl.BlockSpec((1, tm, D_MOBA), lambda b, i: (b, i, 0)),
                   pl.BlockSpec((1, tm, D_MOBA), lambda b, i: (b, i, 0)),
                   pl.BlockSpec((1, n_pairs, per_step, LANES, L), lambda b, i: (b, 0, i, 0, 0)),
                   pl.BlockSpec((1, per_step, 8, D_MOBA), lambda b, i: (b, i, 0, 0)),
                   pl.BlockSpec((1, tm, D_SWA_Q), lambda b, i: (b, i, 0)),
                   pl.BlockSpec((1, tm, D_SWA_KV), lambda b, i: (b, i, 0)),
                   pl.BlockSpec((1, D_SWA_KV, tm), lambda b, i: (b, 0, i))),
        out_shape=out_shape,
        compiler_params=_cparams(2),
        name="inproj",
    )(x, mod6, g_attn, w_in_bf, b_in)


def _rel_bucket_static(n):
    n = max(n, 0)
    return n if n < REL_MAX_EXACT else REL_MAX_EXACT + sum(n >= t for t in _REL_THRESHOLDS)


def _rel_bias_values(dist, table_ref, h, d_min, d_max):
    lo, hi = _rel_bucket_static(d_min), _rel_bucket_static(d_max)
    n = jnp.maximum(dist, 0)
    val = jnp.full(n.shape, table_ref[lo, h], F32)
    for bkt in range(lo + 1, hi + 1):
        start = bkt if bkt <= REL_MAX_EXACT else _REL_THRESHOLDS[bkt - REL_MAX_EXACT - 1]
        val = jnp.where(n >= start, table_ref[bkt, h], val)
    return val


def _moba_bias_kernel(table_ref, o_ref):
    h = pl.program_id(0)
    L = MOBA_BLOCK
    key = lax.broadcasted_iota(jnp.int32, (L, L), 0)
    qry = lax.broadcasted_iota(jnp.int32, (L, L), 1)
    for dlt in range(o_ref.shape[1]):
        dist = dlt * L + qry - key
        val = _rel_bias_values(dist, table_ref, h, dlt * L - (L - 1), dlt * L + (L - 1))
        val = val * LOG2E
        o_ref[0, dlt] = jnp.where(dist >= 0, val, NEG) if dlt == 0 else val


def _moba_bias(rel_table, nb):
    L = MOBA_BLOCK
    return pl.pallas_call(
        _moba_bias_kernel,
        grid=(N_HEADS_MOBA,),
        in_specs=[pl.BlockSpec(memory_space=pltpu.SMEM)],
        out_specs=pl.BlockSpec((1, nb, L, L), lambda h: (h, 0, 0, 0)),
        out_shape=jax.ShapeDtypeStruct((N_HEADS_MOBA, nb, L, L), F32),
        compiler_params=_cparams(1),
        name="moba_bias",
    )(rel_table)


def _swa_bias_kernel(table_ref, o_ref):
    h = N_HEADS_MOBA + pl.program_id(0)
    W = SWA_BLOCK
    key = lax.broadcasted_iota(jnp.int32, (2 * W, W), 0)
    qry = lax.broadcasted_iota(jnp.int32, (2 * W, W), 1)
    dist = qry + W - key
    val = _rel_bias_values(dist, table_ref, h, 0, SWA_WINDOW - 1)
    o_ref[...] = jnp.where((dist >= 0) & (dist < SWA_WINDOW), val * LOG2E, NEG)


def _swa_bias(rel_table):
    W = SWA_BLOCK
    return pl.pallas_call(
        _swa_bias_kernel,
        grid=(N_HEADS_SWA,),
        in_specs=[pl.BlockSpec(memory_space=pltpu.SMEM)],
        out_specs=pl.BlockSpec((2 * W, W), lambda h: (0, h)),
        out_shape=jax.ShapeDtypeStruct((2 * W, N_HEADS_SWA * W), F32),
        compiler_params=_cparams(1),
        name="swa_bias",
    )(rel_table)


def _gate_kernel(q_ref, kmean_ref, sel_ref):
    S = q_ref.shape[1]
    nb = kmean_ref.shape[1]
    H = N_HEADS_MOBA
    q = q_ref[0]
    km = kmean_ref[0, :, 0, :]
    km_t = jnp.concatenate([km] * H, axis=0)
    row_h = lax.broadcasted_iota(jnp.int32, (H * nb, D_MOBA), 0) // nb
    col_h = lax.broadcasted_iota(jnp.int32, (H * nb, D_MOBA), 1) // HEAD_DIM
    km_t = jnp.where(row_h == col_h, km_t, 0.0)
    km_hi = km_t.astype(BF16)
    km_lo = (km_t - km_hi.astype(F32)).astype(BF16)
    g_all = _dot_nt(km_hi, q) + _dot_nt(km_lo, q)
    qblk = lax.broadcasted_iota(jnp.int32, (nb, S), 1) // MOBA_BLOCK
    jidx = lax.broadcasted_iota(jnp.int32, (nb, S), 0)
    past = jidx < qblk
    for h in range(H):
        g = g_all[h * nb:(h + 1) * nb, :]
        cnt = jnp.zeros((nb, S), jnp.int32)
        for jp in range(nb):
            gj = g[jp:jp + 1, :]
            ahead = (gj > g) | ((gj == g) & (jp < jidx))
            cnt = cnt + ((jp < qblk) & ahead).astype(jnp.int32)
        sel = (past & (cnt < MOBA_TOPK)).astype(F32)
        for j in range(nb):
            for half in range(2):
                sel_ref[0, h, j, half:half + 1, :] = sel[j:j + 1, half * (S // 2):(half + 1) * (S // 2)]


def _gate(qm, kmean):
    B, S, _ = qm.shape
    nb = kmean.shape[1]
    return pl.pallas_call(
        _gate_kernel,
        grid=(B,),
        in_specs=[pl.BlockSpec((1, S, D_MOBA), lambda b: (b, 0, 0)),
                  pl.BlockSpec((1, nb, 8, D_MOBA), lambda b: (b, 0, 0, 0))],
        out_specs=pl.BlockSpec((1, N_HEADS_MOBA, nb, 2, S // 2), lambda b: (b, 0, 0, 0, 0)),
        out_shape=jax.ShapeDtypeStruct((B, N_HEADS_MOBA, nb, 2, S // 2), F32),
        compiler_params=_cparams(1),
        name="moba_gate",
    )(qm, kmean)


def _moba_kernel(q_ref, k_ref, vT_ref, sel_ref, bias_ref, o_ref, s_buf):
    c = pl.program_id(2)
    L = MOBA_BLOCK
    nb = k_ref.shape[1] // L

    def attend(pair, half, n_past):
        lanes = slice(pair * LANES, (pair + 1) * LANES)
        q = q_ref[0, half, :, lanes]
        lane = lax.broadcasted_iota(jnp.int32, q.shape, 1)
        zero = jnp.zeros_like(q)
        q_heads = [jnp.where(lane < HEAD_DIM, q, zero), jnp.where(lane >= HEAD_DIM, q, zero)]
        outs = []
        for a in range(HEADS_PER_LANE_BLOCK):
            hd = pair * HEADS_PER_LANE_BLOCK + a
            chosen = [None if j == n_past else sel_ref[0, hd, j, half:half + 1, :] > 0.5
                      for j in range(n_past + 1)]
            m = None
            for j in range(n_past + 1):
                s = _dot_nt(k_ref[0, j * L:(j + 1) * L, lanes], q_heads[a]) + bias_ref[hd, n_past - j]
                if n_past > 0:
                    s_buf[half, hd, j * L:(j + 1) * L, :] = s
                mj = jnp.max(s, axis=0, keepdims=True)
                if chosen[j] is not None:
                    mj = jnp.where(chosen[j], mj, NEG)
                m = mj if m is None else jnp.maximum(m, mj)
            l = None
            acc = None
            for j in range(n_past + 1):
                shift = m if chosen[j] is None else jnp.where(chosen[j], m, -NEG)
                p = jnp.exp2((s_buf[half, hd, j * L:(j + 1) * L, :] if n_past > 0 else s) - shift)
                lj = jnp.sum(p, axis=0, keepdims=True)
                vT = vT_ref[0, pair, j, a * HEAD_DIM:(a + 1) * HEAD_DIM, :]
                oj = jnp.dot(vT, p.astype(BF16), preferred_element_type=F32)
                l = lj if l is None else l + lj
                acc = oj if acc is None else acc + oj
            outs.append(acc * (1.0 / l))
        o_ref[0, half, :, lanes] = jnp.concatenate(outs, axis=0).T.astype(BF16)

    def step(c_static):
        for pair in range(MOBA_PAIRS_PER_STEP):
            for half in range(2):
                attend(pair, half, c_static + half * (nb // 2))

    for c_static in range(nb // 2):
        pl.when(c == c_static)(functools.partial(step, c_static))


def _moba(qm, km, vmT, sel, bias):
    B, S, _ = qm.shape
    L = MOBA_BLOCK
    nb = S // L
    pairs = MOBA_PAIRS_PER_STEP
    heads = pairs * HEADS_PER_LANE_BLOCK
    width = pairs * LANES
    half_s = S // 2
    out = pl.pallas_call(
        _moba_kernel,
        grid=(D_MOBA // width, B, nb // 2),
        in_specs=[pl.BlockSpec((1, 2, L, width), lambda p, b, c: (b, 0, c, p)),
                  pl.BlockSpec((1, S, width), lambda p, b, c: (b, 0, p)),
                  pl.BlockSpec((1, pairs, nb, LANES, L), lambda p, b, c: (b, p, 0, 0, 0)),
                  pl.BlockSpec((1, heads, nb, 2, L), lambda p, b, c: (b, p, 0, 0, c)),
                  pl.BlockSpec((heads, nb, L, L), lambda p, b, c: (p, 0, 0, 0))],
        out_specs=pl.BlockSpec((1, 2, L, width), lambda p, b, c: (b, 0, c, p)),
        out_shape=jax.ShapeDtypeStruct((B, 2, half_s, D_MOBA), BF16),
        scratch_shapes=[pltpu.VMEM((2, heads, S, L), F32)],
        compiler_params=_cparams(3),
        name="moba_attn",
    )(qm.reshape(B, 2, half_s, D_MOBA), km, vmT, sel, bias)
    return out.reshape(B, S, D_MOBA)


def _swa_kernel(sink_ref, q_ref, kp0_ref, kc0_ref, kp1_ref, kc1_ref, vp0_ref, vc0_ref, vp1_ref, vc1_ref,
                bias_ref, o_ref):
    c = pl.program_id(1)
    _swa_block(sink_ref, q_ref[0, 0], kp0_ref[0], kc0_ref[0], vp0_ref[0], vc0_ref[0], bias_ref,
               o_ref.at[0, 0], c > 0)
    _swa_block(sink_ref, q_ref[0, 1], kp1_ref[0], kc1_ref[0], vp1_ref[0], vc1_ref[0], bias_ref,
               o_ref.at[0, 1], None)


def _swa_block(sink_ref, q, k_prev, k_cur, vT_prev, vT_cur, bias_ref, o_ref, has_prev):
    W = SWA_BLOCK
    group = N_HEADS_SWA // N_KV_SWA
    kband = jnp.concatenate([k_prev, k_cur], axis=0)
    vbandT = jnp.concatenate([vT_prev, vT_cur], axis=1)
    q_rows = []
    for h in range(N_HEADS_SWA):
        qp = q[:, (h % group) * LANES:(h % group + 1) * LANES]
        lane = lax.broadcasted_iota(jnp.int32, qp.shape, 1)
        in_half = (lane >= HEAD_DIM) if h // group else (lane < HEAD_DIM)
        q_rows.append(jnp.where(in_half, qp, jnp.zeros_like(qp)))
    q_all = jnp.concatenate(q_rows, axis=0)
    sT = _dot_nt(kband, q_all) + bias_ref[...]
    if has_prev is not None:
        key = lax.broadcasted_iota(jnp.int32, sT.shape, 0)
        sT = jnp.where((key >= W) | has_prev, sT, NEG)
    sink = sink_ref[...]
    m = jnp.maximum(jnp.max(sT, axis=0, keepdims=True), sink)
    pr = jnp.exp2(sT - m)
    inv_l = 1.0 / (jnp.sum(pr, axis=0, keepdims=True) + jnp.exp2(sink - m))
    pr = pr.astype(BF16)
    cols = group * W
    outs = [jnp.dot(vbandT[g * HEAD_DIM:(g + 1) * HEAD_DIM, :], pr[:, g * cols:(g + 1) * cols],
                    preferred_element_type=F32) * inv_l[:, g * cols:(g + 1) * cols]
            for g in range(N_KV_SWA)]
    heads = [outs[h // group][:, (h % group) * W:(h % group + 1) * W] for h in range(N_HEADS_SWA)]
    o_ref[...] = jnp.concatenate(heads, axis=0).T.astype(BF16)


def _swa(sink_row, qs, ks, vsT, bias):
    B, S, _ = qs.shape
    W = SWA_BLOCK
    nq = S // W
    far = nq // 2
    prev = lambda c: jnp.maximum(c - 1, 0)
    k_spec = lambda f: pl.BlockSpec((1, W, D_SWA_KV), lambda b, c: (b, f(c), 0))
    v_spec = lambda f: pl.BlockSpec((1, D_SWA_KV, W), lambda b, c: (b, 0, f(c)))
    blocks = [prev, lambda c: c, lambda c: c + far - 1, lambda c: c + far]
    out = pl.pallas_call(
        _swa_kernel,
        grid=(B, far),
        in_specs=[pl.BlockSpec((1, N_HEADS_SWA * W), lambda b, c: (0, 0)),
                  pl.BlockSpec((1, 2, W, D_SWA_Q), lambda b, c: (b, 0, c, 0))]
                 + [k_spec(f) for f in blocks] + [v_spec(f) for f in blocks]
                 + [pl.BlockSpec((2 * W, N_HEADS_SWA * W), lambda b, c: (0, 0))],
        out_specs=pl.BlockSpec((1, 2, W, D_SWA_Q), lambda b, c: (b, 0, c, 0)),
        out_shape=jax.ShapeDtypeStruct((B, 2, S // 2, D_SWA_Q), BF16),
        compiler_params=_cparams(2),
        name="swa_attn",
    )(sink_row, qs.reshape(B, 2, S // 2, D_SWA_Q), ks, ks, ks, ks, vsT, vsT, vsT, vsT, bias)
    return out.reshape(B, S, D_SWA_Q)


def _outproj_kernel(x_ref, om_ref, os_ref, wm_ref, ws_ref, b_ref, mod_ref, g_ref, wrT_ref, br_ref,
                    x1_ref, hrt_ref, idx_ref, w_ref, rank_ref, counts_ref, carry_ref):
    attn = (jnp.dot(om_ref[...], wm_ref[...], preferred_element_type=F32)
            + jnp.dot(os_ref[...], ws_ref[...], preferred_element_type=F32) + b_ref[...])
    x1 = x_ref[...] + mod_ref[0, 2:3, :] * attn
    x1_ref[...] = x1
    h = _rmsnorm_rows(x1, g_ref[...]) * (1.0 + mod_ref[0, 4:5, :]) + mod_ref[0, 3:4, :]
    _store_row_tiles(hrt_ref, h)
    _route(h, wrT_ref, br_ref, idx_ref, w_ref, rank_ref, counts_ref, carry_ref)


def _store_row_tiles(ref, val, first_row=0):
    rows = val.shape[0]
    base = first_row * ROW_CHUNKS
    for c in range(ROW_CHUNKS):
        ref[pl.ds(base + c, rows, stride=ROW_CHUNKS), :] = val[:, c * LANES:(c + 1) * LANES]


def _load_row_tiles(ref, rows, lead=(), first_row=0):
    idx = tuple(lead)
    base = first_row * ROW_CHUNKS
    return jnp.concatenate([ref[idx + (pl.ds(base + c, rows, stride=ROW_CHUNKS), slice(None))]
                            for c in range(ROW_CHUNKS)], axis=1)


def _outproj(x2d, om, os_, w_out_m, w_out_s, b_out, mod6, g_ffn, w_routerT, b_router_col, seq_len):
    N, D = x2d.shape
    tm = 512
    per_seq = seq_len // tm
    row = lambda t: (t, 0)
    col = lambda t: (0, t)
    const = lambda t: (0, 0)
    return pl.pallas_call(
        _outproj_kernel,
        grid=(N // tm,),
        in_specs=[pl.BlockSpec((tm, D), row),
                  pl.BlockSpec((tm, D_MOBA), row),
                  pl.BlockSpec((tm, D_SWA_Q), row),
                  pl.BlockSpec((D_MOBA, D), const),
                  pl.BlockSpec((D_SWA_Q, D), const),
                  pl.BlockSpec((1, D), const),
                  pl.BlockSpec((1, 6, D), lambda t: (t // per_seq, 0, 0)),
                  pl.BlockSpec((1, D), const),
                  pl.BlockSpec((N_EXPERTS, D), const),
                  pl.BlockSpec((N_EXPERTS, 1), const)],
        out_specs=(pl.BlockSpec((tm, D), row),
                   pl.BlockSpec((tm * ROW_CHUNKS, LANES), row),
                   pl.BlockSpec((TOP_K, tm), col),
                   pl.BlockSpec((TOP_K, tm), col),
                   pl.BlockSpec((TOP_K, tm), col),
                   pl.BlockSpec((N_EXPERTS, LANES), const)),
        out_shape=(jax.ShapeDtypeStruct((N, D), F32),
                   jax.ShapeDtypeStruct((N * ROW_CHUNKS, LANES), F32),
                   jax.ShapeDtypeStruct((TOP_K, N), jnp.int32),
                   jax.ShapeDtypeStruct((TOP_K, N), F32),
                   jax.ShapeDtypeStruct((TOP_K, N), jnp.int32),
                   jax.ShapeDtypeStruct((N_EXPERTS, LANES), jnp.int32)),
        scratch_shapes=[pltpu.VMEM((N_EXPERTS, LANES), F32)],
        compiler_params=_cparams(1),
        name="outproj_route",
    )(x2d, om, os_, w_out_m, w_out_s, b_out, mod6, g_ffn, w_routerT, b_router_col)


def _route(h, wT_ref, b_ref, idx_ref, w_ref, rank_ref, counts_ref, carry_ref):
    @pl.when(pl.program_id(0) == 0)
    def _():
        carry_ref[...] = jnp.zeros_like(carry_ref)

    w = wT_ref[...]
    w_hi = w.astype(BF16)
    w_lo = (w - w_hi.astype(F32)).astype(BF16)
    h_hi = h.astype(BF16)
    h_lo = (h - h_hi.astype(F32)).astype(BF16)
    by_h_hi = _dot_nt(jnp.concatenate([w_hi, w_lo], axis=0), h_hi)
    logits = (by_h_hi[:N_EXPERTS] + by_h_hi[N_EXPERTS:] + _dot_nt(w_hi, h_lo)) + b_ref[...]
    tm = logits.shape[1]
    eidx = lax.broadcasted_iota(jnp.int32, logits.shape, 0)
    vals, idxs = [], []
    cur = logits
    for _ in range(TOP_K):
        m = jnp.max(cur, axis=0, keepdims=True)
        am = jnp.min(jnp.where(cur == m, eidx, N_EXPERTS), axis=0, keepdims=True)
        vals.append(m)
        idxs.append(am)
        cur = jnp.where(eidx == am, -jnp.inf, cur)
    exps = [jnp.exp(v - vals[0]) for v in vals]
    inv = 1.0 / functools.reduce(lambda a, b: a + b, exps)
    idx_ref[...] = jnp.concatenate(idxs, axis=0)
    w_ref[...] = jnp.concatenate([e * inv for e in exps], axis=0)

    member = functools.reduce(lambda a, b: a | b, [eidx == am for am in idxs])
    earlier = (lax.broadcasted_iota(jnp.int32, (tm, tm), 0)
               < lax.broadcasted_iota(jnp.int32, (tm, tm), 1)).astype(BF16)
    before = jnp.dot(member.astype(BF16), earlier, preferred_element_type=F32) + carry_ref[:, 0:1]
    rank_ref[...] = jnp.concatenate(
        [jnp.sum(jnp.where(eidx == am, before, 0.0), axis=0, keepdims=True) for am in idxs],
        axis=0).astype(jnp.int32)
    total = carry_ref[...] + jnp.sum(member.astype(F32), axis=1, keepdims=True)
    carry_ref[...] = total
    counts_ref[...] = total.astype(jnp.int32)


def _expert_mlp(h_bf, wgu, bgu, wdn, bdn):
    gu = jnp.dot(h_bf, wgu, preferred_element_type=F32) + bgu
    g = jnp.minimum(gu[:, :D_FF], SWIGLU_LIMIT)
    u = jnp.clip(gu[:, D_FF:], -SWIGLU_LIMIT, SWIGLU_LIMIT)
    act = (u + 1.0) * (g * _sigmoid(SWIGLU_ALPHA * g))
    return jnp.dot(act.astype(BF16), wdn, preferred_element_type=F32) + bdn


def _tile_plan(counts, n_tiles_max):
    tiles_per = (counts + MOE_TILE - 1) // MOE_TILE
    tile_end = jnp.cumsum(tiles_per)
    tile_start = tile_end - tiles_per
    t_ids = jnp.arange(n_tiles_max, dtype=jnp.int32)
    texp = jnp.minimum(jnp.sum((t_ids[:, None] >= tile_end[None, :]).astype(jnp.int32), axis=1),
                       N_EXPERTS - 1)
    onehot = (texp[:, None] == jnp.arange(N_EXPERTS, dtype=jnp.int32)[None, :]).astype(jnp.int32)
    local = t_ids - onehot @ tile_start
    nvalid = jnp.clip(onehot @ counts - local * MOE_TILE, 0, MOE_TILE)
    n_active = tile_end[-1]
    src_tile = jnp.minimum(t_ids, n_active - 1)
    last_tile = jnp.where(counts > 0, tile_end - 1, -1)
    tail = n_active + jnp.arange(N_EXPERTS, dtype=jnp.int32)
    zero_tiles = jnp.concatenate([last_tile, jnp.where(tail < n_tiles_max, tail, -1)])
    ids = jnp.arange(N_EXPERTS, dtype=jnp.int32)
    later = jnp.where((ids[None, :] > ids[:, None]) & (counts[None, :] > 0), ids[None, :], N_EXPERTS)
    next_expert = jnp.min(later, axis=1)
    next_expert = jnp.where(next_expert < N_EXPERTS, next_expert, -1)
    return (texp.astype(jnp.int32), nvalid.astype(jnp.int32), src_tile.astype(jnp.int32),
            (tile_start * MOE_TILE).astype(jnp.int32), zero_tiles.astype(jnp.int32),
            next_expert.astype(jnp.int32))


def _row_copy(src, src_row, dst, dst_row, sem):
    return pltpu.make_async_copy(src.at[pl.ds(pl.multiple_of(src_row * ROW_CHUNKS, ROW_CHUNKS), ROW_CHUNKS)],
                                 dst.at[pl.ds(pl.multiple_of(dst_row * ROW_CHUNKS, ROW_CHUNKS), ROW_CHUNKS)],
                                 sem)


DISPATCH_SLOTS = 3


def _dispatch_kernel(zero_tiles_ref, pos_ref, h_hbm, x_hbm, zbuf, hbuf, zsem, load_sem, row_sem):
    t = pl.program_id(0)
    n_steps = pl.num_programs(0)
    tile_rows = MOE_TILE * ROW_CHUNKS
    block_rows = DISPATCH_TOKENS * ROW_CHUNKS

    def block_load(step, slot):
        start = pl.multiple_of(step * block_rows, block_rows)
        return pltpu.make_async_copy(h_hbm.at[pl.ds(start, block_rows)], hbuf.at[slot], load_sem.at[slot])

    def rows_done(slot):
        for k in range(TOP_K):
            pltpu.make_async_copy(hbuf.at[slot], x_hbm.at[pl.ds(0, block_rows)], row_sem.at[slot]).wait()

    def zero_copy(i):
        start = pl.multiple_of(zero_tiles_ref[i] * tile_rows, tile_rows)
        return pltpu.make_async_copy(zbuf, x_hbm.at[pl.ds(start, tile_rows)], zsem)

    @pl.when(t == 0)
    def _():
        zbuf[...] = jnp.zeros_like(zbuf)
        for i in range(2 * N_EXPERTS):
            @pl.when(zero_tiles_ref[i] >= 0)
            def _():
                zero_copy(i).start()
        block_load(0, 0).start()
        if DISPATCH_SLOTS > 2:
            @pl.when(n_steps > 1)
            def _():
                block_load(1, 1).start()
        for i in range(2 * N_EXPERTS):
            @pl.when(zero_tiles_ref[i] >= 0)
            def _():
                zero_copy(i).wait()

    slot = t % DISPATCH_SLOTS
    block_load(t, slot).wait()

    def start_rows(r, carry):
        for k in range(TOP_K):
            _row_copy(hbuf.at[slot], r, x_hbm, pos_ref[0, 0, k * DISPATCH_TOKENS + r],
                      row_sem.at[slot]).start(priority=k % 2)
        return carry
    lax.fori_loop(0, DISPATCH_TOKENS, start_rows, 0, unroll=8)

    refill = (t + 2) % DISPATCH_SLOTS

    @pl.when(t > 0)
    def _():
        rows_done(refill)

    @pl.when(t + 2 < n_steps)
    def _():
        block_load(t + 2, refill).start()

    @pl.when(t == n_steps - 1)
    def _():
        rows_done(slot)


def _dispatch(zero_tiles, pos_tiles, h_rt, n_rows):
    n_steps = pos_tiles.shape[0]
    grid_spec = pltpu.PrefetchScalarGridSpec(
        num_scalar_prefetch=1,
        grid=(n_steps,),
        in_specs=[pl.BlockSpec((1, 1, TOP_K * DISPATCH_TOKENS), lambda t, zt: (t, 0, 0),
                               memory_space=pltpu.SMEM),
                  pl.BlockSpec(memory_space=pl.ANY)],
        out_specs=pl.BlockSpec(memory_space=pl.ANY),
        scratch_shapes=[pltpu.VMEM((MOE_TILE * ROW_CHUNKS, LANES), F32),
                        pltpu.VMEM((DISPATCH_SLOTS, DISPATCH_TOKENS * ROW_CHUNKS, LANES), F32),
                        pltpu.SemaphoreType.DMA(()),
                        pltpu.SemaphoreType.DMA((DISPATCH_SLOTS,)),
                        pltpu.SemaphoreType.DMA((DISPATCH_SLOTS,))],
    )
    return pl.pallas_call(
        _dispatch_kernel,
        grid_spec=grid_spec,
        out_shape=jax.ShapeDtypeStruct((n_rows * ROW_CHUNKS, LANES), F32),
        compiler_params=_cparams(1),
        name="moe_dispatch",
    )(zero_tiles, pos_tiles, h_rt)


def _experts_kernel(texp_ref, nvalid_ref, src_ref, next_ref, x_ref, wgu_hbm, bgu_ref, wdn_hbm, bdn_ref,
                    y_ref, wgu_f32, wdn_f32, wgu_bf, wdn_bf, wsem):
    t = pl.program_id(0)
    nv = nvalid_ref[t]
    e = texp_ref[t]
    prev_e = texp_ref[jnp.maximum(t - 1, 0)]

    def weight_loads(expert):
        return (pltpu.make_async_copy(wgu_hbm.at[expert], wgu_f32, wsem.at[0]),
                pltpu.make_async_copy(wdn_hbm.at[expert], wdn_f32, wsem.at[1]))

    @pl.when((nv > 0) & ((t == 0) | (e != prev_e)))
    def _():
        @pl.when(t == 0)
        def _():
            for cp in weight_loads(e):
                cp.start()
        for cp in weight_loads(e):
            cp.wait()
        rows = 256
        def cast_gu(i, carry):
            r0 = pl.multiple_of(i * rows, rows)
            wgu_bf[pl.ds(r0, rows), :] = wgu_f32[pl.ds(r0, rows), :].astype(BF16)
            return carry
        lax.fori_loop(0, D_MODEL // rows, cast_gu, 0)
        def cast_dn(i, carry):
            r0 = pl.multiple_of(i * rows, rows)
            wdn_bf[pl.ds(r0, rows), :] = wdn_f32[pl.ds(r0, rows), :].astype(BF16)
            return carry
        lax.fori_loop(0, D_FF // rows, cast_dn, 0)
        nxt = next_ref[e]

        @pl.when(nxt >= 0)
        def _():
            for cp in weight_loads(nxt):
                cp.start()

    def mlp_rows(first_row):
        x = _load_row_tiles(x_ref, MOE_MM_ROWS, first_row=first_row).astype(BF16)
        _store_row_tiles(y_ref, _expert_mlp(x, wgu_bf[...], bgu_ref[0], wdn_bf[...], bdn_ref[0]), first_row)

    n_groups = MOE_TILE // MOE_MM_ROWS
    for live in range(n_groups + 1):
        lo, hi = (live - 1) * MOE_MM_ROWS, live * MOE_MM_ROWS

        @pl.when((nv > lo) & (nv <= hi) if live else (nv == 0))
        def _():
            for g in range(live):
                mlp_rows(g * MOE_MM_ROWS)
            if live < n_groups:
                y_ref[pl.ds(live * MOE_MM_ROWS * ROW_CHUNKS, (n_groups - live) * MOE_MM_ROWS * ROW_CHUNKS), :] = (
                    jnp.zeros(((n_groups - live) * MOE_MM_ROWS * ROW_CHUNKS, LANES), F32))


def _experts(texp, nvalid, src_tile, next_expert, x_sorted, w_gate_up, b_gate_up, w_down, b_down):
    n_tiles = texp.shape[0]
    D = D_MODEL
    tile_rows = MOE_TILE * ROW_CHUNKS
    grid_spec = pltpu.PrefetchScalarGridSpec(
        num_scalar_prefetch=4,
        grid=(n_tiles,),
        in_specs=[pl.BlockSpec((tile_rows, LANES), lambda t, te, nv, st, nx: (st[t], 0)),
                  pl.BlockSpec(memory_space=pl.ANY),
                  pl.BlockSpec((1, 1, 2 * D_FF), lambda t, te, nv, st, nx: (te[t], 0, 0)),
                  pl.BlockSpec(memory_space=pl.ANY),
                  pl.BlockSpec((1, 1, D), lambda t, te, nv, st, nx: (te[t], 0, 0))],
        out_specs=pl.BlockSpec((tile_rows, LANES), lambda t, te, nv, st, nx: (t, 0)),
        scratch_shapes=[pltpu.VMEM((D, 2 * D_FF), F32),
                        pltpu.VMEM((D_FF, D), F32),
                        pltpu.VMEM((D, 2 * D_FF), BF16),
                        pltpu.VMEM((D_FF, D), BF16),
                        pltpu.SemaphoreType.DMA((2,))],
    )
    return pl.pallas_call(
        _experts_kernel,
        grid_spec=grid_spec,
        out_shape=jax.ShapeDtypeStruct(x_sorted.shape, F32),
        compiler_params=_cparams(1),
        name="moe_experts",
    )(texp, nvalid, src_tile, next_expert, x_sorted, w_gate_up, b_gate_up.reshape(N_EXPERTS, 1, 2 * D_FF),
      w_down, b_down.reshape(N_EXPERTS, 1, D))


def _combine_kernel(pos_ref, pos_next_ref, y_hbm, w_ref, x1_ref, mod_ref, g_ref, o_ref, ybuf, sem):
    t = pl.program_id(0)
    n_steps = pl.num_programs(0)
    slot = t % 2

    def start_gather(idx_ref, s):
        def body(r, carry):
            for k in range(TOP_K):
                i = k * COMBINE_TOKENS + r
                _row_copy(y_hbm, idx_ref[0, 0, i], ybuf.at[s], i, sem.at[s]).start(priority=k % 2)
            return carry
        lax.fori_loop(0, COMBINE_TOKENS, body, 0, unroll=8)

    @pl.when(t == 0)
    def _():
        start_gather(pos_ref, 0)

    @pl.when(t + 1 < n_steps)
    def _():
        start_gather(pos_next_ref, 1 - slot)

    slot_rows = TOP_K * COMBINE_TOKENS * ROW_CHUNKS
    pltpu.make_async_copy(y_hbm.at[pl.ds(0, slot_rows)], ybuf.at[slot], sem.at[slot]).wait()

    w = w_ref[...]
    acc = w[:, 0:1] * _load_row_tiles(ybuf, COMBINE_TOKENS, (slot,))
    for k in range(1, TOP_K):
        acc = acc + w[:, k:k + 1] * _load_row_tiles(ybuf, COMBINE_TOKENS, (slot,), k * COMBINE_TOKENS)
    x2 = x1_ref[...] + mod_ref[0, 5:6, :] * acc
    o_ref[...] = _rmsnorm_rows(x2, g_ref[...])


def _combine(pos_tiles, y_sorted, top_w, x1, mod6, g_final, seq_len):
    N, D = x1.shape
    tm = COMBINE_TOKENS
    n_steps = N // tm
    per_seq = seq_len // tm
    return pl.pallas_call(
        _combine_kernel,
        grid=(n_steps,),
        in_specs=[pl.BlockSpec((1, 1, TOP_K * tm), lambda t: (t, 0, 0), memory_space=pltpu.SMEM),
                  pl.BlockSpec((1, 1, TOP_K * tm), lambda t: (jnp.minimum(t + 1, n_steps - 1), 0, 0),
                               memory_space=pltpu.SMEM),
                  pl.BlockSpec(memory_space=pl.ANY),
                  pl.BlockSpec((tm, TOP_K), lambda t: (t, 0)),
                  pl.BlockSpec((tm, D), lambda t: (t, 0)),
                  pl.BlockSpec((1, 6, D), lambda t: (t // per_seq, 0, 0)),
                  pl.BlockSpec((1, D), lambda t: (0, 0))],
        out_specs=pl.BlockSpec((tm, D), lambda t: (t, 0)),
        out_shape=jax.ShapeDtypeStruct((N, D), F32),
        scratch_shapes=[pltpu.VMEM((2, TOP_K * tm * ROW_CHUNKS, LANES), F32),
                        pltpu.SemaphoreType.DMA((2,))],
        compiler_params=_cparams(1),
        name="moe_combine",
    )(pos_tiles, pos_tiles, y_sorted, top_w, x1, mod6, g_final)


def kernel(x, c, g_attn, w_ada, b_ada, w_in, b_in, w_out, b_out, rel_table, sinks, g_ffn, w_router,
           b_router, w_gate_up, b_gate_up, w_down, b_down, g_final):
    B, S, D = x.shape
    assert w_ada.shape[0] == 1, "the final norm is fused into the last layer; one layer supported"
    l = 0
    nb = S // MOBA_BLOCK
    i2 = 3 * D_MOBA
    i3 = i2 + D_SWA_Q
    group = N_HEADS_SWA // N_KV_SWA
    q_scale = ATTN_SCALE * LOG2E

    def prep_in(w):
        lead = w.shape[:-1]
        q_swa = (w[..., i2:i3] * q_scale).reshape(lead + (N_KV_SWA, group, HEAD_DIM))
        q_swa = jnp.swapaxes(q_swa, -3, -2).reshape(lead + (D_SWA_Q,))
        return jnp.concatenate([w[..., :D_MOBA] * q_scale, w[..., D_MOBA:i2], q_swa, w[..., i3:]], axis=-1)

    moba_bias = _moba_bias(rel_table, nb)
    swa_bias = _swa_bias(rel_table)

    mod6 = _ada(c, w_ada[l], b_ada[l]).reshape(B, 6, D)
    w_in_l = prep_in(w_in[l]).astype(BF16)
    b_in_l = prep_in(b_in[l]).reshape(1, D_IN)
    qm, km, vmT, kmean, qs, ks, vsT = _inproj(x, mod6, g_attn[l].reshape(1, D), w_in_l, b_in_l)
    sel = _gate(qm, kmean)
    o_m = _moba(qm, km, vmT, sel, moba_bias)
    sink_row = jnp.repeat(sinks[l] * LOG2E, SWA_BLOCK).reshape(1, N_HEADS_SWA * SWA_BLOCK)
    o_s = _swa(sink_row, qs, ks, vsT, swa_bias)
    w_out_m = w_out[l, :D_MOBA].astype(BF16)
    w_out_s = w_out[l, D_MOBA:].astype(BF16)
    N = B * S
    x1, h_rt, top_idx, top_w, rank, counts = _outproj(
        x.reshape(N, D), o_m.reshape(N, D_MOBA), o_s.reshape(N, D_SWA_Q), w_out_m, w_out_s,
        b_out[l].reshape(1, D), mod6, g_ffn[l].reshape(1, D), w_router[l].T,
        b_router[l].reshape(N_EXPERTS, 1), S)
    counts = counts[:, 0]
    n_tiles_max = N * TOP_K // MOE_TILE + N_EXPERTS
    texp, nvalid, src_tile, group_start, zero_tiles, next_expert = _tile_plan(counts, n_tiles_max)
    experts = jnp.arange(N_EXPERTS, dtype=jnp.int32)
    pos = rank + jnp.sum(jnp.where(top_idx[..., None] == experts, group_start, 0), axis=-1)
    assert DISPATCH_TOKENS == COMBINE_TOKENS
    pos_tiles = pos.reshape(TOP_K, N // DISPATCH_TOKENS, DISPATCH_TOKENS).transpose(1, 0, 2).reshape(
        N // DISPATCH_TOKENS, 1, TOP_K * DISPATCH_TOKENS)
    x_sorted = _dispatch(zero_tiles, pos_tiles, h_rt, n_tiles_max * MOE_TILE)
    y_sorted = _experts(texp, nvalid, src_tile, next_expert, x_sorted, w_gate_up[l], b_gate_up[l], w_down[l],
                        b_down[l])
    y = _combine(pos_tiles, y_sorted, top_w.T, x1, mod6, g_final.reshape(1, D), S)
    return y.reshape(B, S, D)
```

```python
import functools
import math

import jax
import jax.numpy as jnp
from jax import lax
from jax.experimental import pallas as pl
from jax.experimental.pallas import tpu as pltpu

F32 = jnp.float32
BF16 = jnp.bfloat16

D_MODEL = 1024
HEAD_DIM = 64
N_HEADS_MOBA = 8
N_HEADS_SWA = 8
N_KV_SWA = 2
D_MOBA = N_HEADS_MOBA * HEAD_DIM
D_SWA_Q = N_HEADS_SWA * HEAD_DIM
D_SWA_KV = N_KV_SWA * HEAD_DIM
D_IN = 3 * D_MOBA + D_SWA_Q + 2 * D_SWA_KV
MOBA_BLOCK = 256
MOBA_TOPK = 3
SWA_WINDOW = 128
SWA_BLOCK = 128
REL_BUCKETS = 32
REL_MAX_DIST = 1024
REL_MAX_EXACT = REL_BUCKETS // 2
N_EXPERTS = 32
TOP_K = 4
D_FF = 1024
SWIGLU_LIMIT = 7.0
SWIGLU_ALPHA = 1.702
EPS = 1e-5
NEG = -1e30
ATTN_SCALE = HEAD_DIM ** -0.5
LOG2E = math.log2(math.e)

MOE_TILE = 512
MOE_MM_ROWS = 256
ROW_CHUNKS = D_MODEL // 128
DISPATCH_TOKENS = 512
COMBINE_TOKENS = 512
MOBA_PAIRS_PER_STEP = 2
SWA_BLOCKS_PER_STEP = 4
LANES = 128
SUBLANES = 8
HEADS_PER_LANE_BLOCK = LANES // HEAD_DIM
ADA_COLS = 1536
OUTPROJ_TOKENS = 512
CAST_ROWS = 256
VMEM_LIMIT_BYTES = 56 * 1024 * 1024

_REL_THRESHOLDS = tuple(
    math.ceil(REL_MAX_EXACT * (REL_MAX_DIST / REL_MAX_EXACT) ** (k / (REL_BUCKETS - REL_MAX_EXACT)) - 1e-9)
    for k in range(1, REL_BUCKETS - REL_MAX_EXACT))


def _cparams(n_axes):
    return pltpu.CompilerParams(dimension_semantics=("arbitrary",) * n_axes,
                                vmem_limit_bytes=VMEM_LIMIT_BYTES)


def _sigmoid(z):
    return 1.0 / (1.0 + jnp.exp(-z))


def _rmsnorm_rows(xf, g):
    ms = jnp.mean(xf * xf, axis=-1, keepdims=True)
    return xf * lax.rsqrt(ms + EPS) * g


def _dot_nt(a, b, **kw):
    return lax.dot_general(a, b, (((1,), (1,)), ((), ())), preferred_element_type=F32, **kw)


def _ada_kernel(c_ref, w_ref, b_ref, o_ref):
    c = c_ref[...]
    sc = c * _sigmoid(c)
    o_ref[...] = jnp.dot(sc, w_ref[...], preferred_element_type=F32,
                         precision=lax.Precision.HIGHEST) + b_ref[...]


def _ada(c, w_ada, b_ada):
    B = c.shape[0]
    n_out = w_ada.shape[1]
    bn = ADA_COLS
    return pl.pallas_call(
        _ada_kernel,
        grid=(n_out // bn,),
        in_specs=[pl.BlockSpec((B, D_MODEL), lambda n: (0, 0)),
                  pl.BlockSpec((D_MODEL, bn), lambda n: (0, n)),
                  pl.BlockSpec((1, bn), lambda n: (0, n))],
        out_specs=pl.BlockSpec((B, bn), lambda n: (0, n)),
        out_shape=jax.ShapeDtypeStruct((B, n_out), F32),
        compiler_params=_cparams(1),
        name="ada",
    )(c, w_ada, b_ada.reshape(1, n_out))


def _inproj_kernel(x_ref, mod_ref, g_ref, w_ref, b_ref,
                   qm_ref, km_ref, vmT_ref, kmean_ref, qs_ref, ks_ref, vsT_ref):
    xf = x_ref[0]
    shift = mod_ref[0, 0:1, :]
    scale = mod_ref[0, 1:2, :]
    h = _rmsnorm_rows(xf, g_ref[...]) * (1.0 + scale) + shift
    proj = jnp.dot(h.astype(BF16), w_ref[...], preferred_element_type=F32) + b_ref[...]
    i0, i1, i2 = D_MOBA, 2 * D_MOBA, 3 * D_MOBA
    i3 = i2 + D_SWA_Q
    i4 = i3 + D_SWA_KV
    qm_ref[0] = proj[:, :i0].astype(BF16)
    km = proj[:, i0:i1]
    km_ref[0] = km.astype(BF16)
    L = MOBA_BLOCK
    for blk in range(proj.shape[0] // L):
        rows = slice(blk * L, (blk + 1) * L)
        kmean_ref[0, blk] = jnp.broadcast_to(jnp.mean(km[rows], axis=0, keepdims=True), (SUBLANES, D_MOBA))
        vmT = proj[rows, i1:i2].T.astype(BF16)
        vmT_ref[0, :, blk] = vmT.reshape(D_MOBA // LANES, LANES, L)
    qs_ref[0] = proj[:, i2:i3].astype(BF16)
    ks_ref[0] = proj[:, i3:i4].astype(BF16)
    vsT_ref[0] = proj[:, i4:].T.astype(BF16)


def _inproj(x, mod6, g_attn, w_in_bf, b_in):
    B, S, D = x.shape
    L = MOBA_BLOCK
    per_step = 2
    tm = per_step * L
    nb = S // L
    n_pairs = D_MOBA // LANES
    out_shape = (
        jax.ShapeDtypeStruct((B, S, D_MOBA), BF16),
        jax.ShapeDtypeStruct((B, S, D_MOBA), BF16),
        jax.ShapeDtypeStruct((B, n_pairs, nb, LANES, L), BF16),
        jax.ShapeDtypeStruct((B, nb, SUBLANES, D_MOBA), F32),
        jax.ShapeDtypeStruct((B, S, D_SWA_Q), BF16),
        jax.ShapeDtypeStruct((B, S, D_SWA_KV), BF16),
        jax.ShapeDtypeStruct((B, D_SWA_KV, S), BF16),
    )
    return pl.pallas_call(
        _inproj_kernel,
        grid=(B, S // tm),
        in_specs=[pl.BlockSpec((1, tm, D), lambda b, i: (b, i, 0)),
                  pl.BlockSpec((1, 6, D), lambda b, i: (b, 0, 0)),
                  pl.BlockSpec((1, D), lambda b, i: (0, 0)),
                  pl.BlockSpec((D, D_IN), lambda b, i: (0, 0)),
                  pl.BlockSpec((1, D_IN), lambda b, i: (0, 0))],
        out_specs=(pl.BlockSpec((1, tm, D_MOBA), lambda b, i: (b, i, 0)),
                   pl.BlockSpec((1, tm, D_MOBA), lambda b, i: (b, i, 0)),
                   pl.BlockSpec((1, n_pairs, per_step, LANES, L), lambda b, i: (b, 0, i, 0, 0)),
                   pl.BlockSpec((1, per_step, SUBLANES, D_MOBA), lambda b, i: (b, i, 0, 0)),
                   pl.BlockSpec((1, tm, D_SWA_Q), lambda b, i: (b, i, 0)),
                   pl.BlockSpec((1, tm, D_SWA_KV), lambda b, i: (b, i, 0)),
                   pl.BlockSpec((1, D_SWA_KV, tm), lambda b, i: (b, 0, i))),
        out_shape=out_shape,
        compiler_params=_cparams(2),
        name="inproj",
    )(x, mod6, g_attn, w_in_bf, b_in)


def _rel_bucket_static(n):
    n = max(n, 0)
    return n if n < REL_MAX_EXACT else REL_MAX_EXACT + sum(n >= t for t in _REL_THRESHOLDS)


def _rel_bias_values(dist, table_ref, h, d_min, d_max):
    lo, hi = _rel_bucket_static(d_min), _rel_bucket_static(d_max)
    n = jnp.maximum(dist, 0)
    val = jnp.full(n.shape, table_ref[lo, h], F32)
    for bkt in range(lo + 1, hi + 1):
        start = bkt if bkt <= REL_MAX_EXACT else _REL_THRESHOLDS[bkt - REL_MAX_EXACT - 1]
        val = jnp.where(n >= start, table_ref[bkt, h], val)
    return val


def _moba_bias_kernel(table_ref, o_ref):
    h = pl.program_id(0)
    L = MOBA_BLOCK
    key = lax.broadcasted_iota(jnp.int32, (L, L), 0)
    qry = lax.broadcasted_iota(jnp.int32, (L, L), 1)
    for dlt in range(o_ref.shape[1]):
        dist = dlt * L + qry - key
        val = _rel_bias_values(dist, table_ref, h, dlt * L - (L - 1), dlt * L + (L - 1))
        val = val * LOG2E
        o_ref[0, dlt] = jnp.where(dist >= 0, val, NEG) if dlt == 0 else val


def _moba_bias(rel_table, nb):
    L = MOBA_BLOCK
    return pl.pallas_call(
        _moba_bias_kernel,
        grid=(N_HEADS_MOBA,),
        in_specs=[pl.BlockSpec(memory_space=pltpu.SMEM)],
        out_specs=pl.BlockSpec((1, nb, L, L), lambda h: (h, 0, 0, 0)),
        out_shape=jax.ShapeDtypeStruct((N_HEADS_MOBA, nb, L, L), F32),
        compiler_params=_cparams(1),
        name="moba_bias",
    )(rel_table)


def _swa_bias_kernel(table_ref, o_ref):
    h = N_HEADS_MOBA + pl.program_id(0)
    W = SWA_BLOCK
    key = lax.broadcasted_iota(jnp.int32, (2 * W, W), 0)
    qry = lax.broadcasted_iota(jnp.int32, (2 * W, W), 1)
    dist = qry + W - key
    val = _rel_bias_values(dist, table_ref, h, 0, SWA_WINDOW - 1)
    o_ref[...] = jnp.where((dist >= 0) & (dist < SWA_WINDOW), val * LOG2E, NEG)


def _swa_bias(rel_table):
    W = SWA_BLOCK
    return pl.pallas_call(
        _swa_bias_kernel,
        grid=(N_HEADS_SWA,),
        in_specs=[pl.BlockSpec(memory_space=pltpu.SMEM)],
        out_specs=pl.BlockSpec((2 * W, W), lambda h: (0, h)),
        out_shape=jax.ShapeDtypeStruct((2 * W, N_HEADS_SWA * W), F32),
        compiler_params=_cparams(1),
        name="swa_bias",
    )(rel_table)


def _gate_kernel(q_ref, kmean_ref, sel_ref):
    S = q_ref.shape[1]
    nb = kmean_ref.shape[1]
    H = N_HEADS_MOBA
    q = q_ref[0]
    km = kmean_ref[0, :, 0, :]
    km_t = jnp.concatenate([km] * H, axis=0)
    row_h = lax.broadcasted_iota(jnp.int32, (H * nb, D_MOBA), 0) // nb
    col_h = lax.broadcasted_iota(jnp.int32, (H * nb, D_MOBA), 1) // HEAD_DIM
    km_t = jnp.where(row_h == col_h, km_t, 0.0)
    km_hi = km_t.astype(BF16)
    km_lo = (km_t - km_hi.astype(F32)).astype(BF16)
    g_all = _dot_nt(km_hi, q) + _dot_nt(km_lo, q)
    qblk = lax.broadcasted_iota(jnp.int32, (nb, S), 1) // MOBA_BLOCK
    jidx = lax.broadcasted_iota(jnp.int32, (nb, S), 0)
    past = jidx < qblk
    for h in range(H):
        g = g_all[h * nb:(h + 1) * nb, :]
        cnt = jnp.zeros((nb, S), jnp.int32)
        for jp in range(nb):
            gj = g[jp:jp + 1, :]
            ahead = (gj > g) | ((gj == g) & (jp < jidx))
            cnt = cnt + ((jp < qblk) & ahead).astype(jnp.int32)
        sel = (past & (cnt < MOBA_TOPK)).astype(F32)
        for j in range(nb):
            for half in range(2):
                sel_ref[0, h, j, half:half + 1, :] = sel[j:j + 1, half * (S // 2):(half + 1) * (S // 2)]


def _gate(qm, kmean):
    B, S, _ = qm.shape
    nb = kmean.shape[1]
    return pl.pallas_call(
        _gate_kernel,
        grid=(B,),
        in_specs=[pl.BlockSpec((1, S, D_MOBA), lambda b: (b, 0, 0)),
                  pl.BlockSpec((1, nb, SUBLANES, D_MOBA), lambda b: (b, 0, 0, 0))],
        out_specs=pl.BlockSpec((1, N_HEADS_MOBA, nb, 2, S // 2), lambda b: (b, 0, 0, 0, 0)),
        out_shape=jax.ShapeDtypeStruct((B, N_HEADS_MOBA, nb, 2, S // 2), F32),
        compiler_params=_cparams(1),
        name="moba_gate",
    )(qm, kmean)


def _moba_kernel(q_ref, k_ref, vT_ref, sel_ref, bias_ref, o_ref, s_buf):
    c = pl.program_id(2)
    L = MOBA_BLOCK
    nb = k_ref.shape[1] // L

    def attend(pair, half, n_past):
        lanes = slice(pair * LANES, (pair + 1) * LANES)
        q = q_ref[0, half, :, lanes]
        lane = lax.broadcasted_iota(jnp.int32, q.shape, 1)
        zero = jnp.zeros_like(q)
        q_heads = [jnp.where(lane < HEAD_DIM, q, zero), jnp.where(lane >= HEAD_DIM, q, zero)]
        outs = []
        for a in range(HEADS_PER_LANE_BLOCK):
            hd = pair * HEADS_PER_LANE_BLOCK + a
            chosen = [None if j == n_past else sel_ref[0, hd, j, half:half + 1, :] > 0.5
                      for j in range(n_past + 1)]
            m = None
            for j in range(n_past + 1):
                s = _dot_nt(k_ref[0, j * L:(j + 1) * L, lanes], q_heads[a]) + bias_ref[hd, n_past - j]
                if n_past > 0:
                    s_buf[half, hd, j * L:(j + 1) * L, :] = s
                mj = jnp.max(s, axis=0, keepdims=True)
                if chosen[j] is not None:
                    mj = jnp.where(chosen[j], mj, NEG)
                m = mj if m is None else jnp.maximum(m, mj)
            l = None
            acc = None
            for j in range(n_past + 1):
                shift = m if chosen[j] is None else jnp.where(chosen[j], m, -NEG)
                p = jnp.exp2((s_buf[half, hd, j * L:(j + 1) * L, :] if n_past > 0 else s) - shift)
                lj = jnp.sum(p, axis=0, keepdims=True)
                vT = vT_ref[0, pair, j, a * HEAD_DIM:(a + 1) * HEAD_DIM, :]
                oj = jnp.dot(vT, p.astype(BF16), preferred_element_type=F32)
                l = lj if l is None else l + lj
                acc = oj if acc is None else acc + oj
            outs.append(acc * (1.0 / l))
        o_ref[0, half, :, lanes] = jnp.concatenate(outs, axis=0).T.astype(BF16)

    def step(c_static):
        for pair in range(MOBA_PAIRS_PER_STEP):
            for half in range(2):
                attend(pair, half, c_static + half * (nb // 2))

    for c_static in range(nb // 2):
        pl.when(c == c_static)(functools.partial(step, c_static))


def _moba(qm, km, vmT, sel, bias):
    B, S, _ = qm.shape
    L = MOBA_BLOCK
    nb = S // L
    pairs = MOBA_PAIRS_PER_STEP
    heads = pairs * HEADS_PER_LANE_BLOCK
    width = pairs * LANES
    half_s = S // 2
    out = pl.pallas_call(
        _moba_kernel,
        grid=(D_MOBA // width, B, nb // 2),
        in_specs=[pl.BlockSpec((1, 2, L, width), lambda p, b, c: (b, 0, c, p)),
                  pl.BlockSpec((1, S, width), lambda p, b, c: (b, 0, p)),
                  pl.BlockSpec((1, pairs, nb, LANES, L), lambda p, b, c: (b, p, 0, 0, 0)),
                  pl.BlockSpec((1, heads, nb, 2, L), lambda p, b, c: (b, p, 0, 0, c)),
                  pl.BlockSpec((heads, nb, L, L), lambda p, b, c: (p, 0, 0, 0))],
        out_specs=pl.BlockSpec((1, 2, L, width), lambda p, b, c: (b, 0, c, p)),
        out_shape=jax.ShapeDtypeStruct((B, 2, half_s, D_MOBA), BF16),
        scratch_shapes=[pltpu.VMEM((2, heads, S, L), F32)],
        compiler_params=_cparams(3),
        name="moba_attn",
    )(qm.reshape(B, 2, half_s, D_MOBA), km, vmT, sel, bias)
    return out.reshape(B, S, D_MOBA)


def _swa_kernel(sink_ref, q_ref, *refs):
    n = SWA_BLOCKS_PER_STEP
    k_refs, v_refs = refs[:2 * n], refs[2 * n:4 * n]
    bias_ref, o_ref = refs[4 * n], refs[4 * n + 1]
    c = pl.program_id(1)
    for g in range(n):
        _swa_block(sink_ref, q_ref[0, g], k_refs[2 * g][0], k_refs[2 * g + 1][0], v_refs[2 * g][0],
                   v_refs[2 * g + 1][0], bias_ref, o_ref.at[0, g], c > 0 if g == 0 else None)


def _swa_block(sink_ref, q, k_prev, k_cur, vT_prev, vT_cur, bias_ref, o_ref, has_prev):
    W = SWA_BLOCK
    group = N_HEADS_SWA // N_KV_SWA
    kband = jnp.concatenate([k_prev, k_cur], axis=0)
    vbandT = jnp.concatenate([vT_prev, vT_cur], axis=1)
    q_rows = []
    for h in range(N_HEADS_SWA):
        qp = q[:, (h % group) * LANES:(h % group + 1) * LANES]
        lane = lax.broadcasted_iota(jnp.int32, qp.shape, 1)
        in_half = (lane >= HEAD_DIM) if h // group else (lane < HEAD_DIM)
        q_rows.append(jnp.where(in_half, qp, jnp.zeros_like(qp)))
    q_all = jnp.concatenate(q_rows, axis=0)
    sT = _dot_nt(kband, q_all) + bias_ref[...]
    if has_prev is not None:
        key = lax.broadcasted_iota(jnp.int32, sT.shape, 0)
        sT = jnp.where((key >= W) | has_prev, sT, NEG)
    sink = sink_ref[...]
    m = jnp.maximum(jnp.max(sT, axis=0, keepdims=True), sink)
    pr = jnp.exp2(sT - m)
    inv_l = 1.0 / (jnp.sum(pr, axis=0, keepdims=True) + jnp.exp2(sink - m))
    pr = pr.astype(BF16)
    cols = group * W
    outs = [jnp.dot(vbandT[g * HEAD_DIM:(g + 1) * HEAD_DIM, :], pr[:, g * cols:(g + 1) * cols],
                    preferred_element_type=F32) * inv_l[:, g * cols:(g + 1) * cols]
            for g in range(N_KV_SWA)]
    heads = [outs[h // group][:, (h % group) * W:(h % group + 1) * W] for h in range(N_HEADS_SWA)]
    o_ref[...] = jnp.concatenate(heads, axis=0).T.astype(BF16)


def _swa(sink_row, qs, ks, vsT, bias):
    B, S, _ = qs.shape
    W = SWA_BLOCK
    nq = S // W
    n = SWA_BLOCKS_PER_STEP
    far = nq // n
    k_spec = lambda f: pl.BlockSpec((1, W, D_SWA_KV), lambda b, c: (b, f(c), 0))
    v_spec = lambda f: pl.BlockSpec((1, D_SWA_KV, W), lambda b, c: (b, 0, f(c)))
    blocks = []
    for g in range(n):
        blocks.append((lambda c, g=g: jnp.maximum(c + g * far - 1, 0)))
        blocks.append((lambda c, g=g: c + g * far))
    out = pl.pallas_call(
        _swa_kernel,
        grid=(B, far),
        in_specs=[pl.BlockSpec((1, N_HEADS_SWA * W), lambda b, c: (0, 0)),
                  pl.BlockSpec((1, n, W, D_SWA_Q), lambda b, c: (b, 0, c, 0))]
                 + [k_spec(f) for f in blocks] + [v_spec(f) for f in blocks]
                 + [pl.BlockSpec((2 * W, N_HEADS_SWA * W), lambda b, c: (0, 0))],
        out_specs=pl.BlockSpec((1, n, W, D_SWA_Q), lambda b, c: (b, 0, c, 0)),
        out_shape=jax.ShapeDtypeStruct((B, n, S // n, D_SWA_Q), BF16),
        compiler_params=_cparams(2),
        name="swa_attn",
    )(sink_row, qs.reshape(B, n, S // n, D_SWA_Q), *([ks] * (2 * n)), *([vsT] * (2 * n)), bias)
    return out.reshape(B, S, D_SWA_Q)


def _outproj_kernel(x_ref, om_ref, os_ref, wm_ref, ws_ref, b_ref, mod_ref, g_ref, wrT_ref, br_ref,
                    x1_ref, hrt_ref, idx_ref, w_ref, rank_ref, counts_ref, carry_ref):
    attn = (jnp.dot(om_ref[...], wm_ref[...], preferred_element_type=F32)
            + jnp.dot(os_ref[...], ws_ref[...], preferred_element_type=F32) + b_ref[...])
    x1 = x_ref[...] + mod_ref[0, 2:3, :] * attn
    x1_ref[...] = x1
    h = _rmsnorm_rows(x1, g_ref[...]) * (1.0 + mod_ref[0, 4:5, :]) + mod_ref[0, 3:4, :]
    _store_row_tiles(hrt_ref, h)
    _route(h, wrT_ref, br_ref, idx_ref, w_ref, rank_ref, counts_ref, carry_ref)


def _store_row_tiles(ref, val, first_row=0):
    rows = val.shape[0]
    base = first_row * ROW_CHUNKS
    for c in range(ROW_CHUNKS):
        ref[pl.ds(base + c, rows, stride=ROW_CHUNKS), :] = val[:, c * LANES:(c + 1) * LANES]


def _load_row_tiles(ref, rows, lead=(), first_row=0):
    idx = tuple(lead)
    base = first_row * ROW_CHUNKS
    return jnp.concatenate([ref[idx + (pl.ds(base + c, rows, stride=ROW_CHUNKS), slice(None))]
                            for c in range(ROW_CHUNKS)], axis=1)


def _outproj(x2d, om, os_, w_out_m, w_out_s, b_out, mod6, g_ffn, w_routerT, b_router_col, seq_len):
    N, D = x2d.shape
    tm = OUTPROJ_TOKENS
    per_seq = seq_len // tm
    row = lambda t: (t, 0)
    col = lambda t: (0, t)
    const = lambda t: (0, 0)
    return pl.pallas_call(
        _outproj_kernel,
        grid=(N // tm,),
        in_specs=[pl.BlockSpec((tm, D), row),
                  pl.BlockSpec((tm, D_MOBA), row),
                  pl.BlockSpec((tm, D_SWA_Q), row),
                  pl.BlockSpec((D_MOBA, D), const),
                  pl.BlockSpec((D_SWA_Q, D), const),
                  pl.BlockSpec((1, D), const),
                  pl.BlockSpec((1, 6, D), lambda t: (t // per_seq, 0, 0)),
                  pl.BlockSpec((1, D), const),
                  pl.BlockSpec((N_EXPERTS, D), const),
                  pl.BlockSpec((N_EXPERTS, 1), const)],
        out_specs=(pl.BlockSpec((tm, D), row),
                   pl.BlockSpec((tm * ROW_CHUNKS, LANES), row),
                   pl.BlockSpec((TOP_K, tm), col),
                   pl.BlockSpec((TOP_K, tm), col),
                   pl.BlockSpec((TOP_K, tm), col),
                   pl.BlockSpec((N_EXPERTS, LANES), const)),
        out_shape=(jax.ShapeDtypeStruct((N, D), F32),
                   jax.ShapeDtypeStruct((N * ROW_CHUNKS, LANES), F32),
                   jax.ShapeDtypeStruct((TOP_K, N), jnp.int32),
                   jax.ShapeDtypeStruct((TOP_K, N), F32),
                   jax.ShapeDtypeStruct((TOP_K, N), jnp.int32),
                   jax.ShapeDtypeStruct((N_EXPERTS, LANES), jnp.int32)),
        scratch_shapes=[pltpu.VMEM((N_EXPERTS, LANES), F32)],
        compiler_params=_cparams(1),
        name="outproj_route",
    )(x2d, om, os_, w_out_m, w_out_s, b_out, mod6, g_ffn, w_routerT, b_router_col)


def _route(h, wT_ref, b_ref, idx_ref, w_ref, rank_ref, counts_ref, carry_ref):
    @pl.when(pl.program_id(0) == 0)
    def _():
        carry_ref[...] = jnp.zeros_like(carry_ref)

    w = wT_ref[...]
    w_hi = w.astype(BF16)
    w_lo = (w - w_hi.astype(F32)).astype(BF16)
    h_hi = h.astype(BF16)
    h_lo = (h - h_hi.astype(F32)).astype(BF16)
    by_h_hi = _dot_nt(jnp.concatenate([w_hi, w_lo], axis=0), h_hi)
    logits = (by_h_hi[:N_EXPERTS] + by_h_hi[N_EXPERTS:] + _dot_nt(w_hi, h_lo)) + b_ref[...]
    tm = logits.shape[1]
    eidx = lax.broadcasted_iota(jnp.int32, logits.shape, 0)
    vals, idxs = [], []
    cur = logits
    for _ in range(TOP_K):
        m = jnp.max(cur, axis=0, keepdims=True)
        am = jnp.min(jnp.where(cur == m, eidx, N_EXPERTS), axis=0, keepdims=True)
        vals.append(m)
        idxs.append(am)
        cur = jnp.where(eidx == am, -jnp.inf, cur)
    exps = [jnp.exp(v - vals[0]) for v in vals]
    inv = 1.0 / functools.reduce(lambda a, b: a + b, exps)
    idx_ref[...] = jnp.concatenate(idxs, axis=0)
    w_ref[...] = jnp.concatenate([e * inv for e in exps], axis=0)

    member = functools.reduce(lambda a, b: a | b, [eidx == am for am in idxs])
    earlier = (lax.broadcasted_iota(jnp.int32, (tm, tm), 0)
               < lax.broadcasted_iota(jnp.int32, (tm, tm), 1)).astype(BF16)
    before = jnp.dot(member.astype(BF16), earlier, preferred_element_type=F32) + carry_ref[:, 0:1]
    rank_ref[...] = jnp.concatenate(
        [jnp.sum(jnp.where(eidx == am, before, 0.0), axis=0, keepdims=True) for am in idxs],
        axis=0).astype(jnp.int32)
    total = carry_ref[...] + jnp.sum(member.astype(F32), axis=1, keepdims=True)
    carry_ref[...] = total
    counts_ref[...] = total.astype(jnp.int32)


def _expert_mlp(h_bf, wgu, bgu, wdn, bdn):
    gu = jnp.dot(h_bf, wgu, preferred_element_type=F32) + bgu
    g = jnp.minimum(gu[:, :D_FF], SWIGLU_LIMIT)
    u = jnp.clip(gu[:, D_FF:], -SWIGLU_LIMIT, SWIGLU_LIMIT)
    act = (u + 1.0) * (g * _sigmoid(SWIGLU_ALPHA * g))
    return jnp.dot(act.astype(BF16), wdn, preferred_element_type=F32) + bdn


def _tile_plan(counts, n_tiles_max):
    tiles_per = (counts + MOE_TILE - 1) // MOE_TILE
    tile_end = jnp.cumsum(tiles_per)
    tile_start = tile_end - tiles_per
    t_ids = jnp.arange(n_tiles_max, dtype=jnp.int32)
    texp = jnp.minimum(jnp.sum((t_ids[:, None] >= tile_end[None, :]).astype(jnp.int32), axis=1),
                       N_EXPERTS - 1)
    onehot = (texp[:, None] == jnp.arange(N_EXPERTS, dtype=jnp.int32)[None, :]).astype(jnp.int32)
    local = t_ids - onehot @ tile_start
    nvalid = jnp.clip(onehot @ counts - local * MOE_TILE, 0, MOE_TILE)
    n_active = tile_end[-1]
    src_tile = jnp.minimum(t_ids, n_active - 1)
    last_tile = jnp.where(counts > 0, tile_end - 1, -1)
    tail = n_active + jnp.arange(N_EXPERTS, dtype=jnp.int32)
    zero_tiles = jnp.concatenate([last_tile, jnp.where(tail < n_tiles_max, tail, -1)])
    ids = jnp.arange(N_EXPERTS, dtype=jnp.int32)
    later = jnp.where((ids[None, :] > ids[:, None]) & (counts[None, :] > 0), ids[None, :], N_EXPERTS)
    next_expert = jnp.min(later, axis=1)
    next_expert = jnp.where(next_expert < N_EXPERTS, next_expert, -1)
    return (texp.astype(jnp.int32), nvalid.astype(jnp.int32), src_tile.astype(jnp.int32),
            (tile_start * MOE_TILE).astype(jnp.int32), zero_tiles.astype(jnp.int32),
            next_expert.astype(jnp.int32))


def _row_copy(src, src_first, dst, dst_first, sem):
    return pltpu.make_async_copy(src.at[pl.ds(pl.multiple_of(src_first, ROW_CHUNKS), ROW_CHUNKS)],
                                 dst.at[pl.ds(pl.multiple_of(dst_first, ROW_CHUNKS), ROW_CHUNKS)],
                                 sem)


DISPATCH_SLOTS = 3


def _dispatch_kernel(zero_tiles_ref, pos_ref, h_hbm, x_hbm, zbuf, hbuf, zsem, load_sem, row_sem):
    t = pl.program_id(0)
    n_steps = pl.num_programs(0)
    tile_rows = MOE_TILE * ROW_CHUNKS
    block_rows = DISPATCH_TOKENS * ROW_CHUNKS

    def block_load(step, slot):
        start = pl.multiple_of(step * block_rows, block_rows)
        return pltpu.make_async_copy(h_hbm.at[pl.ds(start, block_rows)], hbuf.at[slot], load_sem.at[slot])

    def rows_done(slot):
        for k in range(TOP_K):
            pltpu.make_async_copy(hbuf.at[slot], x_hbm.at[pl.ds(0, block_rows)], row_sem.at[slot]).wait()

    def zero_copy(i):
        start = pl.multiple_of(zero_tiles_ref[i] * tile_rows, tile_rows)
        return pltpu.make_async_copy(zbuf, x_hbm.at[pl.ds(start, tile_rows)], zsem)

    @pl.when(t == 0)
    def _():
        zbuf[...] = jnp.zeros_like(zbuf)
        for i in range(2 * N_EXPERTS):
            @pl.when(zero_tiles_ref[i] >= 0)
            def _():
                zero_copy(i).start()
        block_load(0, 0).start()
        if DISPATCH_SLOTS > 2:
            @pl.when(n_steps > 1)
            def _():
                block_load(1, 1).start()
        for i in range(2 * N_EXPERTS):
            @pl.when(zero_tiles_ref[i] >= 0)
            def _():
                zero_copy(i).wait()

    slot = t % DISPATCH_SLOTS
    block_load(t, slot).wait()

    def start_rows(r, carry):
        for k in range(TOP_K):
            _row_copy(hbuf.at[slot], r * ROW_CHUNKS, x_hbm, pos_ref[0, 0, k * DISPATCH_TOKENS + r],
                      row_sem.at[slot]).start(priority=k % 2)
        return carry
    lax.fori_loop(0, DISPATCH_TOKENS, start_rows, 0, unroll=8)

    refill = (t + 2) % DISPATCH_SLOTS

    @pl.when(t > 0)
    def _():
        rows_done(refill)

    @pl.when(t + 2 < n_steps)
    def _():
        block_load(t + 2, refill).start()

    @pl.when(t == n_steps - 1)
    def _():
        rows_done(slot)


def _dispatch(zero_tiles, pos_tiles, h_rt, n_rows):
    n_steps = pos_tiles.shape[0]
    grid_spec = pltpu.PrefetchScalarGridSpec(
        num_scalar_prefetch=1,
        grid=(n_steps,),
        in_specs=[pl.BlockSpec((1, 1, TOP_K * DISPATCH_TOKENS), lambda t, zt: (t, 0, 0),
                               memory_space=pltpu.SMEM),
                  pl.BlockSpec(memory_space=pl.ANY)],
        out_specs=pl.BlockSpec(memory_space=pl.ANY),
        scratch_shapes=[pltpu.VMEM((MOE_TILE * ROW_CHUNKS, LANES), F32),
                        pltpu.VMEM((DISPATCH_SLOTS, DISPATCH_TOKENS * ROW_CHUNKS, LANES), F32),
                        pltpu.SemaphoreType.DMA(()),
                        pltpu.SemaphoreType.DMA((DISPATCH_SLOTS,)),
                        pltpu.SemaphoreType.DMA((DISPATCH_SLOTS,))],
    )
    return pl.pallas_call(
        _dispatch_kernel,
        grid_spec=grid_spec,
        out_shape=jax.ShapeDtypeStruct((n_rows * ROW_CHUNKS, LANES), F32),
        compiler_params=_cparams(1),
        name="moe_dispatch",
    )(zero_tiles, pos_tiles, h_rt)


def _experts_kernel(texp_ref, nvalid_ref, src_ref, next_ref, x_ref, wgu_hbm, bgu_ref, wdn_hbm, bdn_ref,
                    y_ref, wgu_f32, wdn_f32, wgu_bf, wdn_bf, wsem):
    t = pl.program_id(0)
    nv = nvalid_ref[t]
    e = texp_ref[t]
    prev_e = texp_ref[jnp.maximum(t - 1, 0)]

    def weight_loads(expert):
        return (pltpu.make_async_copy(wgu_hbm.at[expert], wgu_f32, wsem.at[0]),
                pltpu.make_async_copy(wdn_hbm.at[expert], wdn_f32, wsem.at[1]))

    @pl.when((nv > 0) & ((t == 0) | (e != prev_e)))
    def _():
        @pl.when(t == 0)
        def _():
            for cp in weight_loads(e):
                cp.start()
        for cp in weight_loads(e):
            cp.wait()
        rows = CAST_ROWS
        def cast_gu(i, carry):
            r0 = pl.multiple_of(i * rows, rows)
            wgu_bf[pl.ds(r0, rows), :] = wgu_f32[pl.ds(r0, rows), :].astype(BF16)
            return carry
        lax.fori_loop(0, D_MODEL // rows, cast_gu, 0)
        def cast_dn(i, carry):
            r0 = pl.multiple_of(i * rows, rows)
            wdn_bf[pl.ds(r0, rows), :] = wdn_f32[pl.ds(r0, rows), :].astype(BF16)
            return carry
        lax.fori_loop(0, D_FF // rows, cast_dn, 0)
        nxt = next_ref[e]

        @pl.when(nxt >= 0)
        def _():
            for cp in weight_loads(nxt):
                cp.start()

    def mlp_rows(first_row):
        x = _load_row_tiles(x_ref, MOE_MM_ROWS, first_row=first_row).astype(BF16)
        _store_row_tiles(y_ref, _expert_mlp(x, wgu_bf[...], bgu_ref[0], wdn_bf[...], bdn_ref[0]), first_row)

    n_groups = MOE_TILE // MOE_MM_ROWS
    for live in range(n_groups + 1):
        lo, hi = (live - 1) * MOE_MM_ROWS, live * MOE_MM_ROWS

        @pl.when((nv > lo) & (nv <= hi) if live else (nv == 0))
        def _():
            for g in range(live):
                mlp_rows(g * MOE_MM_ROWS)
            if live < n_groups:
                y_ref[pl.ds(live * MOE_MM_ROWS * ROW_CHUNKS, (n_groups - live) * MOE_MM_ROWS * ROW_CHUNKS), :] = (
                    jnp.zeros(((n_groups - live) * MOE_MM_ROWS * ROW_CHUNKS, LANES), F32))


def _experts(texp, nvalid, src_tile, next_expert, x_sorted, w_gate_up, b_gate_up, w_down, b_down):
    n_tiles = texp.shape[0]
    D = D_MODEL
    tile_rows = MOE_TILE * ROW_CHUNKS
    grid_spec = pltpu.PrefetchScalarGridSpec(
        num_scalar_prefetch=4,
        grid=(n_tiles,),
        in_specs=[pl.BlockSpec((tile_rows, LANES), lambda t, te, nv, st, nx: (st[t], 0)),
                  pl.BlockSpec(memory_space=pl.ANY),
                  pl.BlockSpec((1, 1, 2 * D_FF), lambda t, te, nv, st, nx: (te[t], 0, 0)),
                  pl.BlockSpec(memory_space=pl.ANY),
                  pl.BlockSpec((1, 1, D), lambda t, te, nv, st, nx: (te[t], 0, 0))],
        out_specs=pl.BlockSpec((tile_rows, LANES), lambda t, te, nv, st, nx: (t, 0)),
        scratch_shapes=[pltpu.VMEM((D, 2 * D_FF), F32),
                        pltpu.VMEM((D_FF, D), F32),
                        pltpu.VMEM((D, 2 * D_FF), BF16),
                        pltpu.VMEM((D_FF, D), BF16),
                        pltpu.SemaphoreType.DMA((2,))],
    )
    return pl.pallas_call(
        _experts_kernel,
        grid_spec=grid_spec,
        out_shape=jax.ShapeDtypeStruct(x_sorted.shape, F32),
        compiler_params=_cparams(1),
        name="moe_experts",
    )(texp, nvalid, src_tile, next_expert, x_sorted, w_gate_up, b_gate_up.reshape(N_EXPERTS, 1, 2 * D_FF),
      w_down, b_down.reshape(N_EXPERTS, 1, D))


def _combine_kernel(pos_ref, pos_next_ref, y_hbm, w_ref, x1_ref, mod_ref, g_ref, o_ref, ybuf, sem):
    t = pl.program_id(0)
    n_steps = pl.num_programs(0)
    slot = t % 2

    def start_gather(idx_ref, s):
        def body(r, carry):
            for k in range(TOP_K):
                i = k * COMBINE_TOKENS + r
                _row_copy(y_hbm, idx_ref[0, 0, i], ybuf.at[s], i * ROW_CHUNKS, sem.at[s]).start(priority=k % 2)
            return carry
        lax.fori_loop(0, COMBINE_TOKENS, body, 0, unroll=8)

    @pl.when(t == 0)
    def _():
        start_gather(pos_ref, 0)

    @pl.when(t + 1 < n_steps)
    def _():
        start_gather(pos_next_ref, 1 - slot)

    slot_rows = TOP_K * COMBINE_TOKENS * ROW_CHUNKS
    pltpu.make_async_copy(y_hbm.at[pl.ds(0, slot_rows)], ybuf.at[slot], sem.at[slot]).wait()

    w = w_ref[...]
    acc = w[:, 0:1] * _load_row_tiles(ybuf, COMBINE_TOKENS, (slot,))
    for k in range(1, TOP_K):
        acc = acc + w[:, k:k + 1] * _load_row_tiles(ybuf, COMBINE_TOKENS, (slot,), k * COMBINE_TOKENS)
    x2 = x1_ref[...] + mod_ref[0, 5:6, :] * acc
    o_ref[...] = _rmsnorm_rows(x2, g_ref[...])


def _combine(pos_tiles, y_sorted, top_w, x1, mod6, g_final, seq_len):
    N, D = x1.shape
    tm = COMBINE_TOKENS
    n_steps = N // tm
    per_seq = seq_len // tm
    return pl.pallas_call(
        _combine_kernel,
        grid=(n_steps,),
        in_specs=[pl.BlockSpec((1, 1, TOP_K * tm), lambda t: (t, 0, 0), memory_space=pltpu.SMEM),
                  pl.BlockSpec((1, 1, TOP_K * tm), lambda t: (jnp.minimum(t + 1, n_steps - 1), 0, 0),
                               memory_space=pltpu.SMEM),
                  pl.BlockSpec(memory_space=pl.ANY),
                  pl.BlockSpec((tm, TOP_K), lambda t: (t, 0)),
                  pl.BlockSpec((tm, D), lambda t: (t, 0)),
                  pl.BlockSpec((1, 6, D), lambda t: (t // per_seq, 0, 0)),
                  pl.BlockSpec((1, D), lambda t: (0, 0))],
        out_specs=pl.BlockSpec((tm, D), lambda t: (t, 0)),
        out_shape=jax.ShapeDtypeStruct((N, D), F32),
        scratch_shapes=[pltpu.VMEM((2, TOP_K * tm * ROW_CHUNKS, LANES), F32),
                        pltpu.SemaphoreType.DMA((2,))],
        compiler_params=_cparams(1),
        name="moe_combine",
    )(pos_tiles, pos_tiles, y_sorted, top_w, x1, mod6, g_final)


def kernel(x, c, g_attn, w_ada, b_ada, w_in, b_in, w_out, b_out, rel_table, sinks, g_ffn, w_router,
           b_router, w_gate_up, b_gate_up, w_down, b_down, g_final):
    B, S, D = x.shape
    assert w_ada.shape[0] == 1, "the final norm is fused into the last layer; one layer supported"
    l = 0
    nb = S // MOBA_BLOCK
    i2 = 3 * D_MOBA
    i3 = i2 + D_SWA_Q
    group = N_HEADS_SWA // N_KV_SWA
    q_scale = ATTN_SCALE * LOG2E

    def prep_in(w):
        lead = w.shape[:-1]
        q_swa = (w[..., i2:i3] * q_scale).reshape(lead + (N_KV_SWA, group, HEAD_DIM))
        q_swa = jnp.swapaxes(q_swa, -3, -2).reshape(lead + (D_SWA_Q,))
        return jnp.concatenate([w[..., :D_MOBA] * q_scale, w[..., D_MOBA:i2], q_swa, w[..., i3:]], axis=-1)

    moba_bias = _moba_bias(rel_table, nb)
    swa_bias = _swa_bias(rel_table)

    mod6 = _ada(c, w_ada[l], b_ada[l]).reshape(B, 6, D)
    w_in_l = prep_in(w_in[l]).astype(BF16)
    b_in_l = prep_in(b_in[l]).reshape(1, D_IN)
    qm, km, vmT, kmean, qs, ks, vsT = _inproj(x, mod6, g_attn[l].reshape(1, D), w_in_l, b_in_l)
    sel = _gate(qm, kmean)
    o_m = _moba(qm, km, vmT, sel, moba_bias)
    sink_row = jnp.repeat(sinks[l] * LOG2E, SWA_BLOCK).reshape(1, N_HEADS_SWA * SWA_BLOCK)
    o_s = _swa(sink_row, qs, ks, vsT, swa_bias)
    w_out_m = w_out[l, :D_MOBA].astype(BF16)
    w_out_s = w_out[l, D_MOBA:].astype(BF16)
    N = B * S
    x1, h_rt, top_idx, top_w, rank, counts = _outproj(
        x.reshape(N, D), o_m.reshape(N, D_MOBA), o_s.reshape(N, D_SWA_Q), w_out_m, w_out_s,
        b_out[l].reshape(1, D), mod6, g_ffn[l].reshape(1, D), w_router[l].T,
        b_router[l].reshape(N_EXPERTS, 1), S)
    counts = counts[:, 0]
    n_tiles_max = N * TOP_K // MOE_TILE + N_EXPERTS
    texp, nvalid, src_tile, group_start, zero_tiles, next_expert = _tile_plan(counts, n_tiles_max)
    experts = jnp.arange(N_EXPERTS, dtype=jnp.int32)
    pos = rank + jnp.sum(jnp.where(top_idx[..., None] == experts, group_start, 0), axis=-1)
    assert DISPATCH_TOKENS == COMBINE_TOKENS
    pos_tiles = (pos * ROW_CHUNKS).reshape(TOP_K, N // DISPATCH_TOKENS, DISPATCH_TOKENS).transpose(
        1, 0, 2).reshape(N // DISPATCH_TOKENS, 1, TOP_K * DISPATCH_TOKENS)
    x_sorted = _dispatch(zero_tiles, pos_tiles, h_rt, n_tiles_max * MOE_TILE)
    y_sorted = _experts(texp, nvalid, src_tile, next_expert, x_sorted, w_gate_up[l], b_gate_up[l], w_down[l],
                        b_down[l])
    y = _combine(pos_tiles, y_sorted, top_w.T, x1, mod6, g_final.reshape(1, D), S)
    return y.reshape(B, S, D)
```

```python
import functools
import math

import jax
import jax.numpy as jnp
from jax import lax
from jax.experimental import pallas as pl
from jax.experimental.pallas import tpu as pltpu

F32 = jnp.float32
BF16 = jnp.bfloat16

D_MODEL = 1024
HEAD_DIM = 64
N_HEADS_MOBA = 8
N_HEADS_SWA = 8
N_KV_SWA = 2
D_MOBA = N_HEADS_MOBA * HEAD_DIM
D_SWA_Q = N_HEADS_SWA * HEAD_DIM
D_SWA_KV = N_KV_SWA * HEAD_DIM
D_IN = 3 * D_MOBA + D_SWA_Q + 2 * D_SWA_KV
MOBA_BLOCK = 256
MOBA_TOPK = 3
SWA_WINDOW = 128
SWA_BLOCK = 128
REL_BUCKETS = 32
REL_MAX_DIST = 1024
REL_MAX_EXACT = REL_BUCKETS // 2
N_EXPERTS = 32
TOP_K = 4
D_FF = 1024
SWIGLU_LIMIT = 7.0
SWIGLU_ALPHA = 1.702
EPS = 1e-5
NEG = -1e30
ATTN_SCALE = HEAD_DIM ** -0.5
LOG2E = math.log2(math.e)

MOE_TILE = 512
MOE_MM_ROWS = 256
ROW_CHUNKS = D_MODEL // 128
DISPATCH_TOKENS = 512
COMBINE_TOKENS = 512
MOBA_PAIRS_PER_STEP = 4
SWA_BLOCKS_PER_STEP = 4
LANES = 128
SUBLANES = 8
HEADS_PER_LANE_BLOCK = LANES // HEAD_DIM
ADA_COLS = 1536
OUTPROJ_TOKENS = 512
CAST_ROWS = 256
VMEM_LIMIT_BYTES = 56 * 1024 * 1024

_REL_THRESHOLDS = tuple(
    math.ceil(REL_MAX_EXACT * (REL_MAX_DIST / REL_MAX_EXACT) ** (k / (REL_BUCKETS - REL_MAX_EXACT)) - 1e-9)
    for k in range(1, REL_BUCKETS - REL_MAX_EXACT))


def _cparams(n_axes):
    return pltpu.CompilerParams(dimension_semantics=("arbitrary",) * n_axes,
                                vmem_limit_bytes=VMEM_LIMIT_BYTES)


def _sigmoid(z):
    return 1.0 / (1.0 + jnp.exp(-z))


def _rmsnorm_rows(xf, g):
    ms = jnp.mean(xf * xf, axis=-1, keepdims=True)
    return xf * lax.rsqrt(ms + EPS) * g


def _dot_nt(a, b, **kw):
    return lax.dot_general(a, b, (((1,), (1,)), ((), ())), preferred_element_type=F32, **kw)


def _ada_kernel(c_ref, w_ref, b_ref, o_ref):
    c = c_ref[...]
    sc = c * _sigmoid(c)
    o_ref[...] = jnp.dot(sc, w_ref[...], preferred_element_type=F32,
                         precision=lax.Precision.HIGHEST) + b_ref[...]


def _ada(c, w_ada, b_ada):
    B = c.shape[0]
    n_out = w_ada.shape[1]
    bn = ADA_COLS
    return pl.pallas_call(
        _ada_kernel,
        grid=(n_out // bn,),
        in_specs=[pl.BlockSpec((B, D_MODEL), lambda n: (0, 0)),
                  pl.BlockSpec((D_MODEL, bn), lambda n: (0, n)),
                  pl.BlockSpec((1, bn), lambda n: (0, n))],
        out_specs=pl.BlockSpec((B, bn), lambda n: (0, n)),
        out_shape=jax.ShapeDtypeStruct((B, n_out), F32),
        compiler_params=_cparams(1),
        name="ada",
    )(c, w_ada, b_ada.reshape(1, n_out))


def _inproj_kernel(x_ref, mod_ref, g_ref, w_ref, b_ref,
                   qm_ref, km_ref, vmT_ref, kmean_ref, qs_ref, ks_ref, vsT_ref):
    xf = x_ref[0]
    shift = mod_ref[0, 0:1, :]
    scale = mod_ref[0, 1:2, :]
    h = _rmsnorm_rows(xf, g_ref[...]) * (1.0 + scale) + shift
    proj = jnp.dot(h.astype(BF16), w_ref[...], preferred_element_type=F32) + b_ref[...]
    i0, i1, i2 = D_MOBA, 2 * D_MOBA, 3 * D_MOBA
    i3 = i2 + D_SWA_Q
    i4 = i3 + D_SWA_KV
    qm_ref[0] = proj[:, :i0].astype(BF16)
    km = proj[:, i0:i1]
    km_ref[0] = km.astype(BF16)
    L = MOBA_BLOCK
    for blk in range(proj.shape[0] // L):
        rows = slice(blk * L, (blk + 1) * L)
        kmean_ref[0, blk] = jnp.broadcast_to(jnp.mean(km[rows], axis=0, keepdims=True), (SUBLANES, D_MOBA))
        vmT = proj[rows, i1:i2].T.astype(BF16)
        vmT_ref[0, :, blk] = vmT.reshape(D_MOBA // LANES, LANES, L)
    qs_ref[0] = proj[:, i2:i3].astype(BF16)
    ks_ref[0] = proj[:, i3:i4].astype(BF16)
    vsT_ref[0] = proj[:, i4:].T.astype(BF16)


def _inproj(x, mod6, g_attn, w_in_bf, b_in):
    B, S, D = x.shape
    L = MOBA_BLOCK
    per_step = 2
    tm = per_step * L
    nb = S // L
    n_pairs = D_MOBA // LANES
    out_shape = (
        jax.ShapeDtypeStruct((B, S, D_MOBA), BF16),
        jax.ShapeDtypeStruct((B, S, D_MOBA), BF16),
        jax.ShapeDtypeStruct((B, n_pairs, nb, LANES, L), BF16),
        jax.ShapeDtypeStruct((B, nb, SUBLANES, D_MOBA), F32),
        jax.ShapeDtypeStruct((B, S, D_SWA_Q), BF16),
        jax.ShapeDtypeStruct((B, S, D_SWA_KV), BF16),
        jax.ShapeDtypeStruct((B, D_SWA_KV, S), BF16),
    )
    return pl.pallas_call(
        _inproj_kernel,
        grid=(B, S // tm),
        in_specs=[pl.BlockSpec((1, tm, D), lambda b, i: (b, i, 0)),
                  pl.BlockSpec((1, 6, D), lambda b, i: (b, 0, 0)),
                  pl.BlockSpec((1, D), lambda b, i: (0, 0)),
                  pl.BlockSpec((D, D_IN), lambda b, i: (0, 0)),
                  pl.BlockSpec((1, D_IN), lambda b, i: (0, 0))],
        out_specs=(pl.BlockSpec((1, tm, D_MOBA), lambda b, i: (b, i, 0)),
                   pl.BlockSpec((1, tm, D_MOBA), lambda b, i: (b, i, 0)),
                   pl.BlockSpec((1, n_pairs, per_step, LANES, L), lambda b, i: (b, 0, i, 0, 0)),
                   pl.BlockSpec((1, per_step, SUBLANES, D_MOBA), lambda b, i: (b, i, 0, 0)),
                   pl.BlockSpec((1, tm, D_SWA_Q), lambda b, i: (b, i, 0)),
                   pl.BlockSpec((1, tm, D_SWA_KV), lambda b, i: (b, i, 0)),
                   pl.BlockSpec((1, D_SWA_KV, tm), lambda b, i: (b, 0, i))),
        out_shape=out_shape,
        compiler_params=_cparams(2),
        name="inproj",
    )(x, mod6, g_attn, w_in_bf, b_in)


def _rel_bucket_static(n):
    n = max(n, 0)
    return n if n < REL_MAX_EXACT else REL_MAX_EXACT + sum(n >= t for t in _REL_THRESHOLDS)


def _rel_bias_values(dist, table_ref, h, d_min, d_max):
    lo, hi = _rel_bucket_static(d_min), _rel_bucket_static(d_max)
    n = jnp.maximum(dist, 0)
    val = jnp.full(n.shape, table_ref[lo, h], F32)
    for bkt in range(lo + 1, hi + 1):
        start = bkt if bkt <= REL_MAX_EXACT else _REL_THRESHOLDS[bkt - REL_MAX_EXACT - 1]
        val = jnp.where(n >= start, table_ref[bkt, h], val)
    return val


def _moba_bias_kernel(table_ref, o_ref):
    h = pl.program_id(0)
    L = MOBA_BLOCK
    key = lax.broadcasted_iota(jnp.int32, (L, L), 0)
    qry = lax.broadcasted_iota(jnp.int32, (L, L), 1)
    for dlt in range(o_ref.shape[1]):
        dist = dlt * L + qry - key
        val = _rel_bias_values(dist, table_ref, h, dlt * L - (L - 1), dlt * L + (L - 1))
        val = val * LOG2E
        o_ref[0, dlt] = jnp.where(dist >= 0, val, NEG) if dlt == 0 else val


def _moba_bias(rel_table, nb):
    L = MOBA_BLOCK
    return pl.pallas_call(
        _moba_bias_kernel,
        grid=(N_HEADS_MOBA,),
        in_specs=[pl.BlockSpec(memory_space=pltpu.SMEM)],
        out_specs=pl.BlockSpec((1, nb, L, L), lambda h: (h, 0, 0, 0)),
        out_shape=jax.ShapeDtypeStruct((N_HEADS_MOBA, nb, L, L), F32),
        compiler_params=_cparams(1),
        name="moba_bias",
    )(rel_table)


def _swa_bias_kernel(table_ref, o_ref):
    h = N_HEADS_MOBA + pl.program_id(0)
    W = SWA_BLOCK
    key = lax.broadcasted_iota(jnp.int32, (2 * W, W), 0)
    qry = lax.broadcasted_iota(jnp.int32, (2 * W, W), 1)
    dist = qry + W - key
    val = _rel_bias_values(dist, table_ref, h, 0, SWA_WINDOW - 1)
    o_ref[...] = jnp.where((dist >= 0) & (dist < SWA_WINDOW), val * LOG2E, NEG)


def _swa_bias(rel_table):
    W = SWA_BLOCK
    return pl.pallas_call(
        _swa_bias_kernel,
        grid=(N_HEADS_SWA,),
        in_specs=[pl.BlockSpec(memory_space=pltpu.SMEM)],
        out_specs=pl.BlockSpec((2 * W, W), lambda h: (0, h)),
        out_shape=jax.ShapeDtypeStruct((2 * W, N_HEADS_SWA * W), F32),
        compiler_params=_cparams(1),
        name="swa_bias",
    )(rel_table)


def _gate_kernel(q_ref, kmean_ref, sel_ref):
    S = q_ref.shape[1]
    nb = kmean_ref.shape[1]
    H = N_HEADS_MOBA
    q = q_ref[0]
    km = kmean_ref[0, :, 0, :]
    km_t = jnp.concatenate([km] * H, axis=0)
    row_h = lax.broadcasted_iota(jnp.int32, (H * nb, D_MOBA), 0) // nb
    col_h = lax.broadcasted_iota(jnp.int32, (H * nb, D_MOBA), 1) // HEAD_DIM
    km_t = jnp.where(row_h == col_h, km_t, 0.0)
    km_hi = km_t.astype(BF16)
    km_lo = (km_t - km_hi.astype(F32)).astype(BF16)
    g_all = _dot_nt(km_hi, q) + _dot_nt(km_lo, q)
    qblk = lax.broadcasted_iota(jnp.int32, (nb, S), 1) // MOBA_BLOCK
    jidx = lax.broadcasted_iota(jnp.int32, (nb, S), 0)
    past = jidx < qblk
    for h in range(H):
        g = g_all[h * nb:(h + 1) * nb, :]
        cnt = jnp.zeros((nb, S), jnp.int32)
        for jp in range(nb):
            gj = g[jp:jp + 1, :]
            ahead = (gj > g) | ((gj == g) & (jp < jidx))
            cnt = cnt + ((jp < qblk) & ahead).astype(jnp.int32)
        sel = (past & (cnt < MOBA_TOPK)).astype(F32)
        for j in range(nb):
            for half in range(2):
                sel_ref[0, h, j, half:half + 1, :] = sel[j:j + 1, half * (S // 2):(half + 1) * (S // 2)]


def _gate(qm, kmean):
    B, S, _ = qm.shape
    nb = kmean.shape[1]
    return pl.pallas_call(
        _gate_kernel,
        grid=(B,),
        in_specs=[pl.BlockSpec((1, S, D_MOBA), lambda b: (b, 0, 0)),
                  pl.BlockSpec((1, nb, SUBLANES, D_MOBA), lambda b: (b, 0, 0, 0))],
        out_specs=pl.BlockSpec((1, N_HEADS_MOBA, nb, 2, S // 2), lambda b: (b, 0, 0, 0, 0)),
        out_shape=jax.ShapeDtypeStruct((B, N_HEADS_MOBA, nb, 2, S // 2), F32),
        compiler_params=_cparams(1),
        name="moba_gate",
    )(qm, kmean)


def _moba_kernel(q_ref, k_ref, vT_ref, sel_ref, bias_ref, o_ref, s_lo, s_hi):
    s_bufs = (s_lo, s_hi)
    c = pl.program_id(2)
    L = MOBA_BLOCK
    nb = k_ref.shape[1] // L

    def attend(pair, half, n_past):
        lanes = slice(pair * LANES, (pair + 1) * LANES)
        q = q_ref[0, half, :, lanes]
        lane = lax.broadcasted_iota(jnp.int32, q.shape, 1)
        zero = jnp.zeros_like(q)
        q_heads = [jnp.where(lane < HEAD_DIM, q, zero), jnp.where(lane >= HEAD_DIM, q, zero)]
        outs = []
        for a in range(HEADS_PER_LANE_BLOCK):
            hd = pair * HEADS_PER_LANE_BLOCK + a
            chosen = [None if j == n_past else sel_ref[0, hd, j, half:half + 1, :] > 0.5
                      for j in range(n_past + 1)]
            m = None
            for j in range(n_past + 1):
                s = _dot_nt(k_ref[0, j * L:(j + 1) * L, lanes], q_heads[a]) + bias_ref[hd, n_past - j]
                if n_past > 0:
                    s_bufs[half][hd, j * L:(j + 1) * L, :] = s
                mj = jnp.max(s, axis=0, keepdims=True)
                if chosen[j] is not None:
                    mj = jnp.where(chosen[j], mj, NEG)
                m = mj if m is None else jnp.maximum(m, mj)
            l = None
            acc = None
            for j in range(n_past + 1):
                shift = m if chosen[j] is None else jnp.where(chosen[j], m, -NEG)
                p = jnp.exp2((s_bufs[half][hd, j * L:(j + 1) * L, :] if n_past > 0 else s) - shift)
                lj = jnp.sum(p, axis=0, keepdims=True)
                vT = vT_ref[0, pair, j, a * HEAD_DIM:(a + 1) * HEAD_DIM, :]
                oj = jnp.dot(vT, p.astype(BF16), preferred_element_type=F32)
                l = lj if l is None else l + lj
                acc = oj if acc is None else acc + oj
            outs.append(acc * (1.0 / l))
        o_ref[0, half, :, lanes] = jnp.concatenate(outs, axis=0).T.astype(BF16)

    def step(c_static):
        for pair in range(MOBA_PAIRS_PER_STEP):
            for half in range(2):
                attend(pair, half, c_static + half * (nb // 2))

    for c_static in range(nb // 2):
        pl.when(c == c_static)(functools.partial(step, c_static))


def _moba(qm, km, vmT, sel, bias):
    B, S, _ = qm.shape
    L = MOBA_BLOCK
    nb = S // L
    pairs = MOBA_PAIRS_PER_STEP
    heads = pairs * HEADS_PER_LANE_BLOCK
    width = pairs * LANES
    half_s = S // 2
    out = pl.pallas_call(
        _moba_kernel,
        grid=(D_MOBA // width, B, nb // 2),
        in_specs=[pl.BlockSpec((1, 2, L, width), lambda p, b, c: (b, 0, c, p)),
                  pl.BlockSpec((1, S, width), lambda p, b, c: (b, 0, p)),
                  pl.BlockSpec((1, pairs, nb, LANES, L), lambda p, b, c: (b, p, 0, 0, 0)),
                  pl.BlockSpec((1, heads, nb, 2, L), lambda p, b, c: (b, p, 0, 0, c)),
                  pl.BlockSpec((heads, nb, L, L), lambda p, b, c: (p, 0, 0, 0),
                               pipeline_mode=pl.Buffered(1))],
        out_specs=pl.BlockSpec((1, 2, L, width), lambda p, b, c: (b, 0, c, p)),
        out_shape=jax.ShapeDtypeStruct((B, 2, half_s, D_MOBA), BF16),
        scratch_shapes=[pltpu.VMEM((heads, half_s, L), F32), pltpu.VMEM((heads, S, L), F32)],
        compiler_params=_cparams(3),
        name="moba_attn",
    )(qm.reshape(B, 2, half_s, D_MOBA), km, vmT, sel, bias)
    return out.reshape(B, S, D_MOBA)


def _swa_kernel(sink_ref, q_ref, *refs):
    n = SWA_BLOCKS_PER_STEP
    k_refs, v_refs = refs[:2 * n], refs[2 * n:4 * n]
    bias_ref, o_ref = refs[4 * n], refs[4 * n + 1]
    c = pl.program_id(1)
    for g in range(n):
        _swa_block(sink_ref, q_ref[0, g], k_refs[2 * g][0], k_refs[2 * g + 1][0], v_refs[2 * g][0],
                   v_refs[2 * g + 1][0], bias_ref, o_ref.at[0, g], c > 0 if g == 0 else None)


def _swa_block(sink_ref, q, k_prev, k_cur, vT_prev, vT_cur, bias_ref, o_ref, has_prev):
    W = SWA_BLOCK
    group = N_HEADS_SWA // N_KV_SWA
    kband = jnp.concatenate([k_prev, k_cur], axis=0)
    vbandT = jnp.concatenate([vT_prev, vT_cur], axis=1)
    q_rows = []
    for h in range(N_HEADS_SWA):
        qp = q[:, (h % group) * LANES:(h % group + 1) * LANES]
        lane = lax.broadcasted_iota(jnp.int32, qp.shape, 1)
        in_half = (lane >= HEAD_DIM) if h // group else (lane < HEAD_DIM)
        q_rows.append(jnp.where(in_half, qp, jnp.zeros_like(qp)))
    q_all = jnp.concatenate(q_rows, axis=0)
    sT = _dot_nt(kband, q_all) + bias_ref[...]
    if has_prev is not None:
        key = lax.broadcasted_iota(jnp.int32, sT.shape, 0)
        sT = jnp.where((key >= W) | has_prev, sT, NEG)
    sink = sink_ref[...]
    m = jnp.maximum(jnp.max(sT, axis=0, keepdims=True), sink)
    pr = jnp.exp2(sT - m)
    inv_l = 1.0 / (jnp.sum(pr, axis=0, keepdims=True) + jnp.exp2(sink - m))
    pr = pr.astype(BF16)
    cols = group * W
    outs = [jnp.dot(vbandT[g * HEAD_DIM:(g + 1) * HEAD_DIM, :], pr[:, g * cols:(g + 1) * cols],
                    preferred_element_type=F32) * inv_l[:, g * cols:(g + 1) * cols]
            for g in range(N_KV_SWA)]
    heads = [outs[h // group][:, (h % group) * W:(h % group + 1) * W] for h in range(N_HEADS_SWA)]
    o_ref[...] = jnp.concatenate(heads, axis=0).T.astype(BF16)


def _swa(sink_row, qs, ks, vsT, bias):
    B, S, _ = qs.shape
    W = SWA_BLOCK
    nq = S // W
    n = SWA_BLOCKS_PER_STEP
    far = nq // n
    k_spec = lambda f: pl.BlockSpec((1, W, D_SWA_KV), lambda b, c: (b, f(c), 0))
    v_spec = lambda f: pl.BlockSpec((1, D_SWA_KV, W), lambda b, c: (b, 0, f(c)))
    blocks = []
    for g in range(n):
        blocks.append((lambda c, g=g: jnp.maximum(c + g * far - 1, 0)))
        blocks.append((lambda c, g=g: c + g * far))
    out = pl.pallas_call(
        _swa_kernel,
        grid=(B, far),
        in_specs=[pl.BlockSpec((1, N_HEADS_SWA * W), lambda b, c: (0, 0)),
                  pl.BlockSpec((1, n, W, D_SWA_Q), lambda b, c: (b, 0, c, 0))]
                 + [k_spec(f) for f in blocks] + [v_spec(f) for f in blocks]
                 + [pl.BlockSpec((2 * W, N_HEADS_SWA * W), lambda b, c: (0, 0))],
        out_specs=pl.BlockSpec((1, n, W, D_SWA_Q), lambda b, c: (b, 0, c, 0)),
        out_shape=jax.ShapeDtypeStruct((B, n, S // n, D_SWA_Q), BF16),
        compiler_params=_cparams(2),
        name="swa_attn",
    )(sink_row, qs.reshape(B, n, S // n, D_SWA_Q), *([ks] * (2 * n)), *([vsT] * (2 * n)), bias)
    return out.reshape(B, S, D_SWA_Q)


def _outproj_kernel(x_ref, om_ref, os_ref, wm_ref, ws_ref, b_ref, mod_ref, g_ref, wrT_ref, br_ref,
                    x1_ref, hrt_ref, idx_ref, w_ref, rank_ref, counts_ref, carry_ref):
    attn = (jnp.dot(om_ref[...], wm_ref[...], preferred_element_type=F32)
            + jnp.dot(os_ref[...], ws_ref[...], preferred_element_type=F32) + b_ref[...])
    x1 = x_ref[...] + mod_ref[0, 2:3, :] * attn
    x1_ref[...] = x1
    h = _rmsnorm_rows(x1, g_ref[...]) * (1.0 + mod_ref[0, 4:5, :]) + mod_ref[0, 3:4, :]
    _store_row_tiles(hrt_ref, h)
    _route(h, wrT_ref, br_ref, idx_ref, w_ref, rank_ref, counts_ref, carry_ref)


def _store_row_tiles(ref, val, first_row=0):
    rows = val.shape[0]
    base = first_row * ROW_CHUNKS
    for c in range(ROW_CHUNKS):
        ref[pl.ds(base + c, rows, stride=ROW_CHUNKS), :] = val[:, c * LANES:(c + 1) * LANES]


def _load_row_tiles(ref, rows, lead=(), first_row=0):
    idx = tuple(lead)
    base = first_row * ROW_CHUNKS
    return jnp.concatenate([ref[idx + (pl.ds(base + c, rows, stride=ROW_CHUNKS), slice(None))]
                            for c in range(ROW_CHUNKS)], axis=1)


def _outproj(x2d, om, os_, w_out_m, w_out_s, b_out, mod6, g_ffn, w_routerT, b_router_col, seq_len):
    N, D = x2d.shape
    tm = OUTPROJ_TOKENS
    per_seq = seq_len // tm
    row = lambda t: (t, 0)
    col = lambda t: (0, t)
    const = lambda t: (0, 0)
    return pl.pallas_call(
        _outproj_kernel,
        grid=(N // tm,),
        in_specs=[pl.BlockSpec((tm, D), row),
                  pl.BlockSpec((tm, D_MOBA), row),
                  pl.BlockSpec((tm, D_SWA_Q), row),
                  pl.BlockSpec((D_MOBA, D), const),
                  pl.BlockSpec((D_SWA_Q, D), const),
                  pl.BlockSpec((1, D), const),
                  pl.BlockSpec((1, 6, D), lambda t: (t // per_seq, 0, 0)),
                  pl.BlockSpec((1, D), const),
                  pl.BlockSpec((N_EXPERTS, D), const),
                  pl.BlockSpec((N_EXPERTS, 1), const)],
        out_specs=(pl.BlockSpec((tm, D), row),
                   pl.BlockSpec((tm * ROW_CHUNKS, LANES), row),
                   pl.BlockSpec((TOP_K, tm), col),
                   pl.BlockSpec((TOP_K, tm), col),
                   pl.BlockSpec((TOP_K, tm), col),
                   pl.BlockSpec((N_EXPERTS, LANES), const)),
        out_shape=(jax.ShapeDtypeStruct((N, D), F32),
                   jax.ShapeDtypeStruct((N * ROW_CHUNKS, LANES), F32),
                   jax.ShapeDtypeStruct((TOP_K, N), jnp.int32),
                   jax.ShapeDtypeStruct((TOP_K, N), F32),
                   jax.ShapeDtypeStruct((TOP_K, N), jnp.int32),
                   jax.ShapeDtypeStruct((N_EXPERTS, LANES), jnp.int32)),
        scratch_shapes=[pltpu.VMEM((N_EXPERTS, LANES), F32)],
        compiler_params=_cparams(1),
        name="outproj_route",
    )(x2d, om, os_, w_out_m, w_out_s, b_out, mod6, g_ffn, w_routerT, b_router_col)


def _route(h, wT_ref, b_ref, idx_ref, w_ref, rank_ref, counts_ref, carry_ref):
    @pl.when(pl.program_id(0) == 0)
    def _():
        carry_ref[...] = jnp.zeros_like(carry_ref)

    w = wT_ref[...]
    w_hi = w.astype(BF16)
    w_lo = (w - w_hi.astype(F32)).astype(BF16)
    h_hi = h.astype(BF16)
    h_lo = (h - h_hi.astype(F32)).astype(BF16)
    by_h_hi = _dot_nt(jnp.concatenate([w_hi, w_lo], axis=0), h_hi)
    logits = (by_h_hi[:N_EXPERTS] + by_h_hi[N_EXPERTS:] + _dot_nt(w_hi, h_lo)) + b_ref[...]
    tm = logits.shape[1]
    eidx = lax.broadcasted_iota(jnp.int32, logits.shape, 0)
    vals, idxs = [], []
    cur = logits
    for _ in range(TOP_K):
        m = jnp.max(cur, axis=0, keepdims=True)
        am = jnp.min(jnp.where(cur == m, eidx, N_EXPERTS), axis=0, keepdims=True)
        vals.append(m)
        idxs.append(am)
        cur = jnp.where(eidx == am, -jnp.inf, cur)
    exps = [jnp.exp(v - vals[0]) for v in vals]
    inv = 1.0 / functools.reduce(lambda a, b: a + b, exps)
    idx_ref[...] = jnp.concatenate(idxs, axis=0)
    w_ref[...] = jnp.concatenate([e * inv for e in exps], axis=0)

    member = functools.reduce(lambda a, b: a | b, [eidx == am for am in idxs])
    earlier = (lax.broadcasted_iota(jnp.int32, (tm, tm), 0)
               < lax.broadcasted_iota(jnp.int32, (tm, tm), 1)).astype(BF16)
    before = jnp.dot(member.astype(BF16), earlier, preferred_element_type=F32) + carry_ref[:, 0:1]
    rank_ref[...] = jnp.concatenate(
        [jnp.sum(jnp.where(eidx == am, before, 0.0), axis=0, keepdims=True) for am in idxs],
        axis=0).astype(jnp.int32)
    total = carry_ref[...] + jnp.sum(member.astype(F32), axis=1, keepdims=True)
    carry_ref[...] = total
    counts_ref[...] = total.astype(jnp.int32)


def _expert_mlp(h_bf, wgu, bgu, wdn, bdn):
    gu = jnp.dot(h_bf, wgu, preferred_element_type=F32) + bgu
    g = jnp.minimum(gu[:, :D_FF], SWIGLU_LIMIT)
    u = jnp.clip(gu[:, D_FF:], -SWIGLU_LIMIT, SWIGLU_LIMIT)
    act = (u + 1.0) * (g * _sigmoid(SWIGLU_ALPHA * g))
    return jnp.dot(act.astype(BF16), wdn, preferred_element_type=F32) + bdn


def _tile_plan(counts, n_tiles_max):
    tiles_per = (counts + MOE_TILE - 1) // MOE_TILE
    tile_end = jnp.cumsum(tiles_per)
    tile_start = tile_end - tiles_per
    t_ids = jnp.arange(n_tiles_max, dtype=jnp.int32)
    texp = jnp.minimum(jnp.sum((t_ids[:, None] >= tile_end[None, :]).astype(jnp.int32), axis=1),
                       N_EXPERTS - 1)
    onehot = (texp[:, None] == jnp.arange(N_EXPERTS, dtype=jnp.int32)[None, :]).astype(jnp.int32)
    local = t_ids - onehot @ tile_start
    nvalid = jnp.clip(onehot @ counts - local * MOE_TILE, 0, MOE_TILE)
    n_active = tile_end[-1]
    src_tile = jnp.minimum(t_ids, n_active - 1)
    last_tile = jnp.where(counts > 0, tile_end - 1, -1)
    tail = n_active + jnp.arange(N_EXPERTS, dtype=jnp.int32)
    zero_tiles = jnp.concatenate([last_tile, jnp.where(tail < n_tiles_max, tail, -1)])
    ids = jnp.arange(N_EXPERTS, dtype=jnp.int32)
    later = jnp.where((ids[None, :] > ids[:, None]) & (counts[None, :] > 0), ids[None, :], N_EXPERTS)
    next_expert = jnp.min(later, axis=1)
    next_expert = jnp.where(next_expert < N_EXPERTS, next_expert, -1)
    return (texp.astype(jnp.int32), nvalid.astype(jnp.int32), src_tile.astype(jnp.int32),
            (tile_start * MOE_TILE).astype(jnp.int32), zero_tiles.astype(jnp.int32),
            next_expert.astype(jnp.int32))


def _row_copy(src, src_first, dst, dst_first, sem):
    return pltpu.make_async_copy(src.at[pl.ds(pl.multiple_of(src_first, ROW_CHUNKS), ROW_CHUNKS)],
                                 dst.at[pl.ds(pl.multiple_of(dst_first, ROW_CHUNKS), ROW_CHUNKS)],
                                 sem)


DISPATCH_SLOTS = 3


def _dispatch_kernel(zero_tiles_ref, pos_ref, h_hbm, x_hbm, zbuf, hbuf, zsem, load_sem, row_sem):
    t = pl.program_id(0)
    n_steps = pl.num_programs(0)
    tile_rows = MOE_TILE * ROW_CHUNKS
    block_rows = DISPATCH_TOKENS * ROW_CHUNKS

    def block_load(step, slot):
        start = pl.multiple_of(step * block_rows, block_rows)
        return pltpu.make_async_copy(h_hbm.at[pl.ds(start, block_rows)], hbuf.at[slot], load_sem.at[slot])

    def rows_done(slot):
        for k in range(TOP_K):
            pltpu.make_async_copy(hbuf.at[slot], x_hbm.at[pl.ds(0, block_rows)], row_sem.at[slot]).wait()

    def zero_copy(i):
        start = pl.multiple_of(zero_tiles_ref[i] * tile_rows, tile_rows)
        return pltpu.make_async_copy(zbuf, x_hbm.at[pl.ds(start, tile_rows)], zsem)

    @pl.when(t == 0)
    def _():
        zbuf[...] = jnp.zeros_like(zbuf)
        for i in range(2 * N_EXPERTS):
            @pl.when(zero_tiles_ref[i] >= 0)
            def _():
                zero_copy(i).start()
        block_load(0, 0).start()
        if DISPATCH_SLOTS > 2:
            @pl.when(n_steps > 1)
            def _():
                block_load(1, 1).start()
        for i in range(2 * N_EXPERTS):
            @pl.when(zero_tiles_ref[i] >= 0)
            def _():
                zero_copy(i).wait()

    slot = t % DISPATCH_SLOTS
    block_load(t, slot).wait()

    def start_rows(r, carry):
        for k in range(TOP_K):
            _row_copy(hbuf.at[slot], r * ROW_CHUNKS, x_hbm, pos_ref[0, 0, k * DISPATCH_TOKENS + r],
                      row_sem.at[slot]).start(priority=k % 2)
        return carry
    lax.fori_loop(0, DISPATCH_TOKENS, start_rows, 0, unroll=8)

    refill = (t + 2) % DISPATCH_SLOTS

    @pl.when(t > 0)
    def _():
        rows_done(refill)

    @pl.when(t + 2 < n_steps)
    def _():
        block_load(t + 2, refill).start()

    @pl.when(t == n_steps - 1)
    def _():
        rows_done(slot)


def _dispatch(zero_tiles, pos_tiles, h_rt, n_rows):
    n_steps = pos_tiles.shape[0]
    grid_spec = pltpu.PrefetchScalarGridSpec(
        num_scalar_prefetch=1,
        grid=(n_steps,),
        in_specs=[pl.BlockSpec((1, 1, TOP_K * DISPATCH_TOKENS), lambda t, zt: (t, 0, 0),
                               memory_space=pltpu.SMEM),
                  pl.BlockSpec(memory_space=pl.ANY)],
        out_specs=pl.BlockSpec(memory_space=pl.ANY),
        scratch_shapes=[pltpu.VMEM((MOE_TILE * ROW_CHUNKS, LANES), F32),
                        pltpu.VMEM((DISPATCH_SLOTS, DISPATCH_TOKENS * ROW_CHUNKS, LANES), F32),
                        pltpu.SemaphoreType.DMA(()),
                        pltpu.SemaphoreType.DMA((DISPATCH_SLOTS,)),
                        pltpu.SemaphoreType.DMA((DISPATCH_SLOTS,))],
    )
    return pl.pallas_call(
        _dispatch_kernel,
        grid_spec=grid_spec,
        out_shape=jax.ShapeDtypeStruct((n_rows * ROW_CHUNKS, LANES), F32),
        compiler_params=_cparams(1),
        name="moe_dispatch",
    )(zero_tiles, pos_tiles, h_rt)


def _experts_kernel(texp_ref, nvalid_ref, src_ref, next_ref, x_ref, wgu_hbm, bgu_ref, wdn_hbm, bdn_ref,
                    y_ref, wgu_f32, wdn_f32, wgu_bf, wdn_bf, wsem):
    t = pl.program_id(0)
    nv = nvalid_ref[t]
    e = texp_ref[t]
    prev_e = texp_ref[jnp.maximum(t - 1, 0)]

    def weight_loads(expert):
        return (pltpu.make_async_copy(wgu_hbm.at[expert], wgu_f32, wsem.at[0]),
                pltpu.make_async_copy(wdn_hbm.at[expert], wdn_f32, wsem.at[1]))

    @pl.when((nv > 0) & ((t == 0) | (e != prev_e)))
    def _():
        @pl.when(t == 0)
        def _():
            for cp in weight_loads(e):
                cp.start()
        for cp in weight_loads(e):
            cp.wait()
        rows = CAST_ROWS
        def cast_gu(i, carry):
            r0 = pl.multiple_of(i * rows, rows)
            wgu_bf[pl.ds(r0, rows), :] = wgu_f32[pl.ds(r0, rows), :].astype(BF16)
            return carry
        lax.fori_loop(0, D_MODEL // rows, cast_gu, 0)
        def cast_dn(i, carry):
            r0 = pl.multiple_of(i * rows, rows)
            wdn_bf[pl.ds(r0, rows), :] = wdn_f32[pl.ds(r0, rows), :].astype(BF16)
            return carry
        lax.fori_loop(0, D_FF // rows, cast_dn, 0)
        nxt = next_ref[e]

        @pl.when(nxt >= 0)
        def _():
            for cp in weight_loads(nxt):
                cp.start()

    def mlp_rows(first_row):
        x = _load_row_tiles(x_ref, MOE_MM_ROWS, first_row=first_row).astype(BF16)
        _store_row_tiles(y_ref, _expert_mlp(x, wgu_bf[...], bgu_ref[0], wdn_bf[...], bdn_ref[0]), first_row)

    n_groups = MOE_TILE // MOE_MM_ROWS
    for live in range(n_groups + 1):
        lo, hi = (live - 1) * MOE_MM_ROWS, live * MOE_MM_ROWS

        @pl.when((nv > lo) & (nv <= hi) if live else (nv == 0))
        def _():
            for g in range(live):
                mlp_rows(g * MOE_MM_ROWS)
            if live < n_groups:
                y_ref[pl.ds(live * MOE_MM_ROWS * ROW_CHUNKS, (n_groups - live) * MOE_MM_ROWS * ROW_CHUNKS), :] = (
                    jnp.zeros(((n_groups - live) * MOE_MM_ROWS * ROW_CHUNKS, LANES), F32))


def _experts(texp, nvalid, src_tile, next_expert, x_sorted, w_gate_up, b_gate_up, w_down, b_down):
    n_tiles = texp.shape[0]
    D = D_MODEL
    tile_rows = MOE_TILE * ROW_CHUNKS
    grid_spec = pltpu.PrefetchScalarGridSpec(
        num_scalar_prefetch=4,
        grid=(n_tiles,),
        in_specs=[pl.BlockSpec((tile_rows, LANES), lambda t, te, nv, st, nx: (st[t], 0)),
                  pl.BlockSpec(memory_space=pl.ANY),
                  pl.BlockSpec((1, 1, 2 * D_FF), lambda t, te, nv, st, nx: (te[t], 0, 0)),
                  pl.BlockSpec(memory_space=pl.ANY),
                  pl.BlockSpec((1, 1, D), lambda t, te, nv, st, nx: (te[t], 0, 0))],
        out_specs=pl.BlockSpec((tile_rows, LANES), lambda t, te, nv, st, nx: (t, 0)),
        scratch_shapes=[pltpu.VMEM((D, 2 * D_FF), F32),
                        pltpu.VMEM((D_FF, D), F32),
                        pltpu.VMEM((D, 2 * D_FF), BF16),
                        pltpu.VMEM((D_FF, D), BF16),
                        pltpu.SemaphoreType.DMA((2,))],
    )
    return pl.pallas_call(
        _experts_kernel,
        grid_spec=grid_spec,
        out_shape=jax.ShapeDtypeStruct(x_sorted.shape, F32),
        compiler_params=_cparams(1),
        name="moe_experts",
    )(texp, nvalid, src_tile, next_expert, x_sorted, w_gate_up, b_gate_up.reshape(N_EXPERTS, 1, 2 * D_FF),
      w_down, b_down.reshape(N_EXPERTS, 1, D))


def _combine_kernel(pos_ref, pos_next_ref, y_hbm, w_ref, x1_ref, mod_ref, g_ref, o_ref, ybuf, sem):
    t = pl.program_id(0)
    n_steps = pl.num_programs(0)
    slot = t % 2

    def start_gather(idx_ref, s):
        def body(r, carry):
            for k in range(TOP_K):
                i = k * COMBINE_TOKENS + r
                _row_copy(y_hbm, idx_ref[0, 0, i], ybuf.at[s], i * ROW_CHUNKS, sem.at[s]).start(priority=k % 2)
            return carry
        lax.fori_loop(0, COMBINE_TOKENS, body, 0, unroll=8)

    @pl.when(t == 0)
    def _():
        start_gather(pos_ref, 0)

    @pl.when(t + 1 < n_steps)
    def _():
        start_gather(pos_next_ref, 1 - slot)

    slot_rows = TOP_K * COMBINE_TOKENS * ROW_CHUNKS
    pltpu.make_async_copy(y_hbm.at[pl.ds(0, slot_rows)], ybuf.at[slot], sem.at[slot]).wait()

    w = w_ref[...]
    acc = w[:, 0:1] * _load_row_tiles(ybuf, COMBINE_TOKENS, (slot,))
    for k in range(1, TOP_K):
        acc = acc + w[:, k:k + 1] * _load_row_tiles(ybuf, COMBINE_TOKENS, (slot,), k * COMBINE_TOKENS)
    x2 = x1_ref[...] + mod_ref[0, 5:6, :] * acc
    o_ref[...] = _rmsnorm_rows(x2, g_ref[...])


def _combine(pos_tiles, y_sorted, top_w, x1, mod6, g_final, seq_len):
    N, D = x1.shape
    tm = COMBINE_TOKENS
    n_steps = N // tm
    per_seq = seq_len // tm
    return pl.pallas_call(
        _combine_kernel,
        grid=(n_steps,),
        in_specs=[pl.BlockSpec((1, 1, TOP_K * tm), lambda t: (t, 0, 0), memory_space=pltpu.SMEM),
                  pl.BlockSpec((1, 1, TOP_K * tm), lambda t: (jnp.minimum(t + 1, n_steps - 1), 0, 0),
                               memory_space=pltpu.SMEM),
                  pl.BlockSpec(memory_space=pl.ANY),
                  pl.BlockSpec((tm, TOP_K), lambda t: (t, 0)),
                  pl.BlockSpec((tm, D), lambda t: (t, 0)),
                  pl.BlockSpec((1, 6, D), lambda t: (t // per_seq, 0, 0)),
                  pl.BlockSpec((1, D), lambda t: (0, 0))],
        out_specs=pl.BlockSpec((tm, D), lambda t: (t, 0)),
        out_shape=jax.ShapeDtypeStruct((N, D), F32),
        scratch_shapes=[pltpu.VMEM((2, TOP_K * tm * ROW_CHUNKS, LANES), F32),
                        pltpu.SemaphoreType.DMA((2,))],
        compiler_params=_cparams(1),
        name="moe_combine",
    )(pos_tiles, pos_tiles, y_sorted, top_w, x1, mod6, g_final)


def kernel(x, c, g_attn, w_ada, b_ada, w_in, b_in, w_out, b_out, rel_table, sinks, g_ffn, w_router,
           b_router, w_gate_up, b_gate_up, w_down, b_down, g_final):
    B, S, D = x.shape
    assert w_ada.shape[0] == 1, "the final norm is fused into the last layer; one layer supported"
    l = 0
    nb = S // MOBA_BLOCK
    i2 = 3 * D_MOBA
    i3 = i2 + D_SWA_Q
    group = N_HEADS_SWA // N_KV_SWA
    q_scale = ATTN_SCALE * LOG2E

    def prep_in(w):
        lead = w.shape[:-1]
        q_swa = (w[..., i2:i3] * q_scale).reshape(lead + (N_KV_SWA, group, HEAD_DIM))
        q_swa = jnp.swapaxes(q_swa, -3, -2).reshape(lead + (D_SWA_Q,))
        return jnp.concatenate([w[..., :D_MOBA] * q_scale, w[..., D_MOBA:i2], q_swa, w[..., i3:]], axis=-1)

    moba_bias = _moba_bias(rel_table, nb)
    swa_bias = _swa_bias(rel_table)

    mod6 = _ada(c, w_ada[l], b_ada[l]).reshape(B, 6, D)
    w_in_l = prep_in(w_in[l]).astype(BF16)
    b_in_l = prep_in(b_in[l]).reshape(1, D_IN)
    qm, km, vmT, kmean, qs, ks, vsT = _inproj(x, mod6, g_attn[l].reshape(1, D), w_in_l, b_in_l)
    sel = _gate(qm, kmean)
    o_m = _moba(qm, km, vmT, sel, moba_bias)
    sink_row = jnp.repeat(sinks[l] * LOG2E, SWA_BLOCK).reshape(1, N_HEADS_SWA * SWA_BLOCK)
    o_s = _swa(sink_row, qs, ks, vsT, swa_bias)
    w_out_m = w_out[l, :D_MOBA].astype(BF16)
    w_out_s = w_out[l, D_MOBA:].astype(BF16)
    N = B * S
    x1, h_rt, top_idx, top_w, rank, counts = _outproj(
        x.reshape(N, D), o_m.reshape(N, D_MOBA), o_s.reshape(N, D_SWA_Q), w_out_m, w_out_s,
        b_out[l].reshape(1, D), mod6, g_ffn[l].reshape(1, D), w_router[l].T,
        b_router[l].reshape(N_EXPERTS, 1), S)
    counts = counts[:, 0]
    n_tiles_max = N * TOP_K // MOE_TILE + N_EXPERTS
    texp, nvalid, src_tile, group_start, zero_tiles, next_expert = _tile_plan(counts, n_tiles_max)
    experts = jnp.arange(N_EXPERTS, dtype=jnp.int32)
    pos = rank + jnp.sum(jnp.where(top_idx[..., None] == experts, group_start, 0), axis=-1)
    assert DISPATCH_TOKENS == COMBINE_TOKENS
    pos_tiles = (pos * ROW_CHUNKS).reshape(TOP_K, N // DISPATCH_TOKENS, DISPATCH_TOKENS).transpose(
        1, 0, 2).reshape(N // DISPATCH_TOKENS, 1, TOP_K * DISPATCH_TOKENS)
    x_sorted = _dispatch(zero_tiles, pos_tiles, h_rt, n_tiles_max * MOE_TILE)
    y_sorted = _experts(texp, nvalid, src_tile, next_expert, x_sorted, w_gate_up[l], b_gate_up[l], w_down[l],
                        b_down[l])
    y = _combine(pos_tiles, y_sorted, top_w.T, x1, mod6, g_final.reshape(1, D), S)
    return y.reshape(B, S, D)
```

```python
import functools
import math

import jax
import jax.numpy as jnp
from jax import lax
from jax.experimental import pallas as pl
from jax.experimental.pallas import tpu as pltpu

F32 = jnp.float32
BF16 = jnp.bfloat16

D_MODEL = 1024
HEAD_DIM = 64
N_HEADS_MOBA = 8
N_HEADS_SWA = 8
N_KV_SWA = 2
D_MOBA = N_HEADS_MOBA * HEAD_DIM
D_SWA_Q = N_HEADS_SWA * HEAD_DIM
D_SWA_KV = N_KV_SWA * HEAD_DIM
D_IN = 3 * D_MOBA + D_SWA_Q + 2 * D_SWA_KV
MOBA_BLOCK = 256
MOBA_TOPK = 3
SWA_WINDOW = 128
SWA_BLOCK = 128
REL_BUCKETS = 32
REL_MAX_DIST = 1024
REL_MAX_EXACT = REL_BUCKETS // 2
N_EXPERTS = 32
TOP_K = 4
D_FF = 1024
SWIGLU_LIMIT = 7.0
SWIGLU_ALPHA = 1.702
EPS = 1e-5
NEG = -1e30
ATTN_SCALE = HEAD_DIM ** -0.5
LOG2E = math.log2(math.e)

MOE_TILE = 512
MOE_MM_ROWS = 256
ROW_CHUNKS = D_MODEL // 128
DISPATCH_TOKENS = 512
COMBINE_TOKENS = 512
MOBA_PAIRS_PER_STEP = 2
SWA_BLOCKS_PER_STEP = 4
LANES = 128
SUBLANES = 8
HEADS_PER_LANE_BLOCK = LANES // HEAD_DIM
ADA_COLS = 1536
OUTPROJ_TOKENS = 1024
CAST_ROWS = 256
VMEM_LIMIT_BYTES = 56 * 1024 * 1024

_REL_THRESHOLDS = tuple(
    math.ceil(REL_MAX_EXACT * (REL_MAX_DIST / REL_MAX_EXACT) ** (k / (REL_BUCKETS - REL_MAX_EXACT)) - 1e-9)
    for k in range(1, REL_BUCKETS - REL_MAX_EXACT))


def _cparams(n_axes):
    return pltpu.CompilerParams(dimension_semantics=("arbitrary",) * n_axes,
                                vmem_limit_bytes=VMEM_LIMIT_BYTES)


def _sigmoid(z):
    return 1.0 / (1.0 + jnp.exp(-z))


def _rmsnorm_rows(xf, g):
    ms = jnp.mean(xf * xf, axis=-1, keepdims=True)
    return xf * lax.rsqrt(ms + EPS) * g


def _dot_nt(a, b, **kw):
    return lax.dot_general(a, b, (((1,), (1,)), ((), ())), preferred_element_type=F32, **kw)


def _ada_kernel(c_ref, w_ref, b_ref, o_ref):
    c = c_ref[...]
    sc = c * _sigmoid(c)
    o_ref[...] = jnp.dot(sc, w_ref[...], preferred_element_type=F32,
                         precision=lax.Precision.HIGHEST) + b_ref[...]


def _ada(c, w_ada, b_ada):
    B = c.shape[0]
    n_out = w_ada.shape[1]
    bn = ADA_COLS
    return pl.pallas_call(
        _ada_kernel,
        grid=(n_out // bn,),
        in_specs=[pl.BlockSpec((B, D_MODEL), lambda n: (0, 0)),
                  pl.BlockSpec((D_MODEL, bn), lambda n: (0, n)),
                  pl.BlockSpec((1, bn), lambda n: (0, n))],
        out_specs=pl.BlockSpec((B, bn), lambda n: (0, n)),
        out_shape=jax.ShapeDtypeStruct((B, n_out), F32),
        compiler_params=_cparams(1),
        name="ada",
    )(c, w_ada, b_ada.reshape(1, n_out))


def _inproj_kernel(x_ref, mod_ref, g_ref, w_ref, b_ref,
                   qm_ref, km_ref, vmT_ref, kmean_ref, qs_ref, ks_ref, vsT_ref):
    xf = x_ref[0]
    shift = mod_ref[0, 0:1, :]
    scale = mod_ref[0, 1:2, :]
    h = _rmsnorm_rows(xf, g_ref[...]) * (1.0 + scale) + shift
    proj = jnp.dot(h.astype(BF16), w_ref[...], preferred_element_type=F32) + b_ref[...]
    i0, i1, i2 = D_MOBA, 2 * D_MOBA, 3 * D_MOBA
    i3 = i2 + D_SWA_Q
    i4 = i3 + D_SWA_KV
    qm_ref[0] = proj[:, :i0].astype(BF16)
    km = proj[:, i0:i1]
    km_ref[0] = km.astype(BF16)
    L = MOBA_BLOCK
    for blk in range(proj.shape[0] // L):
        rows = slice(blk * L, (blk + 1) * L)
        kmean_ref[0, blk] = jnp.broadcast_to(jnp.mean(km[rows], axis=0, keepdims=True), (SUBLANES, D_MOBA))
        vmT = proj[rows, i1:i2].T.astype(BF16)
        vmT_ref[0, :, blk] = vmT.reshape(D_MOBA // LANES, LANES, L)
    qs_ref[0] = proj[:, i2:i3].astype(BF16)
    ks_ref[0] = proj[:, i3:i4].astype(BF16)
    vsT_ref[0] = proj[:, i4:].T.astype(BF16)


def _inproj(x, mod6, g_attn, w_in_bf, b_in):
    B, S, D = x.shape
    L = MOBA_BLOCK
    per_step = 4
    tm = per_step * L
    nb = S // L
    n_pairs = D_MOBA // LANES
    out_shape = (
        jax.ShapeDtypeStruct((B, S, D_MOBA), BF16),
        jax.ShapeDtypeStruct((B, S, D_MOBA), BF16),
        jax.ShapeDtypeStruct((B, n_pairs, nb, LANES, L), BF16),
        jax.ShapeDtypeStruct((B, nb, SUBLANES, D_MOBA), F32),
        jax.ShapeDtypeStruct((B, S, D_SWA_Q), BF16),
        jax.ShapeDtypeStruct((B, S, D_SWA_KV), BF16),
        jax.ShapeDtypeStruct((B, D_SWA_KV, S), BF16),
    )
    return pl.pallas_call(
        _inproj_kernel,
        grid=(B, S // tm),
        in_specs=[pl.BlockSpec((1, tm, D), lambda b, i: (b, i, 0)),
                  pl.BlockSpec((1, 6, D), lambda b, i: (b, 0, 0)),
                  pl.BlockSpec((1, D), lambda b, i: (0, 0)),
                  pl.BlockSpec((D, D_IN), lambda b, i: (0, 0)),
                  pl.BlockSpec((1, D_IN), lambda b, i: (0, 0))],
        out_specs=(pl.BlockSpec((1, tm, D_MOBA), lambda b, i: (b, i, 0)),
                   pl.BlockSpec((1, tm, D_MOBA), lambda b, i: (b, i, 0)),
                   pl.BlockSpec((1, n_pairs, per_step, LANES, L), lambda b, i: (b, 0, i, 0, 0)),
                   pl.BlockSpec((1, per_step, SUBLANES, D_MOBA), lambda b, i: (b, i, 0, 0)),
                   pl.BlockSpec((1, tm, D_SWA_Q), lambda b, i: (b, i, 0)),
                   pl.BlockSpec((1, tm, D_SWA_KV), lambda b, i: (b, i, 0)),
                   pl.BlockSpec((1, D_SWA_KV, tm), lambda b, i: (b, 0, i))),
        out_shape=out_shape,
        compiler_params=_cparams(2),
        name="inproj",
    )(x, mod6, g_attn, w_in_bf, b_in)


def _rel_bucket_static(n):
    n = max(n, 0)
    return n if n < REL_MAX_EXACT else REL_MAX_EXACT + sum(n >= t for t in _REL_THRESHOLDS)


def _rel_bias_values(dist, table_ref, h, d_min, d_max):
    lo, hi = _rel_bucket_static(d_min), _rel_bucket_static(d_max)
    n = jnp.maximum(dist, 0)
    val = jnp.full(n.shape, table_ref[lo, h], F32)
    for bkt in range(lo + 1, hi + 1):
        start = bkt if bkt <= REL_MAX_EXACT else _REL_THRESHOLDS[bkt - REL_MAX_EXACT - 1]
        val = jnp.where(n >= start, table_ref[bkt, h], val)
    return val


def _moba_bias_kernel(table_ref, o_ref):
    h = pl.program_id(0)
    L = MOBA_BLOCK
    key = lax.broadcasted_iota(jnp.int32, (L, L), 0)
    qry = lax.broadcasted_iota(jnp.int32, (L, L), 1)
    for dlt in range(o_ref.shape[1]):
        dist = dlt * L + qry - key
        val = _rel_bias_values(dist, table_ref, h, dlt * L - (L - 1), dlt * L + (L - 1))
        val = val * LOG2E
        o_ref[0, dlt] = jnp.where(dist >= 0, val, NEG) if dlt == 0 else val


def _moba_bias(rel_table, nb):
    L = MOBA_BLOCK
    return pl.pallas_call(
        _moba_bias_kernel,
        grid=(N_HEADS_MOBA,),
        in_specs=[pl.BlockSpec(memory_space=pltpu.SMEM)],
        out_specs=pl.BlockSpec((1, nb, L, L), lambda h: (h, 0, 0, 0)),
        out_shape=jax.ShapeDtypeStruct((N_HEADS_MOBA, nb, L, L), F32),
        compiler_params=_cparams(1),
        name="moba_bias",
    )(rel_table)


def _swa_bias_kernel(table_ref, o_ref):
    h = N_HEADS_MOBA + pl.program_id(0)
    W = SWA_BLOCK
    key = lax.broadcasted_iota(jnp.int32, (2 * W, W), 0)
    qry = lax.broadcasted_iota(jnp.int32, (2 * W, W), 1)
    dist = qry + W - key
    val = _rel_bias_values(dist, table_ref, h, 0, SWA_WINDOW - 1)
    o_ref[...] = jnp.where((dist >= 0) & (dist < SWA_WINDOW), val * LOG2E, NEG)


def _swa_bias(rel_table):
    W = SWA_BLOCK
    return pl.pallas_call(
        _swa_bias_kernel,
        grid=(N_HEADS_SWA,),
        in_specs=[pl.BlockSpec(memory_space=pltpu.SMEM)],
        out_specs=pl.BlockSpec((2 * W, W), lambda h: (0, h)),
        out_shape=jax.ShapeDtypeStruct((2 * W, N_HEADS_SWA * W), F32),
        compiler_params=_cparams(1),
        name="swa_bias",
    )(rel_table)


def _gate_kernel(q_ref, kmean_ref, sel_ref):
    S = q_ref.shape[1]
    nb = kmean_ref.shape[1]
    H = N_HEADS_MOBA
    q = q_ref[0]
    km = kmean_ref[0, :, 0, :]
    km_t = jnp.concatenate([km] * H, axis=0)
    row_h = lax.broadcasted_iota(jnp.int32, (H * nb, D_MOBA), 0) // nb
    col_h = lax.broadcasted_iota(jnp.int32, (H * nb, D_MOBA), 1) // HEAD_DIM
    km_t = jnp.where(row_h == col_h, km_t, 0.0)
    km_hi = km_t.astype(BF16)
    km_lo = (km_t - km_hi.astype(F32)).astype(BF16)
    g_all = _dot_nt(km_hi, q) + _dot_nt(km_lo, q)
    qblk = lax.broadcasted_iota(jnp.int32, (nb, S), 1) // MOBA_BLOCK
    jidx = lax.broadcasted_iota(jnp.int32, (nb, S), 0)
    past = jidx < qblk
    for h in range(H):
        g = g_all[h * nb:(h + 1) * nb, :]
        cnt = jnp.zeros((nb, S), jnp.int32)
        for jp in range(nb):
            gj = g[jp:jp + 1, :]
            ahead = (gj > g) | ((gj == g) & (jp < jidx))
            cnt = cnt + ((jp < qblk) & ahead).astype(jnp.int32)
        sel = (past & (cnt < MOBA_TOPK)).astype(F32)
        for j in range(nb):
            for half in range(2):
                sel_ref[0, h, j, half:half + 1, :] = sel[j:j + 1, half * (S // 2):(half + 1) * (S // 2)]


def _gate(qm, kmean):
    B, S, _ = qm.shape
    nb = kmean.shape[1]
    return pl.pallas_call(
        _gate_kernel,
        grid=(B,),
        in_specs=[pl.BlockSpec((1, S, D_MOBA), lambda b: (b, 0, 0)),
                  pl.BlockSpec((1, nb, SUBLANES, D_MOBA), lambda b: (b, 0, 0, 0))],
        out_specs=pl.BlockSpec((1, N_HEADS_MOBA, nb, 2, S // 2), lambda b: (b, 0, 0, 0, 0)),
        out_shape=jax.ShapeDtypeStruct((B, N_HEADS_MOBA, nb, 2, S // 2), F32),
        compiler_params=_cparams(1),
        name="moba_gate",
    )(qm, kmean)


def _moba_kernel(q_ref, k_ref, vT_ref, sel_ref, bias_ref, o_ref, s_buf):
    c = pl.program_id(2)
    L = MOBA_BLOCK
    nb = k_ref.shape[1] // L

    def attend(pair, half, n_past):
        lanes = slice(pair * LANES, (pair + 1) * LANES)
        q = q_ref[0, half, :, lanes]
        lane = lax.broadcasted_iota(jnp.int32, q.shape, 1)
        zero = jnp.zeros_like(q)
        q_heads = [jnp.where(lane < HEAD_DIM, q, zero), jnp.where(lane >= HEAD_DIM, q, zero)]
        outs = []
        for a in range(HEADS_PER_LANE_BLOCK):
            hd = pair * HEADS_PER_LANE_BLOCK + a
            chosen = [None if j == n_past else sel_ref[0, hd, j, half:half + 1, :] > 0.5
                      for j in range(n_past + 1)]
            m = None
            for j in range(n_past + 1):
                s = _dot_nt(k_ref[0, j * L:(j + 1) * L, lanes], q_heads[a]) + bias_ref[hd, n_past - j]
                if n_past > 0:
                    s_buf[half, hd, j * L:(j + 1) * L, :] = s
                mj = jnp.max(s, axis=0, keepdims=True)
                if chosen[j] is not None:
                    mj = jnp.where(chosen[j], mj, NEG)
                m = mj if m is None else jnp.maximum(m, mj)
            l = None
            acc = None
            for j in range(n_past + 1):
                shift = m if chosen[j] is None else jnp.where(chosen[j], m, -NEG)
                p = jnp.exp2((s_buf[half, hd, j * L:(j + 1) * L, :] if n_past > 0 else s) - shift)
                lj = jnp.sum(p, axis=0, keepdims=True)
                vT = vT_ref[0, pair, j, a * HEAD_DIM:(a + 1) * HEAD_DIM, :]
                oj = jnp.dot(vT, p.astype(BF16), preferred_element_type=F32)
                l = lj if l is None else l + lj
                acc = oj if acc is None else acc + oj
            outs.append(acc * (1.0 / l))
        o_ref[0, half, :, lanes] = jnp.concatenate(outs, axis=0).T.astype(BF16)

    def step(c_static):
        for pair in range(MOBA_PAIRS_PER_STEP):
            for half in range(2):
                attend(pair, half, c_static + half * (nb // 2))

    for c_static in range(nb // 2):
        pl.when(c == c_static)(functools.partial(step, c_static))


def _moba(qm, km, vmT, sel, bias):
    B, S, _ = qm.shape
    L = MOBA_BLOCK
    nb = S // L
    pairs = MOBA_PAIRS_PER_STEP
    heads = pairs * HEADS_PER_LANE_BLOCK
    width = pairs * LANES
    half_s = S // 2
    out = pl.pallas_call(
        _moba_kernel,
        grid=(D_MOBA // width, B, nb // 2),
        in_specs=[pl.BlockSpec((1, 2, L, width), lambda p, b, c: (b, 0, c, p)),
                  pl.BlockSpec((1, S, width), lambda p, b, c: (b, 0, p)),
                  pl.BlockSpec((1, pairs, nb, LANES, L), lambda p, b, c: (b, p, 0, 0, 0)),
                  pl.BlockSpec((1, heads, nb, 2, L), lambda p, b, c: (b, p, 0, 0, c)),
                  pl.BlockSpec((heads, nb, L, L), lambda p, b, c: (p, 0, 0, 0))],
        out_specs=pl.BlockSpec((1, 2, L, width), lambda p, b, c: (b, 0, c, p)),
        out_shape=jax.ShapeDtypeStruct((B, 2, half_s, D_MOBA), BF16),
        scratch_shapes=[pltpu.VMEM((2, heads, S, L), F32)],
        compiler_params=_cparams(3),
        name="moba_attn",
    )(qm.reshape(B, 2, half_s, D_MOBA), km, vmT, sel, bias)
    return out.reshape(B, S, D_MOBA)


def _swa_kernel(sink_ref, q_ref, *refs):
    n = SWA_BLOCKS_PER_STEP
    k_refs, v_refs = refs[:2 * n], refs[2 * n:4 * n]
    bias_ref, o_ref = refs[4 * n], refs[4 * n + 1]
    c = pl.program_id(1)
    for g in range(n):
        _swa_block(sink_ref, q_ref[0, g], k_refs[2 * g][0], k_refs[2 * g + 1][0], v_refs[2 * g][0],
                   v_refs[2 * g + 1][0], bias_ref, o_ref.at[0, g], c > 0 if g == 0 else None)


def _swa_block(sink_ref, q, k_prev, k_cur, vT_prev, vT_cur, bias_ref, o_ref, has_prev):
    W = SWA_BLOCK
    group = N_HEADS_SWA // N_KV_SWA
    kband = jnp.concatenate([k_prev, k_cur], axis=0)
    vbandT = jnp.concatenate([vT_prev, vT_cur], axis=1)
    q_rows = []
    for h in range(N_HEADS_SWA):
        qp = q[:, (h % group) * LANES:(h % group + 1) * LANES]
        lane = lax.broadcasted_iota(jnp.int32, qp.shape, 1)
        in_half = (lane >= HEAD_DIM) if h // group else (lane < HEAD_DIM)
        q_rows.append(jnp.where(in_half, qp, jnp.zeros_like(qp)))
    q_all = jnp.concatenate(q_rows, axis=0)
    sT = _dot_nt(kband, q_all) + bias_ref[...]
    if has_prev is not None:
        key = lax.broadcasted_iota(jnp.int32, sT.shape, 0)
        sT = jnp.where((key >= W) | has_prev, sT, NEG)
    sink = sink_ref[...]
    m = jnp.maximum(jnp.max(sT, axis=0, keepdims=True), sink)
    pr = jnp.exp2(sT - m)
    inv_l = 1.0 / (jnp.sum(pr, axis=0, keepdims=True) + jnp.exp2(sink - m))
    pr = pr.astype(BF16)
    cols = group * W
    outs = [jnp.dot(vbandT[g * HEAD_DIM:(g + 1) * HEAD_DIM, :], pr[:, g * cols:(g + 1) * cols],
                    preferred_element_type=F32) * inv_l[:, g * cols:(g + 1) * cols]
            for g in range(N_KV_SWA)]
    heads = [outs[h // group][:, (h % group) * W:(h % group + 1) * W] for h in range(N_HEADS_SWA)]
    o_ref[...] = jnp.concatenate(heads, axis=0).T.astype(BF16)


def _swa(sink_row, qs, ks, vsT, bias):
    B, S, _ = qs.shape
    W = SWA_BLOCK
    nq = S // W
    n = SWA_BLOCKS_PER_STEP
    far = nq // n
    k_spec = lambda f: pl.BlockSpec((1, W, D_SWA_KV), lambda b, c: (b, f(c), 0))
    v_spec = lambda f: pl.BlockSpec((1, D_SWA_KV, W), lambda b, c: (b, 0, f(c)))
    blocks = []
    for g in range(n):
        blocks.append((lambda c, g=g: jnp.maximum(c + g * far - 1, 0)))
        blocks.append((lambda c, g=g: c + g * far))
    out = pl.pallas_call(
        _swa_kernel,
        grid=(B, far),
        in_specs=[pl.BlockSpec((1, N_HEADS_SWA * W), lambda b, c: (0, 0)),
                  pl.BlockSpec((1, n, W, D_SWA_Q), lambda b, c: (b, 0, c, 0))]
                 + [k_spec(f) for f in blocks] + [v_spec(f) for f in blocks]
                 + [pl.BlockSpec((2 * W, N_HEADS_SWA * W), lambda b, c: (0, 0))],
        out_specs=pl.BlockSpec((1, n, W, D_SWA_Q), lambda b, c: (b, 0, c, 0)),
        out_shape=jax.ShapeDtypeStruct((B, n, S // n, D_SWA_Q), BF16),
        compiler_params=_cparams(2),
        name="swa_attn",
    )(sink_row, qs.reshape(B, n, S // n, D_SWA_Q), *([ks] * (2 * n)), *([vsT] * (2 * n)), bias)
    return out.reshape(B, S, D_SWA_Q)


def _outproj_kernel(x_ref, om_ref, os_ref, wm_ref, ws_ref, b_ref, mod_ref, g_ref, wrT_ref, br_ref,
                    x1_ref, hrt_ref, idx_ref, w_ref, rank_ref, counts_ref, carry_ref):
    attn = (jnp.dot(om_ref[...], wm_ref[...], preferred_element_type=F32)
            + jnp.dot(os_ref[...], ws_ref[...], preferred_element_type=F32) + b_ref[...])
    x1 = x_ref[...] + mod_ref[0, 2:3, :] * attn
    x1_ref[...] = x1
    h = _rmsnorm_rows(x1, g_ref[...]) * (1.0 + mod_ref[0, 4:5, :]) + mod_ref[0, 3:4, :]
    _store_row_tiles(hrt_ref, h)
    _route(h, wrT_ref, br_ref, idx_ref, w_ref, rank_ref, counts_ref, carry_ref)


def _store_row_tiles(ref, val, first_row=0):
    rows = val.shape[0]
    base = first_row * ROW_CHUNKS
    for c in range(ROW_CHUNKS):
        ref[pl.ds(base + c, rows, stride=ROW_CHUNKS), :] = val[:, c * LANES:(c + 1) * LANES]


def _load_row_tiles(ref, rows, lead=(), first_row=0):
    idx = tuple(lead)
    base = first_row * ROW_CHUNKS
    return jnp.concatenate([ref[idx + (pl.ds(base + c, rows, stride=ROW_CHUNKS), slice(None))]
                            for c in range(ROW_CHUNKS)], axis=1)


def _outproj(x2d, om, os_, w_out_m, w_out_s, b_out, mod6, g_ffn, w_routerT, b_router_col, seq_len):
    N, D = x2d.shape
    tm = OUTPROJ_TOKENS
    per_seq = seq_len // tm
    row = lambda t: (t, 0)
    col = lambda t: (0, t)
    const = lambda t: (0, 0)
    return pl.pallas_call(
        _outproj_kernel,
        grid=(N // tm,),
        in_specs=[pl.BlockSpec((tm, D), row),
                  pl.BlockSpec((tm, D_MOBA), row),
                  pl.BlockSpec((tm, D_SWA_Q), row),
                  pl.BlockSpec((D_MOBA, D), const),
                  pl.BlockSpec((D_SWA_Q, D), const),
                  pl.BlockSpec((1, D), const),
                  pl.BlockSpec((1, 6, D), lambda t: (t // per_seq, 0, 0)),
                  pl.BlockSpec((1, D), const),
                  pl.BlockSpec((N_EXPERTS, D), const),
                  pl.BlockSpec((N_EXPERTS, 1), const)],
        out_specs=(pl.BlockSpec((tm, D), row),
                   pl.BlockSpec((tm * ROW_CHUNKS, LANES), row),
                   pl.BlockSpec((TOP_K, tm), col),
                   pl.BlockSpec((TOP_K, tm), col),
                   pl.BlockSpec((TOP_K, tm), col),
                   pl.BlockSpec((N_EXPERTS, LANES), const)),
        out_shape=(jax.ShapeDtypeStruct((N, D), F32),
                   jax.ShapeDtypeStruct((N * ROW_CHUNKS, LANES), F32),
                   jax.ShapeDtypeStruct((TOP_K, N), jnp.int32),
                   jax.ShapeDtypeStruct((TOP_K, N), F32),
                   jax.ShapeDtypeStruct((TOP_K, N), jnp.int32),
                   jax.ShapeDtypeStruct((N_EXPERTS, LANES), jnp.int32)),
        scratch_shapes=[pltpu.VMEM((N_EXPERTS, LANES), F32)],
        compiler_params=_cparams(1),
        name="outproj_route",
    )(x2d, om, os_, w_out_m, w_out_s, b_out, mod6, g_ffn, w_routerT, b_router_col)


def _route(h, wT_ref, b_ref, idx_ref, w_ref, rank_ref, counts_ref, carry_ref):
    @pl.when(pl.program_id(0) == 0)
    def _():
        carry_ref[...] = jnp.zeros_like(carry_ref)

    w = wT_ref[...]
    w_hi = w.astype(BF16)
    w_lo = (w - w_hi.astype(F32)).astype(BF16)
    h_hi = h.astype(BF16)
    h_lo = (h - h_hi.astype(F32)).astype(BF16)
    by_h_hi = _dot_nt(jnp.concatenate([w_hi, w_lo], axis=0), h_hi)
    logits = (by_h_hi[:N_EXPERTS] + by_h_hi[N_EXPERTS:] + _dot_nt(w_hi, h_lo)) + b_ref[...]
    tm = logits.shape[1]
    eidx = lax.broadcasted_iota(jnp.int32, logits.shape, 0)
    vals, idxs = [], []
    cur = logits
    for _ in range(TOP_K):
        m = jnp.max(cur, axis=0, keepdims=True)
        am = jnp.min(jnp.where(cur == m, eidx, N_EXPERTS), axis=0, keepdims=True)
        vals.append(m)
        idxs.append(am)
        cur = jnp.where(eidx == am, -jnp.inf, cur)
    exps = [jnp.exp(v - vals[0]) for v in vals]
    inv = 1.0 / functools.reduce(lambda a, b: a + b, exps)
    idx_ref[...] = jnp.concatenate(idxs, axis=0)
    w_ref[...] = jnp.concatenate([e * inv for e in exps], axis=0)

    member = functools.reduce(lambda a, b: a | b, [eidx == am for am in idxs])
    earlier = (lax.broadcasted_iota(jnp.int32, (tm, tm), 0)
               < lax.broadcasted_iota(jnp.int32, (tm, tm), 1)).astype(BF16)
    before = jnp.dot(member.astype(BF16), earlier, preferred_element_type=F32) + carry_ref[:, 0:1]
    rank_ref[...] = jnp.concatenate(
        [jnp.sum(jnp.where(eidx == am, before, 0.0), axis=0, keepdims=True) for am in idxs],
        axis=0).astype(jnp.int32)
    total = carry_ref[...] + jnp.sum(member.astype(F32), axis=1, keepdims=True)
    carry_ref[...] = total
    counts_ref[...] = total.astype(jnp.int32)


def _expert_mlp(h_bf, wgu, bgu, wdn, bdn):
    gu = jnp.dot(h_bf, wgu, preferred_element_type=F32) + bgu
    g = jnp.minimum(gu[:, :D_FF], SWIGLU_LIMIT)
    u = jnp.clip(gu[:, D_FF:], -SWIGLU_LIMIT, SWIGLU_LIMIT)
    act = (u + 1.0) * (g * _sigmoid(SWIGLU_ALPHA * g))
    return jnp.dot(act.astype(BF16), wdn, preferred_element_type=F32) + bdn


def _tile_plan(counts, n_tiles_max):
    tiles_per = (counts + MOE_TILE - 1) // MOE_TILE
    tile_end = jnp.cumsum(tiles_per)
    tile_start = tile_end - tiles_per
    t_ids = jnp.arange(n_tiles_max, dtype=jnp.int32)
    texp = jnp.minimum(jnp.sum((t_ids[:, None] >= tile_end[None, :]).astype(jnp.int32), axis=1),
                       N_EXPERTS - 1)
    onehot = (texp[:, None] == jnp.arange(N_EXPERTS, dtype=jnp.int32)[None, :]).astype(jnp.int32)
    local = t_ids - onehot @ tile_start
    nvalid = jnp.clip(onehot @ counts - local * MOE_TILE, 0, MOE_TILE)
    n_active = tile_end[-1]
    src_tile = jnp.minimum(t_ids, n_active - 1)
    last_tile = jnp.where(counts > 0, tile_end - 1, -1)
    tail = n_active + jnp.arange(N_EXPERTS, dtype=jnp.int32)
    zero_tiles = jnp.concatenate([last_tile, jnp.where(tail < n_tiles_max, tail, -1)])
    ids = jnp.arange(N_EXPERTS, dtype=jnp.int32)
    later = jnp.where((ids[None, :] > ids[:, None]) & (counts[None, :] > 0), ids[None, :], N_EXPERTS)
    next_expert = jnp.min(later, axis=1)
    next_expert = jnp.where(next_expert < N_EXPERTS, next_expert, -1)
    return (texp.astype(jnp.int32), nvalid.astype(jnp.int32), src_tile.astype(jnp.int32),
            (tile_start * MOE_TILE).astype(jnp.int32), zero_tiles.astype(jnp.int32),
            next_expert.astype(jnp.int32))


def _row_copy(src, src_first, dst, dst_first, sem):
    return pltpu.make_async_copy(src.at[pl.ds(pl.multiple_of(src_first, ROW_CHUNKS), ROW_CHUNKS)],
                                 dst.at[pl.ds(pl.multiple_of(dst_first, ROW_CHUNKS), ROW_CHUNKS)],
                                 sem)


DISPATCH_SLOTS = 3


def _dispatch_kernel(zero_tiles_ref, pos_ref, h_hbm, x_hbm, zbuf, hbuf, zsem, load_sem, row_sem):
    t = pl.program_id(0)
    n_steps = pl.num_programs(0)
    tile_rows = MOE_TILE * ROW_CHUNKS
    block_rows = DISPATCH_TOKENS * ROW_CHUNKS

    def block_load(step, slot):
        start = pl.multiple_of(step * block_rows, block_rows)
        return pltpu.make_async_copy(h_hbm.at[pl.ds(start, block_rows)], hbuf.at[slot], load_sem.at[slot])

    def rows_done(slot):
        for k in range(TOP_K):
            pltpu.make_async_copy(hbuf.at[slot], x_hbm.at[pl.ds(0, block_rows)], row_sem.at[slot]).wait()

    def zero_copy(i):
        start = pl.multiple_of(zero_tiles_ref[i] * tile_rows, tile_rows)
        return pltpu.make_async_copy(zbuf, x_hbm.at[pl.ds(start, tile_rows)], zsem)

    @pl.when(t == 0)
    def _():
        zbuf[...] = jnp.zeros_like(zbuf)
        for i in range(2 * N_EXPERTS):
            @pl.when(zero_tiles_ref[i] >= 0)
            def _():
                zero_copy(i).start()
        block_load(0, 0).start()
        if DISPATCH_SLOTS > 2:
            @pl.when(n_steps > 1)
            def _():
                block_load(1, 1).start()
        for i in range(2 * N_EXPERTS):
            @pl.when(zero_tiles_ref[i] >= 0)
            def _():
                zero_copy(i).wait()

    slot = t % DISPATCH_SLOTS
    block_load(t, slot).wait()

    def start_rows(r, carry):
        for k in range(TOP_K):
            _row_copy(hbuf.at[slot], r * ROW_CHUNKS, x_hbm, pos_ref[0, 0, k * DISPATCH_TOKENS + r],
                      row_sem.at[slot]).start(priority=k % 2)
        return carry
    lax.fori_loop(0, DISPATCH_TOKENS, start_rows, 0, unroll=8)

    refill = (t + 2) % DISPATCH_SLOTS

    @pl.when(t > 0)
    def _():
        rows_done(refill)

    @pl.when(t + 2 < n_steps)
    def _():
        block_load(t + 2, refill).start()

    @pl.when(t == n_steps - 1)
    def _():
        rows_done(slot)


def _dispatch(zero_tiles, pos_tiles, h_rt, n_rows):
    n_steps = pos_tiles.shape[0]
    grid_spec = pltpu.PrefetchScalarGridSpec(
        num_scalar_prefetch=1,
        grid=(n_steps,),
        in_specs=[pl.BlockSpec((1, 1, TOP_K * DISPATCH_TOKENS), lambda t, zt: (t, 0, 0),
                               memory_space=pltpu.SMEM),
                  pl.BlockSpec(memory_space=pl.ANY)],
        out_specs=pl.BlockSpec(memory_space=pl.ANY),
        scratch_shapes=[pltpu.VMEM((MOE_TILE * ROW_CHUNKS, LANES), F32),
                        pltpu.VMEM((DISPATCH_SLOTS, DISPATCH_TOKENS * ROW_CHUNKS, LANES), F32),
                        pltpu.SemaphoreType.DMA(()),
                        pltpu.SemaphoreType.DMA((DISPATCH_SLOTS,)),
                        pltpu.SemaphoreType.DMA((DISPATCH_SLOTS,))],
    )
    return pl.pallas_call(
        _dispatch_kernel,
        grid_spec=grid_spec,
        out_shape=jax.ShapeDtypeStruct((n_rows * ROW_CHUNKS, LANES), F32),
        compiler_params=_cparams(1),
        name="moe_dispatch",
    )(zero_tiles, pos_tiles, h_rt)


def _experts_kernel(texp_ref, nvalid_ref, src_ref, next_ref, x_ref, wgu_hbm, bgu_ref, wdn_hbm, bdn_ref,
                    y_ref, wgu_f32, wdn_f32, wgu_bf, wdn_bf, wsem):
    t = pl.program_id(0)
    nv = nvalid_ref[t]
    e = texp_ref[t]
    prev_e = texp_ref[jnp.maximum(t - 1, 0)]

    def weight_loads(expert):
        return (pltpu.make_async_copy(wgu_hbm.at[expert], wgu_f32, wsem.at[0]),
                pltpu.make_async_copy(wdn_hbm.at[expert], wdn_f32, wsem.at[1]))

    @pl.when((nv > 0) & ((t == 0) | (e != prev_e)))
    def _():
        @pl.when(t == 0)
        def _():
            for cp in weight_loads(e):
                cp.start()
        for cp in weight_loads(e):
            cp.wait()
        rows = CAST_ROWS
        def cast_gu(i, carry):
            r0 = pl.multiple_of(i * rows, rows)
            wgu_bf[pl.ds(r0, rows), :] = wgu_f32[pl.ds(r0, rows), :].astype(BF16)
            return carry
        lax.fori_loop(0, D_MODEL // rows, cast_gu, 0)
        def cast_dn(i, carry):
            r0 = pl.multiple_of(i * rows, rows)
            wdn_bf[pl.ds(r0, rows), :] = wdn_f32[pl.ds(r0, rows), :].astype(BF16)
            return carry
        lax.fori_loop(0, D_FF // rows, cast_dn, 0)
        nxt = next_ref[e]

        @pl.when(nxt >= 0)
        def _():
            for cp in weight_loads(nxt):
                cp.start()

    def mlp_rows(first_row):
        x = _load_row_tiles(x_ref, MOE_MM_ROWS, first_row=first_row).astype(BF16)
        _store_row_tiles(y_ref, _expert_mlp(x, wgu_bf[...], bgu_ref[0], wdn_bf[...], bdn_ref[0]), first_row)

    n_groups = MOE_TILE // MOE_MM_ROWS
    for live in range(n_groups + 1):
        lo, hi = (live - 1) * MOE_MM_ROWS, live * MOE_MM_ROWS

        @pl.when((nv > lo) & (nv <= hi) if live else (nv == 0))
        def _():
            for g in range(live):
                mlp_rows(g * MOE_MM_ROWS)
            if live < n_groups:
                y_ref[pl.ds(live * MOE_MM_ROWS * ROW_CHUNKS, (n_groups - live) * MOE_MM_ROWS * ROW_CHUNKS), :] = (
                    jnp.zeros(((n_groups - live) * MOE_MM_ROWS * ROW_CHUNKS, LANES), F32))


def _experts(texp, nvalid, src_tile, next_expert, x_sorted, w_gate_up, b_gate_up, w_down, b_down):
    n_tiles = texp.shape[0]
    D = D_MODEL
    tile_rows = MOE_TILE * ROW_CHUNKS
    grid_spec = pltpu.PrefetchScalarGridSpec(
        num_scalar_prefetch=4,
        grid=(n_tiles,),
        in_specs=[pl.BlockSpec((tile_rows, LANES), lambda t, te, nv, st, nx: (st[t], 0)),
                  pl.BlockSpec(memory_space=pl.ANY),
                  pl.BlockSpec((1, 1, 2 * D_FF), lambda t, te, nv, st, nx: (te[t], 0, 0)),
                  pl.BlockSpec(memory_space=pl.ANY),
                  pl.BlockSpec((1, 1, D), lambda t, te, nv, st, nx: (te[t], 0, 0))],
        out_specs=pl.BlockSpec((tile_rows, LANES), lambda t, te, nv, st, nx: (t, 0)),
        scratch_shapes=[pltpu.VMEM((D, 2 * D_FF), F32),
                        pltpu.VMEM((D_FF, D), F32),
                        pltpu.VMEM((D, 2 * D_FF), BF16),
                        pltpu.VMEM((D_FF, D), BF16),
                        pltpu.SemaphoreType.DMA((2,))],
    )
    return pl.pallas_call(
        _experts_kernel,
        grid_spec=grid_spec,
        out_shape=jax.ShapeDtypeStruct(x_sorted.shape, F32),
        compiler_params=_cparams(1),
        name="moe_experts",
    )(texp, nvalid, src_tile, next_expert, x_sorted, w_gate_up, b_gate_up.reshape(N_EXPERTS, 1, 2 * D_FF),
      w_down, b_down.reshape(N_EXPERTS, 1, D))


def _combine_kernel(pos_ref, pos_next_ref, y_hbm, w_ref, x1_ref, mod_ref, g_ref, o_ref, ybuf, sem):
    t = pl.program_id(0)
    n_steps = pl.num_programs(0)
    slot = t % 2

    def start_gather(idx_ref, s):
        def body(r, carry):
            for k in range(TOP_K):
                i = k * COMBINE_TOKENS + r
                _row_copy(y_hbm, idx_ref[0, 0, i], ybuf.at[s], i * ROW_CHUNKS, sem.at[s]).start(priority=k % 2)
            return carry
        lax.fori_loop(0, COMBINE_TOKENS, body, 0, unroll=8)

    @pl.when(t == 0)
    def _():
        start_gather(pos_ref, 0)

    @pl.when(t + 1 < n_steps)
    def _():
        start_gather(pos_next_ref, 1 - slot)

    slot_rows = TOP_K * COMBINE_TOKENS * ROW_CHUNKS
    pltpu.make_async_copy(y_hbm.at[pl.ds(0, slot_rows)], ybuf.at[slot], sem.at[slot]).wait()

    w = w_ref[...]
    acc = w[:, 0:1] * _load_row_tiles(ybuf, COMBINE_TOKENS, (slot,))
    for k in range(1, TOP_K):
        acc = acc + w[:, k:k + 1] * _load_row_tiles(ybuf, COMBINE_TOKENS, (slot,), k * COMBINE_TOKENS)
    x2 = x1_ref[...] + mod_ref[0, 5:6, :] * acc
    o_ref[...] = _rmsnorm_rows(x2, g_ref[...])


def _combine(pos_tiles, y_sorted, top_w, x1, mod6, g_final, seq_len):
    N, D = x1.shape
    tm = COMBINE_TOKENS
    n_steps = N // tm
    per_seq = seq_len // tm
    return pl.pallas_call(
        _combine_kernel,
        grid=(n_steps,),
        in_specs=[pl.BlockSpec((1, 1, TOP_K * tm), lambda t: (t, 0, 0), memory_space=pltpu.SMEM),
                  pl.BlockSpec((1, 1, TOP_K * tm), lambda t: (jnp.minimum(t + 1, n_steps - 1), 0, 0),
                               memory_space=pltpu.SMEM),
                  pl.BlockSpec(memory_space=pl.ANY),
                  pl.BlockSpec((tm, TOP_K), lambda t: (t, 0)),
                  pl.BlockSpec((tm, D), lambda t: (t, 0)),
                  pl.BlockSpec((1, 6, D), lambda t: (t // per_seq, 0, 0)),
                  pl.BlockSpec((1, D), lambda t: (0, 0))],
        out_specs=pl.BlockSpec((tm, D), lambda t: (t, 0)),
        out_shape=jax.ShapeDtypeStruct((N, D), F32),
        scratch_shapes=[pltpu.VMEM((2, TOP_K * tm * ROW_CHUNKS, LANES), F32),
                        pltpu.SemaphoreType.DMA((2,))],
        compiler_params=_cparams(1),
        name="moe_combine",
    )(pos_tiles, pos_tiles, y_sorted, top_w, x1, mod6, g_final)


def kernel(x, c, g_attn, w_ada, b_ada, w_in, b_in, w_out, b_out, rel_table, sinks, g_ffn, w_router,
           b_router, w_gate_up, b_gate_up, w_down, b_down, g_final):
    B, S, D = x.shape
    assert w_ada.shape[0] == 1, "the final norm is fused into the last layer; one layer supported"
    l = 0
    nb = S // MOBA_BLOCK
    i2 = 3 * D_MOBA
    i3 = i2 + D_SWA_Q
    group = N_HEADS_SWA // N_KV_SWA
    q_scale = ATTN_SCALE * LOG2E

    def prep_in(w):
        lead = w.shape[:-1]
        q_swa = (w[..., i2:i3] * q_scale).reshape(lead + (N_KV_SWA, group, HEAD_DIM))
        q_swa = jnp.swapaxes(q_swa, -3, -2).reshape(lead + (D_SWA_Q,))
        return jnp.concatenate([w[..., :D_MOBA] * q_scale, w[..., D_MOBA:i2], q_swa, w[..., i3:]], axis=-1)

    moba_bias = _moba_bias(rel_table, nb)
    swa_bias = _swa_bias(rel_table)

    mod6 = _ada(c, w_ada[l], b_ada[l]).reshape(B, 6, D)
    w_in_l = prep_in(w_in[l]).astype(BF16)
    b_in_l = prep_in(b_in[l]).reshape(1, D_IN)
    qm, km, vmT, kmean, qs, ks, vsT = _inproj(x, mod6, g_attn[l].reshape(1, D), w_in_l, b_in_l)
    sel = _gate(qm, kmean)
    o_m = _moba(qm, km, vmT, sel, moba_bias)
    sink_row = jnp.repeat(sinks[l] * LOG2E, SWA_BLOCK).reshape(1, N_HEADS_SWA * SWA_BLOCK)
    o_s = _swa(sink_row, qs, ks, vsT, swa_bias)
    w_out_m = w_out[l, :D_MOBA].astype(BF16)
    w_out_s = w_out[l, D_MOBA:].astype(BF16)
    N = B * S
    x1, h_rt, top_idx, top_w, rank, counts = _outproj(
        x.reshape(N, D), o_m.reshape(N, D_MOBA), o_s.reshape(N, D_SWA_Q), w_out_m, w_out_s,
        b_out[l].reshape(1, D), mod6, g_ffn[l].reshape(1, D), w_router[l].T,
        b_router[l].reshape(N_EXPERTS, 1), S)
    counts = counts[:, 0]
    n_tiles_max = N * TOP_K // MOE_TILE + N_EXPERTS
    texp, nvalid, src_tile, group_start, zero_tiles, next_expert = _tile_plan(counts, n_tiles_max)
    experts = jnp.arange(N_EXPERTS, dtype=jnp.int32)
    pos = rank + jnp.sum(jnp.where(top_idx[..., None] == experts, group_start, 0), axis=-1)
    assert DISPATCH_TOKENS == COMBINE_TOKENS
    pos_tiles = (pos * ROW_CHUNKS).reshape(TOP_K, N // DISPATCH_TOKENS, DISPATCH_TOKENS).transpose(
        1, 0, 2).reshape(N // DISPATCH_TOKENS, 1, TOP_K * DISPATCH_TOKENS)
    x_sorted = _dispatch(zero_tiles, pos_tiles, h_rt, n_tiles_max * MOE_TILE)
    y_sorted = _experts(texp, nvalid, src_tile, next_expert, x_sorted, w_gate_up[l], b_gate_up[l], w_down[l],
                        b_down[l])
    y = _combine(pos_tiles, y_sorted, top_w.T, x1, mod6, g_final.reshape(1, D), S)
    return y.reshape(B, S, D)
```

```python
import functools
import math

import jax
import jax.numpy as jnp
from jax import lax
from jax.experimental import pallas as pl
from jax.experimental.pallas import tpu as pltpu

F32 = jnp.float32
BF16 = jnp.bfloat16

D_MODEL = 1024
HEAD_DIM = 64
N_HEADS_MOBA = 8
N_HEADS_SWA = 8
N_KV_SWA = 2
D_MOBA = N_HEADS_MOBA * HEAD_DIM
D_SWA_Q = N_HEADS_SWA * HEAD_DIM
D_SWA_KV = N_KV_SWA * HEAD_DIM
D_IN = 3 * D_MOBA + D_SWA_Q + 2 * D_SWA_KV
MOBA_BLOCK = 256
MOBA_TOPK = 3
SWA_WINDOW = 128
SWA_BLOCK = 128
REL_BUCKETS = 32
REL_MAX_DIST = 1024
REL_MAX_EXACT = REL_BUCKETS // 2
N_EXPERTS = 32
TOP_K = 4
D_FF = 1024
SWIGLU_LIMIT = 7.0
SWIGLU_ALPHA = 1.702
EPS = 1e-5
NEG = -1e30
ATTN_SCALE = HEAD_DIM ** -0.5
LOG2E = math.log2(math.e)

MOE_TILE = 512
MOE_MM_ROWS = 256
ROW_CHUNKS = D_MODEL // 128
DISPATCH_TOKENS = 1024
COMBINE_TOKENS = 512
MOBA_PAIRS_PER_STEP = 2
SWA_BLOCKS_PER_STEP = 4
LANES = 128
SUBLANES = 8
HEADS_PER_LANE_BLOCK = LANES // HEAD_DIM
ADA_COLS = 1536
OUTPROJ_TOKENS = 1024
CAST_ROWS = 256
VMEM_LIMIT_BYTES = 56 * 1024 * 1024

_REL_THRESHOLDS = tuple(
    math.ceil(REL_MAX_EXACT * (REL_MAX_DIST / REL_MAX_EXACT) ** (k / (REL_BUCKETS - REL_MAX_EXACT)) - 1e-9)
    for k in range(1, REL_BUCKETS - REL_MAX_EXACT))


def _cparams(n_axes):
    return pltpu.CompilerParams(dimension_semantics=("arbitrary",) * n_axes,
                                vmem_limit_bytes=VMEM_LIMIT_BYTES)


def _sigmoid(z):
    return 1.0 / (1.0 + jnp.exp(-z))


def _rmsnorm_rows(xf, g):
    ms = jnp.mean(xf * xf, axis=-1, keepdims=True)
    return xf * lax.rsqrt(ms + EPS) * g


def _dot_nt(a, b, **kw):
    return lax.dot_general(a, b, (((1,), (1,)), ((), ())), preferred_element_type=F32, **kw)


def _ada_kernel(c_ref, w_ref, b_ref, o_ref):
    c = c_ref[...]
    sc = c * _sigmoid(c)
    sc_hi = sc.astype(BF16)
    sc_lo = (sc - sc_hi.astype(F32)).astype(BF16)
    w = w_ref[...]
    w_hi = w.astype(BF16)
    w_lo = (w - w_hi.astype(F32)).astype(BF16)
    n_rows = sc.shape[0]
    by_w_hi = jnp.dot(jnp.concatenate([sc_hi, sc_lo], axis=0), w_hi, preferred_element_type=F32)
    o_ref[...] = (by_w_hi[:n_rows] + by_w_hi[n_rows:]
                  + jnp.dot(sc_hi, w_lo, preferred_element_type=F32)) + b_ref[...]


def _ada(c, w_ada, b_ada):
    B = c.shape[0]
    n_out = w_ada.shape[1]
    bn = ADA_COLS
    return pl.pallas_call(
        _ada_kernel,
        grid=(n_out // bn,),
        in_specs=[pl.BlockSpec((B, D_MODEL), lambda n: (0, 0)),
                  pl.BlockSpec((D_MODEL, bn), lambda n: (0, n)),
                  pl.BlockSpec((1, bn), lambda n: (0, n))],
        out_specs=pl.BlockSpec((B, bn), lambda n: (0, n)),
        out_shape=jax.ShapeDtypeStruct((B, n_out), F32),
        compiler_params=_cparams(1),
        name="ada",
    )(c, w_ada, b_ada.reshape(1, n_out))


def _inproj_kernel(x_ref, mod_ref, g_ref, w_ref, b_ref,
                   qm_ref, km_ref, vmT_ref, kmean_ref, qs_ref, ks_ref, vsT_ref):
    xf = x_ref[0]
    shift = mod_ref[0, 0:1, :]
    scale = mod_ref[0, 1:2, :]
    h = _rmsnorm_rows(xf, g_ref[...]) * (1.0 + scale) + shift
    proj = jnp.dot(h.astype(BF16), w_ref[...], preferred_element_type=F32) + b_ref[...]
    i0, i1, i2 = D_MOBA, 2 * D_MOBA, 3 * D_MOBA
    i3 = i2 + D_SWA_Q
    i4 = i3 + D_SWA_KV
    qm_ref[0] = proj[:, :i0].astype(BF16)
    km = proj[:, i0:i1]
    km_ref[0] = km.astype(BF16)
    L = MOBA_BLOCK
    for blk in range(proj.shape[0] // L):
        rows = slice(blk * L, (blk + 1) * L)
        kmean_ref[0, blk] = jnp.broadcast_to(jnp.mean(km[rows], axis=0, keepdims=True), (SUBLANES, D_MOBA))
        vmT = proj[rows, i1:i2].T.astype(BF16)
        vmT_ref[0, :, blk] = vmT.reshape(D_MOBA // LANES, LANES, L)
    qs_ref[0] = proj[:, i2:i3].astype(BF16)
    ks_ref[0] = proj[:, i3:i4].astype(BF16)
    vsT_ref[0] = proj[:, i4:].T.astype(BF16)


def _inproj(x, mod6, g_attn, w_in_bf, b_in):
    B, S, D = x.shape
    L = MOBA_BLOCK
    per_step = 4
    tm = per_step * L
    nb = S // L
    n_pairs = D_MOBA // LANES
    out_shape = (
        jax.ShapeDtypeStruct((B, S, D_MOBA), BF16),
        jax.ShapeDtypeStruct((B, S, D_MOBA), BF16),
        jax.ShapeDtypeStruct((B, n_pairs, nb, LANES, L), BF16),
        jax.ShapeDtypeStruct((B, nb, SUBLANES, D_MOBA), F32),
        jax.ShapeDtypeStruct((B, S, D_SWA_Q), BF16),
        jax.ShapeDtypeStruct((B, S, D_SWA_KV), BF16),
        jax.ShapeDtypeStruct((B, D_SWA_KV, S), BF16),
    )
    return pl.pallas_call(
        _inproj_kernel,
        grid=(B, S // tm),
        in_specs=[pl.BlockSpec((1, tm, D), lambda b, i: (b, i, 0)),
                  pl.BlockSpec((1, 6, D), lambda b, i: (b, 0, 0)),
                  pl.BlockSpec((1, D), lambda b, i: (0, 0)),
                  pl.BlockSpec((D, D_IN), lambda b, i: (0, 0)),
                  pl.BlockSpec((1, D_IN), lambda b, i: (0, 0))],
        out_specs=(pl.BlockSpec((1, tm, D_MOBA), lambda b, i: (b, i, 0)),
                   pl.BlockSpec((1, tm, D_MOBA), lambda b, i: (b, i, 0)),
                   pl.BlockSpec((1, n_pairs, per_step, LANES, L), lambda b, i: (b, 0, i, 0, 0)),
                   pl.BlockSpec((1, per_step, SUBLANES, D_MOBA), lambda b, i: (b, i, 0, 0)),
                   pl.BlockSpec((1, tm, D_SWA_Q), lambda b, i: (b, i, 0)),
                   pl.BlockSpec((1, tm, D_SWA_KV), lambda b, i: (b, i, 0)),
                   pl.BlockSpec((1, D_SWA_KV, tm), lambda b, i: (b, 0, i))),
        out_shape=out_shape,
        compiler_params=_cparams(2),
        name="inproj",
    )(x, mod6, g_attn, w_in_bf, b_in)


def _rel_bucket_static(n):
    n = max(n, 0)
    return n if n < REL_MAX_EXACT else REL_MAX_EXACT + sum(n >= t for t in _REL_THRESHOLDS)


def _rel_bias_values(dist, table_ref, h, d_min, d_max):
    lo, hi = _rel_bucket_static(d_min), _rel_bucket_static(d_max)
    n = jnp.maximum(dist, 0)
    val = jnp.full(n.shape, table_ref[lo, h], F32)
    for bkt in range(lo + 1, hi + 1):
        start = bkt if bkt <= REL_MAX_EXACT else _REL_THRESHOLDS[bkt - REL_MAX_EXACT - 1]
        val = jnp.where(n >= start, table_ref[bkt, h], val)
    return val


def _moba_bias_kernel(table_ref, o_ref):
    h = pl.program_id(0)
    L = MOBA_BLOCK
    key = lax.broadcasted_iota(jnp.int32, (L, L), 0)
    qry = lax.broadcasted_iota(jnp.int32, (L, L), 1)
    for dlt in range(o_ref.shape[1]):
        dist = dlt * L + qry - key
        val = _rel_bias_values(dist, table_ref, h, dlt * L - (L - 1), dlt * L + (L - 1))
        val = val * LOG2E
        o_ref[0, dlt] = jnp.where(dist >= 0, val, NEG) if dlt == 0 else val


def _moba_bias(rel_table, nb):
    L = MOBA_BLOCK
    return pl.pallas_call(
        _moba_bias_kernel,
        grid=(N_HEADS_MOBA,),
        in_specs=[pl.BlockSpec(memory_space=pltpu.SMEM)],
        out_specs=pl.BlockSpec((1, nb, L, L), lambda h: (h, 0, 0, 0)),
        out_shape=jax.ShapeDtypeStruct((N_HEADS_MOBA, nb, L, L), F32),
        compiler_params=_cparams(1),
        name="moba_bias",
    )(rel_table)


def _swa_bias_kernel(table_ref, o_ref):
    h = N_HEADS_MOBA + pl.program_id(0)
    W = SWA_BLOCK
    key = lax.broadcasted_iota(jnp.int32, (2 * W, W), 0)
    qry = lax.broadcasted_iota(jnp.int32, (2 * W, W), 1)
    dist = qry + W - key
    val = _rel_bias_values(dist, table_ref, h, 0, SWA_WINDOW - 1)
    o_ref[...] = jnp.where((dist >= 0) & (dist < SWA_WINDOW), val * LOG2E, NEG)


def _swa_bias(rel_table):
    W = SWA_BLOCK
    return pl.pallas_call(
        _swa_bias_kernel,
        grid=(N_HEADS_SWA,),
        in_specs=[pl.BlockSpec(memory_space=pltpu.SMEM)],
        out_specs=pl.BlockSpec((2 * W, W), lambda h: (0, h)),
        out_shape=jax.ShapeDtypeStruct((2 * W, N_HEADS_SWA * W), F32),
        compiler_params=_cparams(1),
        name="swa_bias",
    )(rel_table)


def _gate_kernel(q_ref, kmean_ref, sel_ref):
    S = q_ref.shape[1]
    nb = kmean_ref.shape[1]
    H = N_HEADS_MOBA
    q = q_ref[0]
    km = kmean_ref[0, :, 0, :]
    km_t = jnp.concatenate([km] * H, axis=0)
    row_h = lax.broadcasted_iota(jnp.int32, (H * nb, D_MOBA), 0) // nb
    col_h = lax.broadcasted_iota(jnp.int32, (H * nb, D_MOBA), 1) // HEAD_DIM
    km_t = jnp.where(row_h == col_h, km_t, 0.0)
    km_hi = km_t.astype(BF16)
    km_lo = (km_t - km_hi.astype(F32)).astype(BF16)
    g_all = _dot_nt(km_hi, q) + _dot_nt(km_lo, q)
    qblk = lax.broadcasted_iota(jnp.int32, (nb, S), 1) // MOBA_BLOCK
    jidx = lax.broadcasted_iota(jnp.int32, (nb, S), 0)
    past = jidx < qblk
    for h in range(H):
        g = g_all[h * nb:(h + 1) * nb, :]
        cnt = jnp.zeros((nb, S), jnp.int32)
        for jp in range(nb):
            gj = g[jp:jp + 1, :]
            ahead = (gj > g) | ((gj == g) & (jp < jidx))
            cnt = cnt + ((jp < qblk) & ahead).astype(jnp.int32)
        sel = (past & (cnt < MOBA_TOPK)).astype(F32)
        for j in range(nb):
            for half in range(2):
                sel_ref[0, h, j, half:half + 1, :] = sel[j:j + 1, half * (S // 2):(half + 1) * (S // 2)]


def _gate(qm, kmean):
    B, S, _ = qm.shape
    nb = kmean.shape[1]
    return pl.pallas_call(
        _gate_kernel,
        grid=(B,),
        in_specs=[pl.BlockSpec((1, S, D_MOBA), lambda b: (b, 0, 0)),
                  pl.BlockSpec((1, nb, SUBLANES, D_MOBA), lambda b: (b, 0, 0, 0))],
        out_specs=pl.BlockSpec((1, N_HEADS_MOBA, nb, 2, S // 2), lambda b: (b, 0, 0, 0, 0)),
        out_shape=jax.ShapeDtypeStruct((B, N_HEADS_MOBA, nb, 2, S // 2), F32),
        compiler_params=_cparams(1),
        name="moba_gate",
    )(qm, kmean)


def _moba_kernel(q_ref, k_ref, vT_ref, sel_ref, bias_ref, o_ref, s_buf):
    c = pl.program_id(2)
    L = MOBA_BLOCK
    nb = k_ref.shape[1] // L

    def attend(pair, half, n_past):
        lanes = slice(pair * LANES, (pair + 1) * LANES)
        q = q_ref[0, half, :, lanes]
        lane = lax.broadcasted_iota(jnp.int32, q.shape, 1)
        zero = jnp.zeros_like(q)
        q_heads = [jnp.where(lane < HEAD_DIM, q, zero), jnp.where(lane >= HEAD_DIM, q, zero)]
        outs = []
        for a in range(HEADS_PER_LANE_BLOCK):
            hd = pair * HEADS_PER_LANE_BLOCK + a
            chosen = [None if j == n_past else sel_ref[0, hd, j, half:half + 1, :] > 0.5
                      for j in range(n_past + 1)]
            m = None
            for j in range(n_past + 1):
                s = _dot_nt(k_ref[0, j * L:(j + 1) * L, lanes], q_heads[a]) + bias_ref[hd, n_past - j]
                if n_past > 0:
                    s_buf[half, hd, j * L:(j + 1) * L, :] = s
                mj = jnp.max(s, axis=0, keepdims=True)
                if chosen[j] is not None:
                    mj = jnp.where(chosen[j], mj, NEG)
                m = mj if m is None else jnp.maximum(m, mj)
            l = None
            acc = None
            for j in range(n_past + 1):
                shift = m if chosen[j] is None else jnp.where(chosen[j], m, -NEG)
                p = jnp.exp2((s_buf[half, hd, j * L:(j + 1) * L, :] if n_past > 0 else s) - shift)
                lj = jnp.sum(p, axis=0, keepdims=True)
                vT = vT_ref[0, pair, j, a * HEAD_DIM:(a + 1) * HEAD_DIM, :]
                oj = jnp.dot(vT, p.astype(BF16), preferred_element_type=F32)
                l = lj if l is None else l + lj
                acc = oj if acc is None else acc + oj
            outs.append(acc * (1.0 / l))
        o_ref[0, half, :, lanes] = jnp.concatenate(outs, axis=0).T.astype(BF16)

    def step(c_static):
        for pair in range(MOBA_PAIRS_PER_STEP):
            for half in range(2):
                attend(pair, half, c_static + half * (nb // 2))

    for c_static in range(nb // 2):
        pl.when(c == c_static)(functools.partial(step, c_static))


def _moba(qm, km, vmT, sel, bias):
    B, S, _ = qm.shape
    L = MOBA_BLOCK
    nb = S // L
    pairs = MOBA_PAIRS_PER_STEP
    heads = pairs * HEADS_PER_LANE_BLOCK
    width = pairs * LANES
    half_s = S // 2
    out = pl.pallas_call(
        _moba_kernel,
        grid=(D_MOBA // width, B, nb // 2),
        in_specs=[pl.BlockSpec((1, 2, L, width), lambda p, b, c: (b, 0, c, p)),
                  pl.BlockSpec((1, S, width), lambda p, b, c: (b, 0, p)),
                  pl.BlockSpec((1, pairs, nb, LANES, L), lambda p, b, c: (b, p, 0, 0, 0)),
                  pl.BlockSpec((1, heads, nb, 2, L), lambda p, b, c: (b, p, 0, 0, c)),
                  pl.BlockSpec((heads, nb, L, L), lambda p, b, c: (p, 0, 0, 0))],
        out_specs=pl.BlockSpec((1, 2, L, width), lambda p, b, c: (b, 0, c, p)),
        out_shape=jax.ShapeDtypeStruct((B, 2, half_s, D_MOBA), BF16),
        scratch_shapes=[pltpu.VMEM((2, heads, S, L), F32)],
        compiler_params=_cparams(3),
        name="moba_attn",
    )(qm.reshape(B, 2, half_s, D_MOBA), km, vmT, sel, bias)
    return out.reshape(B, S, D_MOBA)


def _swa_kernel(sink_ref, q_ref, *refs):
    n = SWA_BLOCKS_PER_STEP
    k_refs, v_refs = refs[:2 * n], refs[2 * n:4 * n]
    bias_ref, o_ref = refs[4 * n], refs[4 * n + 1]
    c = pl.program_id(1)
    for g in range(n):
        _swa_block(sink_ref, q_ref[0, g], k_refs[2 * g][0], k_refs[2 * g + 1][0], v_refs[2 * g][0],
                   v_refs[2 * g + 1][0], bias_ref, o_ref.at[0, g], c > 0 if g == 0 else None)


def _swa_block(sink_ref, q, k_prev, k_cur, vT_prev, vT_cur, bias_ref, o_ref, has_prev):
    W = SWA_BLOCK
    group = N_HEADS_SWA // N_KV_SWA
    kband = jnp.concatenate([k_prev, k_cur], axis=0)
    vbandT = jnp.concatenate([vT_prev, vT_cur], axis=1)
    q_rows = []
    for h in range(N_HEADS_SWA):
        qp = q[:, (h % group) * LANES:(h % group + 1) * LANES]
        lane = lax.broadcasted_iota(jnp.int32, qp.shape, 1)
        in_half = (lane >= HEAD_DIM) if h // group else (lane < HEAD_DIM)
        q_rows.append(jnp.where(in_half, qp, jnp.zeros_like(qp)))
    q_all = jnp.concatenate(q_rows, axis=0)
    sT = _dot_nt(kband, q_all) + bias_ref[...]
    if has_prev is not None:
        key = lax.broadcasted_iota(jnp.int32, sT.shape, 0)
        sT = jnp.where((key >= W) | has_prev, sT, NEG)
    sink = sink_ref[...]
    m = jnp.maximum(jnp.max(sT, axis=0, keepdims=True), sink)
    pr = jnp.exp2(sT - m)
    inv_l = 1.0 / (jnp.sum(pr, axis=0, keepdims=True) + jnp.exp2(sink - m))
    pr = pr.astype(BF16)
    cols = group * W
    outs = [jnp.dot(vbandT[g * HEAD_DIM:(g + 1) * HEAD_DIM, :], pr[:, g * cols:(g + 1) * cols],
                    preferred_element_type=F32) * inv_l[:, g * cols:(g + 1) * cols]
            for g in range(N_KV_SWA)]
    heads = [outs[h // group][:, (h % group) * W:(h % group + 1) * W] for h in range(N_HEADS_SWA)]
    o_ref[...] = jnp.concatenate(heads, axis=0).T.astype(BF16)


def _swa(sink_row, qs, ks, vsT, bias):
    B, S, _ = qs.shape
    W = SWA_BLOCK
    nq = S // W
    n = SWA_BLOCKS_PER_STEP
    far = nq // n
    k_spec = lambda f: pl.BlockSpec((1, W, D_SWA_KV), lambda b, c: (b, f(c), 0))
    v_spec = lambda f: pl.BlockSpec((1, D_SWA_KV, W), lambda b, c: (b, 0, f(c)))
    blocks = []
    for g in range(n):
        blocks.append((lambda c, g=g: jnp.maximum(c + g * far - 1, 0)))
        blocks.append((lambda c, g=g: c + g * far))
    out = pl.pallas_call(
        _swa_kernel,
        grid=(B, far),
        in_specs=[pl.BlockSpec((1, N_HEADS_SWA * W), lambda b, c: (0, 0)),
                  pl.BlockSpec((1, n, W, D_SWA_Q), lambda b, c: (b, 0, c, 0))]
                 + [k_spec(f) for f in blocks] + [v_spec(f) for f in blocks]
                 + [pl.BlockSpec((2 * W, N_HEADS_SWA * W), lambda b, c: (0, 0))],
        out_specs=pl.BlockSpec((1, n, W, D_SWA_Q), lambda b, c: (b, 0, c, 0)),
        out_shape=jax.ShapeDtypeStruct((B, n, S // n, D_SWA_Q), BF16),
        compiler_params=_cparams(2),
        name="swa_attn",
    )(sink_row, qs.reshape(B, n, S // n, D_SWA_Q), *([ks] * (2 * n)), *([vsT] * (2 * n)), bias)
    return out.reshape(B, S, D_SWA_Q)


def _outproj_kernel(x_ref, om_ref, os_ref, wm_ref, ws_ref, b_ref, mod_ref, g_ref, wrT_ref, br_ref,
                    x1_ref, hrt_ref, idx_ref, w_ref, rank_ref, counts_ref, carry_ref):
    attn = (jnp.dot(om_ref[...], wm_ref[...], preferred_element_type=F32)
            + jnp.dot(os_ref[...], ws_ref[...], preferred_element_type=F32) + b_ref[...])
    x1 = x_ref[...] + mod_ref[0, 2:3, :] * attn
    x1_ref[...] = x1
    h = _rmsnorm_rows(x1, g_ref[...]) * (1.0 + mod_ref[0, 4:5, :]) + mod_ref[0, 3:4, :]
    _store_row_tiles(hrt_ref, h)
    _route(h, wrT_ref, br_ref, idx_ref, w_ref, rank_ref, counts_ref, carry_ref)


def _store_row_tiles(ref, val, first_row=0):
    rows = val.shape[0]
    base = first_row * ROW_CHUNKS
    for c in range(ROW_CHUNKS):
        ref[pl.ds(base + c, rows, stride=ROW_CHUNKS), :] = val[:, c * LANES:(c + 1) * LANES]


def _load_row_tiles(ref, rows, lead=(), first_row=0):
    idx = tuple(lead)
    base = first_row * ROW_CHUNKS
    return jnp.concatenate([ref[idx + (pl.ds(base + c, rows, stride=ROW_CHUNKS), slice(None))]
                            for c in range(ROW_CHUNKS)], axis=1)


def _outproj(x2d, om, os_, w_out_m, w_out_s, b_out, mod6, g_ffn, w_routerT, b_router_col, seq_len):
    N, D = x2d.shape
    tm = OUTPROJ_TOKENS
    per_seq = seq_len // tm
    row = lambda t: (t, 0)
    col = lambda t: (0, t)
    const = lambda t: (0, 0)
    return pl.pallas_call(
        _outproj_kernel,
        grid=(N // tm,),
        in_specs=[pl.BlockSpec((tm, D), row),
                  pl.BlockSpec((tm, D_MOBA), row),
                  pl.BlockSpec((tm, D_SWA_Q), row),
                  pl.BlockSpec((D_MOBA, D), const),
                  pl.BlockSpec((D_SWA_Q, D), const),
                  pl.BlockSpec((1, D), const),
                  pl.BlockSpec((1, 6, D), lambda t: (t // per_seq, 0, 0)),
                  pl.BlockSpec((1, D), const),
                  pl.BlockSpec((N_EXPERTS, D), const),
                  pl.BlockSpec((N_EXPERTS, 1), const)],
        out_specs=(pl.BlockSpec((tm, D), row),
                   pl.BlockSpec((tm * ROW_CHUNKS, LANES), row),
                   pl.BlockSpec((TOP_K, tm), col),
                   pl.BlockSpec((TOP_K, tm), col),
                   pl.BlockSpec((TOP_K, tm), col),
                   pl.BlockSpec((N_EXPERTS, LANES), const)),
        out_shape=(jax.ShapeDtypeStruct((N, D), F32),
                   jax.ShapeDtypeStruct((N * ROW_CHUNKS, LANES), F32),
                   jax.ShapeDtypeStruct((TOP_K, N), jnp.int32),
                   jax.ShapeDtypeStruct((TOP_K, N), F32),
                   jax.ShapeDtypeStruct((TOP_K, N), jnp.int32),
                   jax.ShapeDtypeStruct((N_EXPERTS, LANES), jnp.int32)),
        scratch_shapes=[pltpu.VMEM((N_EXPERTS, LANES), F32)],
        compiler_params=_cparams(1),
        name="outproj_route",
    )(x2d, om, os_, w_out_m, w_out_s, b_out, mod6, g_ffn, w_routerT, b_router_col)


def _route(h, wT_ref, b_ref, idx_ref, w_ref, rank_ref, counts_ref, carry_ref):
    @pl.when(pl.program_id(0) == 0)
    def _():
        carry_ref[...] = jnp.zeros_like(carry_ref)

    w = wT_ref[...]
    w_hi = w.astype(BF16)
    w_lo = (w - w_hi.astype(F32)).astype(BF16)
    h_hi = h.astype(BF16)
    h_lo = (h - h_hi.astype(F32)).astype(BF16)
    by_h_hi = _dot_nt(jnp.concatenate([w_hi, w_lo], axis=0), h_hi)
    logits = (by_h_hi[:N_EXPERTS] + by_h_hi[N_EXPERTS:] + _dot_nt(w_hi, h_lo)) + b_ref[...]
    tm = logits.shape[1]
    eidx = lax.broadcasted_iota(jnp.int32, logits.shape, 0)
    vals, idxs = [], []
    cur = logits
    for _ in range(TOP_K):
        m = jnp.max(cur, axis=0, keepdims=True)
        am = jnp.min(jnp.where(cur == m, eidx, N_EXPERTS), axis=0, keepdims=True)
        vals.append(m)
        idxs.append(am)
        cur = jnp.where(eidx == am, -jnp.inf, cur)
    exps = [jnp.exp(v - vals[0]) for v in vals]
    inv = 1.0 / functools.reduce(lambda a, b: a + b, exps)
    idx_ref[...] = jnp.concatenate(idxs, axis=0)
    w_ref[...] = jnp.concatenate([e * inv for e in exps], axis=0)

    member = functools.reduce(lambda a, b: a | b, [eidx == am for am in idxs])
    earlier = (lax.broadcasted_iota(jnp.int32, (tm, tm), 0)
               < lax.broadcasted_iota(jnp.int32, (tm, tm), 1)).astype(BF16)
    before = jnp.dot(member.astype(BF16), earlier, preferred_element_type=F32) + carry_ref[:, 0:1]
    rank_ref[...] = jnp.concatenate(
        [jnp.sum(jnp.where(eidx == am, before, 0.0), axis=0, keepdims=True) for am in idxs],
        axis=0).astype(jnp.int32)
    total = carry_ref[...] + jnp.sum(member.astype(F32), axis=1, keepdims=True)
    carry_ref[...] = total
    counts_ref[...] = total.astype(jnp.int32)


def _expert_mlp(h_bf, wgu, bgu, wdn, bdn):
    gu = jnp.dot(h_bf, wgu, preferred_element_type=F32) + bgu
    g = jnp.minimum(gu[:, :D_FF], SWIGLU_LIMIT)
    u = jnp.clip(gu[:, D_FF:], -SWIGLU_LIMIT, SWIGLU_LIMIT)
    act = (u + 1.0) * (g * _sigmoid(SWIGLU_ALPHA * g))
    return jnp.dot(act.astype(BF16), wdn, preferred_element_type=F32) + bdn


def _tile_plan(counts, n_tiles_max):
    tiles_per = (counts + MOE_TILE - 1) // MOE_TILE
    tile_end = jnp.cumsum(tiles_per)
    tile_start = tile_end - tiles_per
    t_ids = jnp.arange(n_tiles_max, dtype=jnp.int32)
    texp = jnp.minimum(jnp.sum((t_ids[:, None] >= tile_end[None, :]).astype(jnp.int32), axis=1),
                       N_EXPERTS - 1)
    onehot = (texp[:, None] == jnp.arange(N_EXPERTS, dtype=jnp.int32)[None, :]).astype(jnp.int32)
    local = t_ids - onehot @ tile_start
    nvalid = jnp.clip(onehot @ counts - local * MOE_TILE, 0, MOE_TILE)
    n_active = tile_end[-1]
    src_tile = jnp.minimum(t_ids, n_active - 1)
    last_tile = jnp.where(counts > 0, tile_end - 1, -1)
    tail = n_active + jnp.arange(N_EXPERTS, dtype=jnp.int32)
    zero_tiles = jnp.concatenate([last_tile, jnp.where(tail < n_tiles_max, tail, -1)])
    ids = jnp.arange(N_EXPERTS, dtype=jnp.int32)
    later = jnp.where((ids[None, :] > ids[:, None]) & (counts[None, :] > 0), ids[None, :], N_EXPERTS)
    next_expert = jnp.min(later, axis=1)
    next_expert = jnp.where(next_expert < N_EXPERTS, next_expert, -1)
    return (texp.astype(jnp.int32), nvalid.astype(jnp.int32), src_tile.astype(jnp.int32),
            (tile_start * MOE_TILE).astype(jnp.int32), zero_tiles.astype(jnp.int32),
            next_expert.astype(jnp.int32))


def _row_copy(src, src_first, dst, dst_first, sem):
    return pltpu.make_async_copy(src.at[pl.ds(pl.multiple_of(src_first, ROW_CHUNKS), ROW_CHUNKS)],
                                 dst.at[pl.ds(pl.multiple_of(dst_first, ROW_CHUNKS), ROW_CHUNKS)],
                                 sem)


DISPATCH_SLOTS = 3


def _dispatch_kernel(zero_tiles_ref, pos_ref, h_hbm, x_hbm, zbuf, hbuf, zsem, load_sem, row_sem):
    t = pl.program_id(0)
    n_steps = pl.num_programs(0)
    tile_rows = MOE_TILE * ROW_CHUNKS
    block_rows = DISPATCH_TOKENS * ROW_CHUNKS

    def block_load(step, slot):
        start = pl.multiple_of(step * block_rows, block_rows)
        return pltpu.make_async_copy(h_hbm.at[pl.ds(start, block_rows)], hbuf.at[slot], load_sem.at[slot])

    def rows_done(slot):
        for k in range(TOP_K):
            pltpu.make_async_copy(hbuf.at[slot], x_hbm.at[pl.ds(0, block_rows)], row_sem.at[slot]).wait()

    def zero_copy(i):
        start = pl.multiple_of(zero_tiles_ref[i] * tile_rows, tile_rows)
        return pltpu.make_async_copy(zbuf, x_hbm.at[pl.ds(start, tile_rows)], zsem)

    @pl.when(t == 0)
    def _():
        zbuf[...] = jnp.zeros_like(zbuf)
        for i in range(2 * N_EXPERTS):
            @pl.when(zero_tiles_ref[i] >= 0)
            def _():
                zero_copy(i).start()
        block_load(0, 0).start()
        if DISPATCH_SLOTS > 2:
            @pl.when(n_steps > 1)
            def _():
                block_load(1, 1).start()
        for i in range(2 * N_EXPERTS):
            @pl.when(zero_tiles_ref[i] >= 0)
            def _():
                zero_copy(i).wait()

    slot = t % DISPATCH_SLOTS
    block_load(t, slot).wait()

    def start_rows(r, carry):
        for k in range(TOP_K):
            _row_copy(hbuf.at[slot], r * ROW_CHUNKS, x_hbm, pos_ref[0, 0, k * DISPATCH_TOKENS + r],
                      row_sem.at[slot]).start(priority=k % 2)
        return carry
    lax.fori_loop(0, DISPATCH_TOKENS, start_rows, 0, unroll=8)

    refill = (t + 2) % DISPATCH_SLOTS

    @pl.when(t > 0)
    def _():
        rows_done(refill)

    @pl.when(t + 2 < n_steps)
    def _():
        block_load(t + 2, refill).start()

    @pl.when(t == n_steps - 1)
    def _():
        rows_done(slot)


def _dispatch(zero_tiles, pos_tiles, h_rt, n_rows):
    n_steps = pos_tiles.shape[0]
    grid_spec = pltpu.PrefetchScalarGridSpec(
        num_scalar_prefetch=1,
        grid=(n_steps,),
        in_specs=[pl.BlockSpec((1, 1, TOP_K * DISPATCH_TOKENS), lambda t, zt: (t, 0, 0),
                               memory_space=pltpu.SMEM),
                  pl.BlockSpec(memory_space=pl.ANY)],
        out_specs=pl.BlockSpec(memory_space=pl.ANY),
        scratch_shapes=[pltpu.VMEM((MOE_TILE * ROW_CHUNKS, LANES), F32),
                        pltpu.VMEM((DISPATCH_SLOTS, DISPATCH_TOKENS * ROW_CHUNKS, LANES), F32),
                        pltpu.SemaphoreType.DMA(()),
                        pltpu.SemaphoreType.DMA((DISPATCH_SLOTS,)),
                        pltpu.SemaphoreType.DMA((DISPATCH_SLOTS,))],
    )
    return pl.pallas_call(
        _dispatch_kernel,
        grid_spec=grid_spec,
        out_shape=jax.ShapeDtypeStruct((n_rows * ROW_CHUNKS, LANES), F32),
        compiler_params=_cparams(1),
        name="moe_dispatch",
    )(zero_tiles, pos_tiles, h_rt)


def _experts_kernel(texp_ref, nvalid_ref, src_ref, next_ref, x_ref, wgu_hbm, bgu_ref, wdn_hbm, bdn_ref,
                    y_ref, wgu_f32, wdn_f32, wgu_bf, wdn_bf, wsem):
    t = pl.program_id(0)
    nv = nvalid_ref[t]
    e = texp_ref[t]
    prev_e = texp_ref[jnp.maximum(t - 1, 0)]

    def weight_loads(expert):
        return (pltpu.make_async_copy(wgu_hbm.at[expert], wgu_f32, wsem.at[0]),
                pltpu.make_async_copy(wdn_hbm.at[expert], wdn_f32, wsem.at[1]))

    @pl.when((nv > 0) & ((t == 0) | (e != prev_e)))
    def _():
        @pl.when(t == 0)
        def _():
            for cp in weight_loads(e):
                cp.start()
        for cp in weight_loads(e):
            cp.wait()
        rows = CAST_ROWS
        def cast_gu(i, carry):
            r0 = pl.multiple_of(i * rows, rows)
            wgu_bf[pl.ds(r0, rows), :] = wgu_f32[pl.ds(r0, rows), :].astype(BF16)
            return carry
        lax.fori_loop(0, D_MODEL // rows, cast_gu, 0)
        def cast_dn(i, carry):
            r0 = pl.multiple_of(i * rows, rows)
            wdn_bf[pl.ds(r0, rows), :] = wdn_f32[pl.ds(r0, rows), :].astype(BF16)
            return carry
        lax.fori_loop(0, D_FF // rows, cast_dn, 0)
        nxt = next_ref[e]

        @pl.when(nxt >= 0)
        def _():
            for cp in weight_loads(nxt):
                cp.start()

    def mlp_rows(first_row):
        x = _load_row_tiles(x_ref, MOE_MM_ROWS, first_row=first_row).astype(BF16)
        _store_row_tiles(y_ref, _expert_mlp(x, wgu_bf[...], bgu_ref[0], wdn_bf[...], bdn_ref[0]), first_row)

    n_groups = MOE_TILE // MOE_MM_ROWS
    for live in range(n_groups + 1):
        lo, hi = (live - 1) * MOE_MM_ROWS, live * MOE_MM_ROWS

        @pl.when((nv > lo) & (nv <= hi) if live else (nv == 0))
        def _():
            for g in range(live):
                mlp_rows(g * MOE_MM_ROWS)
            if live < n_groups:
                y_ref[pl.ds(live * MOE_MM_ROWS * ROW_CHUNKS, (n_groups - live) * MOE_MM_ROWS * ROW_CHUNKS), :] = (
                    jnp.zeros(((n_groups - live) * MOE_MM_ROWS * ROW_CHUNKS, LANES), F32))


def _experts(texp, nvalid, src_tile, next_expert, x_sorted, w_gate_up, b_gate_up, w_down, b_down):
    n_tiles = texp.shape[0]
    D = D_MODEL
    tile_rows = MOE_TILE * ROW_CHUNKS
    grid_spec = pltpu.PrefetchScalarGridSpec(
        num_scalar_prefetch=4,
        grid=(n_tiles,),
        in_specs=[pl.BlockSpec((tile_rows, LANES), lambda t, te, nv, st, nx: (st[t], 0)),
                  pl.BlockSpec(memory_space=pl.ANY),
                  pl.BlockSpec((1, 1, 2 * D_FF), lambda t, te, nv, st, nx: (te[t], 0, 0)),
                  pl.BlockSpec(memory_space=pl.ANY),
                  pl.BlockSpec((1, 1, D), lambda t, te, nv, st, nx: (te[t], 0, 0))],
        out_specs=pl.BlockSpec((tile_rows, LANES), lambda t, te, nv, st, nx: (t, 0)),
        scratch_shapes=[pltpu.VMEM((D, 2 * D_FF), F32),
                        pltpu.VMEM((D_FF, D), F32),
                        pltpu.VMEM((D, 2 * D_FF), BF16),
                        pltpu.VMEM((D_FF, D), BF16),
                        pltpu.SemaphoreType.DMA((2,))],
    )
    return pl.pallas_call(
        _experts_kernel,
        grid_spec=grid_spec,
        out_shape=jax.ShapeDtypeStruct(x_sorted.shape, F32),
        compiler_params=_cparams(1),
        name="moe_experts",
    )(texp, nvalid, src_tile, next_expert, x_sorted, w_gate_up, b_gate_up.reshape(N_EXPERTS, 1, 2 * D_FF),
      w_down, b_down.reshape(N_EXPERTS, 1, D))


def _combine_kernel(pos_ref, pos_next_ref, y_hbm, w_ref, x1_ref, mod_ref, g_ref, o_ref, ybuf, sem):
    t = pl.program_id(0)
    n_steps = pl.num_programs(0)
    slot = t % 2

    def start_gather(idx_ref, s):
        def body(r, carry):
            for k in range(TOP_K):
                i = k * COMBINE_TOKENS + r
                _row_copy(y_hbm, idx_ref[0, 0, i], ybuf.at[s], i * ROW_CHUNKS, sem.at[s]).start(priority=k % 2)
            return carry
        lax.fori_loop(0, COMBINE_TOKENS, body, 0, unroll=8)

    @pl.when(t == 0)
    def _():
        start_gather(pos_ref, 0)

    @pl.when(t + 1 < n_steps)
    def _():
        start_gather(pos_next_ref, 1 - slot)

    slot_rows = TOP_K * COMBINE_TOKENS * ROW_CHUNKS
    pltpu.make_async_copy(y_hbm.at[pl.ds(0, slot_rows)], ybuf.at[slot], sem.at[slot]).wait()

    w = w_ref[...]
    acc = w[:, 0:1] * _load_row_tiles(ybuf, COMBINE_TOKENS, (slot,))
    for k in range(1, TOP_K):
        acc = acc + w[:, k:k + 1] * _load_row_tiles(ybuf, COMBINE_TOKENS, (slot,), k * COMBINE_TOKENS)
    x2 = x1_ref[...] + mod_ref[0, 5:6, :] * acc
    o_ref[...] = _rmsnorm_rows(x2, g_ref[...])


def _combine(pos_tiles, y_sorted, top_w, x1, mod6, g_final, seq_len):
    N, D = x1.shape
    tm = COMBINE_TOKENS
    n_steps = N // tm
    per_seq = seq_len // tm
    return pl.pallas_call(
        _combine_kernel,
        grid=(n_steps,),
        in_specs=[pl.BlockSpec((1, 1, TOP_K * tm), lambda t: (t, 0, 0), memory_space=pltpu.SMEM),
                  pl.BlockSpec((1, 1, TOP_K * tm), lambda t: (jnp.minimum(t + 1, n_steps - 1), 0, 0),
                               memory_space=pltpu.SMEM),
                  pl.BlockSpec(memory_space=pl.ANY),
                  pl.BlockSpec((tm, TOP_K), lambda t: (t, 0)),
                  pl.BlockSpec((tm, D), lambda t: (t, 0)),
                  pl.BlockSpec((1, 6, D), lambda t: (t // per_seq, 0, 0)),
                  pl.BlockSpec((1, D), lambda t: (0, 0))],
        out_specs=pl.BlockSpec((tm, D), lambda t: (t, 0)),
        out_shape=jax.ShapeDtypeStruct((N, D), F32),
        scratch_shapes=[pltpu.VMEM((2, TOP_K * tm * ROW_CHUNKS, LANES), F32),
                        pltpu.SemaphoreType.DMA((2,))],
        compiler_params=_cparams(1),
        name="moe_combine",
    )(pos_tiles, pos_tiles, y_sorted, top_w, x1, mod6, g_final)


def kernel(x, c, g_attn, w_ada, b_ada, w_in, b_in, w_out, b_out, rel_table, sinks, g_ffn, w_router,
           b_router, w_gate_up, b_gate_up, w_down, b_down, g_final):
    B, S, D = x.shape
    assert w_ada.shape[0] == 1, "the final norm is fused into the last layer; one layer supported"
    l = 0
    nb = S // MOBA_BLOCK
    i2 = 3 * D_MOBA
    i3 = i2 + D_SWA_Q
    group = N_HEADS_SWA // N_KV_SWA
    q_scale = ATTN_SCALE * LOG2E

    def prep_in(w):
        lead = w.shape[:-1]
        q_swa = (w[..., i2:i3] * q_scale).reshape(lead + (N_KV_SWA, group, HEAD_DIM))
        q_swa = jnp.swapaxes(q_swa, -3, -2).reshape(lead + (D_SWA_Q,))
        return jnp.concatenate([w[..., :D_MOBA] * q_scale, w[..., D_MOBA:i2], q_swa, w[..., i3:]], axis=-1)

    moba_bias = _moba_bias(rel_table, nb)
    swa_bias = _swa_bias(rel_table)

    mod6 = _ada(c, w_ada[l], b_ada[l]).reshape(B, 6, D)
    w_in_l = prep_in(w_in[l]).astype(BF16)
    b_in_l = prep_in(b_in[l]).reshape(1, D_IN)
    qm, km, vmT, kmean, qs, ks, vsT = _inproj(x, mod6, g_attn[l].reshape(1, D), w_in_l, b_in_l)
    sel = _gate(qm, kmean)
    o_m = _moba(qm, km, vmT, sel, moba_bias)
    sink_row = jnp.repeat(sinks[l] * LOG2E, SWA_BLOCK).reshape(1, N_HEADS_SWA * SWA_BLOCK)
    o_s = _swa(sink_row, qs, ks, vsT, swa_bias)
    w_out_m = w_out[l, :D_MOBA].astype(BF16)
    w_out_s = w_out[l, D_MOBA:].astype(BF16)
    N = B * S
    x1, h_rt, top_idx, top_w, rank, counts = _outproj(
        x.reshape(N, D), o_m.reshape(N, D_MOBA), o_s.reshape(N, D_SWA_Q), w_out_m, w_out_s,
        b_out[l].reshape(1, D), mod6, g_ffn[l].reshape(1, D), w_router[l].T,
        b_router[l].reshape(N_EXPERTS, 1), S)
    counts = counts[:, 0]
    n_tiles_max = N * TOP_K // MOE_TILE + N_EXPERTS
    texp, nvalid, src_tile, group_start, zero_tiles, next_expert = _tile_plan(counts, n_tiles_max)
    experts = jnp.arange(N_EXPERTS, dtype=jnp.int32)
    pos = rank + jnp.sum(jnp.where(top_idx[..., None] == experts, group_start, 0), axis=-1)
    def pos_tiles(tokens):
        return (pos * ROW_CHUNKS).reshape(TOP_K, N // tokens, tokens).transpose(1, 0, 2).reshape(
            N // tokens, 1, TOP_K * tokens)

    x_sorted = _dispatch(zero_tiles, pos_tiles(DISPATCH_TOKENS), h_rt, n_tiles_max * MOE_TILE)
    y_sorted = _experts(texp, nvalid, src_tile, next_expert, x_sorted, w_gate_up[l], b_gate_up[l], w_down[l],
                        b_down[l])
    y = _combine(pos_tiles(COMBINE_TOKENS), y_sorted, top_w.T, x1, mod6, g_final.reshape(1, D), S)
    return y.reshape(B, S, D)
```

```python
import functools
import math

import jax
import jax.numpy as jnp
from jax import lax
from jax.experimental import pallas as pl
from jax.experimental.pallas import tpu as pltpu

F32 = jnp.float32
BF16 = jnp.bfloat16

D_MODEL = 1024
HEAD_DIM = 64
N_HEADS_MOBA = 8
N_HEADS_SWA = 8
N_KV_SWA = 2
D_MOBA = N_HEADS_MOBA * HEAD_DIM
D_SWA_Q = N_HEADS_SWA * HEAD_DIM
D_SWA_KV = N_KV_SWA * HEAD_DIM
D_IN = 3 * D_MOBA + D_SWA_Q + 2 * D_SWA_KV
MOBA_BLOCK = 256
MOBA_TOPK = 3
SWA_WINDOW = 128
SWA_BLOCK = 128
REL_BUCKETS = 32
REL_MAX_DIST = 1024
REL_MAX_EXACT = REL_BUCKETS // 2
N_EXPERTS = 32
TOP_K = 4
D_FF = 1024
SWIGLU_LIMIT = 7.0
SWIGLU_ALPHA = 1.702
EPS = 1e-5
NEG = -1e30
ATTN_SCALE = HEAD_DIM ** -0.5
LOG2E = math.log2(math.e)

MOE_TILE = 512
MOE_MM_ROWS = 256
ROW_CHUNKS = D_MODEL // 128
DISPATCH_TOKENS = 512
COMBINE_TOKENS = 512
MOBA_PAIRS_PER_STEP = 2
SWA_BLOCKS_PER_STEP = 4
LANES = 128
SUBLANES = 8
HEADS_PER_LANE_BLOCK = LANES // HEAD_DIM
ADA_COLS = 1536
OUTPROJ_TOKENS = 1024
CAST_ROWS = 256
VMEM_LIMIT_BYTES = 56 * 1024 * 1024

_REL_THRESHOLDS = tuple(
    math.ceil(REL_MAX_EXACT * (REL_MAX_DIST / REL_MAX_EXACT) ** (k / (REL_BUCKETS - REL_MAX_EXACT)) - 1e-9)
    for k in range(1, REL_BUCKETS - REL_MAX_EXACT))


def _cparams(n_axes):
    return pltpu.CompilerParams(dimension_semantics=("arbitrary",) * n_axes,
                                vmem_limit_bytes=VMEM_LIMIT_BYTES)


def _sigmoid(z):
    return 1.0 / (1.0 + jnp.exp(-z))


def _rmsnorm_rows(xf, g):
    ms = jnp.mean(xf * xf, axis=-1, keepdims=True)
    return xf * lax.rsqrt(ms + EPS) * g


def _dot_nt(a, b, **kw):
    return lax.dot_general(a, b, (((1,), (1,)), ((), ())), preferred_element_type=F32, **kw)


def _ada_kernel(c_ref, w_ref, b_ref, o_ref):
    c = c_ref[...]
    sc = c * _sigmoid(c)
    o_ref[...] = jnp.dot(sc, w_ref[...], preferred_element_type=F32,
                         precision=lax.Precision.HIGHEST) + b_ref[...]


def _ada(c, w_ada, b_ada):
    B = c.shape[0]
    n_out = w_ada.shape[1]
    bn = ADA_COLS
    return pl.pallas_call(
        _ada_kernel,
        grid=(n_out // bn,),
        in_specs=[pl.BlockSpec((B, D_MODEL), lambda n: (0, 0)),
                  pl.BlockSpec((D_MODEL, bn), lambda n: (0, n)),
                  pl.BlockSpec((1, bn), lambda n: (0, n))],
        out_specs=pl.BlockSpec((B, bn), lambda n: (0, n)),
        out_shape=jax.ShapeDtypeStruct((B, n_out), F32),
        compiler_params=_cparams(1),
        name="ada",
    )(c, w_ada, b_ada.reshape(1, n_out))


def _inproj_kernel(x_ref, mod_ref, g_ref, w_ref, b_ref,
                   qm_ref, km_ref, vmT_ref, kmean_ref, qs_ref, ks_ref, vsT_ref):
    xf = x_ref[0]
    shift = mod_ref[0, 0:1, :]
    scale = mod_ref[0, 1:2, :]
    h = _rmsnorm_rows(xf, g_ref[...]) * (1.0 + scale) + shift
    proj = jnp.dot(h.astype(BF16), w_ref[...], preferred_element_type=F32) + b_ref[...]
    i0, i1, i2 = D_MOBA, 2 * D_MOBA, 3 * D_MOBA
    i3 = i2 + D_SWA_Q
    i4 = i3 + D_SWA_KV
    qm_ref[0] = proj[:, :i0].astype(BF16)
    km = proj[:, i0:i1]
    km_ref[0] = km.astype(BF16)
    L = MOBA_BLOCK
    for blk in range(proj.shape[0] // L):
        rows = slice(blk * L, (blk + 1) * L)
        kmean_ref[0, blk] = jnp.broadcast_to(jnp.mean(km[rows], axis=0, keepdims=True), (SUBLANES, D_MOBA))
        vmT = proj[rows, i1:i2].T.astype(BF16)
        vmT_ref[0, :, blk] = vmT.reshape(D_MOBA // LANES, LANES, L)
    qs_ref[0] = proj[:, i2:i3].astype(BF16)
    ks_ref[0] = proj[:, i3:i4].astype(BF16)
    vsT_ref[0] = proj[:, i4:].T.astype(BF16)


def _inproj(x, mod6, g_attn, w_in_bf, b_in):
    B, S, D = x.shape
    L = MOBA_BLOCK
    per_step = 4
    tm = per_step * L
    nb = S // L
    n_pairs = D_MOBA // LANES
    out_shape = (
        jax.ShapeDtypeStruct((B, S, D_MOBA), BF16),
        jax.ShapeDtypeStruct((B, S, D_MOBA), BF16),
        jax.ShapeDtypeStruct((B, n_pairs, nb, LANES, L), BF16),
        jax.ShapeDtypeStruct((B, nb, SUBLANES, D_MOBA), F32),
        jax.ShapeDtypeStruct((B, S, D_SWA_Q), BF16),
        jax.ShapeDtypeStruct((B, S, D_SWA_KV), BF16),
        jax.ShapeDtypeStruct((B, D_SWA_KV, S), BF16),
    )
    return pl.pallas_call(
        _inproj_kernel,
        grid=(B, S // tm),
        in_specs=[pl.BlockSpec((1, tm, D), lambda b, i: (b, i, 0)),
                  pl.BlockSpec((1, 6, D), lambda b, i: (b, 0, 0)),
                  pl.BlockSpec((1, D), lambda b, i: (0, 0)),
                  pl.BlockSpec((D, D_IN), lambda b, i: (0, 0)),
                  pl.BlockSpec((1, D_IN), lambda b, i: (0, 0))],
        out_specs=(pl.BlockSpec((1, tm, D_MOBA), lambda b, i: (b, i, 0)),
                   pl.BlockSpec((1, tm, D_MOBA), lambda b, i: (b, i, 0)),
                   pl.BlockSpec((1, n_pairs, per_step, LANES, L), lambda b, i: (b, 0, i, 0, 0)),
                   pl.BlockSpec((1, per_step, SUBLANES, D_MOBA), lambda b, i: (b, i, 0, 0)),
                   pl.BlockSpec((1, tm, D_SWA_Q), lambda b, i: (b, i, 0)),
                   pl.BlockSpec((1, tm, D_SWA_KV), lambda b, i: (b, i, 0)),
                   pl.BlockSpec((1, D_SWA_KV, tm), lambda b, i: (b, 0, i))),
        out_shape=out_shape,
        compiler_params=_cparams(2),
        name="inproj",
    )(x, mod6, g_attn, w_in_bf, b_in)


def _rel_bucket_static(n):
    n = max(n, 0)
    return n if n < REL_MAX_EXACT else REL_MAX_EXACT + sum(n >= t for t in _REL_THRESHOLDS)


def _rel_bias_values(dist, table_ref, h, d_min, d_max):
    lo, hi = _rel_bucket_static(d_min), _rel_bucket_static(d_max)
    n = jnp.maximum(dist, 0)
    val = jnp.full(n.shape, table_ref[lo, h], F32)
    for bkt in range(lo + 1, hi + 1):
        start = bkt if bkt <= REL_MAX_EXACT else _REL_THRESHOLDS[bkt - REL_MAX_EXACT - 1]
        val = jnp.where(n >= start, table_ref[bkt, h], val)
    return val


def _moba_bias_kernel(table_ref, o_ref):
    h = pl.program_id(0)
    L = MOBA_BLOCK
    key = lax.broadcasted_iota(jnp.int32, (L, L), 0)
    qry = lax.broadcasted_iota(jnp.int32, (L, L), 1)
    for dlt in range(o_ref.shape[1]):
        dist = dlt * L + qry - key
        val = _rel_bias_values(dist, table_ref, h, dlt * L - (L - 1), dlt * L + (L - 1))
        val = val * LOG2E
        o_ref[0, dlt] = jnp.where(dist >= 0, val, NEG) if dlt == 0 else val


def _moba_bias(rel_table, nb):
    L = MOBA_BLOCK
    return pl.pallas_call(
        _moba_bias_kernel,
        grid=(N_HEADS_MOBA,),
        in_specs=[pl.BlockSpec(memory_space=pltpu.SMEM)],
        out_specs=pl.BlockSpec((1, nb, L, L), lambda h: (h, 0, 0, 0)),
        out_shape=jax.ShapeDtypeStruct((N_HEADS_MOBA, nb, L, L), F32),
        compiler_params=_cparams(1),
        name="moba_bias",
    )(rel_table)


def _swa_bias_kernel(table_ref, o_ref):
    h = N_HEADS_MOBA + pl.program_id(0)
    W = SWA_BLOCK
    key = lax.broadcasted_iota(jnp.int32, (2 * W, W), 0)
    qry = lax.broadcasted_iota(jnp.int32, (2 * W, W), 1)
    dist = qry + W - key
    val = _rel_bias_values(dist, table_ref, h, 0, SWA_WINDOW - 1)
    o_ref[...] = jnp.where((dist >= 0) & (dist < SWA_WINDOW), val * LOG2E, NEG)


def _swa_bias(rel_table):
    W = SWA_BLOCK
    return pl.pallas_call(
        _swa_bias_kernel,
        grid=(N_HEADS_SWA,),
        in_specs=[pl.BlockSpec(memory_space=pltpu.SMEM)],
        out_specs=pl.BlockSpec((2 * W, W), lambda h: (0, h)),
        out_shape=jax.ShapeDtypeStruct((2 * W, N_HEADS_SWA * W), F32),
        compiler_params=_cparams(1),
        name="swa_bias",
    )(rel_table)


def _gate_kernel(q_ref, kmean_ref, sel_ref):
    S = q_ref.shape[1]
    nb = kmean_ref.shape[1]
    H = N_HEADS_MOBA
    q = q_ref[0]
    km = kmean_ref[0, :, 0, :]
    km_t = jnp.concatenate([km] * H, axis=0)
    row_h = lax.broadcasted_iota(jnp.int32, (H * nb, D_MOBA), 0) // nb
    col_h = lax.broadcasted_iota(jnp.int32, (H * nb, D_MOBA), 1) // HEAD_DIM
    km_t = jnp.where(row_h == col_h, km_t, 0.0)
    km_hi = km_t.astype(BF16)
    km_lo = (km_t - km_hi.astype(F32)).astype(BF16)
    g_all = _dot_nt(km_hi, q) + _dot_nt(km_lo, q)
    qblk = lax.broadcasted_iota(jnp.int32, (nb, S), 1) // MOBA_BLOCK
    jidx = lax.broadcasted_iota(jnp.int32, (nb, S), 0)
    past = jidx < qblk
    for h in range(H):
        g = g_all[h * nb:(h + 1) * nb, :]
        cnt = jnp.zeros((nb, S), jnp.int32)
        for jp in range(nb):
            gj = g[jp:jp + 1, :]
            ahead = (gj > g) | ((gj == g) & (jp < jidx))
            cnt = cnt + ((jp < qblk) & ahead).astype(jnp.int32)
        sel = (past & (cnt < MOBA_TOPK)).astype(F32)
        for j in range(nb):
            for half in range(2):
                sel_ref[0, h, j, half:half + 1, :] = sel[j:j + 1, half * (S // 2):(half + 1) * (S // 2)]


def _gate(qm, kmean):
    B, S, _ = qm.shape
    nb = kmean.shape[1]
    return pl.pallas_call(
        _gate_kernel,
        grid=(B,),
        in_specs=[pl.BlockSpec((1, S, D_MOBA), lambda b: (b, 0, 0)),
                  pl.BlockSpec((1, nb, SUBLANES, D_MOBA), lambda b: (b, 0, 0, 0))],
        out_specs=pl.BlockSpec((1, N_HEADS_MOBA, nb, 2, S // 2), lambda b: (b, 0, 0, 0, 0)),
        out_shape=jax.ShapeDtypeStruct((B, N_HEADS_MOBA, nb, 2, S // 2), F32),
        compiler_params=_cparams(1),
        name="moba_gate",
    )(qm, kmean)


def _moba_kernel(q_ref, k_ref, vT_ref, sel_ref, bias_ref, o_ref, s_buf):
    c = pl.program_id(2)
    L = MOBA_BLOCK
    nb = k_ref.shape[1] // L

    def attend(pair, half, n_past):
        lanes = slice(pair * LANES, (pair + 1) * LANES)
        q = q_ref[0, half, :, lanes]
        lane = lax.broadcasted_iota(jnp.int32, q.shape, 1)
        zero = jnp.zeros_like(q)
        q_heads = [jnp.where(lane < HEAD_DIM, q, zero), jnp.where(lane >= HEAD_DIM, q, zero)]
        outs = []
        for a in range(HEADS_PER_LANE_BLOCK):
            hd = pair * HEADS_PER_LANE_BLOCK + a
            chosen = [None if j == n_past else sel_ref[0, hd, j, half:half + 1, :] > 0.5
                      for j in range(n_past + 1)]
            m = None
            for j in range(n_past + 1):
                s = _dot_nt(k_ref[0, j * L:(j + 1) * L, lanes], q_heads[a]) + bias_ref[hd, n_past - j]
                if n_past > 0:
                    s_buf[half, hd, j * L:(j + 1) * L, :] = s
                mj = jnp.max(s, axis=0, keepdims=True)
                if chosen[j] is not None:
                    mj = jnp.where(chosen[j], mj, NEG)
                m = mj if m is None else jnp.maximum(m, mj)
            l = None
            acc = None
            for j in range(n_past + 1):
                shift = m if chosen[j] is None else jnp.where(chosen[j], m, -NEG)
                p = jnp.exp2((s_buf[half, hd, j * L:(j + 1) * L, :] if n_past > 0 else s) - shift)
                lj = jnp.sum(p, axis=0, keepdims=True)
                vT = vT_ref[0, pair, j, a * HEAD_DIM:(a + 1) * HEAD_DIM, :]
                oj = jnp.dot(vT, p.astype(BF16), preferred_element_type=F32)
                l = lj if l is None else l + lj
                acc = oj if acc is None else acc + oj
            outs.append(acc * (1.0 / l))
        o_ref[0, half, :, lanes] = jnp.concatenate(outs, axis=0).T.astype(BF16)

    def step(c_static):
        for pair in range(MOBA_PAIRS_PER_STEP):
            for half in range(2):
                attend(pair, half, c_static + half * (nb // 2))

    for c_static in range(nb // 2):
        pl.when(c == c_static)(functools.partial(step, c_static))


def _moba(qm, km, vmT, sel, bias):
    B, S, _ = qm.shape
    L = MOBA_BLOCK
    nb = S // L
    pairs = MOBA_PAIRS_PER_STEP
    heads = pairs * HEADS_PER_LANE_BLOCK
    width = pairs * LANES
    half_s = S // 2
    out = pl.pallas_call(
        _moba_kernel,
        grid=(D_MOBA // width, B, nb // 2),
        in_specs=[pl.BlockSpec((1, 2, L, width), lambda p, b, c: (b, 0, c, p)),
                  pl.BlockSpec((1, S, width), lambda p, b, c: (b, 0, p)),
                  pl.BlockSpec((1, pairs, nb, LANES, L), lambda p, b, c: (b, p, 0, 0, 0)),
                  pl.BlockSpec((1, heads, nb, 2, L), lambda p, b, c: (b, p, 0, 0, c)),
                  pl.BlockSpec((heads, nb, L, L), lambda p, b, c: (p, 0, 0, 0))],
        out_specs=pl.BlockSpec((1, 2, L, width), lambda p, b, c: (b, 0, c, p)),
        out_shape=jax.ShapeDtypeStruct((B, 2, half_s, D_MOBA), BF16),
        scratch_shapes=[pltpu.VMEM((2, heads, S, L), F32)],
        compiler_params=_cparams(3),
        name="moba_attn",
    )(qm.reshape(B, 2, half_s, D_MOBA), km, vmT, sel, bias)
    return out.reshape(B, S, D_MOBA)


def _swa_kernel(sink_ref, q_ref, *refs):
    n = SWA_BLOCKS_PER_STEP
    k_refs, v_refs = refs[:2 * n], refs[2 * n:4 * n]
    bias_ref, o_ref = refs[4 * n], refs[4 * n + 1]
    c = pl.program_id(1)
    for g in range(n):
        _swa_block(sink_ref, q_ref[0, g], k_refs[2 * g][0], k_refs[2 * g + 1][0], v_refs[2 * g][0],
                   v_refs[2 * g + 1][0], bias_ref, o_ref.at[0, g], c > 0 if g == 0 else None)


def _swa_block(sink_ref, q, k_prev, k_cur, vT_prev, vT_cur, bias_ref, o_ref, has_prev):
    W = SWA_BLOCK
    group = N_HEADS_SWA // N_KV_SWA
    kband = jnp.concatenate([k_prev, k_cur], axis=0)
    vbandT = jnp.concatenate([vT_prev, vT_cur], axis=1)
    q_rows = []
    for h in range(N_HEADS_SWA):
        qp = q[:, (h % group) * LANES:(h % group + 1) * LANES]
        lane = lax.broadcasted_iota(jnp.int32, qp.shape, 1)
        in_half = (lane >= HEAD_DIM) if h // group else (lane < HEAD_DIM)
        q_rows.append(jnp.where(in_half, qp, jnp.zeros_like(qp)))
    q_all = jnp.concatenate(q_rows, axis=0)
    sT = _dot_nt(kband, q_all) + bias_ref[...]
    if has_prev is not None:
        key = lax.broadcasted_iota(jnp.int32, sT.shape, 0)
        sT = jnp.where((key >= W) | has_prev, sT, NEG)
    sink = sink_ref[...]
    m = jnp.maximum(jnp.max(sT, axis=0, keepdims=True), sink)
    pr = jnp.exp2(sT - m)
    inv_l = 1.0 / (jnp.sum(pr, axis=0, keepdims=True) + jnp.exp2(sink - m))
    pr = pr.astype(BF16)
    cols = group * W
    outs = [jnp.dot(vbandT[g * HEAD_DIM:(g + 1) * HEAD_DIM, :], pr[:, g * cols:(g + 1) * cols],
                    preferred_element_type=F32) * inv_l[:, g * cols:(g + 1) * cols]
            for g in range(N_KV_SWA)]
    heads = [outs[h // group][:, (h % group) * W:(h % group + 1) * W] for h in range(N_HEADS_SWA)]
    o_ref[...] = jnp.concatenate(heads, axis=0).T.astype(BF16)


def _swa(sink_row, qs, ks, vsT, bias):
    B, S, _ = qs.shape
    W = SWA_BLOCK
    nq = S // W
    n = SWA_BLOCKS_PER_STEP
    far = nq // n
    k_spec = lambda f: pl.BlockSpec((1, W, D_SWA_KV), lambda b, c: (b, f(c), 0))
    v_spec = lambda f: pl.BlockSpec((1, D_SWA_KV, W), lambda b, c: (b, 0, f(c)))
    blocks = []
    for g in range(n):
        blocks.append((lambda c, g=g: jnp.maximum(c + g * far - 1, 0)))
        blocks.append((lambda c, g=g: c + g * far))
    out = pl.pallas_call(
        _swa_kernel,
        grid=(B, far),
        in_specs=[pl.BlockSpec((1, N_HEADS_SWA * W), lambda b, c: (0, 0)),
                  pl.BlockSpec((1, n, W, D_SWA_Q), lambda b, c: (b, 0, c, 0))]
                 + [k_spec(f) for f in blocks] + [v_spec(f) for f in blocks]
                 + [pl.BlockSpec((2 * W, N_HEADS_SWA * W), lambda b, c: (0, 0))],
        out_specs=pl.BlockSpec((1, n, W, D_SWA_Q), lambda b, c: (b, 0, c, 0)),
        out_shape=jax.ShapeDtypeStruct((B, n, S // n, D_SWA_Q), BF16),
        compiler_params=_cparams(2),
        name="swa_attn",
    )(sink_row, qs.reshape(B, n, S // n, D_SWA_Q), *([ks] * (2 * n)), *([vsT] * (2 * n)), bias)
    return out.reshape(B, S, D_SWA_Q)


def _outproj_kernel(x_ref, om_ref, os_ref, wm_ref, ws_ref, b_ref, mod_ref, g_ref, wrT_ref, br_ref,
                    x1_ref, hrt_ref, idx_ref, w_ref, rank_ref, counts_ref, carry_ref):
    attn = (jnp.dot(om_ref[...], wm_ref[...], preferred_element_type=F32)
            + jnp.dot(os_ref[...], ws_ref[...], preferred_element_type=F32) + b_ref[...])
    x1 = x_ref[...] + mod_ref[0, 2:3, :] * attn
    x1_ref[...] = x1
    h = _rmsnorm_rows(x1, g_ref[...]) * (1.0 + mod_ref[0, 4:5, :]) + mod_ref[0, 3:4, :]
    _store_row_tiles(hrt_ref, h)
    _route(h, wrT_ref, br_ref, idx_ref, w_ref, rank_ref, counts_ref, carry_ref)


def _store_row_tiles(ref, val, first_row=0):
    rows = val.shape[0]
    base = first_row * ROW_CHUNKS
    for c in range(ROW_CHUNKS):
        ref[pl.ds(base + c, rows, stride=ROW_CHUNKS), :] = val[:, c * LANES:(c + 1) * LANES]


def _load_row_tiles(ref, rows, lead=(), first_row=0):
    idx = tuple(lead)
    base = first_row * ROW_CHUNKS
    return jnp.concatenate([ref[idx + (pl.ds(base + c, rows, stride=ROW_CHUNKS), slice(None))]
                            for c in range(ROW_CHUNKS)], axis=1)


def _outproj(x2d, om, os_, w_out_m, w_out_s, b_out, mod6, g_ffn, w_routerT, b_router_col, seq_len):
    N, D = x2d.shape
    tm = OUTPROJ_TOKENS
    per_seq = seq_len // tm
    row = lambda t: (t, 0)
    col = lambda t: (0, t)
    const = lambda t: (0, 0)
    return pl.pallas_call(
        _outproj_kernel,
        grid=(N // tm,),
        in_specs=[pl.BlockSpec((tm, D), row),
                  pl.BlockSpec((tm, D_MOBA), row),
                  pl.BlockSpec((tm, D_SWA_Q), row),
                  pl.BlockSpec((D_MOBA, D), const),
                  pl.BlockSpec((D_SWA_Q, D), const),
                  pl.BlockSpec((1, D), const),
                  pl.BlockSpec((1, 6, D), lambda t: (t // per_seq, 0, 0)),
                  pl.BlockSpec((1, D), const),
                  pl.BlockSpec((N_EXPERTS, D), const),
                  pl.BlockSpec((N_EXPERTS, 1), const)],
        out_specs=(pl.BlockSpec((tm, D), row),
                   pl.BlockSpec((tm * ROW_CHUNKS, LANES), row),
                   pl.BlockSpec((TOP_K, tm), col),
                   pl.BlockSpec((TOP_K, tm), col),
                   pl.BlockSpec((TOP_K, tm), col),
                   pl.BlockSpec((N_EXPERTS, LANES), const)),
        out_shape=(jax.ShapeDtypeStruct((N, D), F32),
                   jax.ShapeDtypeStruct((N * ROW_CHUNKS, LANES), F32),
                   jax.ShapeDtypeStruct((TOP_K, N), jnp.int32),
                   jax.ShapeDtypeStruct((TOP_K, N), F32),
                   jax.ShapeDtypeStruct((TOP_K, N), jnp.int32),
                   jax.ShapeDtypeStruct((N_EXPERTS, LANES), jnp.int32)),
        scratch_shapes=[pltpu.VMEM((N_EXPERTS, LANES), F32)],
        compiler_params=_cparams(1),
        name="outproj_route",
    )(x2d, om, os_, w_out_m, w_out_s, b_out, mod6, g_ffn, w_routerT, b_router_col)


def _route(h, wT_ref, b_ref, idx_ref, w_ref, rank_ref, counts_ref, carry_ref):
    @pl.when(pl.program_id(0) == 0)
    def _():
        carry_ref[...] = jnp.zeros_like(carry_ref)

    w = wT_ref[...]
    w_hi = w.astype(BF16)
    w_lo = (w - w_hi.astype(F32)).astype(BF16)
    h_hi = h.astype(BF16)
    h_lo = (h - h_hi.astype(F32)).astype(BF16)
    by_h_hi = _dot_nt(jnp.concatenate([w_hi, w_lo], axis=0), h_hi)
    logits = (by_h_hi[:N_EXPERTS] + by_h_hi[N_EXPERTS:] + _dot_nt(w_hi, h_lo)) + b_ref[...]
    tm = logits.shape[1]
    eidx = lax.broadcasted_iota(jnp.int32, logits.shape, 0)
    vals, idxs = [], []
    cur = logits
    for _ in range(TOP_K):
        m = jnp.max(cur, axis=0, keepdims=True)
        am = jnp.min(jnp.where(cur == m, eidx, N_EXPERTS), axis=0, keepdims=True)
        vals.append(m)
        idxs.append(am)
        cur = jnp.where(eidx == am, -jnp.inf, cur)
    exps = [jnp.exp(v - vals[0]) for v in vals]
    inv = 1.0 / functools.reduce(lambda a, b: a + b, exps)
    idx_ref[...] = jnp.concatenate(idxs, axis=0)
    w_ref[...] = jnp.concatenate([e * inv for e in exps], axis=0)

    member = functools.reduce(lambda a, b: a | b, [eidx == am for am in idxs])
    earlier = (lax.broadcasted_iota(jnp.int32, (tm, tm), 0)
               < lax.broadcasted_iota(jnp.int32, (tm, tm), 1)).astype(BF16)
    before = jnp.dot(member.astype(BF16), earlier, preferred_element_type=F32) + carry_ref[:, 0:1]
    rank_ref[...] = jnp.concatenate(
        [jnp.sum(jnp.where(eidx == am, before, 0.0), axis=0, keepdims=True) for am in idxs],
        axis=0).astype(jnp.int32)
    total = carry_ref[...] + jnp.sum(member.astype(F32), axis=1, keepdims=True)
    carry_ref[...] = total
    counts_ref[...] = total.astype(jnp.int32)


def _expert_mlp(h_bf, wgu, bgu, wdn, bdn):
    gu = jnp.dot(h_bf, wgu, preferred_element_type=F32) + bgu
    g = jnp.minimum(gu[:, :D_FF], SWIGLU_LIMIT)
    u = jnp.clip(gu[:, D_FF:], -SWIGLU_LIMIT, SWIGLU_LIMIT)
    act = (u + 1.0) * (g * _sigmoid(SWIGLU_ALPHA * g))
    return jnp.dot(act.astype(BF16), wdn, preferred_element_type=F32) + bdn


def _tile_plan(counts, n_tiles_max):
    tiles_per = (counts + MOE_TILE - 1) // MOE_TILE
    tile_end = jnp.cumsum(tiles_per)
    tile_start = tile_end - tiles_per
    t_ids = jnp.arange(n_tiles_max, dtype=jnp.int32)
    texp = jnp.minimum(jnp.sum((t_ids[:, None] >= tile_end[None, :]).astype(jnp.int32), axis=1),
                       N_EXPERTS - 1)
    onehot = (texp[:, None] == jnp.arange(N_EXPERTS, dtype=jnp.int32)[None, :]).astype(jnp.int32)
    local = t_ids - onehot @ tile_start
    nvalid = jnp.clip(onehot @ counts - local * MOE_TILE, 0, MOE_TILE)
    n_active = tile_end[-1]
    src_tile = jnp.minimum(t_ids, n_active - 1)
    per_tile = MOE_TILE // MOE_MM_ROWS
    n_last = counts - (tiles_per - 1) * MOE_TILE
    edge = (tile_end - 1) * per_tile + (n_last - 1) // MOE_MM_ROWS
    boundary = jnp.where(counts > 0, edge, -1)
    fill = jnp.arange(1, per_tile, dtype=jnp.int32)[None, :]
    pad_only = jnp.where((counts[:, None] > 0) & (edge[:, None] % per_tile + fill < per_tile),
                         edge[:, None] + fill, -1).reshape(-1)
    tail = n_active * per_tile + jnp.arange(N_EXPERTS * per_tile, dtype=jnp.int32)
    zero_tiles = jnp.concatenate([boundary, pad_only, jnp.where(tail < n_tiles_max * per_tile, tail, -1)])
    ids = jnp.arange(N_EXPERTS, dtype=jnp.int32)
    later = jnp.where((ids[None, :] > ids[:, None]) & (counts[None, :] > 0), ids[None, :], N_EXPERTS)
    next_expert = jnp.min(later, axis=1)
    next_expert = jnp.where(next_expert < N_EXPERTS, next_expert, -1)
    return (texp.astype(jnp.int32), nvalid.astype(jnp.int32), src_tile.astype(jnp.int32),
            (tile_start * MOE_TILE).astype(jnp.int32), zero_tiles.astype(jnp.int32),
            next_expert.astype(jnp.int32))


def _row_copy(src, src_first, dst, dst_first, sem):
    return pltpu.make_async_copy(src.at[pl.ds(pl.multiple_of(src_first, ROW_CHUNKS), ROW_CHUNKS)],
                                 dst.at[pl.ds(pl.multiple_of(dst_first, ROW_CHUNKS), ROW_CHUNKS)],
                                 sem)


DISPATCH_SLOTS = 3


def _dispatch_kernel(zero_tiles_ref, pos_ref, h_hbm, x_hbm, zbuf, hbuf, zsem, load_sem, row_sem):
    t = pl.program_id(0)
    n_steps = pl.num_programs(0)
    group_rows = MOE_MM_ROWS * ROW_CHUNKS
    block_rows = DISPATCH_TOKENS * ROW_CHUNKS

    def block_load(step, slot):
        start = pl.multiple_of(step * block_rows, block_rows)
        return pltpu.make_async_copy(h_hbm.at[pl.ds(start, block_rows)], hbuf.at[slot], load_sem.at[slot])

    def rows_done(slot):
        for k in range(TOP_K):
            pltpu.make_async_copy(hbuf.at[slot], x_hbm.at[pl.ds(0, block_rows)], row_sem.at[slot]).wait()

    n_shared = N_EXPERTS
    n_zero = zero_tiles_ref.shape[0]

    def zero_copy(i):
        start = pl.multiple_of(zero_tiles_ref[i] * group_rows, group_rows)
        return pltpu.make_async_copy(zbuf, x_hbm.at[pl.ds(start, group_rows)],
                                     zsem.at[0 if i < n_shared else 1])

    def for_zero_groups(lo, hi, fn):
        for i in range(lo, hi):
            @pl.when(zero_tiles_ref[i] >= 0)
            def _():
                fn(zero_copy(i))

    @pl.when(t == 0)
    def _():
        zbuf[...] = jnp.zeros_like(zbuf)
        for_zero_groups(0, n_shared, lambda cp: cp.start())
        block_load(0, 0).start()
        if DISPATCH_SLOTS > 2:
            @pl.when(n_steps > 1)
            def _():
                block_load(1, 1).start()
        for_zero_groups(0, n_shared, lambda cp: cp.wait())
        for_zero_groups(n_shared, n_zero, lambda cp: cp.start())

    slot = t % DISPATCH_SLOTS
    block_load(t, slot).wait()

    def start_rows(r, carry):
        for k in range(TOP_K):
            _row_copy(hbuf.at[slot], r * ROW_CHUNKS, x_hbm, pos_ref[0, 0, k * DISPATCH_TOKENS + r],
                      row_sem.at[slot]).start(priority=k % 2)
        return carry
    lax.fori_loop(0, DISPATCH_TOKENS, start_rows, 0, unroll=8)

    refill = (t + 2) % DISPATCH_SLOTS

    @pl.when(t > 0)
    def _():
        rows_done(refill)

    @pl.when(t + 2 < n_steps)
    def _():
        block_load(t + 2, refill).start()

    @pl.when(t == n_steps - 1)
    def _():
        rows_done(slot)
        for_zero_groups(n_shared, n_zero, lambda cp: cp.wait())


def _dispatch(zero_tiles, pos_tiles, h_rt, n_rows):
    n_steps = pos_tiles.shape[0]
    grid_spec = pltpu.PrefetchScalarGridSpec(
        num_scalar_prefetch=1,
        grid=(n_steps,),
        in_specs=[pl.BlockSpec((1, 1, TOP_K * DISPATCH_TOKENS), lambda t, zt: (t, 0, 0),
                               memory_space=pltpu.SMEM),
                  pl.BlockSpec(memory_space=pl.ANY)],
        out_specs=pl.BlockSpec(memory_space=pl.ANY),
        scratch_shapes=[pltpu.VMEM((MOE_MM_ROWS * ROW_CHUNKS, LANES), F32),
                        pltpu.VMEM((DISPATCH_SLOTS, DISPATCH_TOKENS * ROW_CHUNKS, LANES), F32),
                        pltpu.SemaphoreType.DMA((2,)),
                        pltpu.SemaphoreType.DMA((DISPATCH_SLOTS,)),
                        pltpu.SemaphoreType.DMA((DISPATCH_SLOTS,))],
    )
    return pl.pallas_call(
        _dispatch_kernel,
        grid_spec=grid_spec,
        out_shape=jax.ShapeDtypeStruct((n_rows * ROW_CHUNKS, LANES), F32),
        compiler_params=_cparams(1),
        name="moe_dispatch",
    )(zero_tiles, pos_tiles, h_rt)


def _experts_kernel(texp_ref, nvalid_ref, src_ref, next_ref, x_ref, wgu_hbm, bgu_ref, wdn_hbm, bdn_ref,
                    y_ref, wgu_f32, wdn_f32, wgu_bf, wdn_bf, wsem):
    t = pl.program_id(0)
    nv = nvalid_ref[t]
    e = texp_ref[t]
    prev_e = texp_ref[jnp.maximum(t - 1, 0)]

    def weight_loads(expert):
        return (pltpu.make_async_copy(wgu_hbm.at[expert], wgu_f32, wsem.at[0]),
                pltpu.make_async_copy(wdn_hbm.at[expert], wdn_f32, wsem.at[1]))

    @pl.when((nv > 0) & ((t == 0) | (e != prev_e)))
    def _():
        @pl.when(t == 0)
        def _():
            for cp in weight_loads(e):
                cp.start()
        for cp in weight_loads(e):
            cp.wait()
        rows = CAST_ROWS
        def cast_gu(i, carry):
            r0 = pl.multiple_of(i * rows, rows)
            wgu_bf[pl.ds(r0, rows), :] = wgu_f32[pl.ds(r0, rows), :].astype(BF16)
            return carry
        lax.fori_loop(0, D_MODEL // rows, cast_gu, 0)
        def cast_dn(i, carry):
            r0 = pl.multiple_of(i * rows, rows)
            wdn_bf[pl.ds(r0, rows), :] = wdn_f32[pl.ds(r0, rows), :].astype(BF16)
            return carry
        lax.fori_loop(0, D_FF // rows, cast_dn, 0)
        nxt = next_ref[e]

        @pl.when(nxt >= 0)
        def _():
            for cp in weight_loads(nxt):
                cp.start()

    def mlp_rows(first_row):
        x = _load_row_tiles(x_ref, MOE_MM_ROWS, first_row=first_row).astype(BF16)
        _store_row_tiles(y_ref, _expert_mlp(x, wgu_bf[...], bgu_ref[0], wdn_bf[...], bdn_ref[0]), first_row)

    n_groups = MOE_TILE // MOE_MM_ROWS
    for live in range(n_groups + 1):
        lo, hi = (live - 1) * MOE_MM_ROWS, live * MOE_MM_ROWS

        @pl.when((nv > lo) & (nv <= hi) if live else (nv == 0))
        def _():
            for g in range(live):
                mlp_rows(g * MOE_MM_ROWS)
            if live < n_groups:
                y_ref[pl.ds(live * MOE_MM_ROWS * ROW_CHUNKS, (n_groups - live) * MOE_MM_ROWS * ROW_CHUNKS), :] = (
                    jnp.zeros(((n_groups - live) * MOE_MM_ROWS * ROW_CHUNKS, LANES), F32))


def _experts(texp, nvalid, src_tile, next_expert, x_sorted, w_gate_up, b_gate_up, w_down, b_down):
    n_tiles = texp.shape[0]
    D = D_MODEL
    tile_rows = MOE_TILE * ROW_CHUNKS
    grid_spec = pltpu.PrefetchScalarGridSpec(
        num_scalar_prefetch=4,
        grid=(n_tiles,),
        in_specs=[pl.BlockSpec((tile_rows, LANES), lambda t, te, nv, st, nx: (st[t], 0)),
                  pl.BlockSpec(memory_space=pl.ANY),
                  pl.BlockSpec((1, 1, 2 * D_FF), lambda t, te, nv, st, nx: (te[t], 0, 0)),
                  pl.BlockSpec(memory_space=pl.ANY),
                  pl.BlockSpec((1, 1, D), lambda t, te, nv, st, nx: (te[t], 0, 0))],
        out_specs=pl.BlockSpec((tile_rows, LANES), lambda t, te, nv, st, nx: (t, 0)),
        scratch_shapes=[pltpu.VMEM((D, 2 * D_FF), F32),
                        pltpu.VMEM((D_FF, D), F32),
                        pltpu.VMEM((D, 2 * D_FF), BF16),
                        pltpu.VMEM((D_FF, D), BF16),
                        pltpu.SemaphoreType.DMA((2,))],
    )
    return pl.pallas_call(
        _experts_kernel,
        grid_spec=grid_spec,
        out_shape=jax.ShapeDtypeStruct(x_sorted.shape, F32),
        compiler_params=_cparams(1),
        name="moe_experts",
    )(texp, nvalid, src_tile, next_expert, x_sorted, w_gate_up, b_gate_up.reshape(N_EXPERTS, 1, 2 * D_FF),
      w_down, b_down.reshape(N_EXPERTS, 1, D))


def _combine_kernel(pos_ref, pos_next_ref, y_hbm, w_ref, x1_ref, mod_ref, g_ref, o_ref, ybuf, sem):
    t = pl.program_id(0)
    n_steps = pl.num_programs(0)
    slot = t % 2

    def start_gather(idx_ref, s):
        def body(r, carry):
            for k in range(TOP_K):
                i = k * COMBINE_TOKENS + r
                _row_copy(y_hbm, idx_ref[0, 0, i], ybuf.at[s], i * ROW_CHUNKS, sem.at[s]).start(priority=k % 2)
            return carry
        lax.fori_loop(0, COMBINE_TOKENS, body, 0, unroll=8)

    @pl.when(t == 0)
    def _():
        start_gather(pos_ref, 0)

    @pl.when(t + 1 < n_steps)
    def _():
        start_gather(pos_next_ref, 1 - slot)

    slot_rows = TOP_K * COMBINE_TOKENS * ROW_CHUNKS
    pltpu.make_async_copy(y_hbm.at[pl.ds(0, slot_rows)], ybuf.at[slot], sem.at[slot]).wait()

    w = w_ref[...]
    acc = w[:, 0:1] * _load_row_tiles(ybuf, COMBINE_TOKENS, (slot,))
    for k in range(1, TOP_K):
        acc = acc + w[:, k:k + 1] * _load_row_tiles(ybuf, COMBINE_TOKENS, (slot,), k * COMBINE_TOKENS)
    x2 = x1_ref[...] + mod_ref[0, 5:6, :] * acc
    o_ref[...] = _rmsnorm_rows(x2, g_ref[...])


def _combine(pos_tiles, y_sorted, top_w, x1, mod6, g_final, seq_len):
    N, D = x1.shape
    tm = COMBINE_TOKENS
    n_steps = N // tm
    per_seq = seq_len // tm
    return pl.pallas_call(
        _combine_kernel,
        grid=(n_steps,),
        in_specs=[pl.BlockSpec((1, 1, TOP_K * tm), lambda t: (t, 0, 0), memory_space=pltpu.SMEM),
                  pl.BlockSpec((1, 1, TOP_K * tm), lambda t: (jnp.minimum(t + 1, n_steps - 1), 0, 0),
                               memory_space=pltpu.SMEM),
                  pl.BlockSpec(memory_space=pl.ANY),
                  pl.BlockSpec((tm, TOP_K), lambda t: (t, 0)),
                  pl.BlockSpec((tm, D), lambda t: (t, 0)),
                  pl.BlockSpec((1, 6, D), lambda t: (t // per_seq, 0, 0)),
                  pl.BlockSpec((1, D), lambda t: (0, 0))],
        out_specs=pl.BlockSpec((tm, D), lambda t: (t, 0)),
        out_shape=jax.ShapeDtypeStruct((N, D), F32),
        scratch_shapes=[pltpu.VMEM((2, TOP_K * tm * ROW_CHUNKS, LANES), F32),
                        pltpu.SemaphoreType.DMA((2,))],
        compiler_params=_cparams(1),
        name="moe_combine",
    )(pos_tiles, pos_tiles, y_sorted, top_w, x1, mod6, g_final)


def kernel(x, c, g_attn, w_ada, b_ada, w_in, b_in, w_out, b_out, rel_table, sinks, g_ffn, w_router,
           b_router, w_gate_up, b_gate_up, w_down, b_down, g_final):
    B, S, D = x.shape
    assert w_ada.shape[0] == 1, "the final norm is fused into the last layer; one layer supported"
    l = 0
    nb = S // MOBA_BLOCK
    i2 = 3 * D_MOBA
    i3 = i2 + D_SWA_Q
    group = N_HEADS_SWA // N_KV_SWA
    q_scale = ATTN_SCALE * LOG2E

    def prep_in(w):
        lead = w.shape[:-1]
        q_swa = (w[..., i2:i3] * q_scale).reshape(lead + (N_KV_SWA, group, HEAD_DIM))
        q_swa = jnp.swapaxes(q_swa, -3, -2).reshape(lead + (D_SWA_Q,))
        return jnp.concatenate([w[..., :D_MOBA] * q_scale, w[..., D_MOBA:i2], q_swa, w[..., i3:]], axis=-1)

    moba_bias = _moba_bias(rel_table, nb)
    swa_bias = _swa_bias(rel_table)

    mod6 = _ada(c, w_ada[l], b_ada[l]).reshape(B, 6, D)
    w_in_l = prep_in(w_in[l]).astype(BF16)
    b_in_l = prep_in(b_in[l]).reshape(1, D_IN)
    qm, km, vmT, kmean, qs, ks, vsT = _inproj(x, mod6, g_attn[l].reshape(1, D), w_in_l, b_in_l)
    sel = _gate(qm, kmean)
    o_m = _moba(qm, km, vmT, sel, moba_bias)
    sink_row = jnp.repeat(sinks[l] * LOG2E, SWA_BLOCK).reshape(1, N_HEADS_SWA * SWA_BLOCK)
    o_s = _swa(sink_row, qs, ks, vsT, swa_bias)
    w_out_m = w_out[l, :D_MOBA].astype(BF16)
    w_out_s = w_out[l, D_MOBA:].astype(BF16)
    N = B * S
    x1, h_rt, top_idx, top_w, rank, counts = _outproj(
        x.reshape(N, D), o_m.reshape(N, D_MOBA), o_s.reshape(N, D_SWA_Q), w_out_m, w_out_s,
        b_out[l].reshape(1, D), mod6, g_ffn[l].reshape(1, D), w_router[l].T,
        b_router[l].reshape(N_EXPERTS, 1), S)
    counts = counts[:, 0]
    n_tiles_max = N * TOP_K // MOE_TILE + N_EXPERTS
    texp, nvalid, src_tile, group_start, zero_tiles, next_expert = _tile_plan(counts, n_tiles_max)
    experts = jnp.arange(N_EXPERTS, dtype=jnp.int32)
    pos = rank + jnp.sum(jnp.where(top_idx[..., None] == experts, group_start, 0), axis=-1)
    assert DISPATCH_TOKENS == COMBINE_TOKENS
    pos_tiles = (pos * ROW_CHUNKS).reshape(TOP_K, N // DISPATCH_TOKENS, DISPATCH_TOKENS).transpose(
        1, 0, 2).reshape(N // DISPATCH_TOKENS, 1, TOP_K * DISPATCH_TOKENS)
    x_sorted = _dispatch(zero_tiles, pos_tiles, h_rt, n_tiles_max * MOE_TILE)
    y_sorted = _experts(texp, nvalid, src_tile, next_expert, x_sorted, w_gate_up[l], b_gate_up[l], w_down[l],
                        b_down[l])
    y = _combine(pos_tiles, y_sorted, top_w.T, x1, mod6, g_final.reshape(1, D), S)
    return y.reshape(B, S, D)
```

```python
import functools
import math

import jax
import jax.numpy as jnp
from jax import lax
from jax.experimental import pallas as pl
from jax.experimental.pallas import tpu as pltpu

F32 = jnp.float32
BF16 = jnp.bfloat16

D_MODEL = 1024
HEAD_DIM = 64
N_HEADS_MOBA = 8
N_HEADS_SWA = 8
N_KV_SWA = 2
D_MOBA = N_HEADS_MOBA * HEAD_DIM
D_SWA_Q = N_HEADS_SWA * HEAD_DIM
D_SWA_KV = N_KV_SWA * HEAD_DIM
D_IN = 3 * D_MOBA + D_SWA_Q + 2 * D_SWA_KV
MOBA_BLOCK = 256
MOBA_TOPK = 3
SWA_WINDOW = 128
SWA_BLOCK = 128
REL_BUCKETS = 32
REL_MAX_DIST = 1024
REL_MAX_EXACT = REL_BUCKETS // 2
N_EXPERTS = 32
TOP_K = 4
D_FF = 1024
SWIGLU_LIMIT = 7.0
SWIGLU_ALPHA = 1.702
EPS = 1e-5
NEG = -1e30
ATTN_SCALE = HEAD_DIM ** -0.5
LOG2E = math.log2(math.e)

LANES = 128
SUBLANES = 8
MOE_TILE = 512
MOE_MM_ROWS = 256
ROW_CHUNKS = D_MODEL // LANES
DISPATCH_TOKENS = 512
COMBINE_TOKENS = 512
MOBA_PAIRS_PER_STEP = 2
SWA_BLOCKS_PER_STEP = 8
HEADS_PER_LANE_BLOCK = LANES // HEAD_DIM
ADA_COLS = 1536
OUTPROJ_TOKENS = 1024
CAST_ROWS = 256
VMEM_LIMIT_BYTES = 56 * 1024 * 1024

_REL_THRESHOLDS = tuple(
    math.ceil(REL_MAX_EXACT * (REL_MAX_DIST / REL_MAX_EXACT) ** (k / (REL_BUCKETS - REL_MAX_EXACT)) - 1e-9)
    for k in range(1, REL_BUCKETS - REL_MAX_EXACT))


def _cparams(n_axes):
    return pltpu.CompilerParams(dimension_semantics=("arbitrary",) * n_axes,
                                vmem_limit_bytes=VMEM_LIMIT_BYTES)


def _sigmoid(z):
    return 1.0 / (1.0 + jnp.exp(-z))


def _rmsnorm_rows(xf, g):
    ms = jnp.mean(xf * xf, axis=-1, keepdims=True)
    return xf * lax.rsqrt(ms + EPS) * g


def _dot_nt(a, b, **kw):
    return lax.dot_general(a, b, (((1,), (1,)), ((), ())), preferred_element_type=F32, **kw)


def _ada_kernel(c_ref, w_ref, b_ref, o_ref):
    c = c_ref[...]
    sc = c * _sigmoid(c)
    o_ref[...] = jnp.dot(sc, w_ref[...], preferred_element_type=F32,
                         precision=lax.Precision.HIGHEST) + b_ref[...]


def _ada(c, w_ada, b_ada):
    B = c.shape[0]
    n_out = w_ada.shape[1]
    bn = ADA_COLS
    return pl.pallas_call(
        _ada_kernel,
        grid=(n_out // bn,),
        in_specs=[pl.BlockSpec((B, D_MODEL), lambda n: (0, 0)),
                  pl.BlockSpec((D_MODEL, bn), lambda n: (0, n)),
                  pl.BlockSpec((1, bn), lambda n: (0, n))],
        out_specs=pl.BlockSpec((B, bn), lambda n: (0, n)),
        out_shape=jax.ShapeDtypeStruct((B, n_out), F32),
        compiler_params=_cparams(1),
        name="ada",
    )(c, w_ada, b_ada.reshape(1, n_out))


def _inproj_kernel(x_ref, mod_ref, g_ref, w_ref, b_ref,
                   qm_ref, km_ref, vmT_ref, kmean_ref, qs_ref, ks_ref, vsT_ref):
    xf = x_ref[0]
    shift = mod_ref[0, 0:1, :]
    scale = mod_ref[0, 1:2, :]
    h = _rmsnorm_rows(xf, g_ref[...]) * (1.0 + scale) + shift
    proj = jnp.dot(h.astype(BF16), w_ref[...], preferred_element_type=F32) + b_ref[...]
    i0, i1, i2 = D_MOBA, 2 * D_MOBA, 3 * D_MOBA
    i3 = i2 + D_SWA_Q
    i4 = i3 + D_SWA_KV
    qm_ref[0] = proj[:, :i0].astype(BF16)
    km = proj[:, i0:i1]
    km_ref[0] = km.astype(BF16)
    L = MOBA_BLOCK
    for blk in range(proj.shape[0] // L):
        rows = slice(blk * L, (blk + 1) * L)
        kmean_ref[0, blk] = jnp.broadcast_to(jnp.mean(km[rows], axis=0, keepdims=True), (SUBLANES, D_MOBA))
        vmT = proj[rows, i1:i2].T.astype(BF16)
        vmT_ref[0, :, blk] = vmT.reshape(D_MOBA // LANES, LANES, L)
    qs_ref[0] = proj[:, i2:i3].astype(BF16)
    ks_ref[0] = proj[:, i3:i4].astype(BF16)
    vsT_ref[0] = proj[:, i4:].T.astype(BF16)


def _inproj(x, mod6, g_attn, w_in_bf, b_in):
    B, S, D = x.shape
    L = MOBA_BLOCK
    per_step = 4
    tm = per_step * L
    nb = S // L
    n_pairs = D_MOBA // LANES
    out_shape = (
        jax.ShapeDtypeStruct((B, S, D_MOBA), BF16),
        jax.ShapeDtypeStruct((B, S, D_MOBA), BF16),
        jax.ShapeDtypeStruct((B, n_pairs, nb, LANES, L), BF16),
        jax.ShapeDtypeStruct((B, nb, SUBLANES, D_MOBA), F32),
        jax.ShapeDtypeStruct((B, S, D_SWA_Q), BF16),
        jax.ShapeDtypeStruct((B, S, D_SWA_KV), BF16),
        jax.ShapeDtypeStruct((B, D_SWA_KV, S), BF16),
    )
    return pl.pallas_call(
        _inproj_kernel,
        grid=(B, S // tm),
        in_specs=[pl.BlockSpec((1, tm, D), lambda b, i: (b, i, 0)),
                  pl.BlockSpec((1, 6, D), lambda b, i: (b, 0, 0)),
                  pl.BlockSpec((1, D), lambda b, i: (0, 0)),
                  pl.BlockSpec((D, D_IN), lambda b, i: (0, 0)),
                  pl.BlockSpec((1, D_IN), lambda b, i: (0, 0))],
        out_specs=(pl.BlockSpec((1, tm, D_MOBA), lambda b, i: (b, i, 0)),
                   pl.BlockSpec((1, tm, D_MOBA), lambda b, i: (b, i, 0)),
                   pl.BlockSpec((1, n_pairs, per_step, LANES, L), lambda b, i: (b, 0, i, 0, 0)),
                   pl.BlockSpec((1, per_step, SUBLANES, D_MOBA), lambda b, i: (b, i, 0, 0)),
                   pl.BlockSpec((1, tm, D_SWA_Q), lambda b, i: (b, i, 0)),
                   pl.BlockSpec((1, tm, D_SWA_KV), lambda b, i: (b, i, 0)),
                   pl.BlockSpec((1, D_SWA_KV, tm), lambda b, i: (b, 0, i))),
        out_shape=out_shape,
        compiler_params=_cparams(2),
        name="inproj",
    )(x, mod6, g_attn, w_in_bf, b_in)


def _rel_bucket_static(n):
    n = max(n, 0)
    return n if n < REL_MAX_EXACT else REL_MAX_EXACT + sum(n >= t for t in _REL_THRESHOLDS)


def _rel_bias_values(dist, table_ref, h, d_min, d_max):
    lo, hi = _rel_bucket_static(d_min), _rel_bucket_static(d_max)
    n = jnp.maximum(dist, 0)
    val = jnp.full(n.shape, table_ref[lo, h], F32)
    for bkt in range(lo + 1, hi + 1):
        start = bkt if bkt <= REL_MAX_EXACT else _REL_THRESHOLDS[bkt - REL_MAX_EXACT - 1]
        val = jnp.where(n >= start, table_ref[bkt, h], val)
    return val


def _moba_bias_kernel(table_ref, o_ref):
    h = pl.program_id(0)
    L = MOBA_BLOCK
    key = lax.broadcasted_iota(jnp.int32, (L, L), 0)
    qry = lax.broadcasted_iota(jnp.int32, (L, L), 1)
    for dlt in range(o_ref.shape[1]):
        dist = dlt * L + qry - key
        val = _rel_bias_values(dist, table_ref, h, dlt * L - (L - 1), dlt * L + (L - 1))
        val = val * LOG2E
        o_ref[0, dlt] = jnp.where(dist >= 0, val, NEG) if dlt == 0 else val


def _moba_bias(rel_table, nb):
    L = MOBA_BLOCK
    return pl.pallas_call(
        _moba_bias_kernel,
        grid=(N_HEADS_MOBA,),
        in_specs=[pl.BlockSpec(memory_space=pltpu.SMEM)],
        out_specs=pl.BlockSpec((1, nb, L, L), lambda h: (h, 0, 0, 0)),
        out_shape=jax.ShapeDtypeStruct((N_HEADS_MOBA, nb, L, L), F32),
        compiler_params=_cparams(1),
        name="moba_bias",
    )(rel_table)


def _swa_bias_kernel(table_ref, o_ref):
    h = N_HEADS_MOBA + pl.program_id(0)
    W = SWA_BLOCK
    key = lax.broadcasted_iota(jnp.int32, (2 * W, W), 0)
    qry = lax.broadcasted_iota(jnp.int32, (2 * W, W), 1)
    dist = qry + W - key
    val = _rel_bias_values(dist, table_ref, h, 0, SWA_WINDOW - 1)
    o_ref[...] = jnp.where((dist >= 0) & (dist < SWA_WINDOW), val * LOG2E, NEG)


def _swa_bias(rel_table):
    W = SWA_BLOCK
    return pl.pallas_call(
        _swa_bias_kernel,
        grid=(N_HEADS_SWA,),
        in_specs=[pl.BlockSpec(memory_space=pltpu.SMEM)],
        out_specs=pl.BlockSpec((2 * W, W), lambda h: (0, h)),
        out_shape=jax.ShapeDtypeStruct((2 * W, N_HEADS_SWA * W), F32),
        compiler_params=_cparams(1),
        name="swa_bias",
    )(rel_table)


def _gate_kernel(q_ref, kmean_ref, sel_ref):
    S = q_ref.shape[1]
    nb = kmean_ref.shape[1]
    H = N_HEADS_MOBA
    q = q_ref[0]
    km = kmean_ref[0, :, 0, :]
    km_t = jnp.concatenate([km] * H, axis=0)
    row_h = lax.broadcasted_iota(jnp.int32, (H * nb, D_MOBA), 0) // nb
    col_h = lax.broadcasted_iota(jnp.int32, (H * nb, D_MOBA), 1) // HEAD_DIM
    km_t = jnp.where(row_h == col_h, km_t, 0.0)
    km_hi = km_t.astype(BF16)
    km_lo = (km_t - km_hi.astype(F32)).astype(BF16)
    g_all = _dot_nt(km_hi, q) + _dot_nt(km_lo, q)
    qblk = lax.broadcasted_iota(jnp.int32, (nb, S), 1) // MOBA_BLOCK
    jidx = lax.broadcasted_iota(jnp.int32, (nb, S), 0)
    past = jidx < qblk
    for h in range(H):
        g = g_all[h * nb:(h + 1) * nb, :]
        cnt = jnp.zeros((nb, S), jnp.int32)
        for jp in range(nb):
            gj = g[jp:jp + 1, :]
            ahead = (gj > g) | ((gj == g) & (jp < jidx))
            cnt = cnt + ((jp < qblk) & ahead).astype(jnp.int32)
        sel = (past & (cnt < MOBA_TOPK)).astype(F32)
        for j in range(nb):
            for half in range(2):
                sel_ref[0, h, j, half:half + 1, :] = sel[j:j + 1, half * (S // 2):(half + 1) * (S // 2)]


def _gate(qm, kmean):
    B, S, _ = qm.shape
    nb = kmean.shape[1]
    return pl.pallas_call(
        _gate_kernel,
        grid=(B,),
        in_specs=[pl.BlockSpec((1, S, D_MOBA), lambda b: (b, 0, 0)),
                  pl.BlockSpec((1, nb, SUBLANES, D_MOBA), lambda b: (b, 0, 0, 0))],
        out_specs=pl.BlockSpec((1, N_HEADS_MOBA, nb, 2, S // 2), lambda b: (b, 0, 0, 0, 0)),
        out_shape=jax.ShapeDtypeStruct((B, N_HEADS_MOBA, nb, 2, S // 2), F32),
        compiler_params=_cparams(1),
        name="moba_gate",
    )(qm, kmean)


def _moba_kernel(q_ref, k_ref, vT_ref, sel_ref, bias_ref, o_ref, s_buf):
    c = pl.program_id(2)
    L = MOBA_BLOCK
    nb = k_ref.shape[1] // L

    def attend(pair, half, n_past):
        lanes = slice(pair * LANES, (pair + 1) * LANES)
        q = q_ref[0, half, :, lanes]
        lane = lax.broadcasted_iota(jnp.int32, q.shape, 1)
        zero = jnp.zeros_like(q)
        q_heads = [jnp.where(lane < HEAD_DIM, q, zero), jnp.where(lane >= HEAD_DIM, q, zero)]
        outs = []
        for a in range(HEADS_PER_LANE_BLOCK):
            hd = pair * HEADS_PER_LANE_BLOCK + a
            chosen = [None if j == n_past else sel_ref[0, hd, j, half:half + 1, :] > 0.5
                      for j in range(n_past + 1)]
            m = None
            for j in range(n_past + 1):
                s = _dot_nt(k_ref[0, j * L:(j + 1) * L, lanes], q_heads[a]) + bias_ref[hd, n_past - j]
                if n_past > 0:
                    s_buf[half, hd, j * L:(j + 1) * L, :] = s
                mj = jnp.max(s, axis=0, keepdims=True)
                if chosen[j] is not None:
                    mj = jnp.where(chosen[j], mj, NEG)
                m = mj if m is None else jnp.maximum(m, mj)
            l = None
            acc = None
            for j in range(n_past + 1):
                shift = m if chosen[j] is None else jnp.where(chosen[j], m, -NEG)
                p = jnp.exp2((s_buf[half, hd, j * L:(j + 1) * L, :] if n_past > 0 else s) - shift)
                lj = jnp.sum(p, axis=0, keepdims=True)
                vT = vT_ref[0, pair, j, a * HEAD_DIM:(a + 1) * HEAD_DIM, :]
                oj = jnp.dot(vT, p.astype(BF16), preferred_element_type=F32)
                l = lj if l is None else l + lj
                acc = oj if acc is None else acc + oj
            outs.append(acc * (1.0 / l))
        o_ref[0, half, :, lanes] = jnp.concatenate(outs, axis=0).T.astype(BF16)

    def step(c_static):
        for pair in range(MOBA_PAIRS_PER_STEP):
            for half in range(2):
                attend(pair, half, c_static + half * (nb // 2))

    for c_static in range(nb // 2):
        pl.when(c == c_static)(functools.partial(step, c_static))


def _moba(qm, km, vmT, sel, bias):
    B, S, _ = qm.shape
    L = MOBA_BLOCK
    nb = S // L
    pairs = MOBA_PAIRS_PER_STEP
    heads = pairs * HEADS_PER_LANE_BLOCK
    width = pairs * LANES
    half_s = S // 2
    out = pl.pallas_call(
        _moba_kernel,
        grid=(D_MOBA // width, B, nb // 2),
        in_specs=[pl.BlockSpec((1, 2, L, width), lambda p, b, c: (b, 0, c, p)),
                  pl.BlockSpec((1, S, width), lambda p, b, c: (b, 0, p)),
                  pl.BlockSpec((1, pairs, nb, LANES, L), lambda p, b, c: (b, p, 0, 0, 0)),
                  pl.BlockSpec((1, heads, nb, 2, L), lambda p, b, c: (b, p, 0, 0, c)),
                  pl.BlockSpec((heads, nb, L, L), lambda p, b, c: (p, 0, 0, 0))],
        out_specs=pl.BlockSpec((1, 2, L, width), lambda p, b, c: (b, 0, c, p)),
        out_shape=jax.ShapeDtypeStruct((B, 2, half_s, D_MOBA), BF16),
        scratch_shapes=[pltpu.VMEM((2, heads, S, L), F32)],
        compiler_params=_cparams(3),
        name="moba_attn",
    )(qm.reshape(B, 2, half_s, D_MOBA), km, vmT, sel, bias)
    return out.reshape(B, S, D_MOBA)


def _swa_kernel(sink_ref, q_ref, *refs):
    n = SWA_BLOCKS_PER_STEP
    k_refs, v_refs = refs[:2 * n], refs[2 * n:4 * n]
    bias_ref, o_ref = refs[4 * n], refs[4 * n + 1]
    c = pl.program_id(1)
    for g in range(n):
        _swa_block(sink_ref, q_ref[0, g], k_refs[2 * g][0], k_refs[2 * g + 1][0], v_refs[2 * g][0],
                   v_refs[2 * g + 1][0], bias_ref, o_ref.at[0, g], c > 0 if g == 0 else None)


def _swa_block(sink_ref, q, k_prev, k_cur, vT_prev, vT_cur, bias_ref, o_ref, has_prev):
    W = SWA_BLOCK
    group = N_HEADS_SWA // N_KV_SWA
    kband = jnp.concatenate([k_prev, k_cur], axis=0)
    vbandT = jnp.concatenate([vT_prev, vT_cur], axis=1)
    q_rows = []
    for h in range(N_HEADS_SWA):
        qp = q[:, (h % group) * LANES:(h % group + 1) * LANES]
        lane = lax.broadcasted_iota(jnp.int32, qp.shape, 1)
        in_half = (lane >= HEAD_DIM) if h // group else (lane < HEAD_DIM)
        q_rows.append(jnp.where(in_half, qp, jnp.zeros_like(qp)))
    q_all = jnp.concatenate(q_rows, axis=0)
    sT = _dot_nt(kband, q_all) + bias_ref[...]
    if has_prev is not None:
        key = lax.broadcasted_iota(jnp.int32, sT.shape, 0)
        sT = jnp.where((key >= W) | has_prev, sT, NEG)
    sink = sink_ref[...]
    m = jnp.maximum(jnp.max(sT, axis=0, keepdims=True), sink)
    pr = jnp.exp2(sT - m)
    inv_l = 1.0 / (jnp.sum(pr, axis=0, keepdims=True) + jnp.exp2(sink - m))
    pr = pr.astype(BF16)
    cols = group * W
    outs = [jnp.dot(vbandT[g * HEAD_DIM:(g + 1) * HEAD_DIM, :], pr[:, g * cols:(g + 1) * cols],
                    preferred_element_type=F32) * inv_l[:, g * cols:(g + 1) * cols]
            for g in range(N_KV_SWA)]
    heads = [outs[h // group][:, (h % group) * W:(h % group + 1) * W] for h in range(N_HEADS_SWA)]
    o_ref[...] = jnp.concatenate(heads, axis=0).T.astype(BF16)


def _swa(sink_row, qs, ks, vsT, bias):
    B, S, _ = qs.shape
    W = SWA_BLOCK
    nq = S // W
    n = SWA_BLOCKS_PER_STEP
    far = nq // n
    k_spec = lambda f: pl.BlockSpec((1, W, D_SWA_KV), lambda b, c: (b, f(c), 0))
    v_spec = lambda f: pl.BlockSpec((1, D_SWA_KV, W), lambda b, c: (b, 0, f(c)))
    blocks = []
    for g in range(n):
        blocks.append((lambda c, g=g: jnp.maximum(c + g * far - 1, 0)))
        blocks.append((lambda c, g=g: c + g * far))
    out = pl.pallas_call(
        _swa_kernel,
        grid=(B, far),
        in_specs=[pl.BlockSpec((1, N_HEADS_SWA * W), lambda b, c: (0, 0)),
                  pl.BlockSpec((1, n, W, D_SWA_Q), lambda b, c: (b, 0, c, 0))]
                 + [k_spec(f) for f in blocks] + [v_spec(f) for f in blocks]
                 + [pl.BlockSpec((2 * W, N_HEADS_SWA * W), lambda b, c: (0, 0))],
        out_specs=pl.BlockSpec((1, n, W, D_SWA_Q), lambda b, c: (b, 0, c, 0)),
        out_shape=jax.ShapeDtypeStruct((B, n, S // n, D_SWA_Q), BF16),
        compiler_params=_cparams(2),
        name="swa_attn",
    )(sink_row, qs.reshape(B, n, S // n, D_SWA_Q), *([ks] * (2 * n)), *([vsT] * (2 * n)), bias)
    return out.reshape(B, S, D_SWA_Q)


def _outproj_kernel(x_ref, om_ref, os_ref, wm_ref, ws_ref, b_ref, mod_ref, g_ref, wrT_ref, br_ref,
                    x1_ref, hrt_ref, idx_ref, w_ref, rank_ref, counts_ref, carry_ref):
    attn = (jnp.dot(om_ref[...], wm_ref[...], preferred_element_type=F32)
            + jnp.dot(os_ref[...], ws_ref[...], preferred_element_type=F32) + b_ref[...])
    x1 = x_ref[...] + mod_ref[0, 2:3, :] * attn
    x1_ref[...] = x1
    h = _rmsnorm_rows(x1, g_ref[...]) * (1.0 + mod_ref[0, 4:5, :]) + mod_ref[0, 3:4, :]
    _store_row_tiles(hrt_ref, h)
    _route(h, wrT_ref, br_ref, idx_ref, w_ref, rank_ref, counts_ref, carry_ref)


def _store_row_tiles(ref, val, first_row=0):
    rows = val.shape[0]
    base = first_row * ROW_CHUNKS
    for c in range(ROW_CHUNKS):
        ref[pl.ds(base + c, rows, stride=ROW_CHUNKS), :] = val[:, c * LANES:(c + 1) * LANES]


def _load_row_tiles(ref, rows, lead=(), first_row=0):
    idx = tuple(lead)
    base = first_row * ROW_CHUNKS
    return jnp.concatenate([ref[idx + (pl.ds(base + c, rows, stride=ROW_CHUNKS), slice(None))]
                            for c in range(ROW_CHUNKS)], axis=1)


def _outproj(x2d, om, os_, w_out_m, w_out_s, b_out, mod6, g_ffn, w_routerT, b_router_col, seq_len):
    N, D = x2d.shape
    tm = OUTPROJ_TOKENS
    per_seq = seq_len // tm
    row = lambda t: (t, 0)
    col = lambda t: (0, t)
    const = lambda t: (0, 0)
    return pl.pallas_call(
        _outproj_kernel,
        grid=(N // tm,),
        in_specs=[pl.BlockSpec((tm, D), row),
                  pl.BlockSpec((tm, D_MOBA), row),
                  pl.BlockSpec((tm, D_SWA_Q), row),
                  pl.BlockSpec((D_MOBA, D), const),
                  pl.BlockSpec((D_SWA_Q, D), const),
                  pl.BlockSpec((1, D), const),
                  pl.BlockSpec((1, 6, D), lambda t: (t // per_seq, 0, 0)),
                  pl.BlockSpec((1, D), const),
                  pl.BlockSpec((N_EXPERTS, D), const),
                  pl.BlockSpec((N_EXPERTS, 1), const)],
        out_specs=(pl.BlockSpec((tm, D), row),
                   pl.BlockSpec((tm * ROW_CHUNKS, LANES), row),
                   pl.BlockSpec((TOP_K, tm), col),
                   pl.BlockSpec((TOP_K, tm), col),
                   pl.BlockSpec((TOP_K, tm), col),
                   pl.BlockSpec((N_EXPERTS, LANES), const)),
        out_shape=(jax.ShapeDtypeStruct((N, D), F32),
                   jax.ShapeDtypeStruct((N * ROW_CHUNKS, LANES), F32),
                   jax.ShapeDtypeStruct((TOP_K, N), jnp.int32),
                   jax.ShapeDtypeStruct((TOP_K, N), F32),
                   jax.ShapeDtypeStruct((TOP_K, N), jnp.int32),
                   jax.ShapeDtypeStruct((N_EXPERTS, LANES), jnp.int32)),
        scratch_shapes=[pltpu.VMEM((N_EXPERTS, LANES), F32)],
        compiler_params=_cparams(1),
        name="outproj_route",
    )(x2d, om, os_, w_out_m, w_out_s, b_out, mod6, g_ffn, w_routerT, b_router_col)


def _route(h, wT_ref, b_ref, idx_ref, w_ref, rank_ref, counts_ref, carry_ref):
    @pl.when(pl.program_id(0) == 0)
    def _():
        carry_ref[...] = jnp.zeros_like(carry_ref)

    w = wT_ref[...]
    w_hi = w.astype(BF16)
    w_lo = (w - w_hi.astype(F32)).astype(BF16)
    h_hi = h.astype(BF16)
    h_lo = (h - h_hi.astype(F32)).astype(BF16)
    by_h_hi = _dot_nt(jnp.concatenate([w_hi, w_lo], axis=0), h_hi)
    logits = (by_h_hi[:N_EXPERTS] + by_h_hi[N_EXPERTS:] + _dot_nt(w_hi, h_lo)) + b_ref[...]
    tm = logits.shape[1]
    eidx = lax.broadcasted_iota(jnp.int32, logits.shape, 0)
    vals, idxs = [], []
    cur = logits
    for _ in range(TOP_K):
        m = jnp.max(cur, axis=0, keepdims=True)
        am = jnp.min(jnp.where(cur == m, eidx, N_EXPERTS), axis=0, keepdims=True)
        vals.append(m)
        idxs.append(am)
        cur = jnp.where(eidx == am, -jnp.inf, cur)
    exps = [jnp.exp(v - vals[0]) for v in vals]
    inv = 1.0 / functools.reduce(lambda a, b: a + b, exps)
    idx_ref[...] = jnp.concatenate(idxs, axis=0)
    w_ref[...] = jnp.concatenate([e * inv for e in exps], axis=0)

    member = functools.reduce(lambda a, b: a | b, [eidx == am for am in idxs])
    earlier = (lax.broadcasted_iota(jnp.int32, (tm, tm), 0)
               < lax.broadcasted_iota(jnp.int32, (tm, tm), 1)).astype(BF16)
    before = jnp.dot(member.astype(BF16), earlier, preferred_element_type=F32) + carry_ref[:, 0:1]
    rank_ref[...] = jnp.concatenate(
        [jnp.sum(jnp.where(eidx == am, before, 0.0), axis=0, keepdims=True) for am in idxs],
        axis=0).astype(jnp.int32)
    total = carry_ref[...] + jnp.sum(member.astype(F32), axis=1, keepdims=True)
    carry_ref[...] = total
    counts_ref[...] = total.astype(jnp.int32)


def _expert_mlp(h_bf, wgu, bgu, wdn, bdn):
    gu = jnp.dot(h_bf, wgu, preferred_element_type=F32) + bgu
    g = jnp.minimum(gu[:, :D_FF], SWIGLU_LIMIT)
    u = jnp.clip(gu[:, D_FF:], -SWIGLU_LIMIT, SWIGLU_LIMIT)
    act = (u + 1.0) * (g * _sigmoid(SWIGLU_ALPHA * g))
    return jnp.dot(act.astype(BF16), wdn, preferred_element_type=F32) + bdn


def _tile_plan(counts, n_tiles_max):
    tiles_per = (counts + MOE_TILE - 1) // MOE_TILE
    tile_end = jnp.cumsum(tiles_per)
    tile_start = tile_end - tiles_per
    t_ids = jnp.arange(n_tiles_max, dtype=jnp.int32)
    texp = jnp.minimum(jnp.sum((t_ids[:, None] >= tile_end[None, :]).astype(jnp.int32), axis=1),
                       N_EXPERTS - 1)
    onehot = (texp[:, None] == jnp.arange(N_EXPERTS, dtype=jnp.int32)[None, :]).astype(jnp.int32)
    local = t_ids - onehot @ tile_start
    nvalid = jnp.clip(onehot @ counts - local * MOE_TILE, 0, MOE_TILE)
    n_active = tile_end[-1]
    src_tile = jnp.minimum(t_ids, n_active - 1)
    per_tile = MOE_TILE // MOE_MM_ROWS
    n_last = counts - (tiles_per - 1) * MOE_TILE
    edge = (tile_end - 1) * per_tile + (n_last - 1) // MOE_MM_ROWS
    boundary = jnp.where(counts > 0, edge, -1)
    fill = jnp.arange(1, per_tile, dtype=jnp.int32)[None, :]
    pad_only = jnp.where((counts[:, None] > 0) & (edge[:, None] % per_tile + fill < per_tile),
                         edge[:, None] + fill, -1).reshape(-1)
    tail = n_active * per_tile + jnp.arange(N_EXPERTS * per_tile, dtype=jnp.int32)
    zero_tiles = jnp.concatenate([boundary, pad_only, jnp.where(tail < n_tiles_max * per_tile, tail, -1)])
    ids = jnp.arange(N_EXPERTS, dtype=jnp.int32)
    later = jnp.where((ids[None, :] > ids[:, None]) & (counts[None, :] > 0), ids[None, :], N_EXPERTS)
    next_expert = jnp.min(later, axis=1)
    next_expert = jnp.where(next_expert < N_EXPERTS, next_expert, -1)
    return (texp.astype(jnp.int32), nvalid.astype(jnp.int32), src_tile.astype(jnp.int32),
            (tile_start * MOE_TILE).astype(jnp.int32), zero_tiles.astype(jnp.int32),
            next_expert.astype(jnp.int32))


def _row_copy(src, src_first, dst, dst_first, sem):
    return pltpu.make_async_copy(src.at[pl.ds(pl.multiple_of(src_first, ROW_CHUNKS), ROW_CHUNKS)],
                                 dst.at[pl.ds(pl.multiple_of(dst_first, ROW_CHUNKS), ROW_CHUNKS)],
                                 sem)


DISPATCH_SLOTS = 3


def _dispatch_kernel(zero_tiles_ref, pos_ref, h_hbm, x_hbm, zbuf, hbuf, zsem, load_sem, row_sem):
    t = pl.program_id(0)
    n_steps = pl.num_programs(0)
    group_rows = MOE_MM_ROWS * ROW_CHUNKS
    block_rows = DISPATCH_TOKENS * ROW_CHUNKS

    def block_load(step, slot):
        start = pl.multiple_of(step * block_rows, block_rows)
        return pltpu.make_async_copy(h_hbm.at[pl.ds(start, block_rows)], hbuf.at[slot], load_sem.at[slot])

    def rows_done(slot):
        for k in range(TOP_K):
            pltpu.make_async_copy(hbuf.at[slot], x_hbm.at[pl.ds(0, block_rows)], row_sem.at[slot]).wait()

    n_shared = N_EXPERTS
    n_zero = zero_tiles_ref.shape[0]

    def zero_copy(i):
        start = pl.multiple_of(zero_tiles_ref[i] * group_rows, group_rows)
        return pltpu.make_async_copy(zbuf, x_hbm.at[pl.ds(start, group_rows)],
                                     zsem.at[0 if i < n_shared else 1])

    def for_zero_groups(lo, hi, fn):
        for i in range(lo, hi):
            @pl.when(zero_tiles_ref[i] >= 0)
            def _():
                fn(zero_copy(i))

    @pl.when(t == 0)
    def _():
        zbuf[...] = jnp.zeros_like(zbuf)
        for_zero_groups(0, n_shared, lambda cp: cp.start())
        block_load(0, 0).start()
        if DISPATCH_SLOTS > 2:
            @pl.when(n_steps > 1)
            def _():
                block_load(1, 1).start()
        for_zero_groups(0, n_shared, lambda cp: cp.wait())
        for_zero_groups(n_shared, n_zero, lambda cp: cp.start())

    slot = t % DISPATCH_SLOTS
    block_load(t, slot).wait()

    def start_rows(r, carry):
        for k in range(TOP_K):
            _row_copy(hbuf.at[slot], r * ROW_CHUNKS, x_hbm, pos_ref[0, 0, k * DISPATCH_TOKENS + r],
                      row_sem.at[slot]).start(priority=k % 2)
        return carry
    lax.fori_loop(0, DISPATCH_TOKENS, start_rows, 0, unroll=8)

    refill = (t + 2) % DISPATCH_SLOTS

    @pl.when(t > 0)
    def _():
        rows_done(refill)

    @pl.when(t + 2 < n_steps)
    def _():
        block_load(t + 2, refill).start()

    @pl.when(t == n_steps - 1)
    def _():
        rows_done(slot)
        for_zero_groups(n_shared, n_zero, lambda cp: cp.wait())


def _dispatch(zero_tiles, pos_tiles, h_rt, n_rows):
    n_steps = pos_tiles.shape[0]
    grid_spec = pltpu.PrefetchScalarGridSpec(
        num_scalar_prefetch=1,
        grid=(n_steps,),
        in_specs=[pl.BlockSpec((1, 1, TOP_K * DISPATCH_TOKENS), lambda t, zt: (t, 0, 0),
                               memory_space=pltpu.SMEM),
                  pl.BlockSpec(memory_space=pl.ANY)],
        out_specs=pl.BlockSpec(memory_space=pl.ANY),
        scratch_shapes=[pltpu.VMEM((MOE_MM_ROWS * ROW_CHUNKS, LANES), F32),
                        pltpu.VMEM((DISPATCH_SLOTS, DISPATCH_TOKENS * ROW_CHUNKS, LANES), F32),
                        pltpu.SemaphoreType.DMA((2,)),
                        pltpu.SemaphoreType.DMA((DISPATCH_SLOTS,)),
                        pltpu.SemaphoreType.DMA((DISPATCH_SLOTS,))],
    )
    return pl.pallas_call(
        _dispatch_kernel,
        grid_spec=grid_spec,
        out_shape=jax.ShapeDtypeStruct((n_rows * ROW_CHUNKS, LANES), F32),
        compiler_params=_cparams(1),
        name="moe_dispatch",
    )(zero_tiles, pos_tiles, h_rt)


def _experts_kernel(texp_ref, nvalid_ref, src_ref, next_ref, x_ref, wgu_hbm, bgu_ref, wdn_hbm, bdn_ref,
                    y_ref, wgu_f32, wdn_f32, wgu_bf, wdn_bf, wsem):
    t = pl.program_id(0)
    nv = nvalid_ref[t]
    e = texp_ref[t]
    prev_e = texp_ref[jnp.maximum(t - 1, 0)]

    def weight_loads(expert):
        return (pltpu.make_async_copy(wgu_hbm.at[expert], wgu_f32, wsem.at[0]),
                pltpu.make_async_copy(wdn_hbm.at[expert], wdn_f32, wsem.at[1]))

    @pl.when((nv > 0) & ((t == 0) | (e != prev_e)))
    def _():
        @pl.when(t == 0)
        def _():
            for cp in weight_loads(e):
                cp.start()
        for cp in weight_loads(e):
            cp.wait()
        rows = CAST_ROWS
        def cast_gu(i, carry):
            r0 = pl.multiple_of(i * rows, rows)
            wgu_bf[pl.ds(r0, rows), :] = wgu_f32[pl.ds(r0, rows), :].astype(BF16)
            return carry
        lax.fori_loop(0, D_MODEL // rows, cast_gu, 0)
        def cast_dn(i, carry):
            r0 = pl.multiple_of(i * rows, rows)
            wdn_bf[pl.ds(r0, rows), :] = wdn_f32[pl.ds(r0, rows), :].astype(BF16)
            return carry
        lax.fori_loop(0, D_FF // rows, cast_dn, 0)
        nxt = next_ref[e]

        @pl.when(nxt >= 0)
        def _():
            for cp in weight_loads(nxt):
                cp.start()

    def mlp_rows(first_row):
        x = _load_row_tiles(x_ref, MOE_MM_ROWS, first_row=first_row).astype(BF16)
        _store_row_tiles(y_ref, _expert_mlp(x, wgu_bf[...], bgu_ref[0], wdn_bf[...], bdn_ref[0]), first_row)

    n_groups = MOE_TILE // MOE_MM_ROWS
    for live in range(n_groups + 1):
        lo, hi = (live - 1) * MOE_MM_ROWS, live * MOE_MM_ROWS

        @pl.when((nv > lo) & (nv <= hi) if live else (nv == 0))
        def _():
            for g in range(live):
                mlp_rows(g * MOE_MM_ROWS)
            if live < n_groups:
                y_ref[pl.ds(live * MOE_MM_ROWS * ROW_CHUNKS, (n_groups - live) * MOE_MM_ROWS * ROW_CHUNKS), :] = (
                    jnp.zeros(((n_groups - live) * MOE_MM_ROWS * ROW_CHUNKS, LANES), F32))


def _experts(texp, nvalid, src_tile, next_expert, x_sorted, w_gate_up, b_gate_up, w_down, b_down):
    n_tiles = texp.shape[0]
    D = D_MODEL
    tile_rows = MOE_TILE * ROW_CHUNKS
    grid_spec = pltpu.PrefetchScalarGridSpec(
        num_scalar_prefetch=4,
        grid=(n_tiles,),
        in_specs=[pl.BlockSpec((tile_rows, LANES), lambda t, te, nv, st, nx: (st[t], 0)),
                  pl.BlockSpec(memory_space=pl.ANY),
                  pl.BlockSpec((1, 1, 2 * D_FF), lambda t, te, nv, st, nx: (te[t], 0, 0)),
                  pl.BlockSpec(memory_space=pl.ANY),
                  pl.BlockSpec((1, 1, D), lambda t, te, nv, st, nx: (te[t], 0, 0))],
        out_specs=pl.BlockSpec((tile_rows, LANES), lambda t, te, nv, st, nx: (t, 0)),
        scratch_shapes=[pltpu.VMEM((D, 2 * D_FF), F32),
                        pltpu.VMEM((D_FF, D), F32),
                        pltpu.VMEM((D, 2 * D_FF), BF16),
                        pltpu.VMEM((D_FF, D), BF16),
                        pltpu.SemaphoreType.DMA((2,))],
    )
    return pl.pallas_call(
        _experts_kernel,
        grid_spec=grid_spec,
        out_shape=jax.ShapeDtypeStruct(x_sorted.shape, F32),
        compiler_params=_cparams(1),
        name="moe_experts",
    )(texp, nvalid, src_tile, next_expert, x_sorted, w_gate_up, b_gate_up.reshape(N_EXPERTS, 1, 2 * D_FF),
      w_down, b_down.reshape(N_EXPERTS, 1, D))


def _combine_kernel(pos_ref, pos_next_ref, y_hbm, w_ref, x1_ref, mod_ref, g_ref, o_ref, ybuf, sem):
    t = pl.program_id(0)
    n_steps = pl.num_programs(0)
    slot = t % 2

    def start_gather(idx_ref, s):
        def body(r, carry):
            for k in range(TOP_K):
                i = k * COMBINE_TOKENS + r
                _row_copy(y_hbm, idx_ref[0, 0, i], ybuf.at[s], i * ROW_CHUNKS, sem.at[s]).start(priority=k % 2)
            return carry
        lax.fori_loop(0, COMBINE_TOKENS, body, 0, unroll=8)

    @pl.when(t == 0)
    def _():
        start_gather(pos_ref, 0)

    @pl.when(t + 1 < n_steps)
    def _():
        start_gather(pos_next_ref, 1 - slot)

    slot_rows = TOP_K * COMBINE_TOKENS * ROW_CHUNKS
    pltpu.make_async_copy(y_hbm.at[pl.ds(0, slot_rows)], ybuf.at[slot], sem.at[slot]).wait()

    w = w_ref[...]
    acc = w[:, 0:1] * _load_row_tiles(ybuf, COMBINE_TOKENS, (slot,))
    for k in range(1, TOP_K):
        acc = acc + w[:, k:k + 1] * _load_row_tiles(ybuf, COMBINE_TOKENS, (slot,), k * COMBINE_TOKENS)
    x2 = x1_ref[...] + mod_ref[0, 5:6, :] * acc
    o_ref[...] = _rmsnorm_rows(x2, g_ref[...])


def _combine(pos_tiles, y_sorted, top_w, x1, mod6, g_final, seq_len):
    N, D = x1.shape
    tm = COMBINE_TOKENS
    n_steps = N // tm
    per_seq = seq_len // tm
    return pl.pallas_call(
        _combine_kernel,
        grid=(n_steps,),
        in_specs=[pl.BlockSpec((1, 1, TOP_K * tm), lambda t: (t, 0, 0), memory_space=pltpu.SMEM),
                  pl.BlockSpec((1, 1, TOP_K * tm), lambda t: (jnp.minimum(t + 1, n_steps - 1), 0, 0),
                               memory_space=pltpu.SMEM),
                  pl.BlockSpec(memory_space=pl.ANY),
                  pl.BlockSpec((tm, TOP_K), lambda t: (t, 0)),
                  pl.BlockSpec((tm, D), lambda t: (t, 0)),
                  pl.BlockSpec((1, 6, D), lambda t: (t // per_seq, 0, 0)),
                  pl.BlockSpec((1, D), lambda t: (0, 0))],
        out_specs=pl.BlockSpec((tm, D), lambda t: (t, 0)),
        out_shape=jax.ShapeDtypeStruct((N, D), F32),
        scratch_shapes=[pltpu.VMEM((2, TOP_K * tm * ROW_CHUNKS, LANES), F32),
                        pltpu.SemaphoreType.DMA((2,))],
        compiler_params=_cparams(1),
        name="moe_combine",
    )(pos_tiles, pos_tiles, y_sorted, top_w, x1, mod6, g_final)


def kernel(x, c, g_attn, w_ada, b_ada, w_in, b_in, w_out, b_out, rel_table, sinks, g_ffn, w_router,
           b_router, w_gate_up, b_gate_up, w_down, b_down, g_final):
    B, S, D = x.shape
    assert w_ada.shape[0] == 1, "the final norm is fused into the last layer; one layer supported"
    l = 0
    nb = S // MOBA_BLOCK
    i2 = 3 * D_MOBA
    i3 = i2 + D_SWA_Q
    group = N_HEADS_SWA // N_KV_SWA
    q_scale = ATTN_SCALE * LOG2E

    def prep_in(w):
        lead = w.shape[:-1]
        q_swa = (w[..., i2:i3] * q_scale).reshape(lead + (N_KV_SWA, group, HEAD_DIM))
        q_swa = jnp.swapaxes(q_swa, -3, -2).reshape(lead + (D_SWA_Q,))
        return jnp.concatenate([w[..., :D_MOBA] * q_scale, w[..., D_MOBA:i2], q_swa, w[..., i3:]], axis=-1)

    moba_bias = _moba_bias(rel_table, nb)
    swa_bias = _swa_bias(rel_table)

    mod6 = _ada(c, w_ada[l], b_ada[l]).reshape(B, 6, D)
    w_in_l = prep_in(w_in[l]).astype(BF16)
    b_in_l = prep_in(b_in[l]).reshape(1, D_IN)
    qm, km, vmT, kmean, qs, ks, vsT = _inproj(x, mod6, g_attn[l].reshape(1, D), w_in_l, b_in_l)
    sel = _gate(qm, kmean)
    o_m = _moba(qm, km, vmT, sel, moba_bias)
    sink_row = jnp.repeat(sinks[l] * LOG2E, SWA_BLOCK).reshape(1, N_HEADS_SWA * SWA_BLOCK)
    o_s = _swa(sink_row, qs, ks, vsT, swa_bias)
    w_out_m = w_out[l, :D_MOBA].astype(BF16)
    w_out_s = w_out[l, D_MOBA:].astype(BF16)
    N = B * S
    x1, h_rt, top_idx, top_w, rank, counts = _outproj(
        x.reshape(N, D), o_m.reshape(N, D_MOBA), o_s.reshape(N, D_SWA_Q), w_out_m, w_out_s,
        b_out[l].reshape(1, D), mod6, g_ffn[l].reshape(1, D), w_router[l].T,
        b_router[l].reshape(N_EXPERTS, 1), S)
    counts = counts[:, 0]
    n_tiles_max = N * TOP_K // MOE_TILE + N_EXPERTS
    texp, nvalid, src_tile, group_start, zero_tiles, next_expert = _tile_plan(counts, n_tiles_max)
    experts = jnp.arange(N_EXPERTS, dtype=jnp.int32)
    pos = rank + jnp.sum(jnp.where(top_idx[..., None] == experts, group_start, 0), axis=-1)
    assert DISPATCH_TOKENS == COMBINE_TOKENS
    pos_tiles = (pos * ROW_CHUNKS).reshape(TOP_K, N // DISPATCH_TOKENS, DISPATCH_TOKENS).transpose(
        1, 0, 2).reshape(N // DISPATCH_TOKENS, 1, TOP_K * DISPATCH_TOKENS)
    x_sorted = _dispatch(zero_tiles, pos_tiles, h_rt, n_tiles_max * MOE_TILE)
    y_sorted = _experts(texp, nvalid, src_tile, next_expert, x_sorted, w_gate_up[l], b_gate_up[l], w_down[l],
                        b_down[l])
    y = _combine(pos_tiles, y_sorted, top_w.T, x1, mod6, g_final.reshape(1, D), S)
    return y.reshape(B, S, D)
```
